```python
import math
import jax, jax.numpy as jnp
from jax import lax
import numpy as np

D_MODEL = 2048
BATCH = 4
SEQ = 8192
DEPTH = 2

HEAD_DIM = 64
HEADS_PER_MIXER = 8
N_MIXERS = 4
MIX_WIDTH = N_MIXERS * HEADS_PER_MIXER * HEAD_DIM
D_FF = 4 * D_MODEL
QBLK = 128
EPS = 1e-6
NEG = -1e30
BIG = 1e30

N_BUCKETS = 32
MAX_EXACT = 16
MAX_DISTANCE = 4096
N_BIAS_HEADS = 3 * HEADS_PER_MIXER

NSA_KV_GROUPS = 2
NSA_Q_PER_GROUP = HEADS_PER_MIXER // NSA_KV_GROUPS
NSA_KV_WIDTH = NSA_KV_GROUPS * HEAD_DIM
NSA_CMP_LEN = 32
NSA_CMP_STRIDE = 16
NSA_CMP_HIDDEN = 256
NSA_SEL_LEN = 64
NSA_TOP_N = 16
NSA_WINDOW = 512

MOBA_BLOCK = 256
MOBA_TOPK = 3

LONGNET_PATTERNS = ((128, 1), (512, 4), (2048, 16))

GW = HEADS_PER_MIXER * HEAD_DIM
PROJ_SIZES = (
    GW, NSA_KV_WIDTH, NSA_KV_WIDTH, NSA_KV_WIDTH, NSA_KV_WIDTH, NSA_KV_WIDTH, NSA_KV_WIDTH, 3 * HEADS_PER_MIXER,
    GW, GW, GW,
    GW, GW, GW, HEADS_PER_MIXER,
    GW, GW, GW,
)
PROJ_WIDTH = 5920

kernel_name = "hybrid_nsa_moba_fox_longnet_block"


def rmsnorm(x, g):
    xf = x.astype(jnp.float32)
    y = xf * lax.rsqrt(jnp.mean(xf * xf, axis=-1, keepdims=True) + EPS)
    return (y * g.astype(jnp.float32)).astype(x.dtype)


def t5_bucket(dist):
    dist = jnp.maximum(dist, 0)
    rel = jnp.log(jnp.maximum(dist, 1).astype(jnp.float32) / MAX_EXACT) / math.log(MAX_DISTANCE / MAX_EXACT)
    large = jnp.minimum(MAX_EXACT + (rel * (N_BUCKETS - MAX_EXACT)).astype(jnp.int32), N_BUCKETS - 1)
    return jnp.where(dist < MAX_EXACT, dist, large)


def masked_softmax(s, mask):
    s = jnp.where(mask, s, NEG)
    p = jax.nn.softmax(s, axis=-1)
    return jnp.where(mask, p, 0.0)


def merge_blocks(o):
    o = jnp.moveaxis(o, 0, -3)
    return o.reshape(o.shape[:-3] + (o.shape[-3] * o.shape[-2], o.shape[-1]))


def banded_attention(q, k, v, max_dist, dist_step, bias_tab):
    B, G, R, L, hd = q.shape
    nb = -(-L // QBLK)
    Lp = nb * QBLK
    nw = -(-max_dist // QBLK)
    q = jnp.pad(q, ((0, 0), (0, 0), (0, 0), (0, Lp - L), (0, 0)))
    kv_pad = ((0, 0), (0, 0), (nw * QBLK, Lp - L), (0, 0))
    kb = jnp.pad(k, kv_pad).reshape(B, G, nb + nw, QBLK, hd)
    vb = jnp.pad(v, kv_pad).reshape(B, G, nb + nw, QBLK, hd)
    blk_idx = jnp.arange(nb)[:, None] + jnp.arange(nw + 1)[None, :]
    kw = kb[:, :, blk_idx].reshape(B, G, nb, (nw + 1) * QBLK, hd)
    vw = vb[:, :, blk_idx].reshape(B, G, nb, (nw + 1) * QBLK, hd)
    qb = q.reshape(B, G, R, nb, QBLK, hd)
    s = jnp.einsum('bgrnqd,bgnkd->bgrnqk', qb, kw).astype(jnp.float32) * (hd ** -0.5)
    qpos = jnp.arange(Lp).reshape(nb, QBLK)
    kpos = (blk_idx[:, :, None] * QBLK + jnp.arange(QBLK)).reshape(nb, -1) - nw * QBLK
    dist = qpos[:, :, None] - kpos[:, None, :]
    mask = (dist >= 0) & (dist <= max_dist) & (kpos[:, None, :] >= 0)
    bias = jnp.moveaxis(bias_tab[t5_bucket(dist * dist_step)], -1, 0).reshape(G, R, nb, QBLK, -1)
    s = jnp.where(mask, s + bias, NEG)
    lse = jax.nn.logsumexp(s, axis=-1)
    p = jnp.exp(s - lse[..., None])
    o = jnp.einsum('bgrnqk,bgnkd->bgrnqd', p.astype(v.dtype), vw)
    o = o.reshape(B, G, R, Lp, hd)[:, :, :, :L]
    lse = lse.reshape(B, G, R, Lp)[..., :L]
    return o, lse


def nsa_compress(kx, pe, w1, w2):
    B, G, T, hd = kx.shape
    nc = (T - NSA_CMP_LEN) // NSA_CMP_STRIDE + 1
    idx = jnp.arange(nc)[:, None] * NSA_CMP_STRIDE + jnp.arange(NSA_CMP_LEN)[None, :]
    blk = kx[:, :, idx] + pe
    hmid = jax.nn.gelu(blk.reshape(B, G, nc, NSA_CMP_LEN * hd) @ w1)
    return hmid @ w2


def nsa_attention(q, k_cmp, v_cmp, k_slc, v_slc, k_win, v_win, gate_logits,
                  cmp_pe, phik_w1, phik_w2, phiv_w1, phiv_w2, bias_tab):
    B, G, R, T, hd = q.shape
    scale = hd ** -0.5
    kc = nsa_compress(k_cmp, cmp_pe, phik_w1, phik_w2)
    vc = nsa_compress(v_cmp, cmp_pe, phiv_w1, phiv_w2)
    nc = kc.shape[2]
    ns = T // NSA_SEL_LEN
    n_sel = min(NSA_TOP_N, ns)
    c_end = jnp.arange(nc) * NSA_CMP_STRIDE + NSA_CMP_LEN - 1
    s_start = jnp.arange(ns) * NSA_SEL_LEN
    overlap = ((c_end[:, None] - NSA_CMP_LEN + 1 < s_start[None, :] + NSA_SEL_LEN)
               & (c_end[:, None] >= s_start[None, :])).astype(jnp.float32)
    kbl = k_slc.reshape(B, G, ns, NSA_SEL_LEN, hd)
    vbl = v_slc.reshape(B, G, ns, NSA_SEL_LEN, hd)
    tab_gr = bias_tab.T.reshape(G, R, N_BUCKETS)
    b_i = jnp.arange(B)[:, None, None, None]
    g_i = jnp.arange(G)[None, :, None, None]
    g6 = jnp.arange(G)[None, :, None, None, None, None]
    r6 = jnp.arange(R)[None, None, :, None, None, None]
    blk_id = jnp.arange(ns)

    def block(i):
        t = i * QBLK + jnp.arange(QBLK)
        qb = lax.dynamic_slice_in_dim(q, i * QBLK, QBLK, axis=3)
        sc = jnp.einsum('bgrqd,bgcd->bgrqc', qb, kc).astype(jnp.float32) * scale
        dc = t[:, None] - c_end[None, :]
        sc = sc + jnp.moveaxis(bias_tab[t5_bucket(dc)], -1, 0).reshape(G, R, QBLK, nc)
        pc = masked_softmax(sc, dc >= 0)
        o_c = jnp.einsum('bgrqc,bgcd->bgrqd', pc.astype(vc.dtype), vc)
        imp = jnp.einsum('bgrqc,cn->bgqn', pc, overlap)
        cur = t // NSA_SEL_LEN
        forced = (blk_id[None, :] == 0) | (blk_id[None, :] == cur[:, None]) | (blk_id[None, :] == cur[:, None] - 1)
        valid = s_start[None, :] <= t[:, None]
        imp = jnp.where(forced, BIG, jnp.where(valid, imp, NEG))
        _, sel = lax.top_k(imp, n_sel)
        ks = kbl[b_i, g_i, sel]
        vs = vbl[b_i, g_i, sel]
        ss = jnp.einsum('bgrqd,bgqnkd->bgrqnk', qb, ks).astype(jnp.float32) * scale
        spos = sel[..., None] * NSA_SEL_LEN + jnp.arange(NSA_SEL_LEN)
        ds = t[None, None, :, None, None] - spos
        ss = ss + tab_gr[g6, r6, t5_bucket(ds)[:, :, None]]
        ss = ss.reshape(B, G, R, QBLK, n_sel * NSA_SEL_LEN)
        ps = masked_softmax(ss, (ds >= 0).reshape(B, G, 1, QBLK, n_sel * NSA_SEL_LEN))
        o_s = jnp.einsum('bgrqk,bgqkd->bgrqd', ps.astype(vs.dtype),
                         vs.reshape(B, G, QBLK, n_sel * NSA_SEL_LEN, hd))
        return o_c, o_s

    o_c, o_s = lax.map(block, jnp.arange(T // QBLK))
    o_c = merge_blocks(o_c)
    o_s = merge_blocks(o_s)
    o_w, _ = banded_attention(q, k_win, v_win, NSA_WINDOW - 1, 1, bias_tab)
    gates = jax.nn.sigmoid(gate_logits.astype(jnp.float32)).reshape(B, T, 3, G, R).transpose(2, 0, 3, 4, 1)[..., None]
    o = gates[0] * o_c + gates[1] * o_s + gates[2] * o_w
    return o.reshape(B, G * R, T, hd)


def moba_attention(q, k, v, bias_tab):
    B, H, T, hd = q.shape
    scale = hd ** -0.5
    nkb = -(-T // MOBA_BLOCK)
    pad = ((0, 0), (0, 0), (0, nkb * MOBA_BLOCK - T), (0, 0))
    kb = jnp.pad(k, pad).reshape(B, H, nkb, MOBA_BLOCK, hd)
    vb = jnp.pad(v, pad).reshape(B, H, nkb, MOBA_BLOCK, hd)
    kmean = jnp.mean(kb.astype(jnp.float32), axis=3).astype(k.dtype)
    n_top = min(MOBA_TOPK, nkb)
    tab_t = bias_tab.T
    b_i = jnp.arange(B)[:, None, None, None]
    h_i = jnp.arange(H)[None, :, None, None]
    h5 = jnp.arange(H)[None, :, None, None, None]

    def block(i):
        t = i * QBLK + jnp.arange(QBLK)
        cur = (i * QBLK) // MOBA_BLOCK
        qb = lax.dynamic_slice_in_dim(q, i * QBLK, QBLK, axis=2)
        gate = jnp.einsum('bhqd,bhnd->bhqn', qb, kmean).astype(jnp.float32)
        gate = jnp.where(jnp.arange(nkb) < cur, gate, NEG)
        _, sel = lax.top_k(gate, n_top)
        ks = kb[b_i, h_i, sel]
        vs = vb[b_i, h_i, sel]
        s_sel = jnp.einsum('bhqd,bhqnkd->bhqnk', qb, ks).astype(jnp.float32) * scale
        pos_sel = sel[..., None] * MOBA_BLOCK + jnp.arange(MOBA_BLOCK)
        s_sel = s_sel + tab_t[h5, t5_bucket(t[None, None, :, None, None] - pos_sel)]
        mask_sel = jnp.broadcast_to((sel < cur)[..., None], pos_sel.shape)
        k_own = lax.dynamic_index_in_dim(kb, cur, axis=2, keepdims=False)
        v_own = lax.dynamic_index_in_dim(vb, cur, axis=2, keepdims=False)
        d_own = t[:, None] - (cur * MOBA_BLOCK + jnp.arange(MOBA_BLOCK))[None, :]
        s_own = jnp.einsum('bhqd,bhkd->bhqk', qb, k_own).astype(jnp.float32) * scale + tab_t[:, t5_bucket(d_own)]
        n_sel_keys = n_top * MOBA_BLOCK
        s = jnp.concatenate([s_sel.reshape(B, H, QBLK, n_sel_keys), s_own], axis=-1)
        mask = jnp.concatenate([mask_sel.reshape(B, H, QBLK, n_sel_keys),
                                jnp.broadcast_to(d_own >= 0, (B, H, QBLK, MOBA_BLOCK))], axis=-1)
        p = masked_softmax(s, mask).astype(v.dtype)
        return (jnp.einsum('bhqk,bhqkd->bhqd', p[..., :n_sel_keys], vs.reshape(B, H, QBLK, n_sel_keys, hd))
                + jnp.einsum('bhqk,bhkd->bhqd', p[..., n_sel_keys:], v_own))

    return merge_blocks(lax.map(block, jnp.arange(T // QBLK)))


def fox_attention(q, k, v, log_f):
    B, H, T, hd = q.shape
    scale = hd ** -0.5
    c = jnp.cumsum(log_f, axis=-1)
    kpos = jnp.arange(T)

    def block(i):
        t = i * QBLK + jnp.arange(QBLK)
        qb = lax.dynamic_slice_in_dim(q, i * QBLK, QBLK, axis=2)
        cq = lax.dynamic_slice_in_dim(c, i * QBLK, QBLK, axis=2)
        s = jnp.einsum('bhqd,bhkd->bhqk', qb, k).astype(jnp.float32) * scale + (cq[..., None] - c[:, :, None, :])
        p = masked_softmax(s, kpos[None, :] <= t[:, None])
        return jnp.einsum('bhqk,bhkd->bhqd', p.astype(v.dtype), v)

    return merge_blocks(lax.map(block, jnp.arange(T // QBLK)))


def dilated_attention(q, k, v, bias_tab):
    B, H, T, hd = q.shape
    outs, lses = [], []
    for window, dil in LONGNET_PATTERNS:
        L = T // dil

        def fold(z):
            return z.reshape(B, H, L, dil, hd).transpose(0, 3, 1, 2, 4).reshape(B * dil, H, L, hd)

        o, lse = banded_attention(fold(q)[:, :, None], fold(k), fold(v), window // dil, dil, bias_tab)
        outs.append(o[:, :, 0].reshape(B, dil, H, L, hd).transpose(0, 2, 3, 1, 4).reshape(B, H, T, hd))
        lses.append(lse[:, :, 0].reshape(B, dil, H, L).transpose(0, 2, 3, 1).reshape(B, H, T))
    w = jax.nn.softmax(jnp.stack(lses, axis=0), axis=0)
    out = w[0][..., None] * outs[0] + w[1][..., None] * outs[1] + w[2][..., None] * outs[2]
    return out.astype(q.dtype)


def hybrid_mixer(h, w_in, w_out, cmp_pe, phik_w1, phik_w2, phiv_w1, phiv_w2, fox_bias, rel_bias):
    B, T, _ = h.shape
    proj = h @ w_in
    splits = [int(s) for s in np.cumsum(PROJ_SIZES)[:-1]]
    (qa, kca, vca, ksa, vsa, kwa, vwa, ga,
     qb, kb, vb,
     qc, kc, vc, fc,
     qd, kd, vd) = jnp.split(proj, splits, axis=-1)

    def heads(z):
        n = z.shape[-1] // HEAD_DIM
        return z.reshape(B, T, n, HEAD_DIM).transpose(0, 2, 1, 3)

    hm = HEADS_PER_MIXER
    qa_g = heads(qa).reshape(B, NSA_KV_GROUPS, NSA_Q_PER_GROUP, T, HEAD_DIM)
    o_a = nsa_attention(qa_g, heads(kca), heads(vca), heads(ksa), heads(vsa), heads(kwa), heads(vwa), ga,
                        cmp_pe, phik_w1, phik_w2, phiv_w1, phiv_w2, rel_bias[:, 0:hm])
    o_b = moba_attention(heads(qb), heads(kb), heads(vb), rel_bias[:, hm:2 * hm])
    log_f = jax.nn.log_sigmoid((fc + fox_bias).astype(jnp.float32)).transpose(0, 2, 1)
    o_c = fox_attention(heads(qc), heads(kc), heads(vc), log_f)
    o_d = dilated_attention(heads(qd), heads(kd), heads(vd), rel_bias[:, 2 * hm:3 * hm])
    o = jnp.concatenate([o_a.astype(h.dtype), o_b.astype(h.dtype), o_c.astype(h.dtype), o_d.astype(h.dtype)], axis=1)
    o = o.transpose(0, 2, 1, 3).reshape(B, T, MIX_WIDTH)
    return o @ w_out


def setup_inputs(seed: int = 0) -> dict:
    key = jax.random.key(seed)
    ks = jax.random.split(key, 17)

    def nrm(k, shape, fan):
        return jax.random.normal(k, shape, jnp.float32) * fan ** -0.5

    def gain(k):
        return 1.0 + 0.05 * jax.random.normal(k, (DEPTH, D_MODEL), jnp.float32)

    cmp_in = NSA_CMP_LEN * HEAD_DIM
    return {
        "x": jax.random.normal(ks[0], (BATCH, SEQ, D_MODEL), jnp.float32),
        "w_in": nrm(ks[1], (DEPTH, D_MODEL, PROJ_WIDTH), D_MODEL),
        "w_out": nrm(ks[2], (DEPTH, MIX_WIDTH, D_MODEL), MIX_WIDTH),
        "g_mix_pre": gain(ks[3]),
        "g_mix_post": gain(ks[4]),
        "g_mlp_pre": gain(ks[5]),
        "g_mlp_post": gain(ks[6]),
        "w_up": nrm(ks[7], (DEPTH, D_MODEL, D_FF), D_MODEL),
        "w_down": nrm(ks[8], (DEPTH, D_FF, D_MODEL), D_FF),
        "cmp_pe": 0.1 * jax.random.normal(ks[9], (DEPTH, NSA_CMP_LEN, HEAD_DIM), jnp.float32),
        "phik_w1": nrm(ks[10], (DEPTH, cmp_in, NSA_CMP_HIDDEN), cmp_in),
        "phik_w2": nrm(ks[11], (DEPTH, NSA_CMP_HIDDEN, HEAD_DIM), NSA_CMP_HIDDEN),
        "phiv_w1": nrm(ks[12], (DEPTH, cmp_in, NSA_CMP_HIDDEN), cmp_in),
        "phiv_w2": nrm(ks[13], (DEPTH, NSA_CMP_HIDDEN, HEAD_DIM), NSA_CMP_HIDDEN),
        "fox_bias": 4.0 + 0.5 * jax.random.normal(ks[14], (DEPTH, HEADS_PER_MIXER), jnp.float32),
        "rel_bias": 0.5 * jax.random.normal(ks[15], (N_BUCKETS, N_BIAS_HEADS), jnp.float32),
    }


def reference(x, w_in, w_out, g_mix_pre, g_mix_post, g_mlp_pre, g_mlp_post, w_up, w_down,
              cmp_pe, phik_w1, phik_w2, phiv_w1, phiv_w2, fox_bias, rel_bias):
    for l in range(DEPTH):
        h = rmsnorm(x, g_mix_pre[l])
        mix = hybrid_mixer(h, w_in[l], w_out[l], cmp_pe[l], phik_w1[l], phik_w2[l],
                           phiv_w1[l], phiv_w2[l], fox_bias[l], rel_bias)
        x = x + rmsnorm(mix, g_mix_post[l])
        h2 = rmsnorm(x, g_mlp_pre[l])
        y = jnp.square(jax.nn.relu(h2 @ w_up[l])) @ w_down[l]
        x = x + rmsnorm(y, g_mlp_post[l])
    return x
```

```python
import functools
import math

import jax
import jax.numpy as jnp
from jax import lax
from jax.experimental import pallas as pl
from jax.experimental.pallas import tpu as pltpu

F32 = jnp.float32
BF16 = jnp.bfloat16

D_MODEL = 2048
D_FF = 4 * D_MODEL
HEAD_DIM = 64
LANES = 128
EPS = 1e-6
NEG = -1e30
BIG = 1e30
SCALE = HEAD_DIM ** -0.5

N_BUCKETS = 32
MAX_EXACT = 16
MAX_DISTANCE = 4096
LAST_BUCKET_DIST = 2897

NSA_CMP_LEN = 32
NSA_CMP_STRIDE = 16
NSA_CMP_HIDDEN = 256
NSA_SEL_LEN = 64
NSA_TOP_N = 16
NSA_WINDOW = 512
MOBA_BLOCK = 256
MOBA_TOPK = 3
LONGNET_PATTERNS = ((128, 1), (512, 4), (2048, 16))
DIL_BAND = 128

P_WIDTH = 5632
PB_QA, PB_KSA, PB_VSA, PB_KWA, PB_VWA = 0, 4, 5, 6, 7
PB_QB, PB_KB, PB_VB = 8, 12, 16
PB_QC, PB_KC, PB_VC = 20, 24, 28
PB_QD, PB_KD, PB_VD = 32, 36, 40
P_BLOCKS = P_WIDTH // LANES

VMEM_LIMIT = 56 * 1024 * 1024


def _cparams(sem):
    return pltpu.CompilerParams(dimension_semantics=sem, vmem_limit_bytes=VMEM_LIMIT)


def _dot(a, b):
    return jnp.dot(a, b, preferred_element_type=F32)


def _dot_nt(a, b):
    return lax.dot_general(a, b, (((1,), (1,)), ((), ())), preferred_element_type=F32)


def _dot_hi(a, b):
    return jnp.dot(a, b, preferred_element_type=F32, precision=lax.Precision.HIGHEST)


def _dot_nt_hi(a, b):
    return lax.dot_general(a, b, (((1,), (1,)), ((), ())), preferred_element_type=F32,
                           precision=lax.Precision.HIGHEST)


def _t5_bucket(dist):
    dist = jnp.maximum(dist, 0)
    rel = jnp.log(jnp.maximum(dist, 1).astype(jnp.float32) / MAX_EXACT) / math.log(MAX_DISTANCE / MAX_EXACT)
    large = jnp.minimum(MAX_EXACT + (rel * (N_BUCKETS - MAX_EXACT)).astype(jnp.int32), N_BUCKETS - 1)
    return jnp.where(dist < MAX_EXACT, dist, large)


def _bias_tile_kernel(tab_ref, idx_ref, o_ref, *, head0):
    h = pl.program_id(0) + head0
    idx = idx_ref[0]
    acc = jnp.zeros(idx.shape, F32)
    for b in range(N_BUCKETS):
        acc = jnp.where(idx == b, tab_ref[b, h], acc)
    o_ref[0, 0] = acc


def _bias_tiles(rel_bias, idx, head0, n_heads):
    n, r, c = idx.shape
    return pl.pallas_call(
        functools.partial(_bias_tile_kernel, head0=head0),
        out_shape=jax.ShapeDtypeStruct((n_heads, n, r, c), F32),
        grid=(n_heads, n),
        in_specs=[pl.BlockSpec(memory_space=pltpu.SMEM),
                  pl.BlockSpec((1, r, c), lambda h, i: (i, 0, 0))],
        out_specs=pl.BlockSpec((1, 1, r, c), lambda h, i: (h, i, 0, 0)),
        compiler_params=_cparams(("parallel", "parallel")),
        name="bias_tiles",
    )(rel_bias, idx)


def _toeplitz_idx(n_tiles, rows, cols, row_stride, dist_step=1):
    nd = jnp.arange(n_tiles, dtype=jnp.int32)[:, None, None]
    r = jnp.arange(rows, dtype=jnp.int32)[None, :, None]
    c = jnp.arange(cols, dtype=jnp.int32)[None, None, :]
    return _t5_bucket((nd * row_stride + r - c) * dist_step)


def _n_toeplitz_tiles(row_stride, cols):
    return -(-(LAST_BUCKET_DIST + cols - 1) // row_stride) + 1


def _proj_kernel(x_ref, g_ref, w_ref, ws_ref, p_ref, s_ref, kc_ref, vc_ref, h_ref):
    @pl.when(pl.program_id(1) == 0)
    def _():
        x = x_ref[...]
        y = x * lax.rsqrt(jnp.mean(x * x, axis=-1, keepdims=True) + EPS)
        hb = (y * g_ref[...]).astype(BF16)
        h_ref[...] = hb
        small = _dot(hb, ws_ref[...])
        s_ref[...] = small[:, 0:LANES]
        kc_ref[...] = small[:, LANES:2 * LANES]
        vc_ref[...] = small[:, 2 * LANES:3 * LANES]

    p_ref[...] = _dot(h_ref[...], w_ref[...]).astype(BF16)


def _proj(x2, g, w_main, w_small, tm=512, tn=512):
    m = x2.shape[0]
    return pl.pallas_call(
        _proj_kernel,
        out_shape=(jax.ShapeDtypeStruct((m, P_WIDTH), BF16),
                   jax.ShapeDtypeStruct((m, LANES), F32),
                   jax.ShapeDtypeStruct((m, LANES), F32),
                   jax.ShapeDtypeStruct((m, LANES), F32)),
        grid=(m // tm, P_WIDTH // tn),
        in_specs=[pl.BlockSpec((tm, D_MODEL), lambda i, j: (i, 0)),
                  pl.BlockSpec((1, D_MODEL), lambda i, j: (0, 0)),
                  pl.BlockSpec((D_MODEL, tn), lambda i, j: (0, j)),
                  pl.BlockSpec((D_MODEL, 3 * LANES), lambda i, j: (0, 0))],
        out_specs=(pl.BlockSpec((tm, tn), lambda i, j: (i, j)),
                   pl.BlockSpec((tm, LANES), lambda i, j: (i, 0)),
                   pl.BlockSpec((tm, LANES), lambda i, j: (i, 0)),
                   pl.BlockSpec((tm, LANES), lambda i, j: (i, 0))),
        scratch_shapes=[pltpu.VMEM((tm, D_MODEL), BF16)],
        compiler_params=_cparams(("parallel", "arbitrary")),
        name="rms_in_proj",
    )(x2, g, w_main, w_small)


def _lane_lo(shape):
    return lax.broadcasted_iota(jnp.int32, shape, 1) < HEAD_DIM


def _softmax_first(s, v, m_ref, l_ref, acc_ref):
    m = jnp.max(s, axis=-1, keepdims=True)
    p = jnp.exp(s - m)
    m_ref[...] = m
    l_ref[...] = jnp.sum(p, axis=-1, keepdims=True)
    acc_ref[...] = _dot(p.astype(BF16), v)


def _softmax_next(s, v, m_ref, l_ref, acc_ref):
    m_prev = m_ref[...]
    m_new = jnp.maximum(m_prev, jnp.max(s, axis=-1, keepdims=True))
    alpha = jnp.exp(m_prev - m_new)
    p = jnp.exp(s - m_new)
    l_ref[...] = alpha * l_ref[...] + jnp.sum(p, axis=-1, keepdims=True)
    acc_ref[...] = alpha * acc_ref[...] + _dot(p.astype(BF16), v)
    m_ref[...] = m_new


def _pair_queries(q_ref):
    q = q_ref[...] * jnp.asarray(SCALE, BF16)
    lo = _lane_lo(q.shape)
    zero = jnp.zeros_like(q)
    return jnp.where(lo, q, zero), jnp.where(lo, zero, q)


def _rank_desc(v, q_lanes):
    n = v.shape[0]
    groups = n // 8
    rows = lax.broadcasted_iota(jnp.int32, (8, q_lanes), 0)
    blks = [v[8 * g:8 * g + 8, :] for g in range(groups)]
    cnt = [jnp.zeros((8, q_lanes), F32) for _ in range(groups)]
    one = jnp.ones((8, q_lanes), F32)
    zero = jnp.zeros((8, q_lanes), F32)
    for np_ in range(n):
        gk, r = divmod(np_, 8)
        row = jnp.broadcast_to(v[np_:np_ + 1, :], (8, q_lanes))
        for g in range(groups):
            if g < gk:
                beats = row > blks[g]
            elif g > gk:
                beats = row >= blks[g]
            else:
                beats = (row > blks[g]) | ((row >= blks[g]) & (rows > r))
            cnt[g] = cnt[g] + jnp.where(beats, one, zero)
    return jnp.concatenate(cnt, axis=0)


def _fox_cumsum_kernel(s_ref, fb_ref, c_ref, carry_ref, *, tc):
    @pl.when(pl.program_id(1) == 0)
    def _():
        carry_ref[...] = jnp.zeros_like(carry_ref)

    xt = s_ref[0].T
    z = xt[24:32, :] + fb_ref[...]
    logf = -(jnp.maximum(-z, 0.0) + jnp.log(1.0 + jnp.exp(-jnp.abs(z))))
    ri = lax.broadcasted_iota(jnp.int32, (tc, tc), 0)
    ci = lax.broadcasted_iota(jnp.int32, (tc, tc), 1)
    upper = jnp.where(ri <= ci, 1.0, 0.0).astype(F32)
    c = _dot_hi(logf, upper) + carry_ref[...]
    c_ref[0] = c
    carry_ref[...] = c[:, tc - 1:tc]


def _fox_cumsum(small3, fox_bias_col, tc=512):
    b, t, _ = small3.shape
    tc = min(tc, t)
    return pl.pallas_call(
        functools.partial(_fox_cumsum_kernel, tc=tc),
        out_shape=jax.ShapeDtypeStruct((b, 8, t), F32),
        grid=(b, t // tc),
        in_specs=[pl.BlockSpec((1, tc, LANES), lambda i, j: (i, j, 0)),
                  pl.BlockSpec((8, 1), lambda i, j: (0, 0))],
        out_specs=pl.BlockSpec((1, 8, tc), lambda i, j: (i, 0, j)),
        scratch_shapes=[pltpu.VMEM((8, 1), F32)],
        compiler_params=_cparams(("parallel", "arbitrary")),
        name="fox_cumsum",
    )(small3, fox_bias_col)


def _fox_kernel(q_ref, k_ref, v_ref, c_ref, o_ref, m_ref, l_ref, acc_ref, *, tq):
    hp = pl.program_id(1)
    qi = pl.program_id(2)
    qs = _pair_queries(q_ref.at[0])
    ri = lax.broadcasted_iota(jnp.int32, (tq, tq), 0)
    ci = lax.broadcasted_iota(jnp.int32, (tq, tq), 1)
    causal = ri >= ci

    def tile(j, first):
        k0 = pl.multiple_of(j * tq, tq)
        kt = k_ref[0, pl.ds(k0, tq), :]
        vt = v_ref[0, pl.ds(k0, tq), :]
        for hh in range(2):
            ck = c_ref[0, pl.ds(2 * hp + hh, 1), pl.ds(k0, tq)]
            s = _dot_nt(qs[hh], kt) - ck
            rows = slice(hh * tq, (hh + 1) * tq)
            args = (vt, m_ref.at[rows], l_ref.at[rows], acc_ref.at[rows])
            if first:
                _softmax_first(jnp.where(causal, s, NEG), *args)
            else:
                _softmax_next(s, *args)

    tile(qi, True)
    lax.fori_loop(0, qi, lambda j, c: (tile(j, False), c)[1], 0)
    out_lo = acc_ref[0:tq, :] / l_ref[0:tq, :]
    out_hi = acc_ref[tq:2 * tq, :] / l_ref[tq:2 * tq, :]
    o_ref[0] = jnp.where(_lane_lo((tq, LANES)), out_lo, out_hi).astype(BF16)


def _fox(p3, c_row, tq=256):
    b, t, _ = p3.shape
    tq = min(tq, t)
    return pl.pallas_call(
        functools.partial(_fox_kernel, tq=tq),
        out_shape=jax.ShapeDtypeStruct((b, t, 4 * LANES), BF16),
        grid=(b, 4, t // tq),
        in_specs=[pl.BlockSpec((1, tq, LANES), lambda i, h, q: (i, q, PB_QC + h)),
                  pl.BlockSpec((1, t, LANES), lambda i, h, q: (i, 0, PB_KC + h)),
                  pl.BlockSpec((1, t, LANES), lambda i, h, q: (i, 0, PB_VC + h)),
                  pl.BlockSpec((1, 8, t), lambda i, h, q: (i, 0, 0))],
        out_specs=pl.BlockSpec((1, tq, LANES), lambda i, h, q: (i, q, h)),
        scratch_shapes=[pltpu.VMEM((2 * tq, 1), F32), pltpu.VMEM((2 * tq, 1), F32),
                        pltpu.VMEM((2 * tq, LANES), F32)],
        compiler_params=_cparams(("parallel", "parallel", "arbitrary")),
        name="fox_attention",
    )(p3, p3, p3, c_row)


def _moba_kernel(q_ref, k_ref, v_ref, tab_ref, o_ref, km_ref, m_ref, l_ref, acc_ref, *, nkb, nkp, n_tab):
    tq = MOBA_BLOCK
    qi = pl.program_id(2)

    @pl.when(qi == 0)
    def _():
        km_ref[...] = jnp.zeros_like(km_ref)
        for n in range(nkb):
            kb = k_ref[0, n * tq:(n + 1) * tq, :].astype(F32)
            km_ref[n:n + 1, :] = jnp.mean(kb, axis=0, keepdims=True)

    qs = _pair_queries(q_ref.at[0])
    q_both = jnp.concatenate([qs[0], qs[1]], axis=0).astype(F32)
    gate = _dot_nt_hi(km_ref[...], q_both)
    past = lax.broadcasted_iota(jnp.int32, gate.shape, 0) < qi
    gate = jnp.where(past, gate, NEG)
    chosen = (_rank_desc(gate, 2 * tq) < MOBA_TOPK) & past
    negsel = jnp.where(chosen, 0.0, NEG).astype(F32)
    if nkp < LANES:
        negsel = jnp.concatenate([negsel, jnp.zeros((LANES - nkp, 2 * tq), F32)], axis=0)
    negsel = negsel.T.astype(BF16)

    ri = lax.broadcasted_iota(jnp.int32, (tq, tq), 0)
    ci = lax.broadcasted_iota(jnp.int32, (tq, tq), 1)
    causal = ri >= ci

    def tile(j, first):
        k0 = pl.multiple_of(j * tq, tq)
        kt = k_ref[0, pl.ds(k0, tq), :]
        vt = v_ref[0, pl.ds(k0, tq), :]
        nd = jnp.minimum(qi - j, n_tab - 1)
        if not first:
            pick = lax.broadcasted_iota(jnp.int32, (LANES, tq), 0) == j
            pick = jnp.where(pick, 1.0, 0.0).astype(BF16)
        for hh in range(2):
            rows = slice(hh * tq, (hh + 1) * tq)
            s = _dot_nt(qs[hh], kt) + tab_ref[hh, nd]
            args = (vt, m_ref.at[rows], l_ref.at[rows], acc_ref.at[rows])
            if first:
                _softmax_first(jnp.where(causal, s, NEG), *args)
            else:
                _softmax_next(s + _dot(negsel[rows, :], pick), *args)

    tile(qi, True)
    lax.fori_loop(0, qi, lambda j, c: (tile(j, False), c)[1], 0)
    out_lo = acc_ref[0:tq, :] / l_ref[0:tq, :]
    out_hi = acc_ref[tq:2 * tq, :] / l_ref[tq:2 * tq, :]
    o_ref[0] = jnp.where(_lane_lo((tq, LANES)), out_lo, out_hi).astype(BF16)


def _moba(p3, tab):
    b, t, _ = p3.shape
    tq = MOBA_BLOCK
    nkb = t // tq
    nkp = -(-nkb // 8) * 8
    n_tab = tab.shape[1]
    return pl.pallas_call(
        functools.partial(_moba_kernel, nkb=nkb, nkp=nkp, n_tab=n_tab),
        out_shape=jax.ShapeDtypeStruct((b, t, 4 * LANES), BF16),
        grid=(b, 4, t // tq),
        in_specs=[pl.BlockSpec((1, tq, LANES), lambda i, h, q: (i, q, PB_QB + h)),
                  pl.BlockSpec((1, t, LANES), lambda i, h, q: (i, 0, PB_KB + h)),
                  pl.BlockSpec((1, t, LANES), lambda i, h, q: (i, 0, PB_VB + h)),
                  pl.BlockSpec((2, n_tab, tq, tq), lambda i, h, q: (h, 0, 0, 0))],
        out_specs=pl.BlockSpec((1, tq, LANES), lambda i, h, q: (i, q, h)),
        scratch_shapes=[pltpu.VMEM((nkp, LANES), F32),
                        pltpu.VMEM((2 * tq, 1), F32), pltpu.VMEM((2 * tq, 1), F32),
                        pltpu.VMEM((2 * tq, LANES), F32)],
        compiler_params=_cparams(("parallel", "parallel", "arbitrary")),
        name="moba_attention",
    )(p3, p3, p3, tab)


def _gelu_tanh(x):
    return 0.5 * x * (1.0 + jnp.tanh(math.sqrt(2.0 / math.pi) * (x + 0.044715 * (x * x * x))))


def _nsa_compress_kernel(r_ref, pe_ref, wlo_ref, whi_ref, w2_ref, o_ref, *, nr):
    r = r_ref[0]
    pe = pe_ref[...]
    a = _dot((r + pe[0:1, :]).astype(BF16), wlo_ref[...])
    b = _dot((r + pe[1:2, :]).astype(BF16), whi_ref[...])
    hid = _gelu_tanh(a + pltpu.roll(b, nr - 1, axis=0))
    o_ref[0] = _dot(hid.astype(BF16), w2_ref[...]).astype(BF16)


def _nsa_compress(r3, pe2, wlo, whi, w2):
    b, nr, w = r3.shape
    return pl.pallas_call(
        functools.partial(_nsa_compress_kernel, nr=nr),
        out_shape=jax.ShapeDtypeStruct((b, nr, LANES), BF16),
        grid=(b,),
        in_specs=[pl.BlockSpec((1, nr, w), lambda i: (i, 0, 0)),
                  pl.BlockSpec((2, w), lambda i: (0, 0)),
                  pl.BlockSpec((w, 2 * NSA_CMP_HIDDEN), lambda i: (0, 0)),
                  pl.BlockSpec((w, 2 * NSA_CMP_HIDDEN), lambda i: (0, 0)),
                  pl.BlockSpec((2 * NSA_CMP_HIDDEN, LANES), lambda i: (0, 0))],
        out_specs=pl.BlockSpec((1, nr, LANES), lambda i: (i, 0, 0)),
        compiler_params=_cparams(("parallel",)),
        name="nsa_compress",
    )(r3, pe2, wlo, whi, w2)


def _group_queries(q_ref, g):
    half = lax.broadcasted_iota(jnp.int32, (q_ref.shape[0], LANES), 1) // HEAD_DIM
    lo = half == 0
    in_group = half == g
    out = []
    for pb in range(2):
        q = q_ref[:, pb * LANES:(pb + 1) * LANES].astype(F32) * SCALE
        qr = pltpu.roll(q, HEAD_DIM, axis=1)
        for dup in (jnp.where(lo, q, qr), jnp.where(lo, qr, q)):
            out.append(jnp.where(in_group, dup, 0.0).astype(BF16))
    return jnp.concatenate(out, axis=0)


def _group_outputs(acc, g, tq):
    half = lax.broadcasted_iota(jnp.int32, (tq, LANES), 1) // HEAD_DIM
    lo = half == 0
    in_group = half == g
    dups = []
    for h in range(4):
        a = acc[h * tq:(h + 1) * tq, :]
        dups.append(jnp.where(in_group, a, pltpu.roll(a, HEAD_DIM, axis=1)))
    return jnp.concatenate([jnp.where(lo, dups[0], dups[1]), jnp.where(lo, dups[2], dups[3])], axis=1)


def _group_gates(s_ref, g, branch):
    sig = 1.0 / (1.0 + jnp.exp(-s_ref[...]))
    col = lax.broadcasted_iota(jnp.int32, (LANES, 2 * LANES), 0)
    lane = lax.broadcasted_iota(jnp.int32, (LANES, 2 * LANES), 1)
    spread = jnp.where(col == branch * 8 + g * 4 + lane // HEAD_DIM, 1.0, 0.0).astype(F32)
    return _dot_hi(sig, spread)


def _nsa_cmp_kernel(q_ref, kc_ref, vc_ref, tab_ref, ovl_ref, s_ref, o_ref, sel_ref, *, tq, ncp, nsp):
    g = pl.program_id(1)
    qi = pl.program_id(2)
    qs = _group_queries(q_ref.at[0], g)
    s = _dot_nt(qs, kc_ref[0]) + tab_ref[...].reshape(4 * tq, ncp)
    t = qi * tq + (lax.broadcasted_iota(jnp.int32, (4 * tq, ncp), 0) & (tq - 1))
    c = lax.broadcasted_iota(jnp.int32, (4 * tq, ncp), 1)
    ok = (t - (c * NSA_CMP_STRIDE + NSA_CMP_LEN - 1) >= 0) & (c < ncp - 1)
    s = jnp.where(ok, s, NEG)
    e = jnp.exp(s - jnp.max(s, axis=-1, keepdims=True))
    p = jnp.where(ok, e / jnp.sum(e, axis=-1, keepdims=True), 0.0)
    o = _dot(p.astype(BF16), vc_ref[0])
    gate = _group_gates(s_ref.at[0], g, 0)
    o_ref[0] = (_group_outputs(o, g, tq) * gate).astype(BF16)

    psum = p[0:tq] + p[tq:2 * tq] + p[2 * tq:3 * tq] + p[3 * tq:4 * tq]
    imp = _dot_nt_hi(ovl_ref[...], psum)
    n = lax.broadcasted_iota(jnp.int32, (nsp, tq), 0)
    cur = (qi * tq + lax.broadcasted_iota(jnp.int32, (nsp, tq), 1)) // NSA_SEL_LEN
    forced = (n == 0) | (n == cur) | (n == cur - 1)
    valid = n <= cur
    imp = jnp.where(forced, BIG, jnp.where(valid, imp, NEG))
    chosen = (_rank_desc(imp, tq) < NSA_TOP_N) & valid
    negsel = jnp.where(chosen, 0.0, NEG).astype(F32)
    sel_ref[0] = negsel.T.astype(BF16)


def _nsa_cmp(p3, kc, vc, tab, ovl_t, small3, tq=128):
    b, t, _ = p3.shape
    ncp = kc.shape[1]
    nsp = ovl_t.shape[0]
    return pl.pallas_call(
        functools.partial(_nsa_cmp_kernel, tq=tq, ncp=ncp, nsp=nsp),
        out_shape=(jax.ShapeDtypeStruct((b, t, 4 * LANES), BF16),
                   jax.ShapeDtypeStruct((b, t, 2 * nsp), BF16)),
        grid=(b, 2, t // tq),
        in_specs=[pl.BlockSpec((1, tq, 2 * LANES), lambda i, g, q: (i, q, g)),
                  pl.BlockSpec((1, ncp, LANES), lambda i, g, q: (i, 0, 0)),
                  pl.BlockSpec((1, ncp, LANES), lambda i, g, q: (i, 0, 0)),
                  pl.BlockSpec((4, 1, tq, ncp), lambda i, g, q: (g, q, 0, 0)),
                  pl.BlockSpec((nsp, ncp), lambda i, g, q: (0, 0)),
                  pl.BlockSpec((1, tq, LANES), lambda i, g, q: (i, q, 0))],
        out_specs=(pl.BlockSpec((1, tq, 2 * LANES), lambda i, g, q: (i, q, g)),
                   pl.BlockSpec((1, tq, nsp), lambda i, g, q: (i, q, g))),
        compiler_params=_cparams(("parallel", "parallel", "parallel")),
        name="nsa_compressed",
    )(p3, kc, vc, tab, ovl_t, small3)


def _nsa_sel_kernel(q_ref, k_ref, v_ref, tab_ref, sel_ref, s_ref, o_ref, m_ref, l_ref, acc_ref,
                    *, tq, tk, n_tab, nsp):
    g = pl.program_id(1)
    qi = pl.program_id(2)
    qs = _group_queries(q_ref.at[0], g)
    negsel = sel_ref[0]
    ri = lax.broadcasted_iota(jnp.int32, (tq, tk), 0)
    ci = lax.broadcasted_iota(jnp.int32, (tq, tk), 1)
    blk = lax.broadcasted_iota(jnp.int32, (nsp, tk), 0)
    blk_of_key = lax.broadcasted_iota(jnp.int32, (nsp, tk), 1) // NSA_SEL_LEN
    j0 = (qi * tq) // tk

    def tile(j, first):
        k0 = pl.multiple_of(j * tk, tk)
        kt = k_ref[0, pl.ds(k0, tk), :]
        vt = v_ref[0, pl.ds(k0, tk), :]
        nd = jnp.minimum(qi - j * (tk // tq), n_tab - 1)
        pick = jnp.where(blk == j * (tk // NSA_SEL_LEN) + blk_of_key, 1.0, 0.0).astype(BF16)
        mask = _dot(negsel, pick)
        if first:
            mask = jnp.where(qi * tq + ri >= j * tk + ci, mask, NEG)
        s = _dot_nt(qs, kt).reshape(4, tq, tk) + tab_ref[:, nd] + mask[None]
        s = s.reshape(4 * tq, tk)
        (_softmax_first if first else _softmax_next)(s, vt, m_ref, l_ref, acc_ref)

    tile(j0, True)
    lax.fori_loop(0, j0, lambda j, c: (tile(j, False), c)[1], 0)
    gate = _group_gates(s_ref.at[0], g, 1)
    o_ref[0] = (_group_outputs(acc_ref[...] / l_ref[...], g, tq) * gate).astype(BF16)


def _nsa_win_kernel(q_ref, k_ref, v_ref, tab_ref, s_ref, o_ref, m_ref, l_ref, acc_ref, *, tq, tk, n_tab):
    g = pl.program_id(1)
    qi = pl.program_id(2)
    qs = _group_queries(q_ref.at[0], g)
    ri = lax.broadcasted_iota(jnp.int32, (tq, tk), 0)
    ci = lax.broadcasted_iota(jnp.int32, (tq, tk), 1)
    j0 = (qi * tq) // tk
    n_back = -(-(NSA_WINDOW - 1) // tk)

    def tile(j, first):
        k0 = pl.multiple_of(j * tk, tk)
        kt = k_ref[0, pl.ds(k0, tk), :]
        vt = v_ref[0, pl.ds(k0, tk), :]
        nd = jnp.minimum(qi - j * (tk // tq), n_tab - 1)
        dist = (qi * tq - j * tk) + ri - ci
        mask = jnp.where((dist >= 0) & (dist <= NSA_WINDOW - 1), 0.0, NEG)
        s = _dot_nt(qs, kt).reshape(4, tq, tk) + tab_ref[:, nd] + mask[None]
        s = s.reshape(4 * tq, tk)
        (_softmax_first if first else _softmax_next)(s, vt, m_ref, l_ref, acc_ref)

    tile(j0, True)
    lax.fori_loop(jnp.maximum(j0 - n_back, 0), j0, lambda j, c: (tile(j, False), c)[1], 0)
    gate = _group_gates(s_ref.at[0], g, 2)
    o_ref[0] = (_group_outputs(acc_ref[...] / l_ref[...], g, tq) * gate).astype(BF16)


def _nsa_dense(p3, tab, small3, negsel, tq=128, tk=256):
    b, t, _ = p3.shape
    tk = min(tk, t)
    n_tab = tab.shape[1]
    pb_k, pb_v = (PB_KSA, PB_VSA) if negsel is not None else (PB_KWA, PB_VWA)
    in_specs = [pl.BlockSpec((1, tq, 2 * LANES), lambda i, g, q: (i, q, g)),
                pl.BlockSpec((1, t, LANES), lambda i, g, q: (i, 0, pb_k)),
                pl.BlockSpec((1, t, LANES), lambda i, g, q: (i, 0, pb_v)),
                pl.BlockSpec((4, n_tab, tq, tk), lambda i, g, q: (g, 0, 0, 0))]
    args = [p3, p3, p3, tab]
    if negsel is not None:
        nsp = negsel.shape[2] // 2
        in_specs.append(pl.BlockSpec((1, tq, nsp), lambda i, g, q: (i, q, g)))
        args.append(negsel)
        body = functools.partial(_nsa_sel_kernel, tq=tq, tk=tk, n_tab=n_tab, nsp=nsp)
        name = "nsa_selected"
    else:
        body = functools.partial(_nsa_win_kernel, tq=tq, tk=tk, n_tab=n_tab)
        name = "nsa_window"
    in_specs.append(pl.BlockSpec((1, tq, LANES), lambda i, g, q: (i, q, 0)))
    args.append(small3)
    return pl.pallas_call(
        body,
        out_shape=jax.ShapeDtypeStruct((b, t, 4 * LANES), BF16),
        grid=(b, 2, t // tq),
        in_specs=in_specs,
        out_specs=pl.BlockSpec((1, tq, 2 * LANES), lambda i, g, q: (i, q, g)),
        scratch_shapes=[pltpu.VMEM((4 * tq, 1), F32), pltpu.VMEM((4 * tq, 1), F32),
                        pltpu.VMEM((4 * tq, LANES), F32)],
        compiler_params=_cparams(("parallel", "parallel", "arbitrary")),
        name=name,
    )(*args)


def _dilated_kernel(q_ref, kp_ref, kc_ref, vp_ref, vc_ref, tab_ref, o_ref, lse_ref, *, tq):
    qi = pl.program_id(3)
    qs = _pair_queries(q_ref.at[0])
    ri = lax.broadcasted_iota(jnp.int32, (tq, tq), 0)
    ci = lax.broadcasted_iota(jnp.int32, (tq, tq), 1)
    ok_cur = ri >= ci
    ok_prev = (ri - ci) <= jnp.where(qi > 0, 0, -tq - 1)
    outs, lses = [], []
    for hh in range(2):
        s_c = jnp.where(ok_cur, _dot_nt(qs[hh], kc_ref[0]) + tab_ref[hh, 0], NEG)
        s_p = jnp.where(ok_prev, _dot_nt(qs[hh], kp_ref[0]) + tab_ref[hh, 1], NEG)
        m = jnp.maximum(jnp.max(s_c, axis=-1, keepdims=True), jnp.max(s_p, axis=-1, keepdims=True))
        p_c = jnp.exp(s_c - m)
        p_p = jnp.exp(s_p - m)
        l = jnp.sum(p_c, axis=-1, keepdims=True) + jnp.sum(p_p, axis=-1, keepdims=True)
        o = _dot(p_c.astype(BF16), vc_ref[0]) + _dot(p_p.astype(BF16), vp_ref[0])
        outs.append(o / l)
        lses.append(jnp.broadcast_to(m + jnp.log(l), (tq, LANES)))
    lo = _lane_lo((tq, LANES))
    o_ref[0] = jnp.where(lo, outs[0], outs[1]).astype(BF16)
    lse_ref[0] = jnp.where(lo, lses[0], lses[1])


def _dilated(p3, tab, dil, tq=DIL_BAND):
    b, t, _ = p3.shape
    l = t // dil
    pv = p3.reshape(b, l, dil * P_WIDTH)

    def spec(pb, prev):
        def index(i, r, h, q):
            return (i, jnp.maximum(q - 1, 0) if prev else q, r * P_BLOCKS + pb + h)
        return pl.BlockSpec((1, tq, LANES), index)

    out_spec = pl.BlockSpec((1, tq, LANES), lambda i, r, h, q: (i, q, r * 4 + h))
    o, lse = pl.pallas_call(
        functools.partial(_dilated_kernel, tq=tq),
        out_shape=(jax.ShapeDtypeStruct((b, l, dil * 4 * LANES), BF16),
                   jax.ShapeDtypeStruct((b, l, dil * 4 * LANES), F32)),
        grid=(b, dil, 4, l // tq),
        in_specs=[spec(PB_QD, False), spec(PB_KD, True), spec(PB_KD, False),
                  spec(PB_VD, True), spec(PB_VD, False),
                  pl.BlockSpec((2, 2, tq, tq), lambda i, r, h, q: (h, 0, 0, 0))],
        out_specs=(out_spec, out_spec),
        compiler_params=_cparams(("parallel", "parallel", "parallel", "parallel")),
        name="dilated_attention_%d" % dil,
    )(pv, pv, pv, pv, pv, tab)
    return o.reshape(b * t, 4 * LANES), lse.reshape(b * t, 4 * LANES)


def _out_proj_kernel(x_ref, oc_ref, os_ref, ow_ref, ob_ref, of_ref,
                     d1_ref, d2_ref, d3_ref, l1_ref, l2_ref, l3_ref, w_ref, g_ref, o_ref, cat_ref):
    gw = 4 * LANES
    a = oc_ref[...].astype(F32) + os_ref[...].astype(F32) + ow_ref[...].astype(F32)
    cat_ref[:, 0:gw] = a.astype(BF16)
    cat_ref[:, gw:2 * gw] = ob_ref[...]
    cat_ref[:, 2 * gw:3 * gw] = of_ref[...]
    l1, l2, l3 = l1_ref[...], l2_ref[...], l3_ref[...]
    mx = jnp.maximum(jnp.maximum(l1, l2), l3)
    e1, e2, e3 = jnp.exp(l1 - mx), jnp.exp(l2 - mx), jnp.exp(l3 - mx)
    den = e1 + e2 + e3
    d = ((e1 / den) * d1_ref[...].astype(F32) + (e2 / den) * d2_ref[...].astype(F32)
         + (e3 / den) * d3_ref[...].astype(F32))
    cat_ref[:, 3 * gw:4 * gw] = d.astype(BF16)
    y = _dot(cat_ref[...], w_ref[...])
    y = y * lax.rsqrt(jnp.mean(y * y, axis=-1, keepdims=True) + EPS)
    o_ref[...] = x_ref[...] + y * g_ref[...]


def _out_proj(x2, heads, dils, lses, w_out, g, tm=256):
    m = x2.shape[0]
    gw = 4 * LANES
    row = lambda i: (i, 0)
    const = lambda i: (0, 0)
    in_specs = ([pl.BlockSpec((tm, D_MODEL), row)]
                + [pl.BlockSpec((tm, gw), row)] * 11
                + [pl.BlockSpec((D_MODEL, D_MODEL), const), pl.BlockSpec((1, D_MODEL), const)])
    return pl.pallas_call(
        _out_proj_kernel,
        out_shape=jax.ShapeDtypeStruct((m, D_MODEL), F32),
        grid=(m // tm,),
        in_specs=in_specs,
        out_specs=pl.BlockSpec((tm, D_MODEL), row),
        scratch_shapes=[pltpu.VMEM((tm, D_MODEL), BF16)],
        compiler_params=_cparams(("parallel",)),
        name="out_proj_residual",
    )(x2, *heads, *dils, *lses, w_out, g)


def _ffn_kernel(x_ref, gpre_ref, wu_ref, wd_ref, gpost_ref, o_ref, h_ref, acc_ref):
    f = pl.program_id(1)

    @pl.when(f == 0)
    def _():
        x = x_ref[...]
        y = x * lax.rsqrt(jnp.mean(x * x, axis=-1, keepdims=True) + EPS)
        h_ref[...] = (y * gpre_ref[...]).astype(BF16)
        acc_ref[...] = jnp.zeros_like(acc_ref)

    u = jnp.maximum(_dot(h_ref[...], wu_ref[...]), 0.0)
    acc_ref[...] += _dot((u * u).astype(BF16), wd_ref[...])

    @pl.when(f == pl.num_programs(1) - 1)
    def _():
        y = acc_ref[...]
        y = y * lax.rsqrt(jnp.mean(y * y, axis=-1, keepdims=True) + EPS)
        o_ref[...] = x_ref[...] + y * gpost_ref[...]


def _ffn(x2, g_pre, w_up, w_down, g_post, tm=512, tf=512):
    m = x2.shape[0]
    return pl.pallas_call(
        _ffn_kernel,
        out_shape=jax.ShapeDtypeStruct((m, D_MODEL), F32),
        grid=(m // tm, D_FF // tf),
        in_specs=[pl.BlockSpec((tm, D_MODEL), lambda i, f: (i, 0)),
                  pl.BlockSpec((1, D_MODEL), lambda i, f: (0, 0)),
                  pl.BlockSpec((D_MODEL, tf), lambda i, f: (0, f)),
                  pl.BlockSpec((tf, D_MODEL), lambda i, f: (f, 0)),
                  pl.BlockSpec((1, D_MODEL), lambda i, f: (0, 0))],
        out_specs=pl.BlockSpec((tm, D_MODEL), lambda i, f: (i, 0)),
        scratch_shapes=[pltpu.VMEM((tm, D_MODEL), BF16), pltpu.VMEM((tm, D_MODEL), F32)],
        compiler_params=_cparams(("parallel", "arbitrary")),
        name="ffn_residual",
    )(x2, g_pre, w_up, w_down, g_post)


def _split_w_in(w):
    gw, kw = 4 * LANES, LANES
    o = 0
    cols = {}
    for name, width in (("qa", gw), ("kca", kw), ("vca", kw), ("ksa", kw), ("vsa", kw), ("kwa", kw),
                        ("vwa", kw), ("ga", 24), ("qb", gw), ("kb", gw), ("vb", gw), ("qc", gw),
                        ("kc", gw), ("vc", gw), ("fc", 8), ("qd", gw), ("kd", gw), ("vd", gw)):
        cols[name] = w[:, o:o + width]
        o += width
    main = jnp.concatenate([cols[n] for n in ("qa", "ksa", "vsa", "kwa", "vwa", "qb", "kb", "vb",
                                              "qc", "kc", "vc", "qd", "kd", "vd")], axis=1)
    pad = jnp.zeros((w.shape[0], LANES - 32), w.dtype)
    small = jnp.concatenate([cols["ga"], cols["fc"], pad, cols["kca"], cols["vca"]], axis=1)
    return main.astype(BF16), small.astype(BF16)


def _compress_weights(pe, w1, w2):
    half = NSA_CMP_LEN // 2
    hid = NSA_CMP_HIDDEN
    w1r = w1.reshape(2, half, HEAD_DIM, hid)
    z = jnp.zeros_like(w1r)
    per_group = jnp.stack([jnp.concatenate([w1r, z], axis=-1), jnp.concatenate([z, w1r], axis=-1)], axis=2)
    w_halves = per_group.reshape(2, half * 2 * HEAD_DIM, 2 * hid).astype(BF16)
    pe_r = jnp.broadcast_to(pe.reshape(2, half, 1, HEAD_DIM), (2, half, 2, HEAD_DIM)).reshape(2, half * LANES)
    z2 = jnp.zeros_like(w2)
    w2g = jnp.concatenate([jnp.concatenate([w2, z2], axis=1), jnp.concatenate([z2, w2], axis=1)], axis=0)
    return pe_r, w_halves[0], w_halves[1], w2g.astype(BF16)


def _bias_tables(rel_bias, t):
    hm = 8
    tq_a, tk_a = 128, min(256, t)
    n_a = min(_n_toeplitz_tiles(tq_a, tk_a), t // tq_a)
    tab_a = _bias_tiles(rel_bias, _toeplitz_idx(n_a, tq_a, tk_a, tq_a), 0, hm)
    ncp = t // NSA_CMP_STRIDE
    qi = jnp.arange(t // tq_a, dtype=jnp.int32)[:, None, None]
    r = jnp.arange(tq_a, dtype=jnp.int32)[None, :, None]
    c = jnp.arange(ncp, dtype=jnp.int32)[None, None, :]
    idx_c = _t5_bucket(qi * tq_a + r - (c * NSA_CMP_STRIDE + NSA_CMP_LEN - 1))
    tab_c = _bias_tiles(rel_bias, idx_c, 0, hm)
    n_b = min(_n_toeplitz_tiles(MOBA_BLOCK, MOBA_BLOCK), t // MOBA_BLOCK)
    tab_b = _bias_tiles(rel_bias, _toeplitz_idx(n_b, MOBA_BLOCK, MOBA_BLOCK, MOBA_BLOCK), hm, hm)
    tabs_d = [_bias_tiles(rel_bias, _toeplitz_idx(2, DIL_BAND, DIL_BAND, DIL_BAND, dil), 2 * hm, hm)
              for _, dil in LONGNET_PATTERNS]
    return tab_a, tab_c, tab_b, tabs_d


def _overlap_t(t):
    ncp = t // NSA_CMP_STRIDE
    nsp = max(t // NSA_SEL_LEN, 8)
    c_start = jnp.arange(ncp)[None, :] * NSA_CMP_STRIDE
    s_start = jnp.arange(nsp)[:, None] * NSA_SEL_LEN
    ovl = (c_start < s_start + NSA_SEL_LEN) & (c_start + NSA_CMP_LEN - 1 >= s_start)
    ovl = ovl & (jnp.arange(ncp)[None, :] < ncp - 1) & (jnp.arange(nsp)[:, None] < t // NSA_SEL_LEN)
    return ovl.astype(F32)


def _mixer_layer(x2, b, t, w_main, w_small, g_pre, g_post, w_out, cmp_w_k, cmp_w_v, fox_bias, tabs, ovl_t):
    tab_a, tab_c, tab_b, tabs_d = tabs
    m = b * t
    p2, small, kca, vca = _proj(x2, g_pre, w_main, w_small)
    p3 = p2.reshape(b, t, P_WIDTH)
    small3 = small.reshape(b, t, LANES)
    row_w = NSA_CMP_STRIDE * LANES
    kc = _nsa_compress(kca.reshape(b, t // NSA_CMP_STRIDE, row_w), *cmp_w_k)
    vc = _nsa_compress(vca.reshape(b, t // NSA_CMP_STRIDE, row_w), *cmp_w_v)
    o_cmp, negsel = _nsa_cmp(p3, kc, vc, tab_c, ovl_t, small3)
    o_sel = _nsa_dense(p3, tab_a, small3, negsel)
    o_win = _nsa_dense(p3, tab_a, small3, None)
    o_b = _moba(p3, tab_b)
    o_f = _fox(p3, _fox_cumsum(small3, fox_bias.reshape(8, 1)))
    dil = [_dilated(p3, tab, d) for tab, (_, d) in zip(tabs_d, LONGNET_PATTERNS)]
    heads = [a.reshape(m, 4 * LANES) for a in (o_cmp, o_sel, o_win, o_b, o_f)]
    return _out_proj(x2, heads, [d[0] for d in dil], [d[1] for d in dil], w_out, g_post)


def kernel(x, w_in, w_out, g_mix_pre, g_mix_post, g_mlp_pre, g_mlp_post, w_up, w_down, cmp_pe, phik_w1, phik_w2, phiv_w1, phiv_w2, fox_bias, rel_bias):
    b, t, d = x.shape
    depth = w_in.shape[0]
    tabs = _bias_tables(rel_bias, t)
    ovl_t = _overlap_t(t)
    x2 = x.reshape(b * t, d)
    for l in range(depth):
        w_main, w_small = _split_w_in(w_in[l])
        x2 = _mixer_layer(
            x2, b, t, w_main, w_small, g_mix_pre[l][None], g_mix_post[l][None], w_out[l].astype(BF16),
            _compress_weights(cmp_pe[l], phik_w1[l], phik_w2[l]),
            _compress_weights(cmp_pe[l], phiv_w1[l], phiv_w2[l]),
            fox_bias[l], tabs, ovl_t)
        x2 = _ffn(x2, g_mlp_pre[l][None], w_up[l].astype(BF16), w_down[l].astype(BF16), g_mlp_post[l][None])
    return x2.reshape(b, t, d)
```

```python
import functools
import math

import jax
import jax.numpy as jnp
from jax import lax
from jax.experimental import pallas as pl
from jax.experimental.pallas import tpu as pltpu

F32 = jnp.float32
BF16 = jnp.bfloat16

D_MODEL = 2048
D_FF = 4 * D_MODEL
HEAD_DIM = 64
LANES = 128
EPS = 1e-6
NEG = -1e30
BIG = 1e30
SCALE = HEAD_DIM ** -0.5

N_BUCKETS = 32
MAX_EXACT = 16
MAX_DISTANCE = 4096
LAST_BUCKET_DIST = 2897

NSA_CMP_LEN = 32
NSA_CMP_STRIDE = 16
NSA_CMP_HIDDEN = 256
NSA_SEL_LEN = 64
NSA_TOP_N = 16
NSA_WINDOW = 512
MOBA_BLOCK = 256
MOBA_TOPK = 3
LONGNET_PATTERNS = ((128, 1), (512, 4), (2048, 16))
DIL_BAND = 128

P_WIDTH = 5632
PB_QA, PB_KSA, PB_VSA, PB_KWA, PB_VWA = 0, 4, 5, 6, 7
PB_QB, PB_KB, PB_VB = 8, 12, 16
PB_QC, PB_KC, PB_VC = 20, 24, 28
PB_QD, PB_KD, PB_VD = 32, 36, 40
P_BLOCKS = P_WIDTH // LANES

FOX_HI, FOX_MID, FOX_LO = 24, 32, 40

VMEM_LIMIT = 56 * 1024 * 1024


def _cparams(sem):
    return pltpu.CompilerParams(dimension_semantics=sem, vmem_limit_bytes=VMEM_LIMIT)


def _dot(a, b):
    return jnp.dot(a, b, preferred_element_type=F32)


def _dot_nt(a, b):
    return lax.dot_general(a, b, (((1,), (1,)), ((), ())), preferred_element_type=F32)


def _dot_hi(a, b):
    return jnp.dot(a, b, preferred_element_type=F32, precision=lax.Precision.HIGHEST)


def _dot_nt_hi(a, b):
    return lax.dot_general(a, b, (((1,), (1,)), ((), ())), preferred_element_type=F32,
                           precision=lax.Precision.HIGHEST)


def _t5_bucket(dist):
    dist = jnp.maximum(dist, 0)
    rel = jnp.log(jnp.maximum(dist, 1).astype(jnp.float32) / MAX_EXACT) / math.log(MAX_DISTANCE / MAX_EXACT)
    large = jnp.minimum(MAX_EXACT + (rel * (N_BUCKETS - MAX_EXACT)).astype(jnp.int32), N_BUCKETS - 1)
    return jnp.where(dist < MAX_EXACT, dist, large)


def _bias_tile_kernel(tab_ref, idx_ref, o_ref, *, head0, hpg, cols):
    grp = pl.program_id(0)
    idx = idx_ref[0]
    for hh in range(hpg):
        h = head0 + grp * hpg + hh
        acc = jnp.zeros(idx.shape, F32)
        for b in range(N_BUCKETS):
            acc = jnp.where(idx == b, tab_ref[b, h], acc)
        o_ref[0, 0, :, hh * cols:(hh + 1) * cols] = acc


def _bias_tiles(rel_bias, idx, head0, n_groups, hpg=1):
    n, r, c = idx.shape
    return pl.pallas_call(
        functools.partial(_bias_tile_kernel, head0=head0, hpg=hpg, cols=c),
        out_shape=jax.ShapeDtypeStruct((n_groups, n, r, hpg * c), F32),
        grid=(n_groups, n),
        in_specs=[pl.BlockSpec(memory_space=pltpu.SMEM),
                  pl.BlockSpec((1, r, c), lambda g, i: (i, 0, 0))],
        out_specs=pl.BlockSpec((1, 1, r, hpg * c), lambda g, i: (g, i, 0, 0)),
        compiler_params=_cparams(("parallel", "parallel")),
        name="bias_tiles",
    )(rel_bias, idx)


def _toeplitz_idx(n_tiles, rows, cols, row_stride, dist_step=1):
    nd = jnp.arange(n_tiles, dtype=jnp.int32)[:, None, None]
    r = jnp.arange(rows, dtype=jnp.int32)[None, :, None]
    c = jnp.arange(cols, dtype=jnp.int32)[None, None, :]
    return _t5_bucket((nd * row_stride + r - c) * dist_step)


def _toeplitz_idx_t(n_tiles, krows, qcols, q_stride):
    nd = jnp.arange(n_tiles, dtype=jnp.int32)[:, None, None]
    r = jnp.arange(krows, dtype=jnp.int32)[None, :, None]
    c = jnp.arange(qcols, dtype=jnp.int32)[None, None, :]
    return _t5_bucket(nd * q_stride + c - r)


def _n_toeplitz_tiles(q_stride, keys):
    return -(-(LAST_BUCKET_DIST + keys - 1) // q_stride) + 1


def _proj_kernel(x_ref, g_ref, w_ref, ws_ref, p_ref, s_ref, kc_ref, vc_ref, h_ref):
    @pl.when(pl.program_id(1) == 0)
    def _():
        x = x_ref[...]
        y = x * lax.rsqrt(jnp.mean(x * x, axis=-1, keepdims=True) + EPS)
        hb = (y * g_ref[...]).astype(BF16)
        h_ref[...] = hb
        small = _dot(hb, ws_ref[...])
        s_ref[...] = small[:, 0:LANES]
        kc_ref[...] = small[:, LANES:2 * LANES]
        vc_ref[...] = small[:, 2 * LANES:3 * LANES]

    p_ref[...] = _dot(h_ref[...], w_ref[...]).astype(BF16)


def _proj(x2, g, w_main, w_small, tm=512, tn=512):
    m = x2.shape[0]
    return pl.pallas_call(
        _proj_kernel,
        out_shape=(jax.ShapeDtypeStruct((m, P_WIDTH), BF16),
                   jax.ShapeDtypeStruct((m, LANES), F32),
                   jax.ShapeDtypeStruct((m, LANES), F32),
                   jax.ShapeDtypeStruct((m, LANES), F32)),
        grid=(m // tm, P_WIDTH // tn),
        in_specs=[pl.BlockSpec((tm, D_MODEL), lambda i, j: (i, 0)),
                  pl.BlockSpec((1, D_MODEL), lambda i, j: (0, 0)),
                  pl.BlockSpec((D_MODEL, tn), lambda i, j: (0, j)),
                  pl.BlockSpec((D_MODEL, 3 * LANES), lambda i, j: (0, 0))],
        out_specs=(pl.BlockSpec((tm, tn), lambda i, j: (i, j)),
                   pl.BlockSpec((tm, LANES), lambda i, j: (i, 0)),
                   pl.BlockSpec((tm, LANES), lambda i, j: (i, 0)),
                   pl.BlockSpec((tm, LANES), lambda i, j: (i, 0))),
        scratch_shapes=[pltpu.VMEM((tm, D_MODEL), BF16)],
        compiler_params=_cparams(("parallel", "arbitrary")),
        name="rms_in_proj",
    )(x2, g, w_main, w_small)


def _lane_lo(shape):
    return lax.broadcasted_iota(jnp.int32, shape, 1) < HEAD_DIM


def _pair_queries(q_ref):
    q = q_ref[...] * jnp.asarray(SCALE, BF16)
    lo = _lane_lo(q.shape)
    zero = jnp.zeros_like(q)
    return jnp.where(lo, q, zero), jnp.where(lo, zero, q)


def _col_softmax(s, m_ref, l_ref, first):
    if first:
        m = jnp.max(s, axis=0, keepdims=True)
        p = jnp.exp(s - m)
        m_ref[...] = m
        l_ref[...] = jnp.sum(p, axis=0, keepdims=True)
        return p.astype(BF16), None
    m_prev = m_ref[...]
    m_new = jnp.maximum(m_prev, jnp.max(s, axis=0, keepdims=True))
    alpha = jnp.exp(m_prev - m_new)
    p = jnp.exp(s - m_new)
    l_ref[...] = alpha * l_ref[...] + jnp.sum(p, axis=0, keepdims=True)
    m_ref[...] = m_new
    return p.astype(BF16), alpha


def _accumulate(acc_ref, cols, pv, alpha):
    if alpha is None:
        acc_ref[:, cols] = pv
    else:
        acc_ref[:, cols] = alpha[:, cols] * acc_ref[:, cols] + pv


def _rank_desc(v, q_lanes):
    n = v.shape[0]
    groups = n // 8
    rows = lax.broadcasted_iota(jnp.int32, (8, q_lanes), 0)
    blks = [v[8 * g:8 * g + 8, :] for g in range(groups)]
    cnt = [jnp.zeros((8, q_lanes), F32) for _ in range(groups)]
    one = jnp.ones((8, q_lanes), F32)
    zero = jnp.zeros((8, q_lanes), F32)
    for np_ in range(n):
        gk, r = divmod(np_, 8)
        row = jnp.broadcast_to(v[np_:np_ + 1, :], (8, q_lanes))
        for g in range(groups):
            if g < gk:
                beats = row > blks[g]
            elif g > gk:
                beats = row >= blks[g]
            else:
                beats = (row > blks[g]) | ((row >= blks[g]) & (rows > r))
            cnt[g] = cnt[g] + jnp.where(beats, one, zero)
    return jnp.concatenate(cnt, axis=0)


def _pair_finish(acc_ref, l_ref, o_ref, tq):
    o_t = acc_ref[...] / l_ref[...]
    o_ref[0] = jnp.concatenate([o_t[:, 0:tq], o_t[:, tq:2 * tq]], axis=0).T.astype(BF16)


def _fox_cumsum_kernel(s_ref, fb_ref, c_ref, carry_ref, *, tc):
    @pl.when(pl.program_id(1) == 0)
    def _():
        carry_ref[...] = jnp.zeros_like(carry_ref)

    lane = lax.broadcasted_iota(jnp.int32, (tc, LANES), 1)
    z = s_ref[0] + fb_ref[...]
    logf = -(jnp.maximum(-z, 0.0) + jnp.log(1.0 + jnp.exp(-jnp.abs(z))))
    logf = jnp.where((lane >= FOX_HI) & (lane < FOX_MID), logf, 0.0)
    ri = lax.broadcasted_iota(jnp.int32, (tc, tc), 0)
    ci = lax.broadcasted_iota(jnp.int32, (tc, tc), 1)
    lower = jnp.where(ri >= ci, 1.0, 0.0).astype(F32)
    c = _dot_hi(lower, logf) + carry_ref[...]
    carry_ref[...] = c[tc - 1:tc, :]
    hi = c.astype(BF16).astype(F32)
    rest = c - hi
    mid = rest.astype(BF16).astype(F32)
    lo = rest - mid
    pieces = hi + pltpu.roll(mid, FOX_MID - FOX_HI, axis=1) + pltpu.roll(lo, FOX_LO - FOX_HI, axis=1)
    c_ref[0] = pieces.astype(BF16)


def _fox_cumsum(small3, fox_bias_row, tc=512):
    b, t, _ = small3.shape
    tc = min(tc, t)
    return pl.pallas_call(
        functools.partial(_fox_cumsum_kernel, tc=tc),
        out_shape=jax.ShapeDtypeStruct((b, t, LANES), BF16),
        grid=(b, t // tc),
        in_specs=[pl.BlockSpec((1, tc, LANES), lambda i, j: (i, j, 0)),
                  pl.BlockSpec((1, LANES), lambda i, j: (0, 0))],
        out_specs=pl.BlockSpec((1, tc, LANES), lambda i, j: (i, j, 0)),
        scratch_shapes=[pltpu.VMEM((1, LANES), F32)],
        compiler_params=_cparams(("parallel", "arbitrary")),
        name="fox_cumsum",
    )(small3, fox_bias_row)


def _fox_kernel(q_ref, k_ref, c_ref, vt_ref, o_ref, kaug_ref, m_ref, l_ref, acc_ref, *, tq):
    hp = pl.program_id(1)
    qi = pl.program_id(2)

    @pl.when(qi == 0)
    def _():
        kaug_ref[:, 0:LANES] = k_ref[0]
        kaug_ref[:, LANES:2 * LANES] = c_ref[0]

    qs = _pair_queries(q_ref.at[0])
    lane = lax.broadcasted_iota(jnp.int32, (tq, LANES), 1)
    rows = []
    for hh in range(2):
        h = 2 * hp + hh
        pick = (lane == FOX_HI + h) | (lane == FOX_MID + h) | (lane == FOX_LO + h)
        rows.append(jnp.concatenate([qs[hh], jnp.where(pick, -1.0, 0.0).astype(BF16)], axis=1))
    q_aug = jnp.concatenate(rows, axis=0)
    kr = lax.broadcasted_iota(jnp.int32, (tq, 2 * tq), 0)
    qc = lax.broadcasted_iota(jnp.int32, (tq, 2 * tq), 1) & (tq - 1)
    causal = kr <= qc

    def tile(j, first):
        k0 = pl.multiple_of(j * tq, tq)
        s = _dot_nt(kaug_ref[pl.ds(k0, tq), :], q_aug)
        if first:
            s = jnp.where(causal, s, NEG)
        p, alpha = _col_softmax(s, m_ref, l_ref, first)
        for hh in range(2):
            cols = slice(hh * tq, (hh + 1) * tq)
            vt = vt_ref[0, hh * HEAD_DIM:(hh + 1) * HEAD_DIM, pl.ds(k0, tq)]
            _accumulate(acc_ref, cols, _dot(vt, p[:, cols]), alpha)

    tile(qi, True)
    lax.fori_loop(0, qi, lambda j, c: (tile(j, False), c)[1], 0)
    _pair_finish(acc_ref, l_ref, o_ref, tq)


def _fox(p3, c_aug, vt, tq=256):
    b, t, _ = p3.shape
    tq = min(tq, t)
    return pl.pallas_call(
        functools.partial(_fox_kernel, tq=tq),
        out_shape=jax.ShapeDtypeStruct((b, t, 4 * LANES), BF16),
        grid=(b, 4, t // tq),
        in_specs=[pl.BlockSpec((1, tq, LANES), lambda i, h, q: (i, q, PB_QC + h)),
                  pl.BlockSpec((1, t, LANES), lambda i, h, q: (i, 0, PB_KC + h)),
                  pl.BlockSpec((1, t, LANES), lambda i, h, q: (i, 0, 0)),
                  pl.BlockSpec((1, LANES, t), lambda i, h, q: (i, h, 0))],
        out_specs=pl.BlockSpec((1, tq, LANES), lambda i, h, q: (i, q, h)),
        scratch_shapes=[pltpu.VMEM((t, 2 * LANES), BF16),
                        pltpu.VMEM((1, 2 * tq), F32), pltpu.VMEM((1, 2 * tq), F32),
                        pltpu.VMEM((HEAD_DIM, 2 * tq), F32)],
        compiler_params=_cparams(("parallel", "parallel", "arbitrary")),
        name="fox_attention",
    )(p3, p3, c_aug, vt)


def _moba_kernel(q_ref, k_ref, vt_ref, tab_ref, o_ref, km_ref, sel_ref, m_ref, l_ref, acc_ref, *, nkb, n_tab):
    tq = MOBA_BLOCK
    qi = pl.program_id(2)

    @pl.when(qi == 0)
    def _():
        km_ref[...] = jnp.zeros_like(km_ref)
        for n in range(nkb):
            kb = k_ref[0, n * tq:(n + 1) * tq, :].astype(F32)
            km_ref[n:n + 1, :] = jnp.mean(kb, axis=0, keepdims=True)

    qs = _pair_queries(q_ref.at[0])
    q_both = jnp.concatenate([qs[0], qs[1]], axis=0)
    gate = _dot_nt_hi(km_ref[...], q_both.astype(F32))
    past = lax.broadcasted_iota(jnp.int32, gate.shape, 0) < qi
    gate = jnp.where(past, gate, NEG)
    chosen = (_rank_desc(gate, 2 * tq) < MOBA_TOPK) & past
    sel_ref[...] = jnp.where(chosen, 0.0, NEG).astype(F32)

    kr = lax.broadcasted_iota(jnp.int32, (tq, 2 * tq), 0)
    qc = lax.broadcasted_iota(jnp.int32, (tq, 2 * tq), 1) & (tq - 1)
    causal = kr <= qc

    def tile(j, first):
        k0 = pl.multiple_of(j * tq, tq)
        nd = jnp.minimum(qi - j, n_tab - 1)
        s = _dot_nt(k_ref[0, pl.ds(k0, tq), :], q_both) + tab_ref[0, nd]
        if first:
            s = jnp.where(causal, s, NEG)
        else:
            s = s + sel_ref[pl.ds(j, 1), :]
        p, alpha = _col_softmax(s, m_ref, l_ref, first)
        for hh in range(2):
            cols = slice(hh * tq, (hh + 1) * tq)
            vt = vt_ref[0, hh * HEAD_DIM:(hh + 1) * HEAD_DIM, pl.ds(k0, tq)]
            _accumulate(acc_ref, cols, _dot(vt, p[:, cols]), alpha)

    tile(qi, True)
    lax.fori_loop(0, qi, lambda j, c: (tile(j, False), c)[1], 0)
    _pair_finish(acc_ref, l_ref, o_ref, tq)


def _moba(p3, tab, vt):
    b, t, _ = p3.shape
    tq = MOBA_BLOCK
    nkb = t // tq
    nkp = -(-nkb // 8) * 8
    n_tab = tab.shape[1]
    return pl.pallas_call(
        functools.partial(_moba_kernel, nkb=nkb, n_tab=n_tab),
        out_shape=jax.ShapeDtypeStruct((b, t, 4 * LANES), BF16),
        grid=(b, 4, t // tq),
        in_specs=[pl.BlockSpec((1, tq, LANES), lambda i, h, q: (i, q, PB_QB + h)),
                  pl.BlockSpec((1, t, LANES), lambda i, h, q: (i, 0, PB_KB + h)),
                  pl.BlockSpec((1, LANES, t), lambda i, h, q: (i, h, 0)),
                  pl.BlockSpec((1, n_tab, tq, 2 * tq), lambda i, h, q: (h, 0, 0, 0))],
        out_specs=pl.BlockSpec((1, tq, LANES), lambda i, h, q: (i, q, h)),
        scratch_shapes=[pltpu.VMEM((nkp, LANES), F32), pltpu.VMEM((nkp, 2 * tq), F32),
                        pltpu.VMEM((1, 2 * tq), F32), pltpu.VMEM((1, 2 * tq), F32),
                        pltpu.VMEM((HEAD_DIM, 2 * tq), F32)],
        compiler_params=_cparams(("parallel", "parallel", "arbitrary")),
        name="moba_attention",
    )(p3, p3, vt, tab)


def _gelu_tanh(x):
    return 0.5 * x * (1.0 + jnp.tanh(math.sqrt(2.0 / math.pi) * (x + 0.044715 * (x * x * x))))


def _nsa_compress_kernel(r_ref, pe_ref, wlo_ref, whi_ref, w2_ref, o_ref, *, nr):
    r = r_ref[0]
    pe = pe_ref[...]
    a = _dot((r + pe[0:1, :]).astype(BF16), wlo_ref[...])
    b = _dot((r + pe[1:2, :]).astype(BF16), whi_ref[...])
    hid = _gelu_tanh(a + pltpu.roll(b, nr - 1, axis=0))
    o_ref[0] = _dot(hid.astype(BF16), w2_ref[...]).astype(BF16)


def _nsa_compress(r3, pe2, wlo, whi, w2):
    b, nr, w = r3.shape
    return pl.pallas_call(
        functools.partial(_nsa_compress_kernel, nr=nr),
        out_shape=jax.ShapeDtypeStruct((b, nr, LANES), BF16),
        grid=(b,),
        in_specs=[pl.BlockSpec((1, nr, w), lambda i: (i, 0, 0)),
                  pl.BlockSpec((2, w), lambda i: (0, 0)),
                  pl.BlockSpec((w, 2 * NSA_CMP_HIDDEN), lambda i: (0, 0)),
                  pl.BlockSpec((w, 2 * NSA_CMP_HIDDEN), lambda i: (0, 0)),
                  pl.BlockSpec((2 * NSA_CMP_HIDDEN, LANES), lambda i: (0, 0))],
        out_specs=pl.BlockSpec((1, nr, LANES), lambda i: (i, 0, 0)),
        compiler_params=_cparams(("parallel",)),
        name="nsa_compress",
    )(r3, pe2, wlo, whi, w2)


def _group_queries(q_ref, g):
    half = lax.broadcasted_iota(jnp.int32, (q_ref.shape[0], LANES), 1) // HEAD_DIM
    lo = half == 0
    in_group = half == g
    out = []
    for pb in range(2):
        q = q_ref[:, pb * LANES:(pb + 1) * LANES].astype(F32) * SCALE
        qr = pltpu.roll(q, HEAD_DIM, axis=1)
        for dup in (jnp.where(lo, q, qr), jnp.where(lo, qr, q)):
            out.append(jnp.where(in_group, dup, 0.0).astype(BF16))
    return jnp.concatenate(out, axis=0)


def _group_outputs(acc, g, tq):
    half = lax.broadcasted_iota(jnp.int32, (tq, LANES), 1) // HEAD_DIM
    lo = half == 0
    in_group = half == g
    dups = []
    for h in range(4):
        a = acc[h * tq:(h + 1) * tq, :]
        dups.append(jnp.where(in_group, a, pltpu.roll(a, HEAD_DIM, axis=1)))
    return jnp.concatenate([jnp.where(lo, dups[0], dups[1]), jnp.where(lo, dups[2], dups[3])], axis=1)


def _group_gates(s_ref, g, branch):
    sig = 1.0 / (1.0 + jnp.exp(-s_ref[...]))
    col = lax.broadcasted_iota(jnp.int32, (LANES, 2 * LANES), 0)
    lane = lax.broadcasted_iota(jnp.int32, (LANES, 2 * LANES), 1)
    spread = jnp.where(col == branch * 8 + g * 4 + lane // HEAD_DIM, 1.0, 0.0).astype(F32)
    return _dot_hi(sig, spread)


def _nsa_cmp_kernel(q_ref, kc_ref, vc_ref, tab_ref, ovl_ref, s_ref, o_ref, sel_ref, *, tq, ncp, nsp):
    g = pl.program_id(1)
    qi = pl.program_id(2)
    qs = _group_queries(q_ref.at[0], g)
    s = _dot_nt(qs, kc_ref[0]) + tab_ref[...].reshape(4 * tq, ncp)
    t = qi * tq + (lax.broadcasted_iota(jnp.int32, (4 * tq, ncp), 0) & (tq - 1))
    c = lax.broadcasted_iota(jnp.int32, (4 * tq, ncp), 1)
    ok = (t - (c * NSA_CMP_STRIDE + NSA_CMP_LEN - 1) >= 0) & (c < ncp - 1)
    s = jnp.where(ok, s, NEG)
    e = jnp.exp(s - jnp.max(s, axis=-1, keepdims=True))
    p = jnp.where(ok, e / jnp.sum(e, axis=-1, keepdims=True), 0.0)
    o = _dot(p.astype(BF16), vc_ref[0])
    gate = _group_gates(s_ref.at[0], g, 0)
    o_ref[0] = (_group_outputs(o, g, tq) * gate).astype(BF16)

    psum = p[0:tq] + p[tq:2 * tq] + p[2 * tq:3 * tq] + p[3 * tq:4 * tq]
    imp = _dot_nt_hi(ovl_ref[...], psum)
    n = lax.broadcasted_iota(jnp.int32, (nsp, tq), 0)
    cur = (qi * tq + lax.broadcasted_iota(jnp.int32, (nsp, tq), 1)) // NSA_SEL_LEN
    forced = (n == 0) | (n == cur) | (n == cur - 1)
    valid = n <= cur
    imp = jnp.where(forced, BIG, jnp.where(valid, imp, NEG))
    chosen = (_rank_desc(imp, tq) < NSA_TOP_N) & valid
    sel_ref[0, 0, 0] = jnp.where(chosen, 0.0, NEG).astype(F32)


def _nsa_cmp(p3, kc, vc, tab, ovl_t, small3, tq=128):
    b, t, _ = p3.shape
    ncp = kc.shape[1]
    nsp = ovl_t.shape[0]
    return pl.pallas_call(
        functools.partial(_nsa_cmp_kernel, tq=tq, ncp=ncp, nsp=nsp),
        out_shape=(jax.ShapeDtypeStruct((b, t, 4 * LANES), BF16),
                   jax.ShapeDtypeStruct((b, 2, t // tq, nsp, tq), F32)),
        grid=(b, 2, t // tq),
        in_specs=[pl.BlockSpec((1, tq, 2 * LANES), lambda i, g, q: (i, q, g)),
                  pl.BlockSpec((1, ncp, LANES), lambda i, g, q: (i, 0, 0)),
                  pl.BlockSpec((1, ncp, LANES), lambda i, g, q: (i, 0, 0)),
                  pl.BlockSpec((4, 1, tq, ncp), lambda i, g, q: (g, q, 0, 0)),
                  pl.BlockSpec((nsp, ncp), lambda i, g, q: (0, 0)),
                  pl.BlockSpec((1, tq, LANES), lambda i, g, q: (i, q, 0))],
        out_specs=(pl.BlockSpec((1, tq, 2 * LANES), lambda i, g, q: (i, q, g)),
                   pl.BlockSpec((1, 1, 1, nsp, tq), lambda i, g, q: (i, g, q, 0, 0))),
        compiler_params=_cparams(("parallel", "parallel", "parallel")),
        name="nsa_compressed",
    )(p3, kc, vc, tab, ovl_t, small3)


def _nsa_dense_kernel(*refs, tq, tk, n_tab, selected, branch):
    if selected:
        q_ref, k_ref, vt_ref, tab_ref, sel_ref, s_ref, o_ref, m_ref, l_ref, acc_ref = refs
    else:
        q_ref, k_ref, vt_ref, tab_ref, s_ref, o_ref, m_ref, l_ref, acc_ref = refs
    g = pl.program_id(1)
    qi = pl.program_id(2)
    qs = _group_queries(q_ref.at[0], g)
    kr = lax.broadcasted_iota(jnp.int32, (tk, 4 * tq), 0)
    qc = lax.broadcasted_iota(jnp.int32, (tk, 4 * tq), 1) & (tq - 1)
    j0 = (qi * tq) // tk
    per_tile = tk // NSA_SEL_LEN
    g0 = pl.multiple_of(g * HEAD_DIM, HEAD_DIM)

    def tile(j, first):
        k0 = pl.multiple_of(j * tk, tk)
        nd = jnp.minimum(qi - j * (tk // tq), n_tab - 1)
        s = _dot_nt(k_ref[0, pl.ds(k0, tk), :], qs) + tab_ref[0, nd]
        dist = (qi * tq - j * tk) + qc - kr
        if selected:
            parts = []
            for i in range(per_tile):
                row = sel_ref[0, 0, 0, pl.ds(j * per_tile + i, 1), :]
                row4 = jnp.concatenate([row] * 4, axis=1)
                parts.append(s[i * NSA_SEL_LEN:(i + 1) * NSA_SEL_LEN, :] + row4)
            s = jnp.concatenate(parts, axis=0)
            if first:
                s = jnp.where(dist >= 0, s, NEG)
        else:
            s = jnp.where((dist >= 0) & (dist <= NSA_WINDOW - 1), s, NEG)
        p, alpha = _col_softmax(s, m_ref, l_ref, first)
        vt = vt_ref[0, pl.ds(g0, HEAD_DIM), pl.ds(k0, tk)]
        _accumulate(acc_ref, slice(None), _dot(vt, p), alpha)

    tile(j0, True)
    first_j = 0 if selected else jnp.maximum(j0 - (-(-(NSA_WINDOW - 1) // tk)), 0)
    lax.fori_loop(first_j, j0, lambda j, c: (tile(j, False), c)[1], 0)
    o_t = acc_ref[...] / l_ref[...]
    o = jnp.concatenate([o_t[:, h * tq:(h + 1) * tq] for h in range(4)], axis=0).T
    o_ref[0] = (o * _group_gates(s_ref.at[0], g, branch)).astype(BF16)


def _nsa_dense(p3, tab, small3, vt, negsel, tq=128, tk=256):
    b, t, _ = p3.shape
    tk = min(tk, t)
    n_tab = tab.shape[1]
    selected = negsel is not None
    pb_k = PB_KSA if selected else PB_KWA
    in_specs = [pl.BlockSpec((1, tq, 2 * LANES), lambda i, g, q: (i, q, g)),
                pl.BlockSpec((1, t, LANES), lambda i, g, q: (i, 0, pb_k)),
                pl.BlockSpec((1, LANES, t), lambda i, g, q: (i, 0, 0)),
                pl.BlockSpec((1, n_tab, tk, 4 * tq), lambda i, g, q: (g, 0, 0, 0))]
    args = [p3, p3, vt, tab]
    if selected:
        nsp = negsel.shape[3]
        in_specs.append(pl.BlockSpec((1, 1, 1, nsp, tq), lambda i, g, q: (i, g, q, 0, 0)))
        args.append(negsel)
    in_specs.append(pl.BlockSpec((1, tq, LANES), lambda i, g, q: (i, q, 0)))
    args.append(small3)
    return pl.pallas_call(
        functools.partial(_nsa_dense_kernel, tq=tq, tk=tk, n_tab=n_tab, selected=selected,
                          branch=1 if selected else 2),
        out_shape=jax.ShapeDtypeStruct((b, t, 4 * LANES), BF16),
        grid=(b, 2, t // tq),
        in_specs=in_specs,
        out_specs=pl.BlockSpec((1, tq, 2 * LANES), lambda i, g, q: (i, q, g)),
        scratch_shapes=[pltpu.VMEM((1, 4 * tq), F32), pltpu.VMEM((1, 4 * tq), F32),
                        pltpu.VMEM((HEAD_DIM, 4 * tq), F32)],
        compiler_params=_cparams(("parallel", "parallel", "arbitrary")),
        name="nsa_selected" if selected else "nsa_window",
    )(*args)


def _softmax_rows_two(s_c, s_p):
    m = jnp.maximum(jnp.max(s_c, axis=-1, keepdims=True), jnp.max(s_p, axis=-1, keepdims=True))
    p_c = jnp.exp(s_c - m)
    p_p = jnp.exp(s_p - m)
    l = jnp.sum(p_c, axis=-1, keepdims=True) + jnp.sum(p_p, axis=-1, keepdims=True)
    return p_c, p_p, m, l


def _dilated_kernel(q_ref, kp_ref, kc_ref, vp_ref, vc_ref, tab_ref, o_ref, lse_ref, *, tq):
    qi = pl.program_id(3)
    qs = _pair_queries(q_ref.at[0])
    ri = lax.broadcasted_iota(jnp.int32, (tq, tq), 0)
    ci = lax.broadcasted_iota(jnp.int32, (tq, tq), 1)
    ok_cur = ri >= ci
    ok_prev = (ri - ci) <= jnp.where(qi > 0, 0, -tq - 1)
    outs, lses = [], []
    for hh in range(2):
        s_c = jnp.where(ok_cur, _dot_nt(qs[hh], kc_ref[0]) + tab_ref[hh, 0], NEG)
        s_p = jnp.where(ok_prev, _dot_nt(qs[hh], kp_ref[0]) + tab_ref[hh, 1], NEG)
        p_c, p_p, m, l = _softmax_rows_two(s_c, s_p)
        o = _dot(p_c.astype(BF16), vc_ref[0]) + _dot(p_p.astype(BF16), vp_ref[0])
        outs.append(o / l)
        lses.append(jnp.broadcast_to(m + jnp.log(l), (tq, LANES)))
    lo = _lane_lo((tq, LANES))
    o_ref[0] = jnp.where(lo, outs[0], outs[1]).astype(BF16)
    lse_ref[0] = jnp.where(lo, lses[0], lses[1])


def _dilated(p3, tab, dil, tq=DIL_BAND):
    b, t, _ = p3.shape
    l = t // dil
    pv = p3.reshape(b, l, dil * P_WIDTH)

    def spec(pb, prev):
        def index(i, r, h, q):
            return (i, jnp.maximum(q - 1, 0) if prev else q, r * P_BLOCKS + pb + h)
        return pl.BlockSpec((1, tq, LANES), index)

    out_spec = pl.BlockSpec((1, tq, LANES), lambda i, r, h, q: (i, q, r * 4 + h))
    o, lse = pl.pallas_call(
        functools.partial(_dilated_kernel, tq=tq),
        out_shape=(jax.ShapeDtypeStruct((b, l, dil * 4 * LANES), BF16),
                   jax.ShapeDtypeStruct((b, l, dil * 4 * LANES), F32)),
        grid=(b, dil, 4, l // tq),
        in_specs=[spec(PB_QD, False), spec(PB_KD, True), spec(PB_KD, False),
                  spec(PB_VD, True), spec(PB_VD, False),
                  pl.BlockSpec((2, 2, tq, tq), lambda i, r, h, q: (h, 0, 0, 0))],
        out_specs=(out_spec, out_spec),
        compiler_params=_cparams(("parallel", "parallel", "parallel", "parallel")),
        name="dilated_attention_%d" % dil,
    )(pv, pv, pv, pv, pv, tab)
    return o.reshape(b * t, 4 * LANES), lse.reshape(b * t, 4 * LANES)


def _out_proj_kernel(x_ref, oc_ref, os_ref, ow_ref, ob_ref, of_ref,
                     d1_ref, d2_ref, d3_ref, l1_ref, l2_ref, l3_ref, w_ref, g_ref, o_ref, cat_ref):
    gw = 4 * LANES
    a = oc_ref[...].astype(F32) + os_ref[...].astype(F32) + ow_ref[...].astype(F32)
    cat_ref[:, 0:gw] = a.astype(BF16)
    cat_ref[:, gw:2 * gw] = ob_ref[...]
    cat_ref[:, 2 * gw:3 * gw] = of_ref[...]
    l1, l2, l3 = l1_ref[...], l2_ref[...], l3_ref[...]
    mx = jnp.maximum(jnp.maximum(l1, l2), l3)
    e1, e2, e3 = jnp.exp(l1 - mx), jnp.exp(l2 - mx), jnp.exp(l3 - mx)
    den = e1 + e2 + e3
    d = ((e1 / den) * d1_ref[...].astype(F32) + (e2 / den) * d2_ref[...].astype(F32)
         + (e3 / den) * d3_ref[...].astype(F32))
    cat_ref[:, 3 * gw:4 * gw] = d.astype(BF16)
    y = _dot(cat_ref[...], w_ref[...])
    y = y * lax.rsqrt(jnp.mean(y * y, axis=-1, keepdims=True) + EPS)
    o_ref[...] = x_ref[...] + y * g_ref[...]


def _out_proj(x2, heads, dils, lses, w_out, g, tm=256):
    m = x2.shape[0]
    gw = 4 * LANES
    row = lambda i: (i, 0)
    const = lambda i: (0, 0)
    in_specs = ([pl.BlockSpec((tm, D_MODEL), row)]
                + [pl.BlockSpec((tm, gw), row)] * 11
                + [pl.BlockSpec((D_MODEL, D_MODEL), const), pl.BlockSpec((1, D_MODEL), const)])
    return pl.pallas_call(
        _out_proj_kernel,
        out_shape=jax.ShapeDtypeStruct((m, D_MODEL), F32),
        grid=(m // tm,),
        in_specs=in_specs,
        out_specs=pl.BlockSpec((tm, D_MODEL), row),
        scratch_shapes=[pltpu.VMEM((tm, D_MODEL), BF16)],
        compiler_params=_cparams(("parallel",)),
        name="out_proj_residual",
    )(x2, *heads, *dils, *lses, w_out, g)


def _ffn_kernel(x_ref, gpre_ref, wu_ref, wd_ref, gpost_ref, o_ref, h_ref, acc_ref):
    f = pl.program_id(1)

    @pl.when(f == 0)
    def _():
        x = x_ref[...]
        y = x * lax.rsqrt(jnp.mean(x * x, axis=-1, keepdims=True) + EPS)
        h_ref[...] = (y * gpre_ref[...]).astype(BF16)
        acc_ref[...] = jnp.zeros_like(acc_ref)

    u = jnp.maximum(_dot(h_ref[...], wu_ref[...]), 0.0)
    acc_ref[...] += _dot((u * u).astype(BF16), wd_ref[...])

    @pl.when(f == pl.num_programs(1) - 1)
    def _():
        y = acc_ref[...]
        y = y * lax.rsqrt(jnp.mean(y * y, axis=-1, keepdims=True) + EPS)
        o_ref[...] = x_ref[...] + y * gpost_ref[...]


def _ffn(x2, g_pre, w_up, w_down, g_post, tm=512, tf=512):
    m = x2.shape[0]
    return pl.pallas_call(
        _ffn_kernel,
        out_shape=jax.ShapeDtypeStruct((m, D_MODEL), F32),
        grid=(m // tm, D_FF // tf),
        in_specs=[pl.BlockSpec((tm, D_MODEL), lambda i, f: (i, 0)),
                  pl.BlockSpec((1, D_MODEL), lambda i, f: (0, 0)),
                  pl.BlockSpec((D_MODEL, tf), lambda i, f: (0, f)),
                  pl.BlockSpec((tf, D_MODEL), lambda i, f: (f, 0)),
                  pl.BlockSpec((1, D_MODEL), lambda i, f: (0, 0))],
        out_specs=pl.BlockSpec((tm, D_MODEL), lambda i, f: (i, 0)),
        scratch_shapes=[pltpu.VMEM((tm, D_MODEL), BF16), pltpu.VMEM((tm, D_MODEL), F32)],
        compiler_params=_cparams(("parallel", "arbitrary")),
        name="ffn_residual",
    )(x2, g_pre, w_up, w_down, g_post)


def _split_w_in(w):
    gw, kw = 4 * LANES, LANES
    o = 0
    cols = {}
    for name, width in (("qa", gw), ("kca", kw), ("vca", kw), ("ksa", kw), ("vsa", kw), ("kwa", kw),
                        ("vwa", kw), ("ga", 24), ("qb", gw), ("kb", gw), ("vb", gw), ("qc", gw),
                        ("kc", gw), ("vc", gw), ("fc", 8), ("qd", gw), ("kd", gw), ("vd", gw)):
        cols[name] = w[:, o:o + width]
        o += width
    main = jnp.concatenate([cols[n] for n in ("qa", "ksa", "vsa", "kwa", "vwa", "qb", "kb", "vb",
                                              "qc", "kc", "vc", "qd", "kd", "vd")], axis=1)
    pad = jnp.zeros((w.shape[0], LANES - 32), w.dtype)
    small = jnp.concatenate([cols["ga"], cols["fc"], pad, cols["kca"], cols["vca"]], axis=1)
    return main.astype(BF16), small.astype(BF16)


def _compress_weights(pe, w1, w2):
    half = NSA_CMP_LEN // 2
    hid = NSA_CMP_HIDDEN
    w1r = w1.reshape(2, half, HEAD_DIM, hid)
    z = jnp.zeros_like(w1r)
    per_group = jnp.stack([jnp.concatenate([w1r, z], axis=-1), jnp.concatenate([z, w1r], axis=-1)], axis=2)
    w_halves = per_group.reshape(2, half * 2 * HEAD_DIM, 2 * hid).astype(BF16)
    pe_r = jnp.broadcast_to(pe.reshape(2, half, 1, HEAD_DIM), (2, half, 2, HEAD_DIM)).reshape(2, half * LANES)
    z2 = jnp.zeros_like(w2)
    w2g = jnp.concatenate([jnp.concatenate([w2, z2], axis=1), jnp.concatenate([z2, w2], axis=1)], axis=0)
    return pe_r, w_halves[0], w_halves[1], w2g.astype(BF16)


def _bias_tables(rel_bias, t):
    hm = 8
    tq_a, tk_a = 128, min(256, t)
    n_a = min(_n_toeplitz_tiles(tq_a, tk_a), t // tq_a)
    tab_a = _bias_tiles(rel_bias, _toeplitz_idx_t(n_a, tk_a, tq_a, tq_a), 0, 2, 4)
    ncp = t // NSA_CMP_STRIDE
    qi = jnp.arange(t // tq_a, dtype=jnp.int32)[:, None, None]
    r = jnp.arange(tq_a, dtype=jnp.int32)[None, :, None]
    c = jnp.arange(ncp, dtype=jnp.int32)[None, None, :]
    idx_c = _t5_bucket(qi * tq_a + r - (c * NSA_CMP_STRIDE + NSA_CMP_LEN - 1))
    tab_c = _bias_tiles(rel_bias, idx_c, 0, hm)
    n_b = min(_n_toeplitz_tiles(MOBA_BLOCK, MOBA_BLOCK), t // MOBA_BLOCK)
    tab_b = _bias_tiles(rel_bias, _toeplitz_idx_t(n_b, MOBA_BLOCK, MOBA_BLOCK, MOBA_BLOCK), hm, 4, 2)
    tabs_d = [_bias_tiles(rel_bias, _toeplitz_idx(2, DIL_BAND, DIL_BAND, DIL_BAND, dil), 2 * hm, hm)
              for _, dil in LONGNET_PATTERNS]
    return tab_a, tab_c, tab_b, tabs_d


def _overlap_t(t):
    ncp = t // NSA_CMP_STRIDE
    nsp = max(t // NSA_SEL_LEN, 8)
    c_start = jnp.arange(ncp)[None, :] * NSA_CMP_STRIDE
    s_start = jnp.arange(nsp)[:, None] * NSA_SEL_LEN
    ovl = (c_start < s_start + NSA_SEL_LEN) & (c_start + NSA_CMP_LEN - 1 >= s_start)
    ovl = ovl & (jnp.arange(ncp)[None, :] < ncp - 1) & (jnp.arange(nsp)[:, None] < t // NSA_SEL_LEN)
    return ovl.astype(F32)


def _transposed(p3, pb, n_blocks):
    return p3[:, :, pb * LANES:(pb + n_blocks) * LANES].transpose(0, 2, 1)


def _mixer_layer(x2, b, t, w_main, w_small, g_pre, g_post, w_out, cmp_w_k, cmp_w_v, fox_bias, tabs, ovl_t):
    tab_a, tab_c, tab_b, tabs_d = tabs
    m = b * t
    p2, small, kca, vca = _proj(x2, g_pre, w_main, w_small)
    p3 = p2.reshape(b, t, P_WIDTH)
    small3 = small.reshape(b, t, LANES)
    row_w = NSA_CMP_STRIDE * LANES
    kc = _nsa_compress(kca.reshape(b, t // NSA_CMP_STRIDE, row_w), *cmp_w_k)
    vc = _nsa_compress(vca.reshape(b, t // NSA_CMP_STRIDE, row_w), *cmp_w_v)
    o_cmp, negsel = _nsa_cmp(p3, kc, vc, tab_c, ovl_t, small3)
    o_sel = _nsa_dense(p3, tab_a, small3, _transposed(p3, PB_VSA, 1), negsel)
    o_win = _nsa_dense(p3, tab_a, small3, _transposed(p3, PB_VWA, 1), None)
    o_b = _moba(p3, tab_b, _transposed(p3, PB_VB, 4))
    fox_bias_row = jnp.zeros((1, LANES), F32).at[0, FOX_HI:FOX_HI + 8].set(fox_bias)
    o_f = _fox(p3, _fox_cumsum(small3, fox_bias_row), _transposed(p3, PB_VC, 4))
    dil = [_dilated(p3, tab, d) for tab, (_, d) in zip(tabs_d, LONGNET_PATTERNS)]
    heads = [a.reshape(m, 4 * LANES) for a in (o_cmp, o_sel, o_win, o_b, o_f)]
    return _out_proj(x2, heads, [d[0] for d in dil], [d[1] for d in dil], w_out, g_post)


def kernel(x, w_in, w_out, g_mix_pre, g_mix_post, g_mlp_pre, g_mlp_post, w_up, w_down, cmp_pe, phik_w1, phik_w2, phiv_w1, phiv_w2, fox_bias, rel_bias):
    b, t, d = x.shape
    depth = w_in.shape[0]
    tabs = _bias_tables(rel_bias, t)
    ovl_t = _overlap_t(t)
    x2 = x.reshape(b * t, d)
    for l in range(depth):
        w_main, w_small = _split_w_in(w_in[l])
        x2 = _mixer_layer(
            x2, b, t, w_main, w_small, g_mix_pre[l][None], g_mix_post[l][None], w_out[l].astype(BF16),
            _compress_weights(cmp_pe[l], phik_w1[l], phik_w2[l]),
            _compress_weights(cmp_pe[l], phiv_w1[l], phiv_w2[l]),
            fox_bias[l], tabs, ovl_t)
        x2 = _ffn(x2, g_mlp_pre[l][None], w_up[l].astype(BF16), w_down[l].astype(BF16), g_mlp_post[l][None])
    return x2.reshape(b, t, d)
```

```python
import functools
import math

import jax
import jax.numpy as jnp
from jax import lax
from jax.experimental import pallas as pl
from jax.experimental.pallas import tpu as pltpu

F32 = jnp.float32
BF16 = jnp.bfloat16

D_MODEL = 2048
D_FF = 4 * D_MODEL
HEAD_DIM = 64
LANES = 128
EPS = 1e-6
NEG = -1e30
BIG = 1e30
SCALE = HEAD_DIM ** -0.5

N_BUCKETS = 32
MAX_EXACT = 16
MAX_DISTANCE = 4096
LAST_BUCKET_DIST = 2897

NSA_CMP_LEN = 32
NSA_CMP_STRIDE = 16
NSA_CMP_HIDDEN = 256
NSA_SEL_LEN = 64
NSA_TOP_N = 16
NSA_WINDOW = 512
MOBA_BLOCK = 256
MOBA_TOPK = 3
LONGNET_PATTERNS = ((128, 1), (512, 4), (2048, 16))
DIL_BAND = 128

P_WIDTH = 4096
PB_QA, PB_KSA, PB_VSA, PB_KWA, PB_VWA = 0, 4, 5, 6, 7
PB_QB, PB_KB, PB_VB = 8, 12, 16
PB_QC, PB_KC, PB_VC = 20, 24, 28
PD_WIDTH = 1536
GW = 4 * LANES

FOX_HI, FOX_MID, FOX_LO = 24, 32, 40

VMEM_LIMIT = 56 * 1024 * 1024


def _cparams(sem):
    return pltpu.CompilerParams(dimension_semantics=sem, vmem_limit_bytes=VMEM_LIMIT)


def _dot(a, b):
    return jnp.dot(a, b, preferred_element_type=F32)


def _dot_nt(a, b):
    return lax.dot_general(a, b, (((1,), (1,)), ((), ())), preferred_element_type=F32)


def _dot_hi(a, b):
    return jnp.dot(a, b, preferred_element_type=F32, precision=lax.Precision.HIGHEST)


def _dot_nt_hi(a, b):
    return lax.dot_general(a, b, (((1,), (1,)), ((), ())), preferred_element_type=F32,
                           precision=lax.Precision.HIGHEST)


def _t5_bucket(dist):
    dist = jnp.maximum(dist, 0)
    rel = jnp.log(jnp.maximum(dist, 1).astype(jnp.float32) / MAX_EXACT) / math.log(MAX_DISTANCE / MAX_EXACT)
    large = jnp.minimum(MAX_EXACT + (rel * (N_BUCKETS - MAX_EXACT)).astype(jnp.int32), N_BUCKETS - 1)
    return jnp.where(dist < MAX_EXACT, dist, large)


def _bias_tile_kernel(tab_ref, idx_ref, o_ref, *, head0, hpg, cols):
    grp = pl.program_id(0)
    idx = idx_ref[0]
    for hh in range(hpg):
        h = head0 + grp * hpg + hh
        acc = jnp.zeros(idx.shape, F32)
        for b in range(N_BUCKETS):
            acc = jnp.where(idx == b, tab_ref[b, h], acc)
        o_ref[0, 0, :, hh * cols:(hh + 1) * cols] = acc


def _bias_tiles(rel_bias, idx, head0, n_groups, hpg=1):
    n, r, c = idx.shape
    return pl.pallas_call(
        functools.partial(_bias_tile_kernel, head0=head0, hpg=hpg, cols=c),
        out_shape=jax.ShapeDtypeStruct((n_groups, n, r, hpg * c), F32),
        grid=(n_groups, n),
        in_specs=[pl.BlockSpec(memory_space=pltpu.SMEM),
                  pl.BlockSpec((1, r, c), lambda g, i: (i, 0, 0))],
        out_specs=pl.BlockSpec((1, 1, r, hpg * c), lambda g, i: (g, i, 0, 0)),
        compiler_params=_cparams(("parallel", "parallel")),
        name="bias_tiles",
    )(rel_bias, idx)


def _toeplitz_idx(n_tiles, rows, cols, row_stride, dist_step=1):
    nd = jnp.arange(n_tiles, dtype=jnp.int32)[:, None, None]
    r = jnp.arange(rows, dtype=jnp.int32)[None, :, None]
    c = jnp.arange(cols, dtype=jnp.int32)[None, None, :]
    return _t5_bucket((nd * row_stride + r - c) * dist_step)


def _toeplitz_idx_t(n_tiles, krows, qcols, q_stride):
    nd = jnp.arange(n_tiles, dtype=jnp.int32)[:, None, None]
    r = jnp.arange(krows, dtype=jnp.int32)[None, :, None]
    c = jnp.arange(qcols, dtype=jnp.int32)[None, None, :]
    return _t5_bucket(nd * q_stride + c - r)


def _n_toeplitz_tiles(q_stride, keys):
    return -(-(LAST_BUCKET_DIST + keys - 1) // q_stride) + 1


def _proj_kernel(x_ref, g_ref, w_ref, ws_ref, p_ref, pd_ref, s_ref, kc_ref, vc_ref, h_ref, *, n_main):
    j = pl.program_id(1)

    @pl.when(j == 0)
    def _():
        x = x_ref[...]
        y = x * lax.rsqrt(jnp.mean(x * x, axis=-1, keepdims=True) + EPS)
        hb = (y * g_ref[...]).astype(BF16)
        h_ref[...] = hb
        small = _dot(hb, ws_ref[...])
        s_ref[...] = small[:, 0:LANES]
        kc_ref[...] = small[:, LANES:2 * LANES]
        vc_ref[...] = small[:, 2 * LANES:3 * LANES]

    y = _dot(h_ref[...], w_ref[...]).astype(BF16)

    @pl.when(j < n_main)
    def _():
        p_ref[...] = y

    @pl.when(j >= n_main)
    def _():
        pd_ref[...] = y


def _proj(x2, g, w_main, w_small, tm=512, tn=512):
    m = x2.shape[0]
    n_main = P_WIDTH // tn
    return pl.pallas_call(
        functools.partial(_proj_kernel, n_main=n_main),
        out_shape=(jax.ShapeDtypeStruct((m, P_WIDTH), BF16),
                   jax.ShapeDtypeStruct((m, PD_WIDTH), BF16),
                   jax.ShapeDtypeStruct((m, LANES), F32),
                   jax.ShapeDtypeStruct((m, LANES), F32),
                   jax.ShapeDtypeStruct((m, LANES), F32)),
        grid=(m // tm, (P_WIDTH + PD_WIDTH) // tn),
        in_specs=[pl.BlockSpec((tm, D_MODEL), lambda i, j: (i, 0)),
                  pl.BlockSpec((1, D_MODEL), lambda i, j: (0, 0)),
                  pl.BlockSpec((D_MODEL, tn), lambda i, j: (0, j)),
                  pl.BlockSpec((D_MODEL, 3 * LANES), lambda i, j: (0, 0))],
        out_specs=(pl.BlockSpec((tm, tn), lambda i, j: (i, jnp.minimum(j, n_main - 1))),
                   pl.BlockSpec((tm, tn), lambda i, j: (i, jnp.maximum(j - n_main, 0))),
                   pl.BlockSpec((tm, LANES), lambda i, j: (i, 0)),
                   pl.BlockSpec((tm, LANES), lambda i, j: (i, 0)),
                   pl.BlockSpec((tm, LANES), lambda i, j: (i, 0))),
        scratch_shapes=[pltpu.VMEM((tm, D_MODEL), BF16)],
        compiler_params=_cparams(("parallel", "arbitrary")),
        name="rms_in_proj",
    )(x2, g, w_main, w_small)


def _lane_lo(shape):
    return lax.broadcasted_iota(jnp.int32, shape, 1) < HEAD_DIM


def _pair_queries(q_ref):
    q = q_ref[...] * jnp.asarray(SCALE, BF16)
    lo = _lane_lo(q.shape)
    zero = jnp.zeros_like(q)
    return jnp.where(lo, q, zero), jnp.where(lo, zero, q)


def _col_softmax(s, m_ref, l_ref, first):
    if first:
        m = jnp.max(s, axis=0, keepdims=True)
        p = jnp.exp(s - m)
        m_ref[...] = m
        l_ref[...] = jnp.sum(p, axis=0, keepdims=True)
        return p.astype(BF16), None
    m_prev = m_ref[...]
    m_new = jnp.maximum(m_prev, jnp.max(s, axis=0, keepdims=True))
    alpha = jnp.exp(m_prev - m_new)
    p = jnp.exp(s - m_new)
    l_ref[...] = alpha * l_ref[...] + jnp.sum(p, axis=0, keepdims=True)
    m_ref[...] = m_new
    return p.astype(BF16), alpha


def _accumulate(acc_ref, cols, pv, alpha):
    if alpha is None:
        acc_ref[:, cols] = pv
    else:
        acc_ref[:, cols] = alpha[:, cols] * acc_ref[:, cols] + pv


def _key_tiles(j_diag, n_past, qk, consume, sa_ref, sb_ref):
    last = jnp.maximum(n_past - 1, 0)
    sa_ref[...] = qk(0)
    consume(j_diag, qk(j_diag), True)

    def pair(i, carry):
        ja = 2 * i
        sb_ref[...] = qk(ja + 1)
        consume(ja, sa_ref[...], False)
        sa_ref[...] = qk(jnp.minimum(ja + 2, last))
        consume(ja + 1, sb_ref[...], False)
        return carry

    lax.fori_loop(0, n_past // 2, pair, 0)

    @pl.when(n_past % 2 == 1)
    def _():
        consume(n_past - 1, sa_ref[...], False)


def _rank_desc(v, q_lanes):
    n = v.shape[0]
    groups = n // 8
    rows = lax.broadcasted_iota(jnp.int32, (8, q_lanes), 0)
    blks = [v[8 * g:8 * g + 8, :] for g in range(groups)]
    cnt = [jnp.zeros((8, q_lanes), F32) for _ in range(groups)]
    one = jnp.ones((8, q_lanes), F32)
    zero = jnp.zeros((8, q_lanes), F32)
    for np_ in range(n):
        gk, r = divmod(np_, 8)
        row = jnp.broadcast_to(v[np_:np_ + 1, :], (8, q_lanes))
        for g in range(groups):
            if g < gk:
                beats = row > blks[g]
            elif g > gk:
                beats = row >= blks[g]
            else:
                beats = (row > blks[g]) | ((row >= blks[g]) & (rows > r))
            cnt[g] = cnt[g] + jnp.where(beats, one, zero)
    return jnp.concatenate(cnt, axis=0)


def _pair_finish(acc_ref, l_ref, o_ref, tq):
    o_t = acc_ref[...] / l_ref[...]
    o_ref[0] = jnp.concatenate([o_t[:, 0:tq], o_t[:, tq:2 * tq]], axis=0).T.astype(BF16)


def _fox_cumsum_kernel(s_ref, fb_ref, c_ref, carry_ref, *, tc):
    @pl.when(pl.program_id(1) == 0)
    def _():
        carry_ref[...] = jnp.zeros_like(carry_ref)

    lane = lax.broadcasted_iota(jnp.int32, (tc, LANES), 1)
    z = s_ref[0] + fb_ref[...]
    logf = -(jnp.maximum(-z, 0.0) + jnp.log(1.0 + jnp.exp(-jnp.abs(z))))
    logf = jnp.where((lane >= FOX_HI) & (lane < FOX_MID), logf, 0.0)
    ri = lax.broadcasted_iota(jnp.int32, (tc, tc), 0)
    ci = lax.broadcasted_iota(jnp.int32, (tc, tc), 1)
    lower = jnp.where(ri >= ci, 1.0, 0.0).astype(F32)
    c = _dot_hi(lower, logf) + carry_ref[...]
    carry_ref[...] = c[tc - 1:tc, :]
    hi = c.astype(BF16).astype(F32)
    rest = c - hi
    mid = rest.astype(BF16).astype(F32)
    lo = rest - mid
    pieces = hi + pltpu.roll(mid, FOX_MID - FOX_HI, axis=1) + pltpu.roll(lo, FOX_LO - FOX_HI, axis=1)
    c_ref[0] = pieces.astype(BF16)


def _fox_cumsum(small3, fox_bias_row, tc=512):
    b, t, _ = small3.shape
    tc = min(tc, t)
    return pl.pallas_call(
        functools.partial(_fox_cumsum_kernel, tc=tc),
        out_shape=jax.ShapeDtypeStruct((b, t, LANES), BF16),
        grid=(b, t // tc),
        in_specs=[pl.BlockSpec((1, tc, LANES), lambda i, j: (i, j, 0)),
                  pl.BlockSpec((1, LANES), lambda i, j: (0, 0))],
        out_specs=pl.BlockSpec((1, tc, LANES), lambda i, j: (i, j, 0)),
        scratch_shapes=[pltpu.VMEM((1, LANES), F32)],
        compiler_params=_cparams(("parallel", "arbitrary")),
        name="fox_cumsum",
    )(small3, fox_bias_row)


def _fox_kernel(q_ref, k_ref, c_ref, vt_ref, o_ref, kaug_ref, sa_ref, sb_ref, m_ref, l_ref, acc_ref, *, tq):
    hp = pl.program_id(1)
    qi = pl.program_id(2)

    @pl.when(qi == 0)
    def _():
        kaug_ref[:, 0:LANES] = k_ref[0]
        kaug_ref[:, LANES:2 * LANES] = c_ref[0]

    qs = _pair_queries(q_ref.at[0])
    lane = lax.broadcasted_iota(jnp.int32, (tq, LANES), 1)
    rows = []
    for hh in range(2):
        h = 2 * hp + hh
        pick = (lane == FOX_HI + h) | (lane == FOX_MID + h) | (lane == FOX_LO + h)
        rows.append(jnp.concatenate([qs[hh], jnp.where(pick, -1.0, 0.0).astype(BF16)], axis=1))
    q_aug = jnp.concatenate(rows, axis=0)
    kr = lax.broadcasted_iota(jnp.int32, (tq, 2 * tq), 0)
    qc = lax.broadcasted_iota(jnp.int32, (tq, 2 * tq), 1) & (tq - 1)
    causal = kr <= qc

    def qk(j):
        return _dot_nt(kaug_ref[pl.ds(pl.multiple_of(j * tq, tq), tq), :], q_aug)

    def consume(j, s, first):
        k0 = pl.multiple_of(j * tq, tq)
        if first:
            s = jnp.where(causal, s, NEG)
        p, alpha = _col_softmax(s, m_ref, l_ref, first)
        for hh in range(2):
            cols = slice(hh * tq, (hh + 1) * tq)
            vt = vt_ref[0, hh * HEAD_DIM:(hh + 1) * HEAD_DIM, pl.ds(k0, tq)]
            _accumulate(acc_ref, cols, _dot(vt, p[:, cols]), alpha)

    _key_tiles(qi, qi, qk, consume, sa_ref, sb_ref)
    _pair_finish(acc_ref, l_ref, o_ref, tq)


def _fox(p3, c_aug, vt, tq=256):
    b, t, _ = p3.shape
    tq = min(tq, t)
    return pl.pallas_call(
        functools.partial(_fox_kernel, tq=tq),
        out_shape=jax.ShapeDtypeStruct((b, t, 4 * LANES), BF16),
        grid=(b, 4, t // tq),
        in_specs=[pl.BlockSpec((1, tq, LANES), lambda i, h, q: (i, q, PB_QC + h)),
                  pl.BlockSpec((1, t, LANES), lambda i, h, q: (i, 0, PB_KC + h)),
                  pl.BlockSpec((1, t, LANES), lambda i, h, q: (i, 0, 0)),
                  pl.BlockSpec((1, LANES, t), lambda i, h, q: (i, h, 0))],
        out_specs=pl.BlockSpec((1, tq, LANES), lambda i, h, q: (i, q, h)),
        scratch_shapes=[pltpu.VMEM((t, 2 * LANES), BF16),
                        pltpu.VMEM((tq, 2 * tq), F32), pltpu.VMEM((tq, 2 * tq), F32),
                        pltpu.VMEM((1, 2 * tq), F32), pltpu.VMEM((1, 2 * tq), F32),
                        pltpu.VMEM((HEAD_DIM, 2 * tq), F32)],
        compiler_params=_cparams(("parallel", "parallel", "arbitrary")),
        name="fox_attention",
    )(p3, p3, c_aug, vt)


def _moba_kernel(q_ref, k_ref, vt_ref, tab_ref, o_ref, km_ref, sel_ref, sa_ref, sb_ref, m_ref, l_ref, acc_ref,
                 *, nkb, n_tab):
    tq = MOBA_BLOCK
    qi = pl.program_id(2)

    @pl.when(qi == 0)
    def _():
        km_ref[...] = jnp.zeros_like(km_ref)
        for n in range(nkb):
            kb = k_ref[0, n * tq:(n + 1) * tq, :].astype(F32)
            km_ref[n:n + 1, :] = jnp.mean(kb, axis=0, keepdims=True)

    qs = _pair_queries(q_ref.at[0])
    q_both = jnp.concatenate([qs[0], qs[1]], axis=0)
    gate = _dot_nt_hi(km_ref[...], q_both.astype(F32))
    past = lax.broadcasted_iota(jnp.int32, gate.shape, 0) < qi
    gate = jnp.where(past, gate, NEG)
    chosen = (_rank_desc(gate, 2 * tq) < MOBA_TOPK) & past
    sel_ref[...] = jnp.where(chosen, 0.0, NEG).astype(F32)

    kr = lax.broadcasted_iota(jnp.int32, (tq, 2 * tq), 0)
    qc = lax.broadcasted_iota(jnp.int32, (tq, 2 * tq), 1) & (tq - 1)
    causal = kr <= qc

    def qk(j):
        return _dot_nt(k_ref[0, pl.ds(pl.multiple_of(j * tq, tq), tq), :], q_both)

    def consume(j, s, first):
        k0 = pl.multiple_of(j * tq, tq)
        s = s + tab_ref[0, jnp.minimum(qi - j, n_tab - 1)]
        if first:
            s = jnp.where(causal, s, NEG)
        else:
            s = s + sel_ref[pl.ds(j, 1), :]
        p, alpha = _col_softmax(s, m_ref, l_ref, first)
        for hh in range(2):
            cols = slice(hh * tq, (hh + 1) * tq)
            vt = vt_ref[0, hh * HEAD_DIM:(hh + 1) * HEAD_DIM, pl.ds(k0, tq)]
            _accumulate(acc_ref, cols, _dot(vt, p[:, cols]), alpha)

    _key_tiles(qi, qi, qk, consume, sa_ref, sb_ref)
    _pair_finish(acc_ref, l_ref, o_ref, tq)


def _moba(p3, tab, vt):
    b, t, _ = p3.shape
    tq = MOBA_BLOCK
    nkb = t // tq
    nkp = -(-nkb // 8) * 8
    n_tab = tab.shape[1]
    return pl.pallas_call(
        functools.partial(_moba_kernel, nkb=nkb, n_tab=n_tab),
        out_shape=jax.ShapeDtypeStruct((b, t, 4 * LANES), BF16),
        grid=(b, 4, t // tq),
        in_specs=[pl.BlockSpec((1, tq, LANES), lambda i, h, q: (i, q, PB_QB + h)),
                  pl.BlockSpec((1, t, LANES), lambda i, h, q: (i, 0, PB_KB + h)),
                  pl.BlockSpec((1, LANES, t), lambda i, h, q: (i, h, 0)),
                  pl.BlockSpec((1, n_tab, tq, 2 * tq), lambda i, h, q: (h, 0, 0, 0))],
        out_specs=pl.BlockSpec((1, tq, LANES), lambda i, h, q: (i, q, h)),
        scratch_shapes=[pltpu.VMEM((nkp, LANES), F32), pltpu.VMEM((nkp, 2 * tq), F32),
                        pltpu.VMEM((tq, 2 * tq), F32), pltpu.VMEM((tq, 2 * tq), F32),
                        pltpu.VMEM((1, 2 * tq), F32), pltpu.VMEM((1, 2 * tq), F32),
                        pltpu.VMEM((HEAD_DIM, 2 * tq), F32)],
        compiler_params=_cparams(("parallel", "parallel", "arbitrary")),
        name="moba_attention",
    )(p3, p3, vt, tab)


def _gelu_tanh(x):
    return 0.5 * x * (1.0 + jnp.tanh(math.sqrt(2.0 / math.pi) * (x + 0.044715 * (x * x * x))))


def _nsa_compress_kernel(r_ref, pe_ref, wlo_ref, whi_ref, w2_ref, o_ref, *, nr):
    r = r_ref[0]
    pe = pe_ref[...]
    a = _dot((r + pe[0:1, :]).astype(BF16), wlo_ref[...])
    b = _dot((r + pe[1:2, :]).astype(BF16), whi_ref[...])
    hid = _gelu_tanh(a + pltpu.roll(b, nr - 1, axis=0))
    o_ref[0] = _dot(hid.astype(BF16), w2_ref[...]).astype(BF16)


def _nsa_compress(r3, pe2, wlo, whi, w2):
    b, nr, w = r3.shape
    return pl.pallas_call(
        functools.partial(_nsa_compress_kernel, nr=nr),
        out_shape=jax.ShapeDtypeStruct((b, nr, LANES), BF16),
        grid=(b,),
        in_specs=[pl.BlockSpec((1, nr, w), lambda i: (i, 0, 0)),
                  pl.BlockSpec((2, w), lambda i: (0, 0)),
                  pl.BlockSpec((w, 2 * NSA_CMP_HIDDEN), lambda i: (0, 0)),
                  pl.BlockSpec((w, 2 * NSA_CMP_HIDDEN), lambda i: (0, 0)),
                  pl.BlockSpec((2 * NSA_CMP_HIDDEN, LANES), lambda i: (0, 0))],
        out_specs=pl.BlockSpec((1, nr, LANES), lambda i: (i, 0, 0)),
        compiler_params=_cparams(("parallel",)),
        name="nsa_compress",
    )(r3, pe2, wlo, whi, w2)


def _group_queries(q_ref, g):
    half = lax.broadcasted_iota(jnp.int32, (q_ref.shape[0], LANES), 1) // HEAD_DIM
    lo = half == 0
    in_group = half == g
    out = []
    for pb in range(2):
        q = q_ref[:, pb * LANES:(pb + 1) * LANES].astype(F32) * SCALE
        qr = pltpu.roll(q, HEAD_DIM, axis=1)
        for dup in (jnp.where(lo, q, qr), jnp.where(lo, qr, q)):
            out.append(jnp.where(in_group, dup, 0.0).astype(BF16))
    return jnp.concatenate(out, axis=0)


def _group_outputs(acc, g, tq):
    half = lax.broadcasted_iota(jnp.int32, (tq, LANES), 1) // HEAD_DIM
    lo = half == 0
    in_group = half == g
    dups = []
    for h in range(4):
        a = acc[h * tq:(h + 1) * tq, :]
        dups.append(jnp.where(in_group, a, pltpu.roll(a, HEAD_DIM, axis=1)))
    return jnp.concatenate([jnp.where(lo, dups[0], dups[1]), jnp.where(lo, dups[2], dups[3])], axis=1)


def _group_gates(s_ref, g, branch):
    sig = 1.0 / (1.0 + jnp.exp(-s_ref[...]))
    col = lax.broadcasted_iota(jnp.int32, (LANES, 2 * LANES), 0)
    lane = lax.broadcasted_iota(jnp.int32, (LANES, 2 * LANES), 1)
    spread = jnp.where(col == branch * 8 + g * 4 + lane // HEAD_DIM, 1.0, 0.0).astype(F32)
    return _dot_hi(sig, spread)


def _nsa_cmp_kernel(q_ref, kc_ref, vc_ref, tab_ref, ovl_ref, s_ref, o_ref, sel_ref, *, tq, ncp, nsp):
    g = pl.program_id(1)
    qi = pl.program_id(2)
    qs = _group_queries(q_ref.at[0], g)
    s = _dot_nt(qs, kc_ref[0]) + tab_ref[...].reshape(4 * tq, ncp)
    t = qi * tq + (lax.broadcasted_iota(jnp.int32, (4 * tq, ncp), 0) & (tq - 1))
    c = lax.broadcasted_iota(jnp.int32, (4 * tq, ncp), 1)
    ok = (t - (c * NSA_CMP_STRIDE + NSA_CMP_LEN - 1) >= 0) & (c < ncp - 1)
    s = jnp.where(ok, s, NEG)
    e = jnp.exp(s - jnp.max(s, axis=-1, keepdims=True))
    p = jnp.where(ok, e / jnp.sum(e, axis=-1, keepdims=True), 0.0)
    o = _dot(p.astype(BF16), vc_ref[0])
    gate = _group_gates(s_ref.at[0], g, 0)
    o_ref[0] = (_group_outputs(o, g, tq) * gate).astype(BF16)

    psum = p[0:tq] + p[tq:2 * tq] + p[2 * tq:3 * tq] + p[3 * tq:4 * tq]
    imp = _dot_nt_hi(ovl_ref[...], psum)
    n = lax.broadcasted_iota(jnp.int32, (nsp, tq), 0)
    cur = (qi * tq + lax.broadcasted_iota(jnp.int32, (nsp, tq), 1)) // NSA_SEL_LEN
    forced = (n == 0) | (n == cur) | (n == cur - 1)
    valid = n <= cur
    imp = jnp.where(forced, BIG, jnp.where(valid, imp, NEG))
    chosen = (_rank_desc(imp, tq) < NSA_TOP_N) & valid
    sel_ref[0, 0, 0] = jnp.where(chosen, 0.0, NEG).astype(F32)


def _nsa_cmp(p3, kc, vc, tab, ovl_t, small3, tq=128):
    b, t, _ = p3.shape
    ncp = kc.shape[1]
    nsp = ovl_t.shape[0]
    return pl.pallas_call(
        functools.partial(_nsa_cmp_kernel, tq=tq, ncp=ncp, nsp=nsp),
        out_shape=(jax.ShapeDtypeStruct((b, t, 4 * LANES), BF16),
                   jax.ShapeDtypeStruct((b, 2, t // tq, nsp, tq), F32)),
        grid=(b, 2, t // tq),
        in_specs=[pl.BlockSpec((1, tq, 2 * LANES), lambda i, g, q: (i, q, g)),
                  pl.BlockSpec((1, ncp, LANES), lambda i, g, q: (i, 0, 0)),
                  pl.BlockSpec((1, ncp, LANES), lambda i, g, q: (i, 0, 0)),
                  pl.BlockSpec((4, 1, tq, ncp), lambda i, g, q: (g, q, 0, 0)),
                  pl.BlockSpec((nsp, ncp), lambda i, g, q: (0, 0)),
                  pl.BlockSpec((1, tq, LANES), lambda i, g, q: (i, q, 0))],
        out_specs=(pl.BlockSpec((1, tq, 2 * LANES), lambda i, g, q: (i, q, g)),
                   pl.BlockSpec((1, 1, 1, nsp, tq), lambda i, g, q: (i, g, q, 0, 0))),
        compiler_params=_cparams(("parallel", "parallel", "parallel")),
        name="nsa_compressed",
    )(p3, kc, vc, tab, ovl_t, small3)


def _nsa_dense_kernel(*refs, tq, tk, n_tab, selected, branch):
    if selected:
        q_ref, k_ref, vt_ref, tab_ref, sel_ref, s_ref, o_ref, sa_ref, sb_ref, m_ref, l_ref, acc_ref = refs
    else:
        q_ref, k_ref, vt_ref, tab_ref, s_ref, o_ref, m_ref, l_ref, acc_ref = refs
    g = pl.program_id(1)
    qi = pl.program_id(2)
    qs = _group_queries(q_ref.at[0], g)
    kr = lax.broadcasted_iota(jnp.int32, (tk, 4 * tq), 0)
    qc = lax.broadcasted_iota(jnp.int32, (tk, 4 * tq), 1) & (tq - 1)
    j0 = (qi * tq) // tk
    per_tile = tk // NSA_SEL_LEN
    g0 = pl.multiple_of(g * HEAD_DIM, HEAD_DIM)

    def qk(j):
        return _dot_nt(k_ref[0, pl.ds(pl.multiple_of(j * tk, tk), tk), :], qs)

    def consume(j, s, first):
        k0 = pl.multiple_of(j * tk, tk)
        s = s + tab_ref[0, jnp.minimum(qi - j * (tk // tq), n_tab - 1)]
        dist = (qi * tq - j * tk) + qc - kr
        if selected:
            parts = []
            for i in range(per_tile):
                row = sel_ref[0, 0, 0, pl.ds(j * per_tile + i, 1), :]
                row4 = jnp.concatenate([row] * 4, axis=1)
                parts.append(s[i * NSA_SEL_LEN:(i + 1) * NSA_SEL_LEN, :] + row4)
            s = jnp.concatenate(parts, axis=0)
            if first:
                s = jnp.where(dist >= 0, s, NEG)
        else:
            s = jnp.where((dist >= 0) & (dist <= NSA_WINDOW - 1), s, NEG)
        p, alpha = _col_softmax(s, m_ref, l_ref, first)
        vt = vt_ref[0, pl.ds(g0, HEAD_DIM), pl.ds(k0, tk)]
        _accumulate(acc_ref, slice(None), _dot(vt, p), alpha)

    if selected:
        _key_tiles(j0, j0, qk, consume, sa_ref, sb_ref)
    else:
        consume(j0, qk(j0), True)
        first_j = jnp.maximum(j0 - (-(-(NSA_WINDOW - 1) // tk)), 0)
        lax.fori_loop(first_j, j0, lambda j, c: (consume(j, qk(j), False), c)[1], 0)
    o_t = acc_ref[...] / l_ref[...]
    o = jnp.concatenate([o_t[:, h * tq:(h + 1) * tq] for h in range(4)], axis=0).T
    o_ref[0] = (o * _group_gates(s_ref.at[0], g, branch)).astype(BF16)


def _nsa_dense(p3, tab, small3, vt, negsel, tq=128, tk=256):
    b, t, _ = p3.shape
    tk = min(tk, t)
    n_tab = tab.shape[1]
    selected = negsel is not None
    pb_k = PB_KSA if selected else PB_KWA
    in_specs = [pl.BlockSpec((1, tq, 2 * LANES), lambda i, g, q: (i, q, g)),
                pl.BlockSpec((1, t, LANES), lambda i, g, q: (i, 0, pb_k)),
                pl.BlockSpec((1, LANES, t), lambda i, g, q: (i, 0, 0)),
                pl.BlockSpec((1, n_tab, tk, 4 * tq), lambda i, g, q: (g, 0, 0, 0))]
    args = [p3, p3, vt, tab]
    if selected:
        nsp = negsel.shape[3]
        in_specs.append(pl.BlockSpec((1, 1, 1, nsp, tq), lambda i, g, q: (i, g, q, 0, 0)))
        args.append(negsel)
    in_specs.append(pl.BlockSpec((1, tq, LANES), lambda i, g, q: (i, q, 0)))
    args.append(small3)
    return pl.pallas_call(
        functools.partial(_nsa_dense_kernel, tq=tq, tk=tk, n_tab=n_tab, selected=selected,
                          branch=1 if selected else 2),
        out_shape=jax.ShapeDtypeStruct((b, t, 4 * LANES), BF16),
        grid=(b, 2, t // tq),
        in_specs=in_specs,
        out_specs=pl.BlockSpec((1, tq, 2 * LANES), lambda i, g, q: (i, q, g)),
        scratch_shapes=([pltpu.VMEM((tk, 4 * tq), F32)] * (2 if selected else 0)
                        + [pltpu.VMEM((1, 4 * tq), F32), pltpu.VMEM((1, 4 * tq), F32),
                           pltpu.VMEM((HEAD_DIM, 4 * tq), F32)]),
        compiler_params=_cparams(("parallel", "parallel", "arbitrary")),
        name="nsa_selected" if selected else "nsa_window",
    )(*args)


def _softmax_rows_two(s_c, s_p):
    m = jnp.maximum(jnp.max(s_c, axis=-1, keepdims=True), jnp.max(s_p, axis=-1, keepdims=True))
    p_c = jnp.exp(s_c - m)
    p_p = jnp.exp(s_p - m)
    l = jnp.sum(p_c, axis=-1, keepdims=True) + jnp.sum(p_p, axis=-1, keepdims=True)
    return p_c, p_p, m, l


def _dilated_kernel(q_ref, kp_ref, kc_ref, vp_ref, vc_ref, tab_ref, o_ref, lse_ref, *, tq):
    qi = pl.program_id(2)
    ri = lax.broadcasted_iota(jnp.int32, (tq, tq), 0)
    ci = lax.broadcasted_iota(jnp.int32, (tq, tq), 1)
    ok_cur = ri >= ci
    ok_prev = (ri - ci) <= jnp.where(qi > 0, 0, -tq - 1)
    lo = _lane_lo((tq, LANES))
    for pb in range(4):
        cols = slice(pb * LANES, (pb + 1) * LANES)
        qs = _pair_queries(q_ref.at[0, :, cols])
        kc, kp, vc, vp = kc_ref[0, :, cols], kp_ref[0, :, cols], vc_ref[0, :, cols], vp_ref[0, :, cols]
        outs, lses = [], []
        for hh in range(2):
            h = 2 * pb + hh
            s_c = jnp.where(ok_cur, _dot_nt(qs[hh], kc) + tab_ref[h, 0], NEG)
            s_p = jnp.where(ok_prev, _dot_nt(qs[hh], kp) + tab_ref[h, 1], NEG)
            p_c, p_p, m, l = _softmax_rows_two(s_c, s_p)
            o = _dot(p_c.astype(BF16), vc) + _dot(p_p.astype(BF16), vp)
            outs.append(o / l)
            lses.append(jnp.broadcast_to(m + jnp.log(l), (tq, LANES)))
        o_ref[0, :, cols] = jnp.where(lo, outs[0], outs[1]).astype(BF16)
        lse_ref[0, :, cols] = jnp.where(lo, lses[0], lses[1])


def _dilated(pd3, tab, dil, tq=DIL_BAND):
    b, t, _ = pd3.shape
    l = t // dil
    pv = pd3.reshape(b, l, dil * PD_WIDTH)

    def spec(part, prev):
        def index(i, r, q):
            return (i, jnp.maximum(q - 1, 0) if prev else q, r * 3 + part)
        return pl.BlockSpec((1, tq, GW), index)

    out_spec = pl.BlockSpec((1, tq, GW), lambda i, r, q: (i, q, r))
    o, lse = pl.pallas_call(
        functools.partial(_dilated_kernel, tq=tq),
        out_shape=(jax.ShapeDtypeStruct((b, l, dil * GW), BF16),
                   jax.ShapeDtypeStruct((b, l, dil * GW), F32)),
        grid=(b, dil, l // tq),
        in_specs=[spec(0, False), spec(1, True), spec(1, False), spec(2, True), spec(2, False),
                  pl.BlockSpec((8, 2, tq, tq), lambda i, r, q: (0, 0, 0, 0))],
        out_specs=(out_spec, out_spec),
        compiler_params=_cparams(("parallel", "parallel", "parallel")),
        name="dilated_attention_%d" % dil,
    )(pv, pv, pv, pv, pv, tab)
    return o.reshape(b * t, GW), lse.reshape(b * t, GW)


def _out_proj_kernel(x_ref, oc_ref, os_ref, ow_ref, ob_ref, of_ref,
                     d1_ref, d2_ref, d3_ref, l1_ref, l2_ref, l3_ref, w_ref, g_ref, o_ref, cat_ref):
    gw = 4 * LANES
    a = oc_ref[...].astype(F32) + os_ref[...].astype(F32) + ow_ref[...].astype(F32)
    cat_ref[:, 0:gw] = a.astype(BF16)
    cat_ref[:, gw:2 * gw] = ob_ref[...]
    cat_ref[:, 2 * gw:3 * gw] = of_ref[...]
    l1, l2, l3 = l1_ref[...], l2_ref[...], l3_ref[...]
    mx = jnp.maximum(jnp.maximum(l1, l2), l3)
    e1, e2, e3 = jnp.exp(l1 - mx), jnp.exp(l2 - mx), jnp.exp(l3 - mx)
    den = e1 + e2 + e3
    d = ((e1 / den) * d1_ref[...].astype(F32) + (e2 / den) * d2_ref[...].astype(F32)
         + (e3 / den) * d3_ref[...].astype(F32))
    cat_ref[:, 3 * gw:4 * gw] = d.astype(BF16)
    y = _dot(cat_ref[...], w_ref[...])
    y = y * lax.rsqrt(jnp.mean(y * y, axis=-1, keepdims=True) + EPS)
    o_ref[...] = x_ref[...] + y * g_ref[...]


def _out_proj(x2, heads, dils, lses, w_out, g, tm=256):
    m = x2.shape[0]
    gw = 4 * LANES
    row = lambda i: (i, 0)
    const = lambda i: (0, 0)
    in_specs = ([pl.BlockSpec((tm, D_MODEL), row)]
                + [pl.BlockSpec((tm, gw), row)] * 11
                + [pl.BlockSpec((D_MODEL, D_MODEL), const), pl.BlockSpec((1, D_MODEL), const)])
    return pl.pallas_call(
        _out_proj_kernel,
        out_shape=jax.ShapeDtypeStruct((m, D_MODEL), F32),
        grid=(m // tm,),
        in_specs=in_specs,
        out_specs=pl.BlockSpec((tm, D_MODEL), row),
        scratch_shapes=[pltpu.VMEM((tm, D_MODEL), BF16)],
        compiler_params=_cparams(("parallel",)),
        name="out_proj_residual",
    )(x2, *heads, *dils, *lses, w_out, g)


def _ffn_kernel(x_ref, gpre_ref, wu_ref, wd_ref, gpost_ref, o_ref, h_ref, acc_ref):
    f = pl.program_id(1)

    @pl.when(f == 0)
    def _():
        x = x_ref[...]
        y = x * lax.rsqrt(jnp.mean(x * x, axis=-1, keepdims=True) + EPS)
        h_ref[...] = (y * gpre_ref[...]).astype(BF16)
        acc_ref[...] = jnp.zeros_like(acc_ref)

    u = jnp.maximum(_dot(h_ref[...], wu_ref[...]), 0.0)
    acc_ref[...] += _dot((u * u).astype(BF16), wd_ref[...])

    @pl.when(f == pl.num_programs(1) - 1)
    def _():
        y = acc_ref[...]
        y = y * lax.rsqrt(jnp.mean(y * y, axis=-1, keepdims=True) + EPS)
        o_ref[...] = x_ref[...] + y * gpost_ref[...]


def _ffn(x2, g_pre, w_up, w_down, g_post, tm=512, tf=512):
    m = x2.shape[0]
    return pl.pallas_call(
        _ffn_kernel,
        out_shape=jax.ShapeDtypeStruct((m, D_MODEL), F32),
        grid=(m // tm, D_FF // tf),
        in_specs=[pl.BlockSpec((tm, D_MODEL), lambda i, f: (i, 0)),
                  pl.BlockSpec((1, D_MODEL), lambda i, f: (0, 0)),
                  pl.BlockSpec((D_MODEL, tf), lambda i, f: (0, f)),
                  pl.BlockSpec((tf, D_MODEL), lambda i, f: (f, 0)),
                  pl.BlockSpec((1, D_MODEL), lambda i, f: (0, 0))],
        out_specs=pl.BlockSpec((tm, D_MODEL), lambda i, f: (i, 0)),
        scratch_shapes=[pltpu.VMEM((tm, D_MODEL), BF16), pltpu.VMEM((tm, D_MODEL), F32)],
        compiler_params=_cparams(("parallel", "arbitrary")),
        name="ffn_residual",
    )(x2, g_pre, w_up, w_down, g_post)


def _split_w_in(w):
    gw, kw = 4 * LANES, LANES
    o = 0
    cols = {}
    for name, width in (("qa", gw), ("kca", kw), ("vca", kw), ("ksa", kw), ("vsa", kw), ("kwa", kw),
                        ("vwa", kw), ("ga", 24), ("qb", gw), ("kb", gw), ("vb", gw), ("qc", gw),
                        ("kc", gw), ("vc", gw), ("fc", 8), ("qd", gw), ("kd", gw), ("vd", gw)):
        cols[name] = w[:, o:o + width]
        o += width
    main = jnp.concatenate([cols[n] for n in ("qa", "ksa", "vsa", "kwa", "vwa", "qb", "kb", "vb",
                                              "qc", "kc", "vc", "qd", "kd", "vd")], axis=1)
    pad = jnp.zeros((w.shape[0], LANES - 32), w.dtype)
    small = jnp.concatenate([cols["ga"], cols["fc"], pad, cols["kca"], cols["vca"]], axis=1)
    return main.astype(BF16), small.astype(BF16)


def _compress_weights(pe, w1, w2):
    half = NSA_CMP_LEN // 2
    hid = NSA_CMP_HIDDEN
    w1r = w1.reshape(2, half, HEAD_DIM, hid)
    z = jnp.zeros_like(w1r)
    per_group = jnp.stack([jnp.concatenate([w1r, z], axis=-1), jnp.concatenate([z, w1r], axis=-1)], axis=2)
    w_halves = per_group.reshape(2, half * 2 * HEAD_DIM, 2 * hid).astype(BF16)
    pe_r = jnp.broadcast_to(pe.reshape(2, half, 1, HEAD_DIM), (2, half, 2, HEAD_DIM)).reshape(2, half * LANES)
    z2 = jnp.zeros_like(w2)
    w2g = jnp.concatenate([jnp.concatenate([w2, z2], axis=1), jnp.concatenate([z2, w2], axis=1)], axis=0)
    return pe_r, w_halves[0], w_halves[1], w2g.astype(BF16)


def _bias_tables(rel_bias, t):
    hm = 8
    tq_a, tk_a = 128, min(256, t)
    n_a = min(_n_toeplitz_tiles(tq_a, tk_a), t // tq_a)
    tab_a = _bias_tiles(rel_bias, _toeplitz_idx_t(n_a, tk_a, tq_a, tq_a), 0, 2, 4)
    ncp = t // NSA_CMP_STRIDE
    qi = jnp.arange(t // tq_a, dtype=jnp.int32)[:, None, None]
    r = jnp.arange(tq_a, dtype=jnp.int32)[None, :, None]
    c = jnp.arange(ncp, dtype=jnp.int32)[None, None, :]
    idx_c = _t5_bucket(qi * tq_a + r - (c * NSA_CMP_STRIDE + NSA_CMP_LEN - 1))
    tab_c = _bias_tiles(rel_bias, idx_c, 0, hm)
    n_b = min(_n_toeplitz_tiles(MOBA_BLOCK, MOBA_BLOCK), t // MOBA_BLOCK)
    tab_b = _bias_tiles(rel_bias, _toeplitz_idx_t(n_b, MOBA_BLOCK, MOBA_BLOCK, MOBA_BLOCK), hm, 4, 2)
    tabs_d = [_bias_tiles(rel_bias, _toeplitz_idx(2, DIL_BAND, DIL_BAND, DIL_BAND, dil), 2 * hm, hm)
              for _, dil in LONGNET_PATTERNS]
    return tab_a, tab_c, tab_b, tabs_d


def _overlap_t(t):
    ncp = t // NSA_CMP_STRIDE
    nsp = max(t // NSA_SEL_LEN, 8)
    c_start = jnp.arange(ncp)[None, :] * NSA_CMP_STRIDE
    s_start = jnp.arange(nsp)[:, None] * NSA_SEL_LEN
    ovl = (c_start < s_start + NSA_SEL_LEN) & (c_start + NSA_CMP_LEN - 1 >= s_start)
    ovl = ovl & (jnp.arange(ncp)[None, :] < ncp - 1) & (jnp.arange(nsp)[:, None] < t // NSA_SEL_LEN)
    return ovl.astype(F32)


def _transposed(p3, pb, n_blocks):
    return p3[:, :, pb * LANES:(pb + n_blocks) * LANES].transpose(0, 2, 1)


def _mixer_layer(x2, b, t, w_main, w_small, g_pre, g_post, w_out, cmp_w_k, cmp_w_v, fox_bias, tabs, ovl_t):
    tab_a, tab_c, tab_b, tabs_d = tabs
    m = b * t
    p2, pd2, small, kca, vca = _proj(x2, g_pre, w_main, w_small)
    p3 = p2.reshape(b, t, P_WIDTH)
    pd3 = pd2.reshape(b, t, PD_WIDTH)
    small3 = small.reshape(b, t, LANES)
    row_w = NSA_CMP_STRIDE * LANES
    kc = _nsa_compress(kca.reshape(b, t // NSA_CMP_STRIDE, row_w), *cmp_w_k)
    vc = _nsa_compress(vca.reshape(b, t // NSA_CMP_STRIDE, row_w), *cmp_w_v)
    o_cmp, negsel = _nsa_cmp(p3, kc, vc, tab_c, ovl_t, small3)
    o_sel = _nsa_dense(p3, tab_a, small3, _transposed(p3, PB_VSA, 1), negsel)
    o_win = _nsa_dense(p3, tab_a, small3, _transposed(p3, PB_VWA, 1), None)
    o_b = _moba(p3, tab_b, _transposed(p3, PB_VB, 4))
    fox_bias_row = jnp.zeros((1, LANES), F32).at[0, FOX_HI:FOX_HI + 8].set(fox_bias)
    o_f = _fox(p3, _fox_cumsum(small3, fox_bias_row), _transposed(p3, PB_VC, 4))
    dil = [_dilated(pd3, tab, d) for tab, (_, d) in zip(tabs_d, LONGNET_PATTERNS)]
    heads = [a.reshape(m, 4 * LANES) for a in (o_cmp, o_sel, o_win, o_b, o_f)]
    return _out_proj(x2, heads, [d[0] for d in dil], [d[1] for d in dil], w_out, g_post)


def kernel(x, w_in, w_out, g_mix_pre, g_mix_post, g_mlp_pre, g_mlp_post, w_up, w_down, cmp_pe, phik_w1, phik_w2, phiv_w1, phiv_w2, fox_bias, rel_bias):
    b, t, d = x.shape
    depth = w_in.shape[0]
    tabs = _bias_tables(rel_bias, t)
    ovl_t = _overlap_t(t)
    x2 = x.reshape(b * t, d)
    for l in range(depth):
        w_main, w_small = _split_w_in(w_in[l])
        x2 = _mixer_layer(
            x2, b, t, w_main, w_small, g_mix_pre[l][None], g_mix_post[l][None], w_out[l].astype(BF16),
            _compress_weights(cmp_pe[l], phik_w1[l], phik_w2[l]),
            _compress_weights(cmp_pe[l], phiv_w1[l], phiv_w2[l]),
            fox_bias[l], tabs, ovl_t)
        x2 = _ffn(x2, g_mlp_pre[l][None], w_up[l].astype(BF16), w_down[l].astype(BF16), g_mlp_post[l][None])
    return x2.reshape(b, t, d)
```

```python
import functools
import math

import jax
import jax.numpy as jnp
from jax import lax
from jax.experimental import pallas as pl
from jax.experimental.pallas import tpu as pltpu

F32 = jnp.float32
BF16 = jnp.bfloat16

D_MODEL = 2048
D_FF = 4 * D_MODEL
HEAD_DIM = 64
LANES = 128
EPS = 1e-6
NEG = -1e30
BIG = 1e30
SCALE = HEAD_DIM ** -0.5
LOG2E = math.log2(math.e)
LN2 = math.log(2.0)
Q_FOLD = SCALE * LOG2E

N_BUCKETS = 32
MAX_EXACT = 16
MAX_DISTANCE = 4096
LAST_BUCKET_DIST = 2897

NSA_CMP_LEN = 32
NSA_CMP_STRIDE = 16
NSA_CMP_HIDDEN = 256
NSA_SEL_LEN = 64
NSA_TOP_N = 16
NSA_WINDOW = 512
MOBA_BLOCK = 256
MOBA_TOPK = 3
LONGNET_PATTERNS = ((128, 1), (512, 4), (2048, 16))
DIL_BAND = 128

P_WIDTH = 4096
PB_QA, PB_KSA, PB_VSA, PB_KWA, PB_VWA = 0, 4, 5, 6, 7
PB_QB, PB_KB, PB_VB = 8, 12, 16
PB_QC, PB_KC, PB_VC = 20, 24, 28
PD_WIDTH = 1536
GW = 4 * LANES

FOX_HI, FOX_MID, FOX_LO = 24, 32, 40

VMEM_LIMIT = 56 * 1024 * 1024


def _cparams(sem):
    return pltpu.CompilerParams(dimension_semantics=sem, vmem_limit_bytes=VMEM_LIMIT)


def _dot(a, b):
    return jnp.dot(a, b, preferred_element_type=F32)


def _dot_nt(a, b):
    return lax.dot_general(a, b, (((1,), (1,)), ((), ())), preferred_element_type=F32)


def _dot_hi(a, b):
    return jnp.dot(a, b, preferred_element_type=F32, precision=lax.Precision.HIGHEST)


def _dot_nt_hi(a, b):
    return lax.dot_general(a, b, (((1,), (1,)), ((), ())), preferred_element_type=F32,
                           precision=lax.Precision.HIGHEST)


def _t5_bucket(dist):
    dist = jnp.maximum(dist, 0)
    rel = jnp.log(jnp.maximum(dist, 1).astype(jnp.float32) / MAX_EXACT) / math.log(MAX_DISTANCE / MAX_EXACT)
    large = jnp.minimum(MAX_EXACT + (rel * (N_BUCKETS - MAX_EXACT)).astype(jnp.int32), N_BUCKETS - 1)
    return jnp.where(dist < MAX_EXACT, dist, large)


def _bias_tile_kernel(tab_ref, idx_ref, o_ref, *, head0, hpg, cols):
    grp = pl.program_id(0)
    idx = idx_ref[0]
    for hh in range(hpg):
        h = head0 + grp * hpg + hh
        acc = jnp.zeros(idx.shape, F32)
        for b in range(N_BUCKETS):
            acc = jnp.where(idx == b, tab_ref[b, h], acc)
        o_ref[0, 0, :, hh * cols:(hh + 1) * cols] = acc * LOG2E


def _bias_tiles(rel_bias, idx, head0, n_groups, hpg=1):
    n, r, c = idx.shape
    return pl.pallas_call(
        functools.partial(_bias_tile_kernel, head0=head0, hpg=hpg, cols=c),
        out_shape=jax.ShapeDtypeStruct((n_groups, n, r, hpg * c), F32),
        grid=(n_groups, n),
        in_specs=[pl.BlockSpec(memory_space=pltpu.SMEM),
                  pl.BlockSpec((1, r, c), lambda g, i: (i, 0, 0))],
        out_specs=pl.BlockSpec((1, 1, r, hpg * c), lambda g, i: (g, i, 0, 0)),
        compiler_params=_cparams(("parallel", "parallel")),
        name="bias_tiles",
    )(rel_bias, idx)


def _toeplitz_idx(n_tiles, rows, cols, row_stride, dist_step=1):
    nd = jnp.arange(n_tiles, dtype=jnp.int32)[:, None, None]
    r = jnp.arange(rows, dtype=jnp.int32)[None, :, None]
    c = jnp.arange(cols, dtype=jnp.int32)[None, None, :]
    return _t5_bucket((nd * row_stride + r - c) * dist_step)


def _toeplitz_idx_t(n_tiles, krows, qcols, q_stride):
    nd = jnp.arange(n_tiles, dtype=jnp.int32)[:, None, None]
    r = jnp.arange(krows, dtype=jnp.int32)[None, :, None]
    c = jnp.arange(qcols, dtype=jnp.int32)[None, None, :]
    return _t5_bucket(nd * q_stride + c - r)


def _n_toeplitz_tiles(q_stride, keys):
    return -(-(LAST_BUCKET_DIST + keys - 1) // q_stride) + 1


def _proj_kernel(x_ref, g_ref, w_ref, ws_ref, p_ref, pd_ref, s_ref, kc_ref, vc_ref, h_ref, *, n_main):
    j = pl.program_id(1)

    @pl.when(j == 0)
    def _():
        x = x_ref[...]
        y = x * lax.rsqrt(jnp.mean(x * x, axis=-1, keepdims=True) + EPS)
        hb = (y * g_ref[...]).astype(BF16)
        h_ref[...] = hb
        small = _dot(hb, ws_ref[...])
        s_ref[...] = small[:, 0:LANES]
        kc_ref[...] = small[:, LANES:2 * LANES]
        vc_ref[...] = small[:, 2 * LANES:3 * LANES]

    y = _dot(h_ref[...], w_ref[...]).astype(BF16)

    @pl.when(j < n_main)
    def _():
        p_ref[...] = y

    @pl.when(j >= n_main)
    def _():
        pd_ref[...] = y


def _proj(x2, g, w_main, w_small, tm=1024, tn=512):
    m = x2.shape[0]
    tm = min(tm, m)
    n_main = P_WIDTH // tn
    return pl.pallas_call(
        functools.partial(_proj_kernel, n_main=n_main),
        out_shape=(jax.ShapeDtypeStruct((m, P_WIDTH), BF16),
                   jax.ShapeDtypeStruct((m, PD_WIDTH), BF16),
                   jax.ShapeDtypeStruct((m, LANES), F32),
                   jax.ShapeDtypeStruct((m, LANES), F32),
                   jax.ShapeDtypeStruct((m, LANES), F32)),
        grid=(m // tm, (P_WIDTH + PD_WIDTH) // tn),
        in_specs=[pl.BlockSpec((tm, D_MODEL), lambda i, j: (i, 0)),
                  pl.BlockSpec((1, D_MODEL), lambda i, j: (0, 0)),
                  pl.BlockSpec((D_MODEL, tn), lambda i, j: (0, j)),
                  pl.BlockSpec((D_MODEL, 3 * LANES), lambda i, j: (0, 0))],
        out_specs=(pl.BlockSpec((tm, tn), lambda i, j: (i, jnp.minimum(j, n_main - 1))),
                   pl.BlockSpec((tm, tn), lambda i, j: (i, jnp.maximum(j - n_main, 0))),
                   pl.BlockSpec((tm, LANES), lambda i, j: (i, 0)),
                   pl.BlockSpec((tm, LANES), lambda i, j: (i, 0)),
                   pl.BlockSpec((tm, LANES), lambda i, j: (i, 0))),
        scratch_shapes=[pltpu.VMEM((tm, D_MODEL), BF16)],
        compiler_params=_cparams(("parallel", "arbitrary")),
        name="rms_in_proj",
    )(x2, g, w_main, w_small)


def _lane_lo(shape):
    return lax.broadcasted_iota(jnp.int32, shape, 1) < HEAD_DIM


def _pair_queries(q_ref):
    q = q_ref[...]
    lo = _lane_lo(q.shape)
    zero = jnp.zeros_like(q)
    return jnp.where(lo, q, zero), jnp.where(lo, zero, q)


def _col_softmax(s, m_ref, l_ref, first):
    if first:
        m = jnp.max(s, axis=0, keepdims=True)
        p = jnp.exp2(s - m)
        m_ref[...] = m
        l_ref[...] = jnp.sum(p, axis=0, keepdims=True)
        return p.astype(BF16), None
    m_prev = m_ref[...]
    m_new = jnp.maximum(m_prev, jnp.max(s, axis=0, keepdims=True))
    alpha = jnp.exp2(m_prev - m_new)
    p = jnp.exp2(s - m_new)
    l_ref[...] = alpha * l_ref[...] + jnp.sum(p, axis=0, keepdims=True)
    m_ref[...] = m_new
    return p.astype(BF16), alpha


def _accumulate(acc_ref, cols, pv, alpha):
    if alpha is None:
        acc_ref[:, cols] = pv
    else:
        acc_ref[:, cols] = alpha[:, cols] * acc_ref[:, cols] + pv


def _key_tiles(j_diag, j_first, n_past, qk, consume, sa_ref, sb_ref, qk_diag=None):
    last = j_first + jnp.maximum(n_past - 1, 0)
    sa_ref[...] = qk(j_first)
    consume(j_diag, (qk_diag or qk)(j_diag), True)

    def run(base, count):
        for t in range(count):
            cur, nxt = (sa_ref, sb_ref) if t % 2 == 0 else (sb_ref, sa_ref)
            nxt[...] = qk(jnp.minimum(base + t + 1, last))
            consume(base + t, cur[...], False)

    quads = n_past // 4
    lax.fori_loop(0, quads, lambda i, c: (run(j_first + 4 * i, 4), c)[1], 0)
    rest = n_past - 4 * quads

    @pl.when(rest >= 2)
    def _():
        run(j_first + 4 * quads, 2)

    @pl.when(rest % 2 == 1)
    def _():
        consume(j_first + n_past - 1, sa_ref[...], False)


def _top_rows(v, k):
    n = v.shape[0]
    rows = lax.broadcasted_iota(jnp.int32, v.shape, 0)
    chosen = jnp.zeros(v.shape, F32)
    for _ in range(k):
        top = jnp.max(v, axis=0, keepdims=True)
        first = jnp.min(jnp.where(v == top, rows, n), axis=0, keepdims=True)
        pick = rows == first
        chosen = jnp.where(pick, 1.0, chosen)
        v = jnp.where(pick, -jnp.inf, v)
    return chosen


def _rank_desc(v, q_lanes):
    n = v.shape[0]
    groups = n // 8
    rows = lax.broadcasted_iota(jnp.int32, (8, q_lanes), 0)
    blks = [v[8 * g:8 * g + 8, :] for g in range(groups)]
    cnt = [jnp.zeros((8, q_lanes), F32) for _ in range(groups)]
    one = jnp.ones((8, q_lanes), F32)
    zero = jnp.zeros((8, q_lanes), F32)
    for np_ in range(n):
        gk, r = divmod(np_, 8)
        row = jnp.broadcast_to(v[np_:np_ + 1, :], (8, q_lanes))
        for g in range(groups):
            if g < gk:
                beats = row > blks[g]
            elif g > gk:
                beats = row >= blks[g]
            else:
                beats = (row > blks[g]) | ((row >= blks[g]) & (rows > r))
            cnt[g] = cnt[g] + jnp.where(beats, one, zero)
    return jnp.concatenate(cnt, axis=0)


def _pair_finish(acc_ref, l_ref, o_ref, tq):
    o_t = acc_ref[...] / l_ref[...]
    o_ref[0] = jnp.concatenate([o_t[:, 0:tq], o_t[:, tq:2 * tq]], axis=0).T.astype(BF16)


def _fox_cumsum_kernel(s_ref, fb_ref, c_ref, carry_ref, *, tc):
    @pl.when(pl.program_id(1) == 0)
    def _():
        carry_ref[...] = jnp.zeros_like(carry_ref)

    lane = lax.broadcasted_iota(jnp.int32, (tc, LANES), 1)
    z = s_ref[0] + fb_ref[...]
    logf = -(jnp.maximum(-z, 0.0) + jnp.log(1.0 + jnp.exp(-jnp.abs(z))))
    logf = jnp.where((lane >= FOX_HI) & (lane < FOX_MID), logf, 0.0)
    ri = lax.broadcasted_iota(jnp.int32, (tc, tc), 0)
    ci = lax.broadcasted_iota(jnp.int32, (tc, tc), 1)
    lower = jnp.where(ri >= ci, 1.0, 0.0).astype(F32)
    c = _dot_hi(lower, logf) + carry_ref[...]
    carry_ref[...] = c[tc - 1:tc, :]
    c2 = c * LOG2E
    hi = c2.astype(BF16).astype(F32)
    rest = c2 - hi
    mid = rest.astype(BF16).astype(F32)
    lo = rest - mid
    pieces = hi + pltpu.roll(mid, FOX_MID - FOX_HI, axis=1) + pltpu.roll(lo, FOX_LO - FOX_HI, axis=1)
    c_ref[0] = pieces.astype(BF16)


def _fox_cumsum(small3, fox_bias_row, tc=512):
    b, t, _ = small3.shape
    tc = min(tc, t)
    return pl.pallas_call(
        functools.partial(_fox_cumsum_kernel, tc=tc),
        out_shape=jax.ShapeDtypeStruct((b, t, LANES), BF16),
        grid=(b, t // tc),
        in_specs=[pl.BlockSpec((1, tc, LANES), lambda i, j: (i, j, 0)),
                  pl.BlockSpec((1, LANES), lambda i, j: (0, 0))],
        out_specs=pl.BlockSpec((1, tc, LANES), lambda i, j: (i, j, 0)),
        scratch_shapes=[pltpu.VMEM((1, LANES), F32)],
        compiler_params=_cparams(("parallel", "arbitrary")),
        name="fox_cumsum",
    )(small3, fox_bias_row)


def _fox_kernel(q_ref, k_ref, c_ref, vt_ref, o_ref, kaug_ref, sa_ref, sb_ref, m_ref, l_ref, acc_ref, *, tq):
    hp = pl.program_id(1)
    qi = pl.program_id(2)

    @pl.when(qi == 0)
    def _():
        kaug_ref[:, 0:LANES] = k_ref[0]
        kaug_ref[:, LANES:2 * LANES] = c_ref[0]

    qs = _pair_queries(q_ref.at[0])
    lane = lax.broadcasted_iota(jnp.int32, (tq, LANES), 1)
    rows = []
    for hh in range(2):
        h = 2 * hp + hh
        pick = (lane == FOX_HI + h) | (lane == FOX_MID + h) | (lane == FOX_LO + h)
        rows.append(jnp.concatenate([qs[hh], jnp.where(pick, -1.0, 0.0).astype(BF16)], axis=1))
    q_aug = jnp.concatenate(rows, axis=0)
    kr = lax.broadcasted_iota(jnp.int32, (tq, 2 * tq), 0)
    qc = lax.broadcasted_iota(jnp.int32, (tq, 2 * tq), 1) & (tq - 1)
    causal = kr <= qc

    def qk(j):
        return _dot_nt(kaug_ref[pl.ds(pl.multiple_of(j * tq, tq), tq), :], q_aug)

    def consume(j, s, first):
        k0 = pl.multiple_of(j * tq, tq)
        if first:
            s = jnp.where(causal, s, NEG)
        p, alpha = _col_softmax(s, m_ref, l_ref, first)
        for hh in range(2):
            cols = slice(hh * tq, (hh + 1) * tq)
            vt = vt_ref[0, hh * HEAD_DIM:(hh + 1) * HEAD_DIM, pl.ds(k0, tq)]
            _accumulate(acc_ref, cols, _dot(vt, p[:, cols]), alpha)

    _key_tiles(qi, 0, qi, qk, consume, sa_ref, sb_ref)
    _pair_finish(acc_ref, l_ref, o_ref, tq)


def _fox(p3, c_aug, vt, tq=256):
    b, t, _ = p3.shape
    tq = min(tq, t)
    return pl.pallas_call(
        functools.partial(_fox_kernel, tq=tq),
        out_shape=jax.ShapeDtypeStruct((b, t, 4 * LANES), BF16),
        grid=(b, 4, t // tq),
        in_specs=[pl.BlockSpec((1, tq, LANES), lambda i, h, q: (i, q, PB_QC + h)),
                  pl.BlockSpec((1, t, LANES), lambda i, h, q: (i, 0, PB_KC + h)),
                  pl.BlockSpec((1, t, LANES), lambda i, h, q: (i, 0, 0)),
                  pl.BlockSpec((1, LANES, t), lambda i, h, q: (i, h, 0))],
        out_specs=pl.BlockSpec((1, tq, LANES), lambda i, h, q: (i, q, h)),
        scratch_shapes=[pltpu.VMEM((t, 2 * LANES), BF16),
                        pltpu.VMEM((tq, 2 * tq), F32), pltpu.VMEM((tq, 2 * tq), F32),
                        pltpu.VMEM((1, 2 * tq), F32), pltpu.VMEM((1, 2 * tq), F32),
                        pltpu.VMEM((HEAD_DIM, 2 * tq), F32)],
        compiler_params=_cparams(("parallel", "parallel", "arbitrary")),
        name="fox_attention",
    )(p3, p3, c_aug, vt)


def _moba_kernel(q_ref, k_ref, vt_ref, tab_ref, o_ref, km_ref, kaug_ref, sa_ref, sb_ref, m_ref, l_ref, acc_ref,
                 *, nkb, nkp, n_tab):
    tq = MOBA_BLOCK
    qi = pl.program_id(2)

    @pl.when(qi == 0)
    def _():
        km_ref[...] = jnp.zeros_like(km_ref)
        lane = lax.broadcasted_iota(jnp.int32, (tq, LANES), 1)
        for n in range(nkb):
            kb = k_ref[0, n * tq:(n + 1) * tq, :]
            km_ref[n:n + 1, :] = jnp.mean(kb.astype(F32), axis=0, keepdims=True)
            kaug_ref[n * tq:(n + 1) * tq, 0:LANES] = kb
            kaug_ref[n * tq:(n + 1) * tq, LANES:2 * LANES] = jnp.where(lane == n, 1.0, 0.0).astype(BF16)

    qs = _pair_queries(q_ref.at[0])
    q_both = jnp.concatenate([qs[0], qs[1]], axis=0)
    gate = _dot_nt_hi(km_ref[...], q_both.astype(F32))
    past = lax.broadcasted_iota(jnp.int32, gate.shape, 0) < qi
    gate = jnp.where(past, gate, NEG)
    chosen = (_top_rows(gate, MOBA_TOPK) > 0.5) & past
    negsel = jnp.where(chosen, 0.0, NEG).astype(F32)
    if nkp < LANES:
        negsel = jnp.concatenate([negsel, jnp.zeros((LANES - nkp, 2 * tq), F32)], axis=0)
    q_past = jnp.concatenate([q_both, negsel.T.astype(BF16)], axis=1)
    q_own = jnp.concatenate([q_both, jnp.zeros_like(q_both)], axis=1)

    kr = lax.broadcasted_iota(jnp.int32, (tq, 2 * tq), 0)
    qc = lax.broadcasted_iota(jnp.int32, (tq, 2 * tq), 1) & (tq - 1)
    causal = kr <= qc

    def qk(j, q=q_past):
        return _dot_nt(kaug_ref[pl.ds(pl.multiple_of(j * tq, tq), tq), :], q)

    def consume(j, s, first):
        k0 = pl.multiple_of(j * tq, tq)
        s = s + tab_ref[0, jnp.minimum(qi - j, n_tab - 1)]
        if first:
            s = jnp.where(causal, s, NEG)
        p, alpha = _col_softmax(s, m_ref, l_ref, first)
        for hh in range(2):
            cols = slice(hh * tq, (hh + 1) * tq)
            vt = vt_ref[0, hh * HEAD_DIM:(hh + 1) * HEAD_DIM, pl.ds(k0, tq)]
            _accumulate(acc_ref, cols, _dot(vt, p[:, cols]), alpha)

    _key_tiles(qi, 0, qi, qk, consume, sa_ref, sb_ref, qk_diag=lambda j: qk(j, q_own))
    _pair_finish(acc_ref, l_ref, o_ref, tq)


def _moba(p3, tab, vt):
    b, t, _ = p3.shape
    tq = MOBA_BLOCK
    nkb = t // tq
    nkp = -(-nkb // 8) * 8
    n_tab = tab.shape[1]
    return pl.pallas_call(
        functools.partial(_moba_kernel, nkb=nkb, nkp=nkp, n_tab=n_tab),
        out_shape=jax.ShapeDtypeStruct((b, t, 4 * LANES), BF16),
        grid=(b, 4, t // tq),
        in_specs=[pl.BlockSpec((1, tq, LANES), lambda i, h, q: (i, q, PB_QB + h)),
                  pl.BlockSpec((1, t, LANES), lambda i, h, q: (i, 0, PB_KB + h)),
                  pl.BlockSpec((1, LANES, t), lambda i, h, q: (i, h, 0)),
                  pl.BlockSpec((1, n_tab, tq, 2 * tq), lambda i, h, q: (h, 0, 0, 0))],
        out_specs=pl.BlockSpec((1, tq, LANES), lambda i, h, q: (i, q, h)),
        scratch_shapes=[pltpu.VMEM((nkp, LANES), F32), pltpu.VMEM((t, 2 * LANES), BF16),
                        pltpu.VMEM((tq, 2 * tq), F32), pltpu.VMEM((tq, 2 * tq), F32),
                        pltpu.VMEM((1, 2 * tq), F32), pltpu.VMEM((1, 2 * tq), F32),
                        pltpu.VMEM((HEAD_DIM, 2 * tq), F32)],
        compiler_params=_cparams(("parallel", "parallel", "arbitrary")),
        name="moba_attention",
    )(p3, p3, vt, tab)


def _gelu_tanh(x):
    return 0.5 * x * (1.0 + jnp.tanh(math.sqrt(2.0 / math.pi) * (x + 0.044715 * (x * x * x))))


def _nsa_compress_kernel(r_ref, pe_ref, wlo_ref, whi_ref, w2_ref, o_ref, *, nr):
    r = r_ref[0]
    pe = pe_ref[...]
    a = _dot((r + pe[0:1, :]).astype(BF16), wlo_ref[...])
    b = _dot((r + pe[1:2, :]).astype(BF16), whi_ref[...])
    hid = _gelu_tanh(a + pltpu.roll(b, nr - 1, axis=0))
    o_ref[0] = _dot(hid.astype(BF16), w2_ref[...]).astype(BF16)


def _nsa_compress(r3, pe2, wlo, whi, w2):
    b, nr, w = r3.shape
    return pl.pallas_call(
        functools.partial(_nsa_compress_kernel, nr=nr),
        out_shape=jax.ShapeDtypeStruct((b, nr, LANES), BF16),
        grid=(b,),
        in_specs=[pl.BlockSpec((1, nr, w), lambda i: (i, 0, 0)),
                  pl.BlockSpec((2, w), lambda i: (0, 0)),
                  pl.BlockSpec((w, 2 * NSA_CMP_HIDDEN), lambda i: (0, 0)),
                  pl.BlockSpec((w, 2 * NSA_CMP_HIDDEN), lambda i: (0, 0)),
                  pl.BlockSpec((2 * NSA_CMP_HIDDEN, LANES), lambda i: (0, 0))],
        out_specs=pl.BlockSpec((1, nr, LANES), lambda i: (i, 0, 0)),
        compiler_params=_cparams(("parallel",)),
        name="nsa_compress",
    )(r3, pe2, wlo, whi, w2)


def _group_queries(q_ref, g):
    half = lax.broadcasted_iota(jnp.int32, (q_ref.shape[0], LANES), 1) // HEAD_DIM
    lo = half == 0
    in_group = half == g
    out = []
    for pb in range(2):
        q = q_ref[:, pb * LANES:(pb + 1) * LANES].astype(F32)
        qr = pltpu.roll(q, HEAD_DIM, axis=1)
        for dup in (jnp.where(lo, q, qr), jnp.where(lo, qr, q)):
            out.append(jnp.where(in_group, dup, 0.0).astype(BF16))
    return jnp.concatenate(out, axis=0)


def _group_outputs(acc, g, tq):
    half = lax.broadcasted_iota(jnp.int32, (tq, LANES), 1) // HEAD_DIM
    lo = half == 0
    in_group = half == g
    dups = []
    for h in range(4):
        a = acc[h * tq:(h + 1) * tq, :]
        dups.append(jnp.where(in_group, a, pltpu.roll(a, HEAD_DIM, axis=1)))
    return jnp.concatenate([jnp.where(lo, dups[0], dups[1]), jnp.where(lo, dups[2], dups[3])], axis=1)


def _group_gates(s_ref, g, branch):
    sig = 1.0 / (1.0 + jnp.exp(-s_ref[...]))
    col = lax.broadcasted_iota(jnp.int32, (LANES, 2 * LANES), 0)
    lane = lax.broadcasted_iota(jnp.int32, (LANES, 2 * LANES), 1)
    spread = jnp.where(col == branch * 8 + g * 4 + lane // HEAD_DIM, 1.0, 0.0).astype(F32)
    return _dot_hi(sig, spread)


def _nsa_cmp_kernel(q_ref, kc_ref, vc_ref, tab_ref, ovl_ref, s_ref, o_ref, sel_ref, *, tq, ncp, nsp):
    g = pl.program_id(1)
    qi = pl.program_id(2)
    qs = _group_queries(q_ref.at[0], g)
    s = _dot_nt(qs, kc_ref[0]) + tab_ref[...].reshape(4 * tq, ncp)
    t = qi * tq + (lax.broadcasted_iota(jnp.int32, (4 * tq, ncp), 0) & (tq - 1))
    c = lax.broadcasted_iota(jnp.int32, (4 * tq, ncp), 1)
    ok = (t - (c * NSA_CMP_STRIDE + NSA_CMP_LEN - 1) >= 0) & (c < ncp - 1)
    s = jnp.where(ok, s, NEG)
    e = jnp.exp2(s - jnp.max(s, axis=-1, keepdims=True))
    p = jnp.where(ok, e / jnp.sum(e, axis=-1, keepdims=True), 0.0)
    o = _dot(p.astype(BF16), vc_ref[0])
    gate = _group_gates(s_ref.at[0], g, 0)
    o_ref[0] = (_group_outputs(o, g, tq) * gate).astype(BF16)

    psum = p[0:tq] + p[tq:2 * tq] + p[2 * tq:3 * tq] + p[3 * tq:4 * tq]
    imp = _dot_nt_hi(ovl_ref[...], psum)
    n = lax.broadcasted_iota(jnp.int32, (nsp, tq), 0)
    cur = (qi * tq + lax.broadcasted_iota(jnp.int32, (nsp, tq), 1)) // NSA_SEL_LEN
    forced = (n == 0) | (n == cur) | (n == cur - 1)
    valid = n <= cur
    imp = jnp.where(forced, BIG, jnp.where(valid, imp, NEG))
    chosen = (_rank_desc(imp, tq) < NSA_TOP_N) & valid
    sel_ref[0] = jnp.where(chosen, 0.0, NEG).astype(F32).T.astype(BF16)


def _nsa_cmp(p3, kc, vc, tab, ovl_t, small3, tq=128):
    b, t, _ = p3.shape
    ncp = kc.shape[1]
    nsp = ovl_t.shape[0]
    return pl.pallas_call(
        functools.partial(_nsa_cmp_kernel, tq=tq, ncp=ncp, nsp=nsp),
        out_shape=(jax.ShapeDtypeStruct((b, t, 4 * LANES), BF16),
                   jax.ShapeDtypeStruct((b, t, 2 * nsp), BF16)),
        grid=(b, 2, t // tq),
        in_specs=[pl.BlockSpec((1, tq, 2 * LANES), lambda i, g, q: (i, q, g)),
                  pl.BlockSpec((1, ncp, LANES), lambda i, g, q: (i, 0, 0)),
                  pl.BlockSpec((1, ncp, LANES), lambda i, g, q: (i, 0, 0)),
                  pl.BlockSpec((4, 1, tq, ncp), lambda i, g, q: (g, q, 0, 0)),
                  pl.BlockSpec((nsp, ncp), lambda i, g, q: (0, 0)),
                  pl.BlockSpec((1, tq, LANES), lambda i, g, q: (i, q, 0))],
        out_specs=(pl.BlockSpec((1, tq, 2 * LANES), lambda i, g, q: (i, q, g)),
                   pl.BlockSpec((1, tq, nsp), lambda i, g, q: (i, q, g))),
        compiler_params=_cparams(("parallel", "parallel", "parallel")),
        name="nsa_compressed",
    )(p3, kc, vc, tab, ovl_t, small3)


def _nsa_dense_kernel(*refs, tq, tk, n_tab, selected, branch):
    if selected:
        (q_ref, k_ref, vt_ref, tab_ref, blk_ref, sel_ref, s_ref, o_ref,
         kaug_ref, sa_ref, sb_ref, m_ref, l_ref, acc_ref) = refs
    else:
        q_ref, k_ref, vt_ref, tab_ref, s_ref, o_ref, sa_ref, sb_ref, m_ref, l_ref, acc_ref = refs
    g = pl.program_id(1)
    qi = pl.program_id(2)
    qs = _group_queries(q_ref.at[0], g)
    kr = lax.broadcasted_iota(jnp.int32, (tk, 4 * tq), 0)
    qc = lax.broadcasted_iota(jnp.int32, (tk, 4 * tq), 1) & (tq - 1)
    j0 = (qi * tq) // tk
    g0 = pl.multiple_of(g * HEAD_DIM, HEAD_DIM)

    if selected:
        @pl.when(qi == 0)
        def _():
            kaug_ref[:, 0:LANES] = k_ref[0]
            kaug_ref[:, LANES:] = blk_ref[...]

        qs = jnp.concatenate([qs, jnp.concatenate([sel_ref[0]] * 4, axis=0)], axis=1)

        def qk(j):
            return _dot_nt(kaug_ref[pl.ds(pl.multiple_of(j * tk, tk), tk), :], qs)
    else:
        def qk(j):
            return _dot_nt(k_ref[0, pl.ds(pl.multiple_of(j * tk, tk), tk), :], qs)

    def consume(j, s, first):
        k0 = pl.multiple_of(j * tk, tk)
        s = s + tab_ref[0, jnp.minimum(qi - j * (tk // tq), n_tab - 1)]
        dist = (qi * tq - j * tk) + qc - kr
        if selected:
            if first:
                s = jnp.where(dist >= 0, s, NEG)
        else:
            s = jnp.where((dist >= 0) & (dist <= NSA_WINDOW - 1), s, NEG)
        p, alpha = _col_softmax(s, m_ref, l_ref, first)
        vt = vt_ref[0, pl.ds(g0, HEAD_DIM), pl.ds(k0, tk)]
        _accumulate(acc_ref, slice(None), _dot(vt, p), alpha)

    first_j = 0 if selected else jnp.maximum(j0 - (-(-(NSA_WINDOW - 1) // tk)), 0)
    _key_tiles(j0, first_j, j0 - first_j, qk, consume, sa_ref, sb_ref)
    o_t = acc_ref[...] / l_ref[...]
    o = jnp.concatenate([o_t[:, h * tq:(h + 1) * tq] for h in range(4)], axis=0).T
    o_ref[0] = (o * _group_gates(s_ref.at[0], g, branch)).astype(BF16)


def _nsa_dense(p3, tab, small3, vt, negsel, tq=128, tk=256):
    b, t, _ = p3.shape
    tk = min(tk, t)
    n_tab = tab.shape[1]
    selected = negsel is not None
    pb_k = PB_KSA if selected else PB_KWA
    in_specs = [pl.BlockSpec((1, tq, 2 * LANES), lambda i, g, q: (i, q, g)),
                pl.BlockSpec((1, t, LANES), lambda i, g, q: (i, 0, pb_k)),
                pl.BlockSpec((1, LANES, t), lambda i, g, q: (i, 0, 0)),
                pl.BlockSpec((1, n_tab, tk, 4 * tq), lambda i, g, q: (g, 0, 0, 0))]
    args = [p3, p3, vt, tab]
    scratch = []
    if selected:
        nsp = negsel.shape[2] // 2
        block_of_key = (jnp.arange(t)[:, None] // NSA_SEL_LEN == jnp.arange(nsp)[None, :]).astype(BF16)
        in_specs += [pl.BlockSpec((t, nsp), lambda i, g, q: (0, 0)),
                     pl.BlockSpec((1, tq, nsp), lambda i, g, q: (i, q, g))]
        args += [block_of_key, negsel]
        scratch = [pltpu.VMEM((t, LANES + nsp), BF16)]
    in_specs.append(pl.BlockSpec((1, tq, LANES), lambda i, g, q: (i, q, 0)))
    args.append(small3)
    return pl.pallas_call(
        functools.partial(_nsa_dense_kernel, tq=tq, tk=tk, n_tab=n_tab, selected=selected,
                          branch=1 if selected else 2),
        out_shape=jax.ShapeDtypeStruct((b, t, 4 * LANES), BF16),
        grid=(b, 2, t // tq),
        in_specs=in_specs,
        out_specs=pl.BlockSpec((1, tq, 2 * LANES), lambda i, g, q: (i, q, g)),
        scratch_shapes=scratch + [pltpu.VMEM((tk, 4 * tq), F32), pltpu.VMEM((tk, 4 * tq), F32),
                                  pltpu.VMEM((1, 4 * tq), F32), pltpu.VMEM((1, 4 * tq), F32),
                                  pltpu.VMEM((HEAD_DIM, 4 * tq), F32)],
        compiler_params=_cparams(("parallel", "parallel", "arbitrary")),
        name="nsa_selected" if selected else "nsa_window",
    )(*args)


def _softmax_rows_two(s_c, s_p):
    m = jnp.maximum(jnp.max(s_c, axis=-1, keepdims=True), jnp.max(s_p, axis=-1, keepdims=True))
    p_c = jnp.exp2(s_c - m)
    p_p = jnp.exp2(s_p - m)
    l = jnp.sum(p_c, axis=-1, keepdims=True) + jnp.sum(p_p, axis=-1, keepdims=True)
    return p_c, p_p, m, l


def _dilated_kernel(q_ref, kp_ref, kc_ref, vp_ref, vc_ref, tab_ref, o_ref, lse_ref, *, tq):
    qi = pl.program_id(2)
    ri = lax.broadcasted_iota(jnp.int32, (tq, tq), 0)
    ci = lax.broadcasted_iota(jnp.int32, (tq, tq), 1)
    ok_cur = ri >= ci
    ok_prev = (ri - ci) <= jnp.where(qi > 0, 0, -tq - 1)
    lo = _lane_lo((tq, LANES))
    for pb in range(4):
        cols = slice(pb * LANES, (pb + 1) * LANES)
        qs = _pair_queries(q_ref.at[0, :, cols])
        kc, kp, vc, vp = kc_ref[0, :, cols], kp_ref[0, :, cols], vc_ref[0, :, cols], vp_ref[0, :, cols]
        outs, lses = [], []
        for hh in range(2):
            h = 2 * pb + hh
            s_c = jnp.where(ok_cur, _dot_nt(qs[hh], kc) + tab_ref[h, 0], NEG)
            s_p = jnp.where(ok_prev, _dot_nt(qs[hh], kp) + tab_ref[h, 1], NEG)
            p_c, p_p, m, l = _softmax_rows_two(s_c, s_p)
            o = _dot(p_c.astype(BF16), vc) + _dot(p_p.astype(BF16), vp)
            outs.append(o / l)
            lses.append(jnp.broadcast_to((m + jnp.log2(l)) * LN2, (tq, LANES)))
        o_ref[0, :, cols] = jnp.where(lo, outs[0], outs[1]).astype(BF16)
        lse_ref[0, :, cols] = jnp.where(lo, lses[0], lses[1])


def _dilated(pd3, tab, dil, tq=DIL_BAND):
    b, t, _ = pd3.shape
    l = t // dil
    pv = pd3.reshape(b, l, dil * PD_WIDTH)

    def spec(part, prev):
        def index(i, r, q):
            return (i, jnp.maximum(q - 1, 0) if prev else q, r * 3 + part)
        return pl.BlockSpec((1, tq, GW), index)

    out_spec = pl.BlockSpec((1, tq, GW), lambda i, r, q: (i, q, r))
    o, lse = pl.pallas_call(
        functools.partial(_dilated_kernel, tq=tq),
        out_shape=(jax.ShapeDtypeStruct((b, l, dil * GW), BF16),
                   jax.ShapeDtypeStruct((b, l, dil * GW), F32)),
        grid=(b, dil, l // tq),
        in_specs=[spec(0, False), spec(1, True), spec(1, False), spec(2, True), spec(2, False),
                  pl.BlockSpec((8, 2, tq, tq), lambda i, r, q: (0, 0, 0, 0))],
        out_specs=(out_spec, out_spec),
        compiler_params=_cparams(("parallel", "parallel", "parallel")),
        name="dilated_attention_%d" % dil,
    )(pv, pv, pv, pv, pv, tab)
    return o.reshape(b * t, GW), lse.reshape(b * t, GW)


def _out_proj_kernel(x_ref, oc_ref, os_ref, ow_ref, ob_ref, of_ref,
                     d1_ref, d2_ref, d3_ref, l1_ref, l2_ref, l3_ref, w_ref, g_ref, o_ref, cat_ref):
    gw = 4 * LANES
    a = oc_ref[...].astype(F32) + os_ref[...].astype(F32) + ow_ref[...].astype(F32)
    cat_ref[:, 0:gw] = a.astype(BF16)
    cat_ref[:, gw:2 * gw] = ob_ref[...]
    cat_ref[:, 2 * gw:3 * gw] = of_ref[...]
    l1, l2, l3 = l1_ref[...], l2_ref[...], l3_ref[...]
    mx = jnp.maximum(jnp.maximum(l1, l2), l3)
    e1, e2, e3 = jnp.exp(l1 - mx), jnp.exp(l2 - mx), jnp.exp(l3 - mx)
    den = e1 + e2 + e3
    d = ((e1 / den) * d1_ref[...].astype(F32) + (e2 / den) * d2_ref[...].astype(F32)
         + (e3 / den) * d3_ref[...].astype(F32))
    cat_ref[:, 3 * gw:4 * gw] = d.astype(BF16)
    y = _dot(cat_ref[...], w_ref[...])
    y = y * lax.rsqrt(jnp.mean(y * y, axis=-1, keepdims=True) + EPS)
    o_ref[...] = x_ref[...] + y * g_ref[...]


def _out_proj(x2, heads, dils, lses, w_out, g, tm=256):
    m = x2.shape[0]
    gw = 4 * LANES
    row = lambda i: (i, 0)
    const = lambda i: (0, 0)
    in_specs = ([pl.BlockSpec((tm, D_MODEL), row)]
                + [pl.BlockSpec((tm, gw), row)] * 11
                + [pl.BlockSpec((D_MODEL, D_MODEL), const), pl.BlockSpec((1, D_MODEL), const)])
    return pl.pallas_call(
        _out_proj_kernel,
        out_shape=jax.ShapeDtypeStruct((m, D_MODEL), F32),
        grid=(m // tm,),
        in_specs=in_specs,
        out_specs=pl.BlockSpec((tm, D_MODEL), row),
        scratch_shapes=[pltpu.VMEM((tm, D_MODEL), BF16)],
        compiler_params=_cparams(("parallel",)),
        name="out_proj_residual",
    )(x2, *heads, *dils, *lses, w_out, g)


def _ffn_kernel(x_ref, gpre_ref, wu_ref, wd_ref, gpost_ref, o_ref, h_ref, acc_ref):
    f = pl.program_id(1)

    @pl.when(f == 0)
    def _():
        x = x_ref[...]
        y = x * lax.rsqrt(jnp.mean(x * x, axis=-1, keepdims=True) + EPS)
        h_ref[...] = (y * gpre_ref[...]).astype(BF16)
        acc_ref[...] = jnp.zeros_like(acc_ref)

    u = jnp.maximum(_dot(h_ref[...], wu_ref[...]), 0.0)
    acc_ref[...] += _dot((u * u).astype(BF16), wd_ref[...])

    @pl.when(f == pl.num_programs(1) - 1)
    def _():
        y = acc_ref[...]
        y = y * lax.rsqrt(jnp.mean(y * y, axis=-1, keepdims=True) + EPS)
        o_ref[...] = x_ref[...] + y * gpost_ref[...]


def _ffn(x2, g_pre, w_up, w_down, g_post, tm=512, tf=512):
    m = x2.shape[0]
    return pl.pallas_call(
        _ffn_kernel,
        out_shape=jax.ShapeDtypeStruct((m, D_MODEL), F32),
        grid=(m // tm, D_FF // tf),
        in_specs=[pl.BlockSpec((tm, D_MODEL), lambda i, f: (i, 0)),
                  pl.BlockSpec((1, D_MODEL), lambda i, f: (0, 0)),
                  pl.BlockSpec((D_MODEL, tf), lambda i, f: (0, f)),
                  pl.BlockSpec((tf, D_MODEL), lambda i, f: (f, 0)),
                  pl.BlockSpec((1, D_MODEL), lambda i, f: (0, 0))],
        out_specs=pl.BlockSpec((tm, D_MODEL), lambda i, f: (i, 0)),
        scratch_shapes=[pltpu.VMEM((tm, D_MODEL), BF16), pltpu.VMEM((tm, D_MODEL), F32)],
        compiler_params=_cparams(("parallel", "arbitrary")),
        name="ffn_residual",
    )(x2, g_pre, w_up, w_down, g_post)


def _split_w_in(w):
    gw, kw = 4 * LANES, LANES
    o = 0
    cols = {}
    for name, width in (("qa", gw), ("kca", kw), ("vca", kw), ("ksa", kw), ("vsa", kw), ("kwa", kw),
                        ("vwa", kw), ("ga", 24), ("qb", gw), ("kb", gw), ("vb", gw), ("qc", gw),
                        ("kc", gw), ("vc", gw), ("fc", 8), ("qd", gw), ("kd", gw), ("vd", gw)):
        cols[name] = w[:, o:o + width] * (Q_FOLD if name[0] == "q" else 1.0)
        o += width
    main = jnp.concatenate([cols[n] for n in ("qa", "ksa", "vsa", "kwa", "vwa", "qb", "kb", "vb",
                                              "qc", "kc", "vc", "qd", "kd", "vd")], axis=1)
    pad = jnp.zeros((w.shape[0], LANES - 32), w.dtype)
    small = jnp.concatenate([cols["ga"], cols["fc"], pad, cols["kca"], cols["vca"]], axis=1)
    return main.astype(BF16), small.astype(BF16)


def _compress_weights(pe, w1, w2):
    half = NSA_CMP_LEN // 2
    hid = NSA_CMP_HIDDEN
    w1r = w1.reshape(2, half, HEAD_DIM, hid)
    z = jnp.zeros_like(w1r)
    per_group = jnp.stack([jnp.concatenate([w1r, z], axis=-1), jnp.concatenate([z, w1r], axis=-1)], axis=2)
    w_halves = per_group.reshape(2, half * 2 * HEAD_DIM, 2 * hid).astype(BF16)
    pe_r = jnp.broadcast_to(pe.reshape(2, half, 1, HEAD_DIM), (2, half, 2, HEAD_DIM)).reshape(2, half * LANES)
    z2 = jnp.zeros_like(w2)
    w2g = jnp.concatenate([jnp.concatenate([w2, z2], axis=1), jnp.concatenate([z2, w2], axis=1)], axis=0)
    return pe_r, w_halves[0], w_halves[1], w2g.astype(BF16)


def _bias_tables(rel_bias, t):
    hm = 8
    tq_a, tk_a = 128, min(256, t)
    n_a = min(_n_toeplitz_tiles(tq_a, tk_a), t // tq_a)
    tab_a = _bias_tiles(rel_bias, _toeplitz_idx_t(n_a, tk_a, tq_a, tq_a), 0, 2, 4)
    ncp = t // NSA_CMP_STRIDE
    qi = jnp.arange(t // tq_a, dtype=jnp.int32)[:, None, None]
    r = jnp.arange(tq_a, dtype=jnp.int32)[None, :, None]
    c = jnp.arange(ncp, dtype=jnp.int32)[None, None, :]
    idx_c = _t5_bucket(qi * tq_a + r - (c * NSA_CMP_STRIDE + NSA_CMP_LEN - 1))
    tab_c = _bias_tiles(rel_bias, idx_c, 0, hm)
    n_b = min(_n_toeplitz_tiles(MOBA_BLOCK, MOBA_BLOCK), t // MOBA_BLOCK)
    tab_b = _bias_tiles(rel_bias, _toeplitz_idx_t(n_b, MOBA_BLOCK, MOBA_BLOCK, MOBA_BLOCK), hm, 4, 2)
    tabs_d = [_bias_tiles(rel_bias, _toeplitz_idx(2, DIL_BAND, DIL_BAND, DIL_BAND, dil), 2 * hm, hm)
              for _, dil in LONGNET_PATTERNS]
    return tab_a, tab_c, tab_b, tabs_d


def _overlap_t(t):
    ncp = t // NSA_CMP_STRIDE
    nsp = max(t // NSA_SEL_LEN, 8)
    c_start = jnp.arange(ncp)[None, :] * NSA_CMP_STRIDE
    s_start = jnp.arange(nsp)[:, None] * NSA_SEL_LEN
    ovl = (c_start < s_start + NSA_SEL_LEN) & (c_start + NSA_CMP_LEN - 1 >= s_start)
    ovl = ovl & (jnp.arange(ncp)[None, :] < ncp - 1) & (jnp.arange(nsp)[:, None] < t // NSA_SEL_LEN)
    return ovl.astype(F32)


def _transposed(p3, pb, n_blocks):
    return p3[:, :, pb * LANES:(pb + n_blocks) * LANES].transpose(0, 2, 1)


def _mixer_layer(x2, b, t, w_main, w_small, g_pre, g_post, w_out, cmp_w_k, cmp_w_v, fox_bias, tabs, ovl_t):
    tab_a, tab_c, tab_b, tabs_d = tabs
    m = b * t
    p2, pd2, small, kca, vca = _proj(x2, g_pre, w_main, w_small)
    p3 = p2.reshape(b, t, P_WIDTH)
    pd3 = pd2.reshape(b, t, PD_WIDTH)
    small3 = small.reshape(b, t, LANES)
    row_w = NSA_CMP_STRIDE * LANES
    kc = _nsa_compress(kca.reshape(b, t // NSA_CMP_STRIDE, row_w), *cmp_w_k)
    vc = _nsa_compress(vca.reshape(b, t // NSA_CMP_STRIDE, row_w), *cmp_w_v)
    o_cmp, negsel = _nsa_cmp(p3, kc, vc, tab_c, ovl_t, small3)
    o_sel = _nsa_dense(p3, tab_a, small3, _transposed(p3, PB_VSA, 1), negsel)
    o_win = _nsa_dense(p3, tab_a, small3, _transposed(p3, PB_VWA, 1), None)
    o_b = _moba(p3, tab_b, _transposed(p3, PB_VB, 4))
    fox_bias_row = jnp.zeros((1, LANES), F32).at[0, FOX_HI:FOX_HI + 8].set(fox_bias)
    o_f = _fox(p3, _fox_cumsum(small3, fox_bias_row), _transposed(p3, PB_VC, 4))
    dil = [_dilated(pd3, tab, d) for tab, (_, d) in zip(tabs_d, LONGNET_PATTERNS)]
    heads = [a.reshape(m, 4 * LANES) for a in (o_cmp, o_sel, o_win, o_b, o_f)]
    return _out_proj(x2, heads, [d[0] for d in dil], [d[1] for d in dil], w_out, g_post)


def kernel(x, w_in, w_out, g_mix_pre, g_mix_post, g_mlp_pre, g_mlp_post, w_up, w_down, cmp_pe, phik_w1, phik_w2, phiv_w1, phiv_w2, fox_bias, rel_bias):
    b, t, d = x.shape
    depth = w_in.shape[0]
    tabs = _bias_tables(rel_bias, t)
    ovl_t = _overlap_t(t)
    x2 = x.reshape(b * t, d)
    for l in range(depth):
        w_main, w_small = _split_w_in(w_in[l])
        x2 = _mixer_layer(
            x2, b, t, w_main, w_small, g_mix_pre[l][None], g_mix_post[l][None], w_out[l].astype(BF16),
            _compress_weights(cmp_pe[l], phik_w1[l], phik_w2[l]),
            _compress_weights(cmp_pe[l], phiv_w1[l], phiv_w2[l]),
            fox_bias[l], tabs, ovl_t)
        x2 = _ffn(x2, g_mlp_pre[l][None], w_up[l].astype(BF16), w_down[l].astype(BF16), g_mlp_post[l][None])
    return x2.reshape(b, t, d)
```

```python
import functools
import math

import jax
import jax.numpy as jnp
from jax import lax
from jax.experimental import pallas as pl
from jax.experimental.pallas import tpu as pltpu

F32 = jnp.float32
BF16 = jnp.bfloat16

D_MODEL = 2048
D_FF = 4 * D_MODEL
HEAD_DIM = 64
LANES = 128
EPS = 1e-6
NEG = -1e30
BIG = 1e30
SCALE = HEAD_DIM ** -0.5
LOG2E = math.log2(math.e)
LN2 = math.log(2.0)
Q_FOLD = SCALE * LOG2E

N_BUCKETS = 32
MAX_EXACT = 16
MAX_DISTANCE = 4096
LAST_BUCKET_DIST = 2897

NSA_CMP_LEN = 32
NSA_CMP_STRIDE = 16
NSA_CMP_HIDDEN = 256
NSA_SEL_LEN = 64
NSA_TOP_N = 16
NSA_WINDOW = 512
MOBA_BLOCK = 256
MOBA_TOPK = 3
LONGNET_PATTERNS = ((128, 1), (512, 4), (2048, 16))
DIL_BAND = 128

FOX_TQ = 512
NSA_TQ_CMP = 128
NSA_TQ_DENSE = 256

P_WIDTH = 4096
PB_QA, PB_KSA, PB_VSA, PB_KWA, PB_VWA = 0, 4, 5, 6, 7
PB_QB, PB_KB, PB_VB = 8, 12, 16
PB_QC, PB_KC, PB_VC = 20, 24, 28
PD_WIDTH = 1536
GW = 4 * LANES

FOX_HI, FOX_MID, FOX_LO = 24, 32, 40

VMEM_LIMIT = 56 * 1024 * 1024


def _cparams(sem):
    return pltpu.CompilerParams(dimension_semantics=sem, vmem_limit_bytes=VMEM_LIMIT)


def _dot(a, b):
    return jnp.dot(a, b, preferred_element_type=F32)


def _dot_nt(a, b):
    return lax.dot_general(a, b, (((1,), (1,)), ((), ())), preferred_element_type=F32)


def _dot_hi(a, b):
    return jnp.dot(a, b, preferred_element_type=F32, precision=lax.Precision.HIGHEST)


def _dot_nt_hi(a, b):
    return lax.dot_general(a, b, (((1,), (1,)), ((), ())), preferred_element_type=F32,
                           precision=lax.Precision.HIGHEST)


def _t5_bucket(dist):
    dist = jnp.maximum(dist, 0)
    rel = jnp.log(jnp.maximum(dist, 1).astype(jnp.float32) / MAX_EXACT) / math.log(MAX_DISTANCE / MAX_EXACT)
    large = jnp.minimum(MAX_EXACT + (rel * (N_BUCKETS - MAX_EXACT)).astype(jnp.int32), N_BUCKETS - 1)
    return jnp.where(dist < MAX_EXACT, dist, large)


def _bias_tile_kernel(tab_ref, idx_ref, o_ref, *, head0, hpg, cols):
    grp = pl.program_id(0)
    idx = idx_ref[0]
    for hh in range(hpg):
        h = head0 + grp * hpg + hh
        acc = jnp.zeros(idx.shape, F32)
        for b in range(N_BUCKETS):
            acc = jnp.where(idx == b, tab_ref[b, h], acc)
        o_ref[0, 0, :, hh * cols:(hh + 1) * cols] = acc * LOG2E


def _bias_tiles(rel_bias, idx, head0, n_groups, hpg=1):
    n, r, c = idx.shape
    return pl.pallas_call(
        functools.partial(_bias_tile_kernel, head0=head0, hpg=hpg, cols=c),
        out_shape=jax.ShapeDtypeStruct((n_groups, n, r, hpg * c), F32),
        grid=(n_groups, n),
        in_specs=[pl.BlockSpec(memory_space=pltpu.SMEM),
                  pl.BlockSpec((1, r, c), lambda g, i: (i, 0, 0))],
        out_specs=pl.BlockSpec((1, 1, r, hpg * c), lambda g, i: (g, i, 0, 0)),
        compiler_params=_cparams(("parallel", "parallel")),
        name="bias_tiles",
    )(rel_bias, idx)


def _toeplitz_idx_t(n_tiles, krows, qcols, q_stride):
    nd = jnp.arange(n_tiles, dtype=jnp.int32)[:, None, None]
    r = jnp.arange(krows, dtype=jnp.int32)[None, :, None]
    c = jnp.arange(qcols, dtype=jnp.int32)[None, None, :]
    return _t5_bucket(nd * q_stride + c - r)


def _n_toeplitz_tiles(q_stride, keys):
    return -(-(LAST_BUCKET_DIST + keys - 1) // q_stride) + 1


def _proj_kernel(x_ref, g_ref, w_ref, ws_ref, p_ref, pd_ref, s_ref, kc_ref, vc_ref, h_ref, *, n_main):
    j = pl.program_id(1)

    @pl.when(j == 0)
    def _():
        x = x_ref[...]
        y = x * lax.rsqrt(jnp.mean(x * x, axis=-1, keepdims=True) + EPS)
        hb = (y * g_ref[...]).astype(BF16)
        h_ref[...] = hb
        small = _dot(hb, ws_ref[...])
        s_ref[...] = small[:, 0:LANES]
        kc_ref[...] = small[:, LANES:2 * LANES]
        vc_ref[...] = small[:, 2 * LANES:3 * LANES]

    y = _dot(h_ref[...], w_ref[...]).astype(BF16)

    @pl.when(j < n_main)
    def _():
        p_ref[...] = y

    @pl.when(j >= n_main)
    def _():
        pd_ref[...] = y


def _proj(x2, g, w_main, w_small, tm=1024, tn=512):
    m = x2.shape[0]
    tm = min(tm, m)
    n_main = P_WIDTH // tn
    return pl.pallas_call(
        functools.partial(_proj_kernel, n_main=n_main),
        out_shape=(jax.ShapeDtypeStruct((m, P_WIDTH), BF16),
                   jax.ShapeDtypeStruct((m, PD_WIDTH), BF16),
                   jax.ShapeDtypeStruct((m, LANES), F32),
                   jax.ShapeDtypeStruct((m, LANES), F32),
                   jax.ShapeDtypeStruct((m, LANES), F32)),
        grid=(m // tm, (P_WIDTH + PD_WIDTH) // tn),
        in_specs=[pl.BlockSpec((tm, D_MODEL), lambda i, j: (i, 0)),
                  pl.BlockSpec((1, D_MODEL), lambda i, j: (0, 0)),
                  pl.BlockSpec((D_MODEL, tn), lambda i, j: (0, j)),
                  pl.BlockSpec((D_MODEL, 3 * LANES), lambda i, j: (0, 0))],
        out_specs=(pl.BlockSpec((tm, tn), lambda i, j: (i, jnp.minimum(j, n_main - 1))),
                   pl.BlockSpec((tm, tn), lambda i, j: (i, jnp.maximum(j - n_main, 0))),
                   pl.BlockSpec((tm, LANES), lambda i, j: (i, 0)),
                   pl.BlockSpec((tm, LANES), lambda i, j: (i, 0)),
                   pl.BlockSpec((tm, LANES), lambda i, j: (i, 0))),
        scratch_shapes=[pltpu.VMEM((tm, D_MODEL), BF16)],
        compiler_params=_cparams(("parallel", "arbitrary")),
        name="rms_in_proj",
    )(x2, g, w_main, w_small)


def _lane_lo(shape):
    return lax.broadcasted_iota(jnp.int32, shape, 1) < HEAD_DIM


def _pair_queries(q_ref):
    q = q_ref[...]
    lo = _lane_lo(q.shape)
    zero = jnp.zeros_like(q)
    return jnp.where(lo, q, zero), jnp.where(lo, zero, q)


def _col_softmax(s, m_ref, l_ref, first):
    if first:
        m = jnp.max(s, axis=0, keepdims=True)
        p = jnp.exp2(s - m)
        m_ref[...] = m
        l_ref[...] = jnp.sum(p, axis=0, keepdims=True)
        return p.astype(BF16), None
    m_prev = m_ref[...]
    m_new = jnp.maximum(m_prev, jnp.max(s, axis=0, keepdims=True))
    alpha = jnp.exp2(m_prev - m_new)
    p = jnp.exp2(s - m_new)
    l_ref[...] = alpha * l_ref[...] + jnp.sum(p, axis=0, keepdims=True)
    m_ref[...] = m_new
    return p.astype(BF16), alpha


def _accumulate(acc_ref, cols, pv, alpha):
    if alpha is None:
        acc_ref[:, cols] = pv
    else:
        acc_ref[:, cols] = alpha[:, cols] * acc_ref[:, cols] + pv


def _key_tiles(j_diag, j_first, n_past, qk, consume, sa_ref, sb_ref, qk_diag=None):
    last = j_first + jnp.maximum(n_past - 1, 0)
    sa_ref[...] = qk(j_first)
    consume(j_diag, (qk_diag or qk)(j_diag), True)

    def run(base, count):
        for t in range(count):
            cur, nxt = (sa_ref, sb_ref) if t % 2 == 0 else (sb_ref, sa_ref)
            nxt[...] = qk(jnp.minimum(base + t + 1, last))
            consume(base + t, cur[...], False)

    quads = n_past // 4
    lax.fori_loop(0, quads, lambda i, c: (run(j_first + 4 * i, 4), c)[1], 0)
    rest = n_past - 4 * quads

    @pl.when(rest >= 2)
    def _():
        run(j_first + 4 * quads, 2)

    @pl.when(rest % 2 == 1)
    def _():
        consume(j_first + n_past - 1, sa_ref[...], False)


def _top_rows(v, k):
    n = v.shape[0]
    rows = lax.broadcasted_iota(jnp.int32, v.shape, 0)
    chosen = jnp.zeros(v.shape, F32)
    for _ in range(k):
        top = jnp.max(v, axis=0, keepdims=True)
        first = jnp.min(jnp.where(v == top, rows, n), axis=0, keepdims=True)
        pick = rows == first
        chosen = jnp.where(pick, 1.0, chosen)
        v = jnp.where(pick, -jnp.inf, v)
    return chosen


def _rank_desc(v, q_lanes):
    n = v.shape[0]
    groups = n // 8
    rows = lax.broadcasted_iota(jnp.int32, (8, q_lanes), 0)
    blks = [v[8 * g:8 * g + 8, :] for g in range(groups)]
    cnt = [jnp.zeros((8, q_lanes), F32) for _ in range(groups)]
    one = jnp.ones((8, q_lanes), F32)
    zero = jnp.zeros((8, q_lanes), F32)
    for np_ in range(n):
        gk, r = divmod(np_, 8)
        row = jnp.broadcast_to(v[np_:np_ + 1, :], (8, q_lanes))
        for g in range(groups):
            if g < gk:
                beats = row > blks[g]
            elif g > gk:
                beats = row >= blks[g]
            else:
                beats = (row > blks[g]) | ((row >= blks[g]) & (rows > r))
            cnt[g] = cnt[g] + jnp.where(beats, one, zero)
    return jnp.concatenate(cnt, axis=0)


def _pair_finish(acc_ref, l_ref, o_ref, tq):
    o_t = acc_ref[...] / l_ref[...]
    o_ref[0] = jnp.concatenate([o_t[:, 0:tq], o_t[:, tq:2 * tq]], axis=0).T.astype(BF16)


def _fox_cumsum_kernel(s_ref, fb_ref, c_ref, carry_ref, *, tc):
    @pl.when(pl.program_id(1) == 0)
    def _():
        carry_ref[...] = jnp.zeros_like(carry_ref)

    lane = lax.broadcasted_iota(jnp.int32, (tc, LANES), 1)
    z = s_ref[0] + fb_ref[...]
    logf = -(jnp.maximum(-z, 0.0) + jnp.log(1.0 + jnp.exp(-jnp.abs(z))))
    logf = jnp.where((lane >= FOX_HI) & (lane < FOX_MID), logf, 0.0)
    ri = lax.broadcasted_iota(jnp.int32, (tc, tc), 0)
    ci = lax.broadcasted_iota(jnp.int32, (tc, tc), 1)
    lower = jnp.where(ri >= ci, 1.0, 0.0).astype(F32)
    c = _dot_hi(lower, logf) + carry_ref[...]
    carry_ref[...] = c[tc - 1:tc, :]
    c2 = c * LOG2E
    hi = c2.astype(BF16).astype(F32)
    rest = c2 - hi
    mid = rest.astype(BF16).astype(F32)
    lo = rest - mid
    pieces = hi + pltpu.roll(mid, FOX_MID - FOX_HI, axis=1) + pltpu.roll(lo, FOX_LO - FOX_HI, axis=1)
    c_ref[0] = pieces.astype(BF16)


def _fox_cumsum(small3, fox_bias_row, tc=512):
    b, t, _ = small3.shape
    tc = min(tc, t)
    return pl.pallas_call(
        functools.partial(_fox_cumsum_kernel, tc=tc),
        out_shape=jax.ShapeDtypeStruct((b, t, LANES), BF16),
        grid=(b, t // tc),
        in_specs=[pl.BlockSpec((1, tc, LANES), lambda i, j: (i, j, 0)),
                  pl.BlockSpec((1, LANES), lambda i, j: (0, 0))],
        out_specs=pl.BlockSpec((1, tc, LANES), lambda i, j: (i, j, 0)),
        scratch_shapes=[pltpu.VMEM((1, LANES), F32)],
        compiler_params=_cparams(("parallel", "arbitrary")),
        name="fox_cumsum",
    )(small3, fox_bias_row)


def _fox_kernel(q_ref, k_ref, c_ref, vt_ref, o_ref, kaug_ref, sa_ref, sb_ref, m_ref, l_ref, acc_ref, *, tq):
    hp = pl.program_id(1)
    qi = pl.program_id(2)

    @pl.when(qi == 0)
    def _():
        kaug_ref[:, 0:LANES] = k_ref[0]
        kaug_ref[:, LANES:2 * LANES] = c_ref[0]

    qs = _pair_queries(q_ref.at[0])
    lane = lax.broadcasted_iota(jnp.int32, (tq, LANES), 1)
    rows = []
    for hh in range(2):
        h = 2 * hp + hh
        pick = (lane == FOX_HI + h) | (lane == FOX_MID + h) | (lane == FOX_LO + h)
        rows.append(jnp.concatenate([qs[hh], jnp.where(pick, -1.0, 0.0).astype(BF16)], axis=1))
    q_aug = jnp.concatenate(rows, axis=0)
    kr = lax.broadcasted_iota(jnp.int32, (tq, 2 * tq), 0)
    qc = lax.broadcasted_iota(jnp.int32, (tq, 2 * tq), 1) & (tq - 1)
    causal = kr <= qc

    def qk(j):
        return _dot_nt(kaug_ref[pl.ds(pl.multiple_of(j * tq, tq), tq), :], q_aug)

    def consume(j, s, first):
        k0 = pl.multiple_of(j * tq, tq)
        if first:
            s = jnp.where(causal, s, NEG)
        p, alpha = _col_softmax(s, m_ref, l_ref, first)
        for hh in range(2):
            cols = slice(hh * tq, (hh + 1) * tq)
            vt = vt_ref[0, hh * HEAD_DIM:(hh + 1) * HEAD_DIM, pl.ds(k0, tq)]
            _accumulate(acc_ref, cols, _dot(vt, p[:, cols]), alpha)

    _key_tiles(qi, 0, qi, qk, consume, sa_ref, sb_ref)
    _pair_finish(acc_ref, l_ref, o_ref, tq)


def _fox(p3, c_aug, vt, tq=FOX_TQ):
    b, t, _ = p3.shape
    tq = min(tq, t)
    return pl.pallas_call(
        functools.partial(_fox_kernel, tq=tq),
        out_shape=jax.ShapeDtypeStruct((b, t, 4 * LANES), BF16),
        grid=(b, 4, t // tq),
        in_specs=[pl.BlockSpec((1, tq, LANES), lambda i, h, q: (i, q, PB_QC + h)),
                  pl.BlockSpec((1, t, LANES), lambda i, h, q: (i, 0, PB_KC + h)),
                  pl.BlockSpec((1, t, LANES), lambda i, h, q: (i, 0, 0)),
                  pl.BlockSpec((1, LANES, t), lambda i, h, q: (i, h, 0))],
        out_specs=pl.BlockSpec((1, tq, LANES), lambda i, h, q: (i, q, h)),
        scratch_shapes=[pltpu.VMEM((t, 2 * LANES), BF16),
                        pltpu.VMEM((tq, 2 * tq), F32), pltpu.VMEM((tq, 2 * tq), F32),
                        pltpu.VMEM((1, 2 * tq), F32), pltpu.VMEM((1, 2 * tq), F32),
                        pltpu.VMEM((HEAD_DIM, 2 * tq), F32)],
        compiler_params=_cparams(("parallel", "parallel", "arbitrary")),
        name="fox_attention",
    )(p3, p3, c_aug, vt)


def _moba_kernel(q_ref, k_ref, vt_ref, tab_ref, o_ref, km_ref, kaug_ref, sa_ref, sb_ref, m_ref, l_ref, acc_ref,
                 *, nkb, nkp, n_tab):
    tq = MOBA_BLOCK
    qi = pl.program_id(2)

    @pl.when(qi == 0)
    def _():
        km_ref[...] = jnp.zeros_like(km_ref)
        lane = lax.broadcasted_iota(jnp.int32, (tq, LANES), 1)
        for n in range(nkb):
            kb = k_ref[0, n * tq:(n + 1) * tq, :]
            km_ref[n:n + 1, :] = jnp.mean(kb.astype(F32), axis=0, keepdims=True)
            kaug_ref[n * tq:(n + 1) * tq, 0:LANES] = kb
            kaug_ref[n * tq:(n + 1) * tq, LANES:2 * LANES] = jnp.where(lane == n, 1.0, 0.0).astype(BF16)

    qs = _pair_queries(q_ref.at[0])
    q_both = jnp.concatenate([qs[0], qs[1]], axis=0)
    gate = _dot_nt_hi(km_ref[...], q_both.astype(F32))
    past = lax.broadcasted_iota(jnp.int32, gate.shape, 0) < qi
    gate = jnp.where(past, gate, NEG)
    chosen = (_top_rows(gate, MOBA_TOPK) > 0.5) & past
    negsel = jnp.where(chosen, 0.0, NEG).astype(F32)
    if nkp < LANES:
        negsel = jnp.concatenate([negsel, jnp.zeros((LANES - nkp, 2 * tq), F32)], axis=0)
    q_past = jnp.concatenate([q_both, negsel.T.astype(BF16)], axis=1)
    q_own = jnp.concatenate([q_both, jnp.zeros_like(q_both)], axis=1)

    kr = lax.broadcasted_iota(jnp.int32, (tq, 2 * tq), 0)
    qc = lax.broadcasted_iota(jnp.int32, (tq, 2 * tq), 1) & (tq - 1)
    causal = kr <= qc

    def qk(j, q=q_past):
        return _dot_nt(kaug_ref[pl.ds(pl.multiple_of(j * tq, tq), tq), :], q)

    def consume(j, s, first):
        k0 = pl.multiple_of(j * tq, tq)
        s = s + tab_ref[0, jnp.minimum(qi - j, n_tab - 1)]
        if first:
            s = jnp.where(causal, s, NEG)
        p, alpha = _col_softmax(s, m_ref, l_ref, first)
        for hh in range(2):
            cols = slice(hh * tq, (hh + 1) * tq)
            vt = vt_ref[0, hh * HEAD_DIM:(hh + 1) * HEAD_DIM, pl.ds(k0, tq)]
            _accumulate(acc_ref, cols, _dot(vt, p[:, cols]), alpha)

    _key_tiles(qi, 0, qi, qk, consume, sa_ref, sb_ref, qk_diag=lambda j: qk(j, q_own))
    _pair_finish(acc_ref, l_ref, o_ref, tq)


def _moba(p3, tab, vt):
    b, t, _ = p3.shape
    tq = MOBA_BLOCK
    nkb = t // tq
    nkp = -(-nkb // 8) * 8
    n_tab = tab.shape[1]
    return pl.pallas_call(
        functools.partial(_moba_kernel, nkb=nkb, nkp=nkp, n_tab=n_tab),
        out_shape=jax.ShapeDtypeStruct((b, t, 4 * LANES), BF16),
        grid=(b, 4, t // tq),
        in_specs=[pl.BlockSpec((1, tq, LANES), lambda i, h, q: (i, q, PB_QB + h)),
                  pl.BlockSpec((1, t, LANES), lambda i, h, q: (i, 0, PB_KB + h)),
                  pl.BlockSpec((1, LANES, t), lambda i, h, q: (i, h, 0)),
                  pl.BlockSpec((1, n_tab, tq, 2 * tq), lambda i, h, q: (h, 0, 0, 0))],
        out_specs=pl.BlockSpec((1, tq, LANES), lambda i, h, q: (i, q, h)),
        scratch_shapes=[pltpu.VMEM((nkp, LANES), F32), pltpu.VMEM((t, 2 * LANES), BF16),
                        pltpu.VMEM((tq, 2 * tq), F32), pltpu.VMEM((tq, 2 * tq), F32),
                        pltpu.VMEM((1, 2 * tq), F32), pltpu.VMEM((1, 2 * tq), F32),
                        pltpu.VMEM((HEAD_DIM, 2 * tq), F32)],
        compiler_params=_cparams(("parallel", "parallel", "arbitrary")),
        name="moba_attention",
    )(p3, p3, vt, tab)


def _gelu_tanh(x):
    return 0.5 * x * (1.0 + jnp.tanh(math.sqrt(2.0 / math.pi) * (x + 0.044715 * (x * x * x))))


def _nsa_compress_kernel(r_ref, pe_ref, wlo_ref, whi_ref, w2_ref, o_ref, *, nr):
    r = r_ref[0]
    pe = pe_ref[...]
    a = _dot((r + pe[0:1, :]).astype(BF16), wlo_ref[...])
    b = _dot((r + pe[1:2, :]).astype(BF16), whi_ref[...])
    hid = _gelu_tanh(a + pltpu.roll(b, nr - 1, axis=0))
    o_ref[0] = _dot(hid.astype(BF16), w2_ref[...]).astype(BF16)


def _nsa_compress(r3, pe2, wlo, whi, w2):
    b, nr, w = r3.shape
    return pl.pallas_call(
        functools.partial(_nsa_compress_kernel, nr=nr),
        out_shape=jax.ShapeDtypeStruct((b, nr, LANES), BF16),
        grid=(b,),
        in_specs=[pl.BlockSpec((1, nr, w), lambda i: (i, 0, 0)),
                  pl.BlockSpec((2, w), lambda i: (0, 0)),
                  pl.BlockSpec((w, 2 * NSA_CMP_HIDDEN), lambda i: (0, 0)),
                  pl.BlockSpec((w, 2 * NSA_CMP_HIDDEN), lambda i: (0, 0)),
                  pl.BlockSpec((2 * NSA_CMP_HIDDEN, LANES), lambda i: (0, 0))],
        out_specs=pl.BlockSpec((1, nr, LANES), lambda i: (i, 0, 0)),
        compiler_params=_cparams(("parallel",)),
        name="nsa_compress",
    )(r3, pe2, wlo, whi, w2)


def _group_queries(q_ref, g):
    half = lax.broadcasted_iota(jnp.int32, (q_ref.shape[0], LANES), 1) // HEAD_DIM
    lo = half == 0
    in_group = half == g
    out = []
    for pb in range(2):
        q = q_ref[:, pb * LANES:(pb + 1) * LANES].astype(F32)
        qr = pltpu.roll(q, HEAD_DIM, axis=1)
        for dup in (jnp.where(lo, q, qr), jnp.where(lo, qr, q)):
            out.append(jnp.where(in_group, dup, 0.0).astype(BF16))
    return jnp.concatenate(out, axis=0)


def _group_outputs(acc, g, tq):
    half = lax.broadcasted_iota(jnp.int32, (tq, LANES), 1) // HEAD_DIM
    lo = half == 0
    in_group = half == g
    dups = []
    for h in range(4):
        a = acc[h * tq:(h + 1) * tq, :]
        dups.append(jnp.where(in_group, a, pltpu.roll(a, HEAD_DIM, axis=1)))
    return jnp.concatenate([jnp.where(lo, dups[0], dups[1]), jnp.where(lo, dups[2], dups[3])], axis=1)


def _group_gates(s_ref, g, branch):
    sig = 1.0 / (1.0 + jnp.exp(-s_ref[...]))
    col = lax.broadcasted_iota(jnp.int32, (LANES, 2 * LANES), 0)
    lane = lax.broadcasted_iota(jnp.int32, (LANES, 2 * LANES), 1)
    spread = jnp.where(col == branch * 8 + g * 4 + lane // HEAD_DIM, 1.0, 0.0).astype(F32)
    return _dot_hi(sig, spread)


def _nsa_cmp_kernel(q_ref, kc_ref, vc_ref, tab_ref, ovl_ref, s_ref, o_ref, sel_ref, *, tq, ncp, nsp):
    g = pl.program_id(1)
    qi = pl.program_id(2)
    qs = _group_queries(q_ref.at[0], g)
    s = _dot_nt(qs, kc_ref[0]) + tab_ref[...].reshape(4 * tq, ncp)
    t = qi * tq + (lax.broadcasted_iota(jnp.int32, (4 * tq, ncp), 0) & (tq - 1))
    c = lax.broadcasted_iota(jnp.int32, (4 * tq, ncp), 1)
    ok = (t - (c * NSA_CMP_STRIDE + NSA_CMP_LEN - 1) >= 0) & (c < ncp - 1)
    s = jnp.where(ok, s, NEG)
    e = jnp.exp2(s - jnp.max(s, axis=-1, keepdims=True))
    p = jnp.where(ok, e / jnp.sum(e, axis=-1, keepdims=True), 0.0)
    o = _dot(p.astype(BF16), vc_ref[0])
    gate = _group_gates(s_ref.at[0], g, 0)
    o_ref[0] = (_group_outputs(o, g, tq) * gate).astype(BF16)

    psum = p[0:tq] + p[tq:2 * tq] + p[2 * tq:3 * tq] + p[3 * tq:4 * tq]
    imp = _dot_nt_hi(ovl_ref[...], psum)
    n = lax.broadcasted_iota(jnp.int32, (nsp, tq), 0)
    cur = (qi * tq + lax.broadcasted_iota(jnp.int32, (nsp, tq), 1)) // NSA_SEL_LEN
    forced = (n == 0) | (n == cur) | (n == cur - 1)
    valid = n <= cur
    imp = jnp.where(forced, BIG, jnp.where(valid, imp, NEG))
    chosen = (_rank_desc(imp, tq) < NSA_TOP_N) & valid
    sel_ref[0] = jnp.where(chosen, 0.0, NEG).astype(F32).T.astype(BF16)


def _nsa_cmp(p3, kc, vc, tab, ovl_t, small3, tq=NSA_TQ_CMP):
    b, t, _ = p3.shape
    ncp = kc.shape[1]
    nsp = ovl_t.shape[0]
    return pl.pallas_call(
        functools.partial(_nsa_cmp_kernel, tq=tq, ncp=ncp, nsp=nsp),
        out_shape=(jax.ShapeDtypeStruct((b, t, 4 * LANES), BF16),
                   jax.ShapeDtypeStruct((b, t, 2 * nsp), BF16)),
        grid=(b, 2, t // tq),
        in_specs=[pl.BlockSpec((1, tq, 2 * LANES), lambda i, g, q: (i, q, g)),
                  pl.BlockSpec((1, ncp, LANES), lambda i, g, q: (i, 0, 0)),
                  pl.BlockSpec((1, ncp, LANES), lambda i, g, q: (i, 0, 0)),
                  pl.BlockSpec((4, 1, tq, ncp), lambda i, g, q: (g, q, 0, 0)),
                  pl.BlockSpec((nsp, ncp), lambda i, g, q: (0, 0)),
                  pl.BlockSpec((1, tq, LANES), lambda i, g, q: (i, q, 0))],
        out_specs=(pl.BlockSpec((1, tq, 2 * LANES), lambda i, g, q: (i, q, g)),
                   pl.BlockSpec((1, tq, nsp), lambda i, g, q: (i, q, g))),
        compiler_params=_cparams(("parallel", "parallel", "parallel")),
        name="nsa_compressed",
    )(p3, kc, vc, tab, ovl_t, small3)


def _nsa_dense_kernel(*refs, tq, tk, n_tab, selected, branch):
    if selected:
        (q_ref, k_ref, vt_ref, tab_ref, blk_ref, sel_ref, s_ref, o_ref,
         kaug_ref, sa_ref, sb_ref, m_ref, l_ref, acc_ref) = refs
    else:
        q_ref, k_ref, vt_ref, tab_ref, s_ref, o_ref, sa_ref, sb_ref, m_ref, l_ref, acc_ref = refs
    g = pl.program_id(1)
    qi = pl.program_id(2)
    qs = _group_queries(q_ref.at[0], g)
    kr = lax.broadcasted_iota(jnp.int32, (tk, 4 * tq), 0)
    qc = lax.broadcasted_iota(jnp.int32, (tk, 4 * tq), 1) & (tq - 1)
    j0 = (qi * tq) // tk
    g0 = pl.multiple_of(g * HEAD_DIM, HEAD_DIM)

    if selected:
        @pl.when(qi == 0)
        def _():
            kaug_ref[:, 0:LANES] = k_ref[0]
            kaug_ref[:, LANES:] = blk_ref[...]

        qs = jnp.concatenate([qs, jnp.concatenate([sel_ref[0]] * 4, axis=0)], axis=1)

        def qk(j):
            return _dot_nt(kaug_ref[pl.ds(pl.multiple_of(j * tk, tk), tk), :], qs)
    else:
        def qk(j):
            return _dot_nt(k_ref[0, pl.ds(pl.multiple_of(j * tk, tk), tk), :], qs)

    def consume(j, s, first):
        k0 = pl.multiple_of(j * tk, tk)
        s = s + tab_ref[0, jnp.minimum(qi - j * (tk // tq), n_tab - 1)]
        dist = (qi * tq - j * tk) + qc - kr
        if selected:
            if first:
                s = jnp.where(dist >= 0, s, NEG)
        else:
            s = jnp.where((dist >= 0) & (dist <= NSA_WINDOW - 1), s, NEG)
        p, alpha = _col_softmax(s, m_ref, l_ref, first)
        vt = vt_ref[0, pl.ds(g0, HEAD_DIM), pl.ds(k0, tk)]
        _accumulate(acc_ref, slice(None), _dot(vt, p), alpha)

    first_j = 0 if selected else jnp.maximum(j0 - (-(-(NSA_WINDOW - 1) // tk)), 0)
    _key_tiles(j0, first_j, j0 - first_j, qk, consume, sa_ref, sb_ref)
    o_t = acc_ref[...] / l_ref[...]
    o = jnp.concatenate([o_t[:, h * tq:(h + 1) * tq] for h in range(4)], axis=0).T
    o_ref[0] = (o * _group_gates(s_ref.at[0], g, branch)).astype(BF16)


def _nsa_dense(p3, tab, small3, vt, negsel):
    b, t, _ = p3.shape
    tq = tk = min(NSA_TQ_DENSE, t)
    n_tab = tab.shape[1]
    selected = negsel is not None
    pb_k = PB_KSA if selected else PB_KWA
    in_specs = [pl.BlockSpec((1, tq, 2 * LANES), lambda i, g, q: (i, q, g)),
                pl.BlockSpec((1, t, LANES), lambda i, g, q: (i, 0, pb_k)),
                pl.BlockSpec((1, LANES, t), lambda i, g, q: (i, 0, 0)),
                pl.BlockSpec((1, n_tab, tk, 4 * tq), lambda i, g, q: (g, 0, 0, 0))]
    args = [p3, p3, vt, tab]
    scratch = []
    if selected:
        nsp = negsel.shape[2] // 2
        block_of_key = (jnp.arange(t)[:, None] // NSA_SEL_LEN == jnp.arange(nsp)[None, :]).astype(BF16)
        in_specs += [pl.BlockSpec((t, nsp), lambda i, g, q: (0, 0)),
                     pl.BlockSpec((1, tq, nsp), lambda i, g, q: (i, q, g))]
        args += [block_of_key, negsel]
        scratch = [pltpu.VMEM((t, LANES + nsp), BF16)]
    in_specs.append(pl.BlockSpec((1, tq, LANES), lambda i, g, q: (i, q, 0)))
    args.append(small3)
    return pl.pallas_call(
        functools.partial(_nsa_dense_kernel, tq=tq, tk=tk, n_tab=n_tab, selected=selected,
                          branch=1 if selected else 2),
        out_shape=jax.ShapeDtypeStruct((b, t, 4 * LANES), BF16),
        grid=(b, 2, t // tq),
        in_specs=in_specs,
        out_specs=pl.BlockSpec((1, tq, 2 * LANES), lambda i, g, q: (i, q, g)),
        scratch_shapes=scratch + [pltpu.VMEM((tk, 4 * tq), F32), pltpu.VMEM((tk, 4 * tq), F32),
                                  pltpu.VMEM((1, 4 * tq), F32), pltpu.VMEM((1, 4 * tq), F32),
                                  pltpu.VMEM((HEAD_DIM, 4 * tq), F32)],
        compiler_params=_cparams(("parallel", "parallel", "arbitrary")),
        name="nsa_selected" if selected else "nsa_window",
    )(*args)


def _dilated_kernel(q_ref, kp_ref, kc_ref, vtp_ref, vtc_ref, tab_ref, o_ref, lse_ref, *, tq):
    qi = pl.program_id(2)
    kr = lax.broadcasted_iota(jnp.int32, (2 * tq, 2 * tq), 0)
    qc = lax.broadcasted_iota(jnp.int32, (2 * tq, 2 * tq), 1) & (tq - 1)
    dist = tq + qc - kr
    valid = (dist >= 0) & (dist <= tq) & (kr >= jnp.where(qi > 0, 0, tq))
    outs, lses = [], []
    for pb in range(4):
        cols = slice(pb * LANES, (pb + 1) * LANES)
        qs = _pair_queries(q_ref.at[0, :, cols])
        q_both = jnp.concatenate([qs[0], qs[1]], axis=0)
        k2 = jnp.concatenate([kp_ref[0, :, cols], kc_ref[0, :, cols]], axis=0)
        s = jnp.where(valid, _dot_nt(k2, q_both) + tab_ref[pb, 0], NEG)
        m = jnp.max(s, axis=0, keepdims=True)
        p = jnp.exp2(s - m)
        l = jnp.sum(p, axis=0, keepdims=True)
        p = p.astype(BF16)
        lse = (m + jnp.log2(l)) * LN2
        for hh in range(2):
            rows = slice((2 * pb + hh) * HEAD_DIM, (2 * pb + hh + 1) * HEAD_DIM)
            qcols = slice(hh * tq, (hh + 1) * tq)
            vt = jnp.concatenate([vtp_ref[0, 0, rows, :], vtc_ref[0, 0, rows, :]], axis=1)
            outs.append(_dot(vt, p[:, qcols]) / l[:, qcols])
            lses.append(jnp.broadcast_to(lse[:, qcols], (HEAD_DIM, tq)))
    o_ref[0] = jnp.concatenate(outs, axis=0).T.astype(BF16)
    lse_ref[0] = jnp.concatenate(lses, axis=0).T


def _dilated(pd3, tab, dil, tq=DIL_BAND):
    b, t, _ = pd3.shape
    l = t // dil
    qk = pd3[:, :, 0:2 * GW].reshape(b, l, dil * 2 * GW)
    vt = pd3[:, :, 2 * GW:3 * GW].reshape(b, l, dil, GW).transpose(0, 2, 3, 1)

    def prev(q):
        return jnp.maximum(q - 1, 0)

    out_spec = pl.BlockSpec((1, tq, GW), lambda i, r, q: (i, q, r))
    o, lse = pl.pallas_call(
        functools.partial(_dilated_kernel, tq=tq),
        out_shape=(jax.ShapeDtypeStruct((b, l, dil * GW), BF16),
                   jax.ShapeDtypeStruct((b, l, dil * GW), F32)),
        grid=(b, dil, l // tq),
        in_specs=[pl.BlockSpec((1, tq, GW), lambda i, r, q: (i, q, 2 * r)),
                  pl.BlockSpec((1, tq, GW), lambda i, r, q: (i, prev(q), 2 * r + 1)),
                  pl.BlockSpec((1, tq, GW), lambda i, r, q: (i, q, 2 * r + 1)),
                  pl.BlockSpec((1, 1, GW, tq), lambda i, r, q: (i, r, 0, prev(q))),
                  pl.BlockSpec((1, 1, GW, tq), lambda i, r, q: (i, r, 0, q)),
                  pl.BlockSpec((4, 1, 2 * tq, 2 * tq), lambda i, r, q: (0, 0, 0, 0))],
        out_specs=(out_spec, out_spec),
        compiler_params=_cparams(("parallel", "parallel", "parallel")),
        name="dilated_attention_%d" % dil,
    )(qk, qk, qk, vt, vt, tab)
    return o.reshape(b * t, GW), lse.reshape(b * t, GW)


def _out_proj_kernel(x_ref, oc_ref, os_ref, ow_ref, ob_ref, of_ref,
                     d1_ref, d2_ref, d3_ref, l1_ref, l2_ref, l3_ref, w_ref, g_ref, o_ref, cat_ref):
    gw = 4 * LANES
    a = oc_ref[...].astype(F32) + os_ref[...].astype(F32) + ow_ref[...].astype(F32)
    cat_ref[:, 0:gw] = a.astype(BF16)
    cat_ref[:, gw:2 * gw] = ob_ref[...]
    cat_ref[:, 2 * gw:3 * gw] = of_ref[...]
    l1, l2, l3 = l1_ref[...], l2_ref[...], l3_ref[...]
    mx = jnp.maximum(jnp.maximum(l1, l2), l3)
    e1, e2, e3 = jnp.exp(l1 - mx), jnp.exp(l2 - mx), jnp.exp(l3 - mx)
    den = e1 + e2 + e3
    d = ((e1 / den) * d1_ref[...].astype(F32) + (e2 / den) * d2_ref[...].astype(F32)
         + (e3 / den) * d3_ref[...].astype(F32))
    cat_ref[:, 3 * gw:4 * gw] = d.astype(BF16)
    y = _dot(cat_ref[...], w_ref[...])
    y = y * lax.rsqrt(jnp.mean(y * y, axis=-1, keepdims=True) + EPS)
    o_ref[...] = x_ref[...] + y * g_ref[...]


def _out_proj(x2, heads, dils, lses, w_out, g, tm=256):
    m = x2.shape[0]
    gw = 4 * LANES
    row = lambda i: (i, 0)
    const = lambda i: (0, 0)
    in_specs = ([pl.BlockSpec((tm, D_MODEL), row)]
                + [pl.BlockSpec((tm, gw), row)] * 11
                + [pl.BlockSpec((D_MODEL, D_MODEL), const), pl.BlockSpec((1, D_MODEL), const)])
    return pl.pallas_call(
        _out_proj_kernel,
        out_shape=jax.ShapeDtypeStruct((m, D_MODEL), F32),
        grid=(m // tm,),
        in_specs=in_specs,
        out_specs=pl.BlockSpec((tm, D_MODEL), row),
        scratch_shapes=[pltpu.VMEM((tm, D_MODEL), BF16)],
        compiler_params=_cparams(("parallel",)),
        name="out_proj_residual",
    )(x2, *heads, *dils, *lses, w_out, g)


def _ffn_kernel(x_ref, gpre_ref, wu_ref, wd_ref, gpost_ref, o_ref, h_ref, acc_ref):
    f = pl.program_id(1)

    @pl.when(f == 0)
    def _():
        x = x_ref[...]
        y = x * lax.rsqrt(jnp.mean(x * x, axis=-1, keepdims=True) + EPS)
        h_ref[...] = (y * gpre_ref[...]).astype(BF16)
        acc_ref[...] = jnp.zeros_like(acc_ref)

    u = jnp.maximum(_dot(h_ref[...], wu_ref[...]), 0.0)
    acc_ref[...] += _dot((u * u).astype(BF16), wd_ref[...])

    @pl.when(f == pl.num_programs(1) - 1)
    def _():
        y = acc_ref[...]
        y = y * lax.rsqrt(jnp.mean(y * y, axis=-1, keepdims=True) + EPS)
        o_ref[...] = x_ref[...] + y * gpost_ref[...]


def _ffn(x2, g_pre, w_up, w_down, g_post, tm=512, tf=1024):
    m = x2.shape[0]
    return pl.pallas_call(
        _ffn_kernel,
        out_shape=jax.ShapeDtypeStruct((m, D_MODEL), F32),
        grid=(m // tm, D_FF // tf),
        in_specs=[pl.BlockSpec((tm, D_MODEL), lambda i, f: (i, 0)),
                  pl.BlockSpec((1, D_MODEL), lambda i, f: (0, 0)),
                  pl.BlockSpec((D_MODEL, tf), lambda i, f: (0, f)),
                  pl.BlockSpec((tf, D_MODEL), lambda i, f: (f, 0)),
                  pl.BlockSpec((1, D_MODEL), lambda i, f: (0, 0))],
        out_specs=pl.BlockSpec((tm, D_MODEL), lambda i, f: (i, 0)),
        scratch_shapes=[pltpu.VMEM((tm, D_MODEL), BF16), pltpu.VMEM((tm, D_MODEL), F32)],
        compiler_params=_cparams(("parallel", "arbitrary")),
        name="ffn_residual",
    )(x2, g_pre, w_up, w_down, g_post)


def _split_w_in(w):
    gw, kw = 4 * LANES, LANES
    o = 0
    cols = {}
    for name, width in (("qa", gw), ("kca", kw), ("vca", kw), ("ksa", kw), ("vsa", kw), ("kwa", kw),
                        ("vwa", kw), ("ga", 24), ("qb", gw), ("kb", gw), ("vb", gw), ("qc", gw),
                        ("kc", gw), ("vc", gw), ("fc", 8), ("qd", gw), ("kd", gw), ("vd", gw)):
        cols[name] = w[:, o:o + width] * (Q_FOLD if name[0] == "q" else 1.0)
        o += width
    main = jnp.concatenate([cols[n] for n in ("qa", "ksa", "vsa", "kwa", "vwa", "qb", "kb", "vb",
                                              "qc", "kc", "vc", "qd", "kd", "vd")], axis=1)
    pad = jnp.zeros((w.shape[0], LANES - 32), w.dtype)
    small = jnp.concatenate([cols["ga"], cols["fc"], pad, cols["kca"], cols["vca"]], axis=1)
    return main.astype(BF16), small.astype(BF16)


def _compress_weights(pe, w1, w2):
    half = NSA_CMP_LEN // 2
    hid = NSA_CMP_HIDDEN
    w1r = w1.reshape(2, half, HEAD_DIM, hid)
    z = jnp.zeros_like(w1r)
    per_group = jnp.stack([jnp.concatenate([w1r, z], axis=-1), jnp.concatenate([z, w1r], axis=-1)], axis=2)
    w_halves = per_group.reshape(2, half * 2 * HEAD_DIM, 2 * hid).astype(BF16)
    pe_r = jnp.broadcast_to(pe.reshape(2, half, 1, HEAD_DIM), (2, half, 2, HEAD_DIM)).reshape(2, half * LANES)
    z2 = jnp.zeros_like(w2)
    w2g = jnp.concatenate([jnp.concatenate([w2, z2], axis=1), jnp.concatenate([z2, w2], axis=1)], axis=0)
    return pe_r, w_halves[0], w_halves[1], w2g.astype(BF16)


def _bias_tables(rel_bias, t):
    hm = 8
    tq_a = NSA_TQ_CMP
    tq_d = tk_d = min(NSA_TQ_DENSE, t)
    n_a = min(_n_toeplitz_tiles(tq_d, tk_d), t // tq_d)
    tab_a = _bias_tiles(rel_bias, _toeplitz_idx_t(n_a, tk_d, tq_d, tq_d), 0, 2, 4)
    ncp = t // NSA_CMP_STRIDE
    qi = jnp.arange(t // tq_a, dtype=jnp.int32)[:, None, None]
    r = jnp.arange(tq_a, dtype=jnp.int32)[None, :, None]
    c = jnp.arange(ncp, dtype=jnp.int32)[None, None, :]
    idx_c = _t5_bucket(qi * tq_a + r - (c * NSA_CMP_STRIDE + NSA_CMP_LEN - 1))
    tab_c = _bias_tiles(rel_bias, idx_c, 0, hm)
    n_b = min(_n_toeplitz_tiles(MOBA_BLOCK, MOBA_BLOCK), t // MOBA_BLOCK)
    tab_b = _bias_tiles(rel_bias, _toeplitz_idx_t(n_b, MOBA_BLOCK, MOBA_BLOCK, MOBA_BLOCK), hm, 4, 2)
    kr = jnp.arange(2 * DIL_BAND, dtype=jnp.int32)[None, :, None]
    qc = jnp.arange(DIL_BAND, dtype=jnp.int32)[None, None, :]
    tabs_d = [_bias_tiles(rel_bias, _t5_bucket((DIL_BAND + qc - kr) * dil), 2 * hm, 4, 2)
              for _, dil in LONGNET_PATTERNS]
    return tab_a, tab_c, tab_b, tabs_d


def _overlap_t(t):
    ncp = t // NSA_CMP_STRIDE
    nsp = max(t // NSA_SEL_LEN, 8)
    c_start = jnp.arange(ncp)[None, :] * NSA_CMP_STRIDE
    s_start = jnp.arange(nsp)[:, None] * NSA_SEL_LEN
    ovl = (c_start < s_start + NSA_SEL_LEN) & (c_start + NSA_CMP_LEN - 1 >= s_start)
    ovl = ovl & (jnp.arange(ncp)[None, :] < ncp - 1) & (jnp.arange(nsp)[:, None] < t // NSA_SEL_LEN)
    return ovl.astype(F32)


def _transposed(p3, pb, n_blocks):
    return p3[:, :, pb * LANES:(pb + n_blocks) * LANES].transpose(0, 2, 1)


def _mixer_layer(x2, b, t, w_main, w_small, g_pre, g_post, w_out, cmp_w_k, cmp_w_v, fox_bias, tabs, ovl_t):
    tab_a, tab_c, tab_b, tabs_d = tabs
    m = b * t
    p2, pd2, small, kca, vca = _proj(x2, g_pre, w_main, w_small)
    p3 = p2.reshape(b, t, P_WIDTH)
    pd3 = pd2.reshape(b, t, PD_WIDTH)
    small3 = small.reshape(b, t, LANES)
    row_w = NSA_CMP_STRIDE * LANES
    kc = _nsa_compress(kca.reshape(b, t // NSA_CMP_STRIDE, row_w), *cmp_w_k)
    vc = _nsa_compress(vca.reshape(b, t // NSA_CMP_STRIDE, row_w), *cmp_w_v)
    o_cmp, negsel = _nsa_cmp(p3, kc, vc, tab_c, ovl_t, small3)
    o_sel = _nsa_dense(p3, tab_a, small3, _transposed(p3, PB_VSA, 1), negsel)
    o_win = _nsa_dense(p3, tab_a, small3, _transposed(p3, PB_VWA, 1), None)
    o_b = _moba(p3, tab_b, _transposed(p3, PB_VB, 4))
    fox_bias_row = jnp.zeros((1, LANES), F32).at[0, FOX_HI:FOX_HI + 8].set(fox_bias)
    o_f = _fox(p3, _fox_cumsum(small3, fox_bias_row), _transposed(p3, PB_VC, 4))
    dil = [_dilated(pd3, tab, d) for tab, (_, d) in zip(tabs_d, LONGNET_PATTERNS)]
    heads = [a.reshape(m, 4 * LANES) for a in (o_cmp, o_sel, o_win, o_b, o_f)]
    return _out_proj(x2, heads, [d[0] for d in dil], [d[1] for d in dil], w_out, g_post)


def kernel(x, w_in, w_out, g_mix_pre, g_mix_post, g_mlp_pre, g_mlp_post, w_up, w_down, cmp_pe, phik_w1, phik_w2, phiv_w1, phiv_w2, fox_bias, rel_bias):
    b, t, d = x.shape
    depth = w_in.shape[0]
    tabs = _bias_tables(rel_bias, t)
    ovl_t = _overlap_t(t)
    x2 = x.reshape(b * t, d)
    for l in range(depth):
        w_main, w_small = _split_w_in(w_in[l])
        x2 = _mixer_layer(
            x2, b, t, w_main, w_small, g_mix_pre[l][None], g_mix_post[l][None], w_out[l].astype(BF16),
            _compress_weights(cmp_pe[l], phik_w1[l], phik_w2[l]),
            _compress_weights(cmp_pe[l], phiv_w1[l], phiv_w2[l]),
            fox_bias[l], tabs, ovl_t)
        x2 = _ffn(x2, g_mlp_pre[l][None], w_up[l].astype(BF16), w_down[l].astype(BF16), g_mlp_post[l][None])
    return x2.reshape(b, t, d)
```

```python
import functools
import math

import jax
import jax.numpy as jnp
from jax import lax
from jax.experimental import pallas as pl
from jax.experimental.pallas import tpu as pltpu

F32 = jnp.float32
BF16 = jnp.bfloat16

D_MODEL = 2048
D_FF = 4 * D_MODEL
HEAD_DIM = 64
LANES = 128
EPS = 1e-6
NEG = -1e30
BIG = 1e30
SCALE = HEAD_DIM ** -0.5
LOG2E = math.log2(math.e)
LN2 = math.log(2.0)
Q_FOLD = SCALE * LOG2E

N_BUCKETS = 32
MAX_EXACT = 16
MAX_DISTANCE = 4096
LAST_BUCKET_DIST = 2897

NSA_CMP_LEN = 32
NSA_CMP_STRIDE = 16
NSA_CMP_HIDDEN = 256
NSA_SEL_LEN = 64
NSA_TOP_N = 16
NSA_WINDOW = 512
MOBA_BLOCK = 256
MOBA_TOPK = 3
LONGNET_PATTERNS = ((128, 1), (512, 4), (2048, 16))
DIL_BAND = 128

FOX_TQ = 512
MOBA_TQ = 512
NSA_TQ_CMP = 256
NSA_TQ_DENSE = 256

P_WIDTH = 4096
PB_QA, PB_KSA, PB_VSA, PB_KWA, PB_VWA = 0, 4, 5, 6, 7
PB_QB, PB_KB, PB_VB = 8, 12, 16
PB_QC, PB_KC, PB_VC = 20, 24, 28
PD_WIDTH = 1536
GW = 4 * LANES
VT_ROWS = HEAD_DIM + 16

FOX_HI, FOX_MID, FOX_LO = 24, 32, 40

VMEM_LIMIT = 56 * 1024 * 1024


def _cparams(sem):
    return pltpu.CompilerParams(dimension_semantics=sem, vmem_limit_bytes=VMEM_LIMIT)


def _dot(a, b):
    return jnp.dot(a, b, preferred_element_type=F32)


def _dot_nt(a, b):
    return lax.dot_general(a, b, (((1,), (1,)), ((), ())), preferred_element_type=F32)


def _dot_hi(a, b):
    return jnp.dot(a, b, preferred_element_type=F32, precision=lax.Precision.HIGHEST)


def _dot_nt_hi(a, b):
    return lax.dot_general(a, b, (((1,), (1,)), ((), ())), preferred_element_type=F32,
                           precision=lax.Precision.HIGHEST)


def _t5_bucket(dist):
    dist = jnp.maximum(dist, 0)
    rel = jnp.log(jnp.maximum(dist, 1).astype(jnp.float32) / MAX_EXACT) / math.log(MAX_DISTANCE / MAX_EXACT)
    large = jnp.minimum(MAX_EXACT + (rel * (N_BUCKETS - MAX_EXACT)).astype(jnp.int32), N_BUCKETS - 1)
    return jnp.where(dist < MAX_EXACT, dist, large)


def _bias_tile_kernel(tab_ref, idx_ref, o_ref, *, head0, hpg, cols):
    grp = pl.program_id(0)
    idx = idx_ref[0]
    for hh in range(hpg):
        h = head0 + grp * hpg + hh
        acc = jnp.zeros(idx.shape, F32)
        for b in range(N_BUCKETS):
            acc = jnp.where(idx == b, tab_ref[b, h], acc)
        o_ref[0, 0, :, hh * cols:(hh + 1) * cols] = acc * LOG2E


def _bias_tiles(rel_bias, idx, head0, n_groups, hpg=1):
    n, r, c = idx.shape
    return pl.pallas_call(
        functools.partial(_bias_tile_kernel, head0=head0, hpg=hpg, cols=c),
        out_shape=jax.ShapeDtypeStruct((n_groups, n, r, hpg * c), F32),
        grid=(n_groups, n),
        in_specs=[pl.BlockSpec(memory_space=pltpu.SMEM),
                  pl.BlockSpec((1, r, c), lambda g, i: (i, 0, 0))],
        out_specs=pl.BlockSpec((1, 1, r, hpg * c), lambda g, i: (g, i, 0, 0)),
        compiler_params=_cparams(("parallel", "parallel")),
        name="bias_tiles",
    )(rel_bias, idx)


def _toeplitz_idx_t(n_tiles, krows, qcols, q_stride):
    nd = jnp.arange(n_tiles, dtype=jnp.int32)[:, None, None]
    r = jnp.arange(krows, dtype=jnp.int32)[None, :, None]
    c = jnp.arange(qcols, dtype=jnp.int32)[None, None, :]
    return _t5_bucket(nd * q_stride + c - r)


def _n_toeplitz_tiles(q_stride, keys):
    return -(-(LAST_BUCKET_DIST + keys - 1) // q_stride) + 1


def _proj_kernel(x_ref, g_ref, w_ref, ws_ref, p_ref, pd_ref, s_ref, kc_ref, vc_ref, h_ref, *, n_main):
    j = pl.program_id(1)

    @pl.when(j == 0)
    def _():
        x = x_ref[...]
        y = x * lax.rsqrt(jnp.mean(x * x, axis=-1, keepdims=True) + EPS)
        hb = (y * g_ref[...]).astype(BF16)
        h_ref[...] = hb
        small = _dot(hb, ws_ref[...])
        s_ref[...] = small[:, 0:LANES]
        kc_ref[...] = small[:, LANES:2 * LANES]
        vc_ref[...] = small[:, 2 * LANES:3 * LANES]

    y = _dot(h_ref[...], w_ref[...]).astype(BF16)

    @pl.when(j < n_main)
    def _():
        p_ref[...] = y

    @pl.when(j >= n_main)
    def _():
        pd_ref[...] = y


def _proj(x2, g, w_main, w_small, tm=1024, tn=512):
    m = x2.shape[0]
    tm = min(tm, m)
    n_main = P_WIDTH // tn
    return pl.pallas_call(
        functools.partial(_proj_kernel, n_main=n_main),
        out_shape=(jax.ShapeDtypeStruct((m, P_WIDTH), BF16),
                   jax.ShapeDtypeStruct((m, PD_WIDTH), BF16),
                   jax.ShapeDtypeStruct((m, LANES), F32),
                   jax.ShapeDtypeStruct((m, LANES), F32),
                   jax.ShapeDtypeStruct((m, LANES), F32)),
        grid=(m // tm, (P_WIDTH + PD_WIDTH) // tn),
        in_specs=[pl.BlockSpec((tm, D_MODEL), lambda i, j: (i, 0)),
                  pl.BlockSpec((1, D_MODEL), lambda i, j: (0, 0)),
                  pl.BlockSpec((D_MODEL, tn), lambda i, j: (0, j)),
                  pl.BlockSpec((D_MODEL, 3 * LANES), lambda i, j: (0, 0))],
        out_specs=(pl.BlockSpec((tm, tn), lambda i, j: (i, jnp.minimum(j, n_main - 1))),
                   pl.BlockSpec((tm, tn), lambda i, j: (i, jnp.maximum(j - n_main, 0))),
                   pl.BlockSpec((tm, LANES), lambda i, j: (i, 0)),
                   pl.BlockSpec((tm, LANES), lambda i, j: (i, 0)),
                   pl.BlockSpec((tm, LANES), lambda i, j: (i, 0))),
        scratch_shapes=[pltpu.VMEM((tm, D_MODEL), BF16)],
        compiler_params=_cparams(("parallel", "arbitrary")),
        name="rms_in_proj",
    )(x2, g, w_main, w_small)


def _lane_lo(shape):
    return lax.broadcasted_iota(jnp.int32, shape, 1) < HEAD_DIM


def _pair_queries(q_ref):
    q = q_ref[...]
    lo = _lane_lo(q.shape)
    zero = jnp.zeros_like(q)
    return jnp.where(lo, q, zero), jnp.where(lo, zero, q)


def _col_softmax(s, m_ref, first):
    if first:
        m = jnp.max(s, axis=0, keepdims=True)
        m_ref[...] = m
        return jnp.exp2(s - m).astype(BF16), None
    m_prev = m_ref[...]
    m_new = jnp.maximum(m_prev, jnp.max(s, axis=0, keepdims=True))
    m_ref[...] = m_new
    return jnp.exp2(s - m_new).astype(BF16), jnp.exp2(m_prev - m_new)


def _accumulate(acc_ref, cols, pv, alpha):
    if alpha is None:
        acc_ref[:, cols] = pv
    else:
        acc_ref[:, cols] = alpha[:, cols] * acc_ref[:, cols] + pv


def _key_tiles(j_diag, j_first, n_past, qk, consume, sa_ref, sb_ref, qk_diag=None):
    last = j_first + jnp.maximum(n_past - 1, 0)
    sa_ref[...] = qk(j_first)
    consume(j_diag, (qk_diag or qk)(j_diag), True)

    def run(base, count):
        for t in range(count):
            cur, nxt = (sa_ref, sb_ref) if t % 2 == 0 else (sb_ref, sa_ref)
            nxt[...] = qk(jnp.minimum(base + t + 1, last))
            consume(base + t, cur[...], False)

    quads = n_past // 4
    lax.fori_loop(0, quads, lambda i, c: (run(j_first + 4 * i, 4), c)[1], 0)
    rest = n_past - 4 * quads

    @pl.when(rest >= 2)
    def _():
        run(j_first + 4 * quads, 2)

    @pl.when(rest % 2 == 1)
    def _():
        consume(j_first + n_past - 1, sa_ref[...], False)


def _top_rows(v, k):
    n = v.shape[0]
    rows = lax.broadcasted_iota(jnp.int32, v.shape, 0)
    chosen = jnp.zeros(v.shape, F32)
    for _ in range(k):
        top = jnp.max(v, axis=0, keepdims=True)
        first = jnp.min(jnp.where(v == top, rows, n), axis=0, keepdims=True)
        pick = rows == first
        chosen = jnp.where(pick, 1.0, chosen)
        v = jnp.where(pick, -jnp.inf, v)
    return chosen


def _top_rows_bisect(v, k):
    n = v.shape[0]
    key = pltpu.bitcast(v, jnp.int32)
    thr = jnp.zeros((1, v.shape[1]), jnp.int32)
    for bit in range(30, -1, -1):
        cand = thr | (1 << bit)
        cnt = jnp.sum(jnp.where(key >= cand, 1.0, 0.0), axis=0, keepdims=True)
        thr = jnp.where(cnt >= k, cand, thr)
    above = key > thr
    tie = key == thr
    n_above = jnp.sum(jnp.where(above, 1.0, 0.0), axis=0, keepdims=True)
    ri = lax.broadcasted_iota(jnp.int32, (n, n), 0)
    ci = lax.broadcasted_iota(jnp.int32, (n, n), 1)
    before = jnp.where(ci < ri, 1.0, 0.0).astype(BF16)
    ties_before = _dot(before, jnp.where(tie, 1.0, 0.0).astype(BF16))
    return above | (tie & (ties_before < k - n_above))


def _pair_finish(acc_ref, o_ref, tq):
    o_t = acc_ref[0:HEAD_DIM, :] / acc_ref[HEAD_DIM:HEAD_DIM + 1, :]
    o_ref[0] = jnp.concatenate([o_t[:, 0:tq], o_t[:, tq:2 * tq]], axis=0).T.astype(BF16)


def _fox_cumsum_kernel(s_ref, fb_ref, c_ref, carry_ref, *, tc):
    @pl.when(pl.program_id(1) == 0)
    def _():
        carry_ref[...] = jnp.zeros_like(carry_ref)

    lane = lax.broadcasted_iota(jnp.int32, (tc, LANES), 1)
    z = s_ref[0] + fb_ref[...]
    logf = -(jnp.maximum(-z, 0.0) + jnp.log(1.0 + jnp.exp(-jnp.abs(z))))
    logf = jnp.where((lane >= FOX_HI) & (lane < FOX_MID), logf, 0.0)
    ri = lax.broadcasted_iota(jnp.int32, (tc, tc), 0)
    ci = lax.broadcasted_iota(jnp.int32, (tc, tc), 1)
    lower = jnp.where(ri >= ci, 1.0, 0.0).astype(F32)
    c = _dot_hi(lower, logf) + carry_ref[...]
    carry_ref[...] = c[tc - 1:tc, :]
    c2 = c * LOG2E
    hi = c2.astype(BF16).astype(F32)
    rest = c2 - hi
    mid = rest.astype(BF16).astype(F32)
    lo = rest - mid
    pieces = hi + pltpu.roll(mid, FOX_MID - FOX_HI, axis=1) + pltpu.roll(lo, FOX_LO - FOX_HI, axis=1)
    c_ref[0] = pieces.astype(BF16)


def _fox_cumsum(small3, fox_bias_row, tc=512):
    b, t, _ = small3.shape
    tc = min(tc, t)
    return pl.pallas_call(
        functools.partial(_fox_cumsum_kernel, tc=tc),
        out_shape=jax.ShapeDtypeStruct((b, t, LANES), BF16),
        grid=(b, t // tc),
        in_specs=[pl.BlockSpec((1, tc, LANES), lambda i, j: (i, j, 0)),
                  pl.BlockSpec((1, LANES), lambda i, j: (0, 0))],
        out_specs=pl.BlockSpec((1, tc, LANES), lambda i, j: (i, j, 0)),
        scratch_shapes=[pltpu.VMEM((1, LANES), F32)],
        compiler_params=_cparams(("parallel", "arbitrary")),
        name="fox_cumsum",
    )(small3, fox_bias_row)


def _fox_kernel(q_ref, k_ref, c_ref, vt_ref, o_ref, kaug_ref, sa_ref, sb_ref, m_ref, acc_ref, *, tq):
    hp = pl.program_id(1)
    qi = pl.program_id(2)

    @pl.when(qi == 0)
    def _():
        kaug_ref[:, 0:LANES] = k_ref[0]
        kaug_ref[:, LANES:2 * LANES] = c_ref[0]

    qs = _pair_queries(q_ref.at[0])
    lane = lax.broadcasted_iota(jnp.int32, (tq, LANES), 1)
    rows = []
    for hh in range(2):
        h = 2 * hp + hh
        pick = (lane == FOX_HI + h) | (lane == FOX_MID + h) | (lane == FOX_LO + h)
        rows.append(jnp.concatenate([qs[hh], jnp.where(pick, -1.0, 0.0).astype(BF16)], axis=1))
    q_aug = jnp.concatenate(rows, axis=0)
    kr = lax.broadcasted_iota(jnp.int32, (tq, 2 * tq), 0)
    qc = lax.broadcasted_iota(jnp.int32, (tq, 2 * tq), 1) & (tq - 1)
    causal = kr <= qc

    def qk(j):
        return _dot_nt(kaug_ref[pl.ds(pl.multiple_of(j * tq, tq), tq), :], q_aug)

    def consume(j, s, first):
        k0 = pl.multiple_of(j * tq, tq)
        if first:
            s = jnp.where(causal, s, NEG)
        p, alpha = _col_softmax(s, m_ref, first)
        for hh in range(2):
            cols = slice(hh * tq, (hh + 1) * tq)
            vt = vt_ref[0, hh * VT_ROWS:(hh + 1) * VT_ROWS, pl.ds(k0, tq)]
            _accumulate(acc_ref, cols, _dot(vt, p[:, cols]), alpha)

    _key_tiles(qi, 0, qi, qk, consume, sa_ref, sb_ref)
    _pair_finish(acc_ref, o_ref, tq)


def _fox(p3, c_aug, vt, tq=FOX_TQ):
    b, t, _ = p3.shape
    tq = min(tq, t)
    return pl.pallas_call(
        functools.partial(_fox_kernel, tq=tq),
        out_shape=jax.ShapeDtypeStruct((b, t, 4 * LANES), BF16),
        grid=(b, 4, t // tq),
        in_specs=[pl.BlockSpec((1, tq, LANES), lambda i, h, q: (i, q, PB_QC + h)),
                  pl.BlockSpec((1, t, LANES), lambda i, h, q: (i, 0, PB_KC + h)),
                  pl.BlockSpec((1, t, LANES), lambda i, h, q: (i, 0, 0)),
                  pl.BlockSpec((1, 2 * VT_ROWS, t), lambda i, h, q: (i, h, 0))],
        out_specs=pl.BlockSpec((1, tq, LANES), lambda i, h, q: (i, q, h)),
        scratch_shapes=[pltpu.VMEM((t, 2 * LANES), BF16),
                        pltpu.VMEM((tq, 2 * tq), F32), pltpu.VMEM((tq, 2 * tq), F32),
                        pltpu.VMEM((1, 2 * tq), F32), pltpu.VMEM((VT_ROWS, 2 * tq), F32)],
        compiler_params=_cparams(("parallel", "parallel", "arbitrary")),
        name="fox_attention",
    )(p3, p3, c_aug, vt)


def _moba_kernel(q_ref, k_ref, vt_ref, tab_ref, o_ref, km_ref, kaug_ref, sa_ref, sb_ref, m_ref, acc_ref,
                 *, nkb, nkp, n_tab, tq):
    blk = MOBA_BLOCK
    qi = pl.program_id(2)

    @pl.when(qi == 0)
    def _():
        km_ref[...] = jnp.zeros_like(km_ref)
        lane = lax.broadcasted_iota(jnp.int32, (blk, LANES), 1)
        for n in range(nkb):
            kb = k_ref[0, n * blk:(n + 1) * blk, :]
            km_ref[n:n + 1, :] = jnp.mean(kb.astype(F32), axis=0, keepdims=True)
            kaug_ref[n * blk:(n + 1) * blk, 0:LANES] = kb
            kaug_ref[n * blk:(n + 1) * blk, LANES:2 * LANES] = jnp.where(lane == n, 1.0, 0.0).astype(BF16)

    qs = _pair_queries(q_ref.at[0])
    q_both = jnp.concatenate([qs[0], qs[1]], axis=0)
    gate = _dot_nt_hi(km_ref[...], q_both.astype(F32))
    n = lax.broadcasted_iota(jnp.int32, gate.shape, 0)
    q_pos = lax.broadcasted_iota(jnp.int32, gate.shape, 1) & (tq - 1)
    own = qi * (tq // blk) + q_pos // blk
    past = n < own
    gate = jnp.where(past, gate, NEG)
    allowed = ((_top_rows(gate, MOBA_TOPK) > 0.5) & past) | (n == own)
    negsel = jnp.where(allowed, 0.0, NEG).astype(F32)
    if nkp < LANES:
        negsel = jnp.concatenate([negsel, jnp.zeros((LANES - nkp, 2 * tq), F32)], axis=0)
    q_aug = jnp.concatenate([q_both, negsel.T.astype(BF16)], axis=1)

    kr = lax.broadcasted_iota(jnp.int32, (tq, 2 * tq), 0)
    qc = lax.broadcasted_iota(jnp.int32, (tq, 2 * tq), 1) & (tq - 1)
    causal = kr <= qc

    def qk(j):
        return _dot_nt(kaug_ref[pl.ds(pl.multiple_of(j * tq, tq), tq), :], q_aug)

    def consume(j, s, first):
        k0 = pl.multiple_of(j * tq, tq)
        s = s + tab_ref[0, jnp.minimum(qi - j, n_tab - 1)]
        if first:
            s = jnp.where(causal, s, NEG)
        p, alpha = _col_softmax(s, m_ref, first)
        for hh in range(2):
            cols = slice(hh * tq, (hh + 1) * tq)
            vt = vt_ref[0, hh * VT_ROWS:(hh + 1) * VT_ROWS, pl.ds(k0, tq)]
            _accumulate(acc_ref, cols, _dot(vt, p[:, cols]), alpha)

    _key_tiles(qi, 0, qi, qk, consume, sa_ref, sb_ref)
    _pair_finish(acc_ref, o_ref, tq)


def _moba(p3, tab, vt):
    b, t, _ = p3.shape
    tq = tab.shape[2]
    nkb = t // MOBA_BLOCK
    nkp = -(-nkb // 8) * 8
    n_tab = tab.shape[1]
    return pl.pallas_call(
        functools.partial(_moba_kernel, nkb=nkb, nkp=nkp, n_tab=n_tab, tq=tq),
        out_shape=jax.ShapeDtypeStruct((b, t, 4 * LANES), BF16),
        grid=(b, 4, t // tq),
        in_specs=[pl.BlockSpec((1, tq, LANES), lambda i, h, q: (i, q, PB_QB + h)),
                  pl.BlockSpec((1, t, LANES), lambda i, h, q: (i, 0, PB_KB + h)),
                  pl.BlockSpec((1, 2 * VT_ROWS, t), lambda i, h, q: (i, h, 0)),
                  pl.BlockSpec((1, n_tab, tq, 2 * tq), lambda i, h, q: (h, 0, 0, 0),
                               pipeline_mode=pl.Buffered(1))],
        out_specs=pl.BlockSpec((1, tq, LANES), lambda i, h, q: (i, q, h)),
        scratch_shapes=[pltpu.VMEM((nkp, LANES), F32), pltpu.VMEM((t, 2 * LANES), BF16),
                        pltpu.VMEM((tq, 2 * tq), F32), pltpu.VMEM((tq, 2 * tq), F32),
                        pltpu.VMEM((1, 2 * tq), F32), pltpu.VMEM((VT_ROWS, 2 * tq), F32)],
        compiler_params=_cparams(("parallel", "parallel", "arbitrary")),
        name="moba_attention",
    )(p3, p3, vt, tab)


def _gelu_tanh(x):
    return 0.5 * x * (1.0 + jnp.tanh(math.sqrt(2.0 / math.pi) * (x + 0.044715 * (x * x * x))))


def _nsa_compress_kernel(r_ref, pe_ref, wlo_ref, whi_ref, w2_ref, o_ref, *, nr):
    r = r_ref[0]
    pe = pe_ref[...]
    a = _dot((r + pe[0:1, :]).astype(BF16), wlo_ref[...])
    b = _dot((r + pe[1:2, :]).astype(BF16), whi_ref[...])
    hid = _gelu_tanh(a + pltpu.roll(b, nr - 1, axis=0))
    o_ref[0] = _dot(hid.astype(BF16), w2_ref[...]).astype(BF16)


def _nsa_compress(r3, pe2, wlo, whi, w2):
    b, nr, w = r3.shape
    return pl.pallas_call(
        functools.partial(_nsa_compress_kernel, nr=nr),
        out_shape=jax.ShapeDtypeStruct((b, nr, LANES), BF16),
        grid=(b,),
        in_specs=[pl.BlockSpec((1, nr, w), lambda i: (i, 0, 0)),
                  pl.BlockSpec((2, w), lambda i: (0, 0)),
                  pl.BlockSpec((w, 2 * NSA_CMP_HIDDEN), lambda i: (0, 0)),
                  pl.BlockSpec((w, 2 * NSA_CMP_HIDDEN), lambda i: (0, 0)),
                  pl.BlockSpec((2 * NSA_CMP_HIDDEN, LANES), lambda i: (0, 0))],
        out_specs=pl.BlockSpec((1, nr, LANES), lambda i: (i, 0, 0)),
        compiler_params=_cparams(("parallel",)),
        name="nsa_compress",
    )(r3, pe2, wlo, whi, w2)


def _group_queries(q_ref, g):
    half = lax.broadcasted_iota(jnp.int32, (q_ref.shape[0], LANES), 1) // HEAD_DIM
    lo = half == 0
    in_group = half == g
    out = []
    for pb in range(2):
        q = q_ref[:, pb * LANES:(pb + 1) * LANES].astype(F32)
        qr = pltpu.roll(q, HEAD_DIM, axis=1)
        for dup in (jnp.where(lo, q, qr), jnp.where(lo, qr, q)):
            out.append(jnp.where(in_group, dup, 0.0).astype(BF16))
    return jnp.concatenate(out, axis=0)


def _group_outputs(acc, g, tq):
    half = lax.broadcasted_iota(jnp.int32, (tq, LANES), 1) // HEAD_DIM
    lo = half == 0
    in_group = half == g
    dups = []
    for h in range(4):
        a = acc[h * tq:(h + 1) * tq, :]
        dups.append(jnp.where(in_group, a, pltpu.roll(a, HEAD_DIM, axis=1)))
    return jnp.concatenate([jnp.where(lo, dups[0], dups[1]), jnp.where(lo, dups[2], dups[3])], axis=1)


def _group_gates(s_ref, g, branch):
    sig = 1.0 / (1.0 + jnp.exp(-s_ref[...]))
    col = lax.broadcasted_iota(jnp.int32, (LANES, 2 * LANES), 0)
    lane = lax.broadcasted_iota(jnp.int32, (LANES, 2 * LANES), 1)
    spread = jnp.where(col == branch * 8 + g * 4 + lane // HEAD_DIM, 1.0, 0.0).astype(F32)
    return _dot_hi(sig, spread)


def _nsa_cmp_kernel(q_ref, kc_ref, vc_ref, tab_ref, ovl_ref, s_ref, o_ref, sel_ref, *, tq, ncp, nsp):
    g = pl.program_id(1)
    qi = pl.program_id(2)
    qs = _group_queries(q_ref.at[0], g)
    s = _dot_nt(qs, kc_ref[0]) + tab_ref[...].reshape(4 * tq, ncp)
    t = qi * tq + (lax.broadcasted_iota(jnp.int32, (4 * tq, ncp), 0) & (tq - 1))
    c = lax.broadcasted_iota(jnp.int32, (4 * tq, ncp), 1)
    ok = (t - (c * NSA_CMP_STRIDE + NSA_CMP_LEN - 1) >= 0) & (c < ncp - 1)
    s = jnp.where(ok, s, NEG)
    e = jnp.exp2(s - jnp.max(s, axis=-1, keepdims=True))
    p = jnp.where(ok, e / jnp.sum(e, axis=-1, keepdims=True), 0.0)
    o = _dot(p.astype(BF16), vc_ref[0])
    gate = _group_gates(s_ref.at[0], g, 0)
    o_ref[0] = (_group_outputs(o, g, tq) * gate).astype(BF16)

    psum = p[0:tq] + p[tq:2 * tq] + p[2 * tq:3 * tq] + p[3 * tq:4 * tq]
    imp = _dot_nt_hi(ovl_ref[...], psum)
    n = lax.broadcasted_iota(jnp.int32, (nsp, tq), 0)
    cur = (qi * tq + lax.broadcasted_iota(jnp.int32, (nsp, tq), 1)) // NSA_SEL_LEN
    forced = (n == 0) | (n == cur) | (n == cur - 1)
    valid = n <= cur
    imp = jnp.where(forced, BIG, jnp.where(valid, jnp.abs(imp), NEG))
    chosen = _top_rows_bisect(imp, NSA_TOP_N) & valid
    sel_ref[0] = jnp.where(chosen, 0.0, NEG).astype(F32).T.astype(BF16)


def _nsa_cmp(p3, kc, vc, tab, ovl_t, small3, tq=NSA_TQ_CMP):
    b, t, _ = p3.shape
    ncp = kc.shape[1]
    nsp = ovl_t.shape[0]
    return pl.pallas_call(
        functools.partial(_nsa_cmp_kernel, tq=tq, ncp=ncp, nsp=nsp),
        out_shape=(jax.ShapeDtypeStruct((b, t, 4 * LANES), BF16),
                   jax.ShapeDtypeStruct((b, t, 2 * nsp), BF16)),
        grid=(b, 2, t // tq),
        in_specs=[pl.BlockSpec((1, tq, 2 * LANES), lambda i, g, q: (i, q, g)),
                  pl.BlockSpec((1, ncp, LANES), lambda i, g, q: (i, 0, 0)),
                  pl.BlockSpec((1, ncp, LANES), lambda i, g, q: (i, 0, 0)),
                  pl.BlockSpec((4, 1, tq, ncp), lambda i, g, q: (g, q, 0, 0)),
                  pl.BlockSpec((nsp, ncp), lambda i, g, q: (0, 0)),
                  pl.BlockSpec((1, tq, LANES), lambda i, g, q: (i, q, 0))],
        out_specs=(pl.BlockSpec((1, tq, 2 * LANES), lambda i, g, q: (i, q, g)),
                   pl.BlockSpec((1, tq, nsp), lambda i, g, q: (i, q, g))),
        compiler_params=_cparams(("parallel", "parallel", "parallel")),
        name="nsa_compressed",
    )(p3, kc, vc, tab, ovl_t, small3)


def _nsa_dense_kernel(*refs, tq, tk, n_tab, selected, branch):
    if selected:
        (q_ref, k_ref, vt_ref, tab_ref, blk_ref, sel_ref, s_ref, o_ref,
         kaug_ref, sa_ref, sb_ref, m_ref, acc_ref) = refs
    else:
        q_ref, k_ref, vt_ref, tab_ref, s_ref, o_ref, sa_ref, sb_ref, m_ref, acc_ref = refs
    g = pl.program_id(1)
    qi = pl.program_id(2)
    qs = _group_queries(q_ref.at[0], g)
    kr = lax.broadcasted_iota(jnp.int32, (tk, 4 * tq), 0)
    qc = lax.broadcasted_iota(jnp.int32, (tk, 4 * tq), 1) & (tq - 1)
    j0 = (qi * tq) // tk
    g0 = pl.multiple_of(g * VT_ROWS, VT_ROWS)

    if selected:
        @pl.when(qi == 0)
        def _():
            kaug_ref[:, 0:LANES] = k_ref[0]
            kaug_ref[:, LANES:] = blk_ref[...]

        qs = jnp.concatenate([qs, jnp.concatenate([sel_ref[0]] * 4, axis=0)], axis=1)

        def qk(j):
            return _dot_nt(kaug_ref[pl.ds(pl.multiple_of(j * tk, tk), tk), :], qs)
    else:
        def qk(j):
            return _dot_nt(k_ref[0, pl.ds(pl.multiple_of(j * tk, tk), tk), :], qs)

    def consume(j, s, first):
        k0 = pl.multiple_of(j * tk, tk)
        s = s + tab_ref[0, jnp.minimum(qi - j * (tk // tq), n_tab - 1)]
        dist = (qi * tq - j * tk) + qc - kr
        if selected:
            if first:
                s = jnp.where(dist >= 0, s, NEG)
        else:
            s = jnp.where((dist >= 0) & (dist <= NSA_WINDOW - 1), s, NEG)
        p, alpha = _col_softmax(s, m_ref, first)
        vt = vt_ref[0, pl.ds(g0, VT_ROWS), pl.ds(k0, tk)]
        _accumulate(acc_ref, slice(None), _dot(vt, p), alpha)

    first_j = 0 if selected else jnp.maximum(j0 - (-(-(NSA_WINDOW - 1) // tk)), 0)
    _key_tiles(j0, first_j, j0 - first_j, qk, consume, sa_ref, sb_ref)
    o_t = acc_ref[0:HEAD_DIM, :] / acc_ref[HEAD_DIM:HEAD_DIM + 1, :]
    o = jnp.concatenate([o_t[:, h * tq:(h + 1) * tq] for h in range(4)], axis=0).T
    o_ref[0] = (o * _group_gates(s_ref.at[0], g, branch)).astype(BF16)


def _nsa_dense(p3, tab, small3, vt, negsel):
    b, t, _ = p3.shape
    tq = tk = min(NSA_TQ_DENSE, t)
    n_tab = tab.shape[1]
    selected = negsel is not None
    pb_k = PB_KSA if selected else PB_KWA
    in_specs = [pl.BlockSpec((1, tq, 2 * LANES), lambda i, g, q: (i, q, g)),
                pl.BlockSpec((1, t, LANES), lambda i, g, q: (i, 0, pb_k)),
                pl.BlockSpec((1, 2 * VT_ROWS, t), lambda i, g, q: (i, 0, 0)),
                pl.BlockSpec((1, n_tab, tk, 4 * tq), lambda i, g, q: (g, 0, 0, 0),
                             pipeline_mode=pl.Buffered(1))]
    args = [p3, p3, vt, tab]
    scratch = []
    if selected:
        nsp = negsel.shape[2] // 2
        block_of_key = (jnp.arange(t)[:, None] // NSA_SEL_LEN == jnp.arange(nsp)[None, :]).astype(BF16)
        in_specs += [pl.BlockSpec((t, nsp), lambda i, g, q: (0, 0)),
                     pl.BlockSpec((1, tq, nsp), lambda i, g, q: (i, q, g))]
        args += [block_of_key, negsel]
        scratch = [pltpu.VMEM((t, LANES + nsp), BF16)]
    in_specs.append(pl.BlockSpec((1, tq, LANES), lambda i, g, q: (i, q, 0)))
    args.append(small3)
    return pl.pallas_call(
        functools.partial(_nsa_dense_kernel, tq=tq, tk=tk, n_tab=n_tab, selected=selected,
                          branch=1 if selected else 2),
        out_shape=jax.ShapeDtypeStruct((b, t, 4 * LANES), BF16),
        grid=(b, 2, t // tq),
        in_specs=in_specs,
        out_specs=pl.BlockSpec((1, tq, 2 * LANES), lambda i, g, q: (i, q, g)),
        scratch_shapes=scratch + [pltpu.VMEM((tk, 4 * tq), F32), pltpu.VMEM((tk, 4 * tq), F32),
                                  pltpu.VMEM((1, 4 * tq), F32), pltpu.VMEM((VT_ROWS, 4 * tq), F32)],
        compiler_params=_cparams(("parallel", "parallel", "arbitrary")),
        name="nsa_selected" if selected else "nsa_window",
    )(*args)


def _dilated_kernel(q_ref, kp_ref, kc_ref, vtp_ref, vtc_ref, tab_ref, o_ref, lse_ref, *, tq):
    qi = pl.program_id(2)
    kr = lax.broadcasted_iota(jnp.int32, (2 * tq, 2 * tq), 0)
    qc = lax.broadcasted_iota(jnp.int32, (2 * tq, 2 * tq), 1) & (tq - 1)
    dist = tq + qc - kr
    valid = (dist >= 0) & (dist <= tq) & (kr >= jnp.where(qi > 0, 0, tq))
    outs, lses = [], []
    for pb in range(4):
        cols = slice(pb * LANES, (pb + 1) * LANES)
        qs = _pair_queries(q_ref.at[0, :, cols])
        q_both = jnp.concatenate([qs[0], qs[1]], axis=0)
        k2 = jnp.concatenate([kp_ref[0, :, cols], kc_ref[0, :, cols]], axis=0)
        s = jnp.where(valid, _dot_nt(k2, q_both) + tab_ref[pb, 0], NEG)
        m = jnp.max(s, axis=0, keepdims=True)
        p = jnp.exp2(s - m)
        l = jnp.sum(p, axis=0, keepdims=True)
        p = p.astype(BF16)
        lse = (m + jnp.log2(l)) * LN2
        for hh in range(2):
            rows = slice((2 * pb + hh) * HEAD_DIM, (2 * pb + hh + 1) * HEAD_DIM)
            qcols = slice(hh * tq, (hh + 1) * tq)
            vt = jnp.concatenate([vtp_ref[0, 0, rows, :], vtc_ref[0, 0, rows, :]], axis=1)
            outs.append(_dot(vt, p[:, qcols]) / l[:, qcols])
            lses.append(jnp.broadcast_to(lse[:, qcols], (HEAD_DIM, tq)))
    o_ref[0] = jnp.concatenate(outs, axis=0).T.astype(BF16)
    lse_ref[0] = jnp.concatenate(lses, axis=0).T


def _dilated(pd3, tab, dil, tq=DIL_BAND):
    b, t, _ = pd3.shape
    l = t // dil
    qk = pd3[:, :, 0:2 * GW].reshape(b, l, dil * 2 * GW)
    vt = pd3[:, :, 2 * GW:3 * GW].reshape(b, l, dil, GW).transpose(0, 2, 3, 1)

    def prev(q):
        return jnp.maximum(q - 1, 0)

    out_spec = pl.BlockSpec((1, tq, GW), lambda i, r, q: (i, q, r))
    o, lse = pl.pallas_call(
        functools.partial(_dilated_kernel, tq=tq),
        out_shape=(jax.ShapeDtypeStruct((b, l, dil * GW), BF16),
                   jax.ShapeDtypeStruct((b, l, dil * GW), F32)),
        grid=(b, dil, l // tq),
        in_specs=[pl.BlockSpec((1, tq, GW), lambda i, r, q: (i, q, 2 * r)),
                  pl.BlockSpec((1, tq, GW), lambda i, r, q: (i, prev(q), 2 * r + 1)),
                  pl.BlockSpec((1, tq, GW), lambda i, r, q: (i, q, 2 * r + 1)),
                  pl.BlockSpec((1, 1, GW, tq), lambda i, r, q: (i, r, 0, prev(q))),
                  pl.BlockSpec((1, 1, GW, tq), lambda i, r, q: (i, r, 0, q)),
                  pl.BlockSpec((4, 1, 2 * tq, 2 * tq), lambda i, r, q: (0, 0, 0, 0))],
        out_specs=(out_spec, out_spec),
        compiler_params=_cparams(("parallel", "parallel", "parallel")),
        name="dilated_attention_%d" % dil,
    )(qk, qk, qk, vt, vt, tab)
    return o.reshape(b * t, GW), lse.reshape(b * t, GW)


def _out_proj_kernel(x_ref, oc_ref, os_ref, ow_ref, ob_ref, of_ref,
                     d1_ref, d2_ref, d3_ref, l1_ref, l2_ref, l3_ref, w_ref, g_ref, o_ref, cat_ref):
    gw = 4 * LANES
    a = oc_ref[...].astype(F32) + os_ref[...].astype(F32) + ow_ref[...].astype(F32)
    cat_ref[:, 0:gw] = a.astype(BF16)
    cat_ref[:, gw:2 * gw] = ob_ref[...]
    cat_ref[:, 2 * gw:3 * gw] = of_ref[...]
    l1, l2, l3 = l1_ref[...], l2_ref[...], l3_ref[...]
    mx = jnp.maximum(jnp.maximum(l1, l2), l3)
    e1, e2, e3 = jnp.exp(l1 - mx), jnp.exp(l2 - mx), jnp.exp(l3 - mx)
    den = e1 + e2 + e3
    d = ((e1 / den) * d1_ref[...].astype(F32) + (e2 / den) * d2_ref[...].astype(F32)
         + (e3 / den) * d3_ref[...].astype(F32))
    cat_ref[:, 3 * gw:4 * gw] = d.astype(BF16)
    y = _dot(cat_ref[...], w_ref[...])
    y = y * lax.rsqrt(jnp.mean(y * y, axis=-1, keepdims=True) + EPS)
    o_ref[...] = x_ref[...] + y * g_ref[...]


def _out_proj(x2, heads, dils, lses, w_out, g, tm=256):
    m = x2.shape[0]
    gw = 4 * LANES
    row = lambda i: (i, 0)
    const = lambda i: (0, 0)
    in_specs = ([pl.BlockSpec((tm, D_MODEL), row)]
                + [pl.BlockSpec((tm, gw), row)] * 11
                + [pl.BlockSpec((D_MODEL, D_MODEL), const), pl.BlockSpec((1, D_MODEL), const)])
    return pl.pallas_call(
        _out_proj_kernel,
        out_shape=jax.ShapeDtypeStruct((m, D_MODEL), F32),
        grid=(m // tm,),
        in_specs=in_specs,
        out_specs=pl.BlockSpec((tm, D_MODEL), row),
        scratch_shapes=[pltpu.VMEM((tm, D_MODEL), BF16)],
        compiler_params=_cparams(("parallel",)),
        name="out_proj_residual",
    )(x2, *heads, *dils, *lses, w_out, g)


def _ffn_kernel(x_ref, gpre_ref, wu_ref, wd_ref, gpost_ref, o_ref, h_ref, acc_ref):
    f = pl.program_id(1)

    @pl.when(f == 0)
    def _():
        x = x_ref[...]
        y = x * lax.rsqrt(jnp.mean(x * x, axis=-1, keepdims=True) + EPS)
        h_ref[...] = (y * gpre_ref[...]).astype(BF16)
        acc_ref[...] = jnp.zeros_like(acc_ref)

    u = jnp.maximum(_dot(h_ref[...], wu_ref[...]), 0.0)
    acc_ref[...] += _dot((u * u).astype(BF16), wd_ref[...])

    @pl.when(f == pl.num_programs(1) - 1)
    def _():
        y = acc_ref[...]
        y = y * lax.rsqrt(jnp.mean(y * y, axis=-1, keepdims=True) + EPS)
        o_ref[...] = x_ref[...] + y * gpost_ref[...]


def _ffn(x2, g_pre, w_up, w_down, g_post, tm=512, tf=1024):
    m = x2.shape[0]
    return pl.pallas_call(
        _ffn_kernel,
        out_shape=jax.ShapeDtypeStruct((m, D_MODEL), F32),
        grid=(m // tm, D_FF // tf),
        in_specs=[pl.BlockSpec((tm, D_MODEL), lambda i, f: (i, 0)),
                  pl.BlockSpec((1, D_MODEL), lambda i, f: (0, 0)),
                  pl.BlockSpec((D_MODEL, tf), lambda i, f: (0, f)),
                  pl.BlockSpec((tf, D_MODEL), lambda i, f: (f, 0)),
                  pl.BlockSpec((1, D_MODEL), lambda i, f: (0, 0))],
        out_specs=pl.BlockSpec((tm, D_MODEL), lambda i, f: (i, 0)),
        scratch_shapes=[pltpu.VMEM((tm, D_MODEL), BF16), pltpu.VMEM((tm, D_MODEL), F32)],
        compiler_params=_cparams(("parallel", "arbitrary")),
        name="ffn_residual",
    )(x2, g_pre, w_up, w_down, g_post)


def _split_w_in(w):
    gw, kw = 4 * LANES, LANES
    o = 0
    cols = {}
    for name, width in (("qa", gw), ("kca", kw), ("vca", kw), ("ksa", kw), ("vsa", kw), ("kwa", kw),
                        ("vwa", kw), ("ga", 24), ("qb", gw), ("kb", gw), ("vb", gw), ("qc", gw),
                        ("kc", gw), ("vc", gw), ("fc", 8), ("qd", gw), ("kd", gw), ("vd", gw)):
        cols[name] = w[:, o:o + width] * (Q_FOLD if name[0] == "q" else 1.0)
        o += width
    main = jnp.concatenate([cols[n] for n in ("qa", "ksa", "vsa", "kwa", "vwa", "qb", "kb", "vb",
                                              "qc", "kc", "vc", "qd", "kd", "vd")], axis=1)
    pad = jnp.zeros((w.shape[0], LANES - 32), w.dtype)
    small = jnp.concatenate([cols["ga"], cols["fc"], pad, cols["kca"], cols["vca"]], axis=1)
    return main.astype(BF16), small.astype(BF16)


def _compress_weights(pe, w1, w2):
    half = NSA_CMP_LEN // 2
    hid = NSA_CMP_HIDDEN
    w1r = w1.reshape(2, half, HEAD_DIM, hid)
    z = jnp.zeros_like(w1r)
    per_group = jnp.stack([jnp.concatenate([w1r, z], axis=-1), jnp.concatenate([z, w1r], axis=-1)], axis=2)
    w_halves = per_group.reshape(2, half * 2 * HEAD_DIM, 2 * hid).astype(BF16)
    pe_r = jnp.broadcast_to(pe.reshape(2, half, 1, HEAD_DIM), (2, half, 2, HEAD_DIM)).reshape(2, half * LANES)
    z2 = jnp.zeros_like(w2)
    w2g = jnp.concatenate([jnp.concatenate([w2, z2], axis=1), jnp.concatenate([z2, w2], axis=1)], axis=0)
    return pe_r, w_halves[0], w_halves[1], w2g.astype(BF16)


def _bias_tables(rel_bias, t):
    hm = 8
    tq_a = NSA_TQ_CMP
    tq_d = tk_d = min(NSA_TQ_DENSE, t)
    n_a = min(_n_toeplitz_tiles(tq_d, tk_d), t // tq_d)
    tab_a = _bias_tiles(rel_bias, _toeplitz_idx_t(n_a, tk_d, tq_d, tq_d), 0, 2, 4)
    ncp = t // NSA_CMP_STRIDE
    qi = jnp.arange(t // tq_a, dtype=jnp.int32)[:, None, None]
    r = jnp.arange(tq_a, dtype=jnp.int32)[None, :, None]
    c = jnp.arange(ncp, dtype=jnp.int32)[None, None, :]
    idx_c = _t5_bucket(qi * tq_a + r - (c * NSA_CMP_STRIDE + NSA_CMP_LEN - 1))
    tab_c = _bias_tiles(rel_bias, idx_c, 0, hm)
    tq_b = min(MOBA_TQ, t)
    n_b = min(_n_toeplitz_tiles(tq_b, tq_b), t // tq_b)
    tab_b = _bias_tiles(rel_bias, _toeplitz_idx_t(n_b, tq_b, tq_b, tq_b), hm, 4, 2)
    kr = jnp.arange(2 * DIL_BAND, dtype=jnp.int32)[None, :, None]
    qc = jnp.arange(DIL_BAND, dtype=jnp.int32)[None, None, :]
    tabs_d = [_bias_tiles(rel_bias, _t5_bucket((DIL_BAND + qc - kr) * dil), 2 * hm, 4, 2)
              for _, dil in LONGNET_PATTERNS]
    return tab_a, tab_c, tab_b, tabs_d


def _overlap_t(t):
    ncp = t // NSA_CMP_STRIDE
    nsp = max(t // NSA_SEL_LEN, 8)
    c_start = jnp.arange(ncp)[None, :] * NSA_CMP_STRIDE
    s_start = jnp.arange(nsp)[:, None] * NSA_SEL_LEN
    ovl = (c_start < s_start + NSA_SEL_LEN) & (c_start + NSA_CMP_LEN - 1 >= s_start)
    ovl = ovl & (jnp.arange(ncp)[None, :] < ncp - 1) & (jnp.arange(nsp)[:, None] < t // NSA_SEL_LEN)
    return ovl.astype(F32)


def _transposed(p3, pb, n_blocks):
    b, t, _ = p3.shape
    vt = p3[:, :, pb * LANES:(pb + n_blocks) * LANES].transpose(0, 2, 1).reshape(b, 2 * n_blocks, HEAD_DIM, t)
    ones = jnp.ones((b, 2 * n_blocks, VT_ROWS - HEAD_DIM, t), BF16)
    return jnp.concatenate([vt, ones], axis=2).reshape(b, 2 * n_blocks * VT_ROWS, t)


def _mixer_layer(x2, b, t, w_main, w_small, g_pre, g_post, w_out, cmp_w_k, cmp_w_v, fox_bias, tabs, ovl_t):
    tab_a, tab_c, tab_b, tabs_d = tabs
    m = b * t
    p2, pd2, small, kca, vca = _proj(x2, g_pre, w_main, w_small)
    p3 = p2.reshape(b, t, P_WIDTH)
    pd3 = pd2.reshape(b, t, PD_WIDTH)
    small3 = small.reshape(b, t, LANES)
    row_w = NSA_CMP_STRIDE * LANES
    kc = _nsa_compress(kca.reshape(b, t // NSA_CMP_STRIDE, row_w), *cmp_w_k)
    vc = _nsa_compress(vca.reshape(b, t // NSA_CMP_STRIDE, row_w), *cmp_w_v)
    o_cmp, negsel = _nsa_cmp(p3, kc, vc, tab_c, ovl_t, small3)
    o_sel = _nsa_dense(p3, tab_a, small3, _transposed(p3, PB_VSA, 1), negsel)
    o_win = _nsa_dense(p3, tab_a, small3, _transposed(p3, PB_VWA, 1), None)
    o_b = _moba(p3, tab_b, _transposed(p3, PB_VB, 4))
    fox_bias_row = jnp.zeros((1, LANES), F32).at[0, FOX_HI:FOX_HI + 8].set(fox_bias)
    o_f = _fox(p3, _fox_cumsum(small3, fox_bias_row), _transposed(p3, PB_VC, 4))
    dil = [_dilated(pd3, tab, d) for tab, (_, d) in zip(tabs_d, LONGNET_PATTERNS)]
    heads = [a.reshape(m, 4 * LANES) for a in (o_cmp, o_sel, o_win, o_b, o_f)]
    return _out_proj(x2, heads, [d[0] for d in dil], [d[1] for d in dil], w_out, g_post)


def kernel(x, w_in, w_out, g_mix_pre, g_mix_post, g_mlp_pre, g_mlp_post, w_up, w_down, cmp_pe, phik_w1, phik_w2, phiv_w1, phiv_w2, fox_bias, rel_bias):
    b, t, d = x.shape
    depth = w_in.shape[0]
    tabs = _bias_tables(rel_bias, t)
    ovl_t = _overlap_t(t)
    x2 = x.reshape(b * t, d)
    for l in range(depth):
        w_main, w_small = _split_w_in(w_in[l])
        x2 = _mixer_layer(
            x2, b, t, w_main, w_small, g_mix_pre[l][None], g_mix_post[l][None], w_out[l].astype(BF16),
            _compress_weights(cmp_pe[l], phik_w1[l], phik_w2[l]),
            _compress_weights(cmp_pe[l], phiv_w1[l], phiv_w2[l]),
            fox_bias[l], tabs, ovl_t)
        x2 = _ffn(x2, g_mlp_pre[l][None], w_up[l].astype(BF16), w_down[l].astype(BF16), g_mlp_post[l][None])
    return x2.reshape(b, t, d)
```

```python
import functools
import math

import jax
import jax.numpy as jnp
from jax import lax
from jax.experimental import pallas as pl
from jax.experimental.pallas import tpu as pltpu

F32 = jnp.float32
BF16 = jnp.bfloat16

D_MODEL = 2048
D_FF = 4 * D_MODEL
HEAD_DIM = 64
LANES = 128
EPS = 1e-6
NEG = -1e30
BIG = 1e30
SCALE = HEAD_DIM ** -0.5
LOG2E = math.log2(math.e)
LN2 = math.log(2.0)
Q_FOLD = SCALE * LOG2E

N_BUCKETS = 32
MAX_EXACT = 16
MAX_DISTANCE = 4096
LAST_BUCKET_DIST = 2897

NSA_CMP_LEN = 32
NSA_CMP_STRIDE = 16
NSA_CMP_HIDDEN = 256
NSA_SEL_LEN = 64
NSA_TOP_N = 16
NSA_WINDOW = 512
MOBA_BLOCK = 256
MOBA_TOPK = 3
LONGNET_PATTERNS = ((128, 1), (512, 4), (2048, 16))
DIL_BAND = 128

FOX_TQ = 512
MOBA_TQ = 512
NSA_TQ_CMP = 256
NSA_TQ_DENSE = 256

P_WIDTH = 4096
PB_QA, PB_KSA, PB_VSA, PB_KWA, PB_VWA = 0, 4, 5, 6, 7
PB_QB, PB_KB, PB_VB = 8, 12, 16
PB_QC, PB_KC, PB_VC = 20, 24, 28
PD_WIDTH = 1536
GW = 4 * LANES
VT_ROWS = HEAD_DIM + 16

FOX_HI, FOX_MID, FOX_LO = 24, 32, 40

VMEM_LIMIT = 56 * 1024 * 1024


def _cparams(sem):
    return pltpu.CompilerParams(dimension_semantics=sem, vmem_limit_bytes=VMEM_LIMIT)


def _dot(a, b):
    return jnp.dot(a, b, preferred_element_type=F32)


def _dot_nt(a, b):
    return lax.dot_general(a, b, (((1,), (1,)), ((), ())), preferred_element_type=F32)


def _dot_hi(a, b):
    return jnp.dot(a, b, preferred_element_type=F32, precision=lax.Precision.HIGHEST)


def _spread_lanes(w, spread):
    hi = w.astype(BF16)
    lo = (w - hi.astype(F32)).astype(BF16)
    return _dot(jnp.concatenate([hi, lo], axis=1), spread)


def _dot_nt_hi(a, b):
    return lax.dot_general(a, b, (((1,), (1,)), ((), ())), preferred_element_type=F32,
                           precision=lax.Precision.HIGHEST)


def _t5_bucket(dist):
    dist = jnp.maximum(dist, 0)
    rel = jnp.log(jnp.maximum(dist, 1).astype(jnp.float32) / MAX_EXACT) / math.log(MAX_DISTANCE / MAX_EXACT)
    large = jnp.minimum(MAX_EXACT + (rel * (N_BUCKETS - MAX_EXACT)).astype(jnp.int32), N_BUCKETS - 1)
    return jnp.where(dist < MAX_EXACT, dist, large)


def _bias_tile_kernel(tab_ref, idx_ref, o_ref, *, head0, hpg, cols):
    grp = pl.program_id(0)
    idx = idx_ref[0]
    for hh in range(hpg):
        h = head0 + grp * hpg + hh
        acc = jnp.zeros(idx.shape, F32)
        for b in range(N_BUCKETS):
            acc = jnp.where(idx == b, tab_ref[b, h], acc)
        o_ref[0, 0, :, hh * cols:(hh + 1) * cols] = acc * LOG2E


def _bias_tiles(rel_bias, idx, head0, n_groups, hpg=1):
    n, r, c = idx.shape
    return pl.pallas_call(
        functools.partial(_bias_tile_kernel, head0=head0, hpg=hpg, cols=c),
        out_shape=jax.ShapeDtypeStruct((n_groups, n, r, hpg * c), F32),
        grid=(n_groups, n),
        in_specs=[pl.BlockSpec(memory_space=pltpu.SMEM),
                  pl.BlockSpec((1, r, c), lambda g, i: (i, 0, 0))],
        out_specs=pl.BlockSpec((1, 1, r, hpg * c), lambda g, i: (g, i, 0, 0)),
        compiler_params=_cparams(("parallel", "parallel")),
        name="bias_tiles",
    )(rel_bias, idx)


def _toeplitz_idx_t(n_tiles, krows, qcols, q_stride):
    nd = jnp.arange(n_tiles, dtype=jnp.int32)[:, None, None]
    r = jnp.arange(krows, dtype=jnp.int32)[None, :, None]
    c = jnp.arange(qcols, dtype=jnp.int32)[None, None, :]
    return _t5_bucket(nd * q_stride + c - r)


def _n_toeplitz_tiles(q_stride, keys):
    return -(-(LAST_BUCKET_DIST + keys - 1) // q_stride) + 1


def _proj_kernel(x_ref, g_ref, w_ref, ws_ref, p_ref, pd_ref, s_ref, kc_ref, vc_ref, h_ref, *, n_main):
    j = pl.program_id(1)

    @pl.when(j == 0)
    def _():
        x = x_ref[...]
        y = x * lax.rsqrt(jnp.mean(x * x, axis=-1, keepdims=True) + EPS)
        hb = (y * g_ref[...]).astype(BF16)
        h_ref[...] = hb
        small = _dot(hb, ws_ref[...])
        s_ref[...] = small[:, 0:LANES]
        kc_ref[...] = small[:, LANES:2 * LANES]
        vc_ref[...] = small[:, 2 * LANES:3 * LANES]

    y = _dot(h_ref[...], w_ref[...]).astype(BF16)

    @pl.when(j < n_main)
    def _():
        p_ref[...] = y

    @pl.when(j >= n_main)
    def _():
        pd_ref[...] = y


def _proj(x2, g, w_main, w_small, tm=1024, tn=512):
    m = x2.shape[0]
    tm = min(tm, m)
    n_main = P_WIDTH // tn
    return pl.pallas_call(
        functools.partial(_proj_kernel, n_main=n_main),
        out_shape=(jax.ShapeDtypeStruct((m, P_WIDTH), BF16),
                   jax.ShapeDtypeStruct((m, PD_WIDTH), BF16),
                   jax.ShapeDtypeStruct((m, LANES), F32),
                   jax.ShapeDtypeStruct((m, LANES), F32),
                   jax.ShapeDtypeStruct((m, LANES), F32)),
        grid=(m // tm, (P_WIDTH + PD_WIDTH) // tn),
        in_specs=[pl.BlockSpec((tm, D_MODEL), lambda i, j: (i, 0)),
                  pl.BlockSpec((1, D_MODEL), lambda i, j: (0, 0)),
                  pl.BlockSpec((D_MODEL, tn), lambda i, j: (0, j)),
                  pl.BlockSpec((D_MODEL, 3 * LANES), lambda i, j: (0, 0))],
        out_specs=(pl.BlockSpec((tm, tn), lambda i, j: (i, jnp.minimum(j, n_main - 1))),
                   pl.BlockSpec((tm, tn), lambda i, j: (i, jnp.maximum(j - n_main, 0))),
                   pl.BlockSpec((tm, LANES), lambda i, j: (i, 0)),
                   pl.BlockSpec((tm, LANES), lambda i, j: (i, 0)),
                   pl.BlockSpec((tm, LANES), lambda i, j: (i, 0))),
        scratch_shapes=[pltpu.VMEM((tm, D_MODEL), BF16)],
        compiler_params=_cparams(("parallel", "arbitrary")),
        name="rms_in_proj",
    )(x2, g, w_main, w_small)


def _lane_lo(shape):
    return lax.broadcasted_iota(jnp.int32, shape, 1) < HEAD_DIM


def _pair_queries(q_ref):
    q = q_ref[...]
    lo = _lane_lo(q.shape)
    zero = jnp.zeros_like(q)
    return jnp.where(lo, q, zero), jnp.where(lo, zero, q)


def _col_softmax(s, m_ref, first):
    if first:
        m = jnp.max(s, axis=0, keepdims=True)
        m_ref[...] = m
        return jnp.exp2(s - m).astype(BF16), None
    m_prev = m_ref[...]
    m_new = jnp.maximum(m_prev, jnp.max(s, axis=0, keepdims=True))
    m_ref[...] = m_new
    return jnp.exp2(s - m_new).astype(BF16), jnp.exp2(m_prev - m_new)


def _accumulate(acc_ref, cols, pv, alpha):
    if alpha is None:
        acc_ref[:, cols] = pv
    else:
        acc_ref[:, cols] = alpha[:, cols] * acc_ref[:, cols] + pv


def _key_tiles(j_diag, j_first, n_past, qk, consume, sa_ref, sb_ref):
    last = j_first + jnp.maximum(n_past - 1, 0)
    sa_ref[...] = qk(j_first)
    consume(j_diag, qk(j_diag), True)

    def run(base, count):
        for t in range(count):
            cur, nxt = (sa_ref, sb_ref) if t % 2 == 0 else (sb_ref, sa_ref)
            nxt[...] = qk(jnp.minimum(base + t + 1, last))
            consume(base + t, cur[...], False)

    quads = n_past // 4
    lax.fori_loop(0, quads, lambda i, c: (run(j_first + 4 * i, 4), c)[1], 0)
    rest = n_past - 4 * quads

    @pl.when(rest >= 2)
    def _():
        run(j_first + 4 * quads, 2)

    @pl.when(rest % 2 == 1)
    def _():
        consume(j_first + n_past - 1, sa_ref[...], False)


def _top_rows(v, k):
    n = v.shape[0]
    rows = lax.broadcasted_iota(jnp.int32, v.shape, 0)
    chosen = jnp.zeros(v.shape, F32)
    for _ in range(k):
        top = jnp.max(v, axis=0, keepdims=True)
        first = jnp.min(jnp.where(v == top, rows, n), axis=0, keepdims=True)
        pick = rows == first
        chosen = jnp.where(pick, 1.0, chosen)
        v = jnp.where(pick, -jnp.inf, v)
    return chosen


def _top_rows_bisect(v, k):
    n = v.shape[0]
    key = pltpu.bitcast(v, jnp.int32)
    thr = jnp.zeros((1, v.shape[1]), jnp.int32)
    for bit in range(30, -1, -1):
        cand = thr | (1 << bit)
        cnt = jnp.sum(jnp.where(key >= cand, 1.0, 0.0), axis=0, keepdims=True)
        thr = jnp.where(cnt >= k, cand, thr)
    above = key > thr
    tie = key == thr
    n_above = jnp.sum(jnp.where(above, 1.0, 0.0), axis=0, keepdims=True)
    ri = lax.broadcasted_iota(jnp.int32, (n, n), 0)
    ci = lax.broadcasted_iota(jnp.int32, (n, n), 1)
    before = jnp.where(ci < ri, 1.0, 0.0).astype(BF16)
    ties_before = _dot(before, jnp.where(tie, 1.0, 0.0).astype(BF16))
    return above | (tie & (ties_before < k - n_above))


def _pair_finish(acc_ref, o_ref, tq):
    o_t = acc_ref[0:HEAD_DIM, :] / acc_ref[HEAD_DIM:HEAD_DIM + 1, :]
    o_ref[0] = jnp.concatenate([o_t[:, 0:tq], o_t[:, tq:2 * tq]], axis=0).T.astype(BF16)


def _fox_cumsum_kernel(s_ref, fb_ref, c_ref, carry_ref, *, tc):
    @pl.when(pl.program_id(1) == 0)
    def _():
        carry_ref[...] = jnp.zeros_like(carry_ref)

    lane = lax.broadcasted_iota(jnp.int32, (tc, LANES), 1)
    z = s_ref[0] + fb_ref[...]
    logf = -(jnp.maximum(-z, 0.0) + jnp.log(1.0 + jnp.exp(-jnp.abs(z))))
    logf = jnp.where((lane >= FOX_HI) & (lane < FOX_MID), logf, 0.0)
    ri = lax.broadcasted_iota(jnp.int32, (tc, tc), 0)
    ci = lax.broadcasted_iota(jnp.int32, (tc, tc), 1)
    lower = jnp.where(ri >= ci, 1.0, 0.0).astype(F32)
    c = _dot_hi(lower, logf) + carry_ref[...]
    carry_ref[...] = c[tc - 1:tc, :]
    c2 = c * LOG2E
    hi = c2.astype(BF16).astype(F32)
    rest = c2 - hi
    mid = rest.astype(BF16).astype(F32)
    lo = rest - mid
    pieces = hi + pltpu.roll(mid, FOX_MID - FOX_HI, axis=1) + pltpu.roll(lo, FOX_LO - FOX_HI, axis=1)
    c_ref[0] = pieces.astype(BF16)


def _fox_cumsum(small3, fox_bias_row, tc=512):
    b, t, _ = small3.shape
    tc = min(tc, t)
    return pl.pallas_call(
        functools.partial(_fox_cumsum_kernel, tc=tc),
        out_shape=jax.ShapeDtypeStruct((b, t, LANES), BF16),
        grid=(b, t // tc),
        in_specs=[pl.BlockSpec((1, tc, LANES), lambda i, j: (i, j, 0)),
                  pl.BlockSpec((1, LANES), lambda i, j: (0, 0))],
        out_specs=pl.BlockSpec((1, tc, LANES), lambda i, j: (i, j, 0)),
        scratch_shapes=[pltpu.VMEM((1, LANES), F32)],
        compiler_params=_cparams(("parallel", "arbitrary")),
        name="fox_cumsum",
    )(small3, fox_bias_row)


def _fox_kernel(q_ref, k_ref, c_ref, vt_ref, o_ref, kaug_ref, sa_ref, sb_ref, m_ref, acc_ref, *, tq):
    hp = pl.program_id(1)
    qi = pl.program_id(2)

    @pl.when(qi == 0)
    def _():
        kaug_ref[:, 0:LANES] = k_ref[0]
        kaug_ref[:, LANES:2 * LANES] = c_ref[0]

    qs = _pair_queries(q_ref.at[0])
    lane = lax.broadcasted_iota(jnp.int32, (tq, LANES), 1)
    rows = []
    for hh in range(2):
        h = 2 * hp + hh
        pick = (lane == FOX_HI + h) | (lane == FOX_MID + h) | (lane == FOX_LO + h)
        rows.append(jnp.concatenate([qs[hh], jnp.where(pick, -1.0, 0.0).astype(BF16)], axis=1))
    q_aug = jnp.concatenate(rows, axis=0)
    kr = lax.broadcasted_iota(jnp.int32, (tq, 2 * tq), 0)
    qc = lax.broadcasted_iota(jnp.int32, (tq, 2 * tq), 1) & (tq - 1)

    def qk(j):
        return _dot_nt(kaug_ref[pl.ds(pl.multiple_of(j * tq, tq), tq), :], q_aug)

    def consume(j, s, first):
        k0 = pl.multiple_of(j * tq, tq)
        if first:
            s = jnp.where(kr <= qc, s, NEG)
        p, alpha = _col_softmax(s, m_ref, first)
        for hh in range(2):
            cols = slice(hh * tq, (hh + 1) * tq)
            vt = vt_ref[0, hh * VT_ROWS:(hh + 1) * VT_ROWS, pl.ds(k0, tq)]
            _accumulate(acc_ref, cols, _dot(vt, p[:, cols]), alpha)

    _key_tiles(qi, 0, qi, qk, consume, sa_ref, sb_ref)
    _pair_finish(acc_ref, o_ref, tq)


def _fox(p3, c_aug, vt, tq=FOX_TQ):
    b, t, _ = p3.shape
    tq = min(tq, t)
    return pl.pallas_call(
        functools.partial(_fox_kernel, tq=tq),
        out_shape=jax.ShapeDtypeStruct((b, t, 4 * LANES), BF16),
        grid=(b, 4, t // tq),
        in_specs=[pl.BlockSpec((1, tq, LANES), lambda i, h, q: (i, q, PB_QC + h)),
                  pl.BlockSpec((1, t, LANES), lambda i, h, q: (i, 0, PB_KC + h)),
                  pl.BlockSpec((1, t, LANES), lambda i, h, q: (i, 0, 0)),
                  pl.BlockSpec((1, 2 * VT_ROWS, t), lambda i, h, q: (i, h, 0))],
        out_specs=pl.BlockSpec((1, tq, LANES), lambda i, h, q: (i, q, h)),
        scratch_shapes=[pltpu.VMEM((t, 2 * LANES), BF16),
                        pltpu.VMEM((tq, 2 * tq), F32), pltpu.VMEM((tq, 2 * tq), F32),
                        pltpu.VMEM((1, 2 * tq), F32), pltpu.VMEM((VT_ROWS, 2 * tq), F32)],
        compiler_params=_cparams(("parallel", "parallel", "arbitrary")),
        name="fox_attention",
    )(p3, p3, c_aug, vt)


def _moba_kernel(q_ref, k_ref, vt_ref, tab_ref, o_ref, km_ref, kaug_ref, sa_ref, sb_ref, m_ref, acc_ref,
                 *, nkb, nkp, n_tab, tq):
    blk = MOBA_BLOCK
    qi = pl.program_id(2)

    @pl.when(qi == 0)
    def _():
        km_ref[...] = jnp.zeros_like(km_ref)
        lane = lax.broadcasted_iota(jnp.int32, (blk, LANES), 1)
        for n in range(nkb):
            kb = k_ref[0, n * blk:(n + 1) * blk, :]
            km_ref[n:n + 1, :] = jnp.mean(kb.astype(F32), axis=0, keepdims=True)
            kaug_ref[n * blk:(n + 1) * blk, 0:LANES] = kb
            kaug_ref[n * blk:(n + 1) * blk, LANES:2 * LANES] = jnp.where(lane == n, 1.0, 0.0).astype(BF16)

    qs = _pair_queries(q_ref.at[0])
    q_both = jnp.concatenate([qs[0], qs[1]], axis=0)
    gate = _dot_nt_hi(km_ref[...], q_both.astype(F32))
    n = lax.broadcasted_iota(jnp.int32, gate.shape, 0)
    q_pos = lax.broadcasted_iota(jnp.int32, gate.shape, 1) & (tq - 1)
    own = qi * (tq // blk) + q_pos // blk
    past = n < own
    gate = jnp.where(past, gate, NEG)
    allowed = ((_top_rows(gate, MOBA_TOPK) > 0.5) & past) | (n == own)
    negsel = jnp.where(allowed, 0.0, NEG).astype(F32)
    if nkp < LANES:
        negsel = jnp.concatenate([negsel, jnp.zeros((LANES - nkp, 2 * tq), F32)], axis=0)
    q_aug = jnp.concatenate([q_both, negsel.T.astype(BF16)], axis=1)

    kr = lax.broadcasted_iota(jnp.int32, (tq, 2 * tq), 0)
    qc = lax.broadcasted_iota(jnp.int32, (tq, 2 * tq), 1) & (tq - 1)

    def qk(j):
        return _dot_nt(kaug_ref[pl.ds(pl.multiple_of(j * tq, tq), tq), :], q_aug)

    def consume(j, s, first):
        k0 = pl.multiple_of(j * tq, tq)
        s = s + tab_ref[0, jnp.minimum(qi - j, n_tab - 1)]
        if first:
            s = jnp.where(kr <= qc, s, NEG)
        p, alpha = _col_softmax(s, m_ref, first)
        for hh in range(2):
            cols = slice(hh * tq, (hh + 1) * tq)
            vt = vt_ref[0, hh * VT_ROWS:(hh + 1) * VT_ROWS, pl.ds(k0, tq)]
            _accumulate(acc_ref, cols, _dot(vt, p[:, cols]), alpha)

    _key_tiles(qi, 0, qi, qk, consume, sa_ref, sb_ref)
    _pair_finish(acc_ref, o_ref, tq)


def _moba(p3, tab, vt):
    b, t, _ = p3.shape
    tq = tab.shape[2]
    nkb = t // MOBA_BLOCK
    nkp = -(-nkb // 8) * 8
    n_tab = tab.shape[1]
    return pl.pallas_call(
        functools.partial(_moba_kernel, nkb=nkb, nkp=nkp, n_tab=n_tab, tq=tq),
        out_shape=jax.ShapeDtypeStruct((b, t, 4 * LANES), BF16),
        grid=(b, 4, t // tq),
        in_specs=[pl.BlockSpec((1, tq, LANES), lambda i, h, q: (i, q, PB_QB + h)),
                  pl.BlockSpec((1, t, LANES), lambda i, h, q: (i, 0, PB_KB + h)),
                  pl.BlockSpec((1, 2 * VT_ROWS, t), lambda i, h, q: (i, h, 0)),
                  pl.BlockSpec((1, n_tab, tq, 2 * tq), lambda i, h, q: (h, 0, 0, 0),
                               pipeline_mode=pl.Buffered(1))],
        out_specs=pl.BlockSpec((1, tq, LANES), lambda i, h, q: (i, q, h)),
        scratch_shapes=[pltpu.VMEM((nkp, LANES), F32), pltpu.VMEM((t, 2 * LANES), BF16),
                        pltpu.VMEM((tq, 2 * tq), F32), pltpu.VMEM((tq, 2 * tq), F32),
                        pltpu.VMEM((1, 2 * tq), F32), pltpu.VMEM((VT_ROWS, 2 * tq), F32)],
        compiler_params=_cparams(("parallel", "parallel", "arbitrary")),
        name="moba_attention",
    )(p3, p3, vt, tab)


def _gelu_tanh(x):
    return 0.5 * x * (1.0 + jnp.tanh(math.sqrt(2.0 / math.pi) * (x + 0.044715 * (x * x * x))))


def _nsa_compress_kernel(r_ref, pe_ref, wlo_ref, whi_ref, w2_ref, o_ref, *, nr):
    r = r_ref[0]
    pe = pe_ref[...]
    a = _dot((r + pe[0:1, :]).astype(BF16), wlo_ref[...])
    b = _dot((r + pe[1:2, :]).astype(BF16), whi_ref[...])
    hid = _gelu_tanh(a + pltpu.roll(b, nr - 1, axis=0))
    o_ref[0] = _dot(hid.astype(BF16), w2_ref[...]).astype(BF16)


def _nsa_compress(r3, pe2, wlo, whi, w2):
    b, nr, w = r3.shape
    return pl.pallas_call(
        functools.partial(_nsa_compress_kernel, nr=nr),
        out_shape=jax.ShapeDtypeStruct((b, nr, LANES), BF16),
        grid=(b,),
        in_specs=[pl.BlockSpec((1, nr, w), lambda i: (i, 0, 0)),
                  pl.BlockSpec((2, w), lambda i: (0, 0)),
                  pl.BlockSpec((w, 2 * NSA_CMP_HIDDEN), lambda i: (0, 0)),
                  pl.BlockSpec((w, 2 * NSA_CMP_HIDDEN), lambda i: (0, 0)),
                  pl.BlockSpec((2 * NSA_CMP_HIDDEN, LANES), lambda i: (0, 0))],
        out_specs=pl.BlockSpec((1, nr, LANES), lambda i: (i, 0, 0)),
        compiler_params=_cparams(("parallel",)),
        name="nsa_compress",
    )(r3, pe2, wlo, whi, w2)


def _group_queries(q_ref, g):
    half = lax.broadcasted_iota(jnp.int32, (q_ref.shape[0], LANES), 1) // HEAD_DIM
    lo = half == 0
    in_group = half == g
    out = []
    for pb in range(2):
        q = q_ref[:, pb * LANES:(pb + 1) * LANES].astype(F32)
        qr = pltpu.roll(q, HEAD_DIM, axis=1)
        for dup in (jnp.where(lo, q, qr), jnp.where(lo, qr, q)):
            out.append(jnp.where(in_group, dup, 0.0).astype(BF16))
    return jnp.concatenate(out, axis=0)


def _group_outputs(acc, g, tq):
    half = lax.broadcasted_iota(jnp.int32, (tq, LANES), 1) // HEAD_DIM
    lo = half == 0
    in_group = half == g
    dups = []
    for h in range(4):
        a = acc[h * tq:(h + 1) * tq, :]
        dups.append(jnp.where(in_group, a, pltpu.roll(a, HEAD_DIM, axis=1)))
    return jnp.concatenate([jnp.where(lo, dups[0], dups[1]), jnp.where(lo, dups[2], dups[3])], axis=1)


def _group_gates(s_ref, g, branch):
    sig = 1.0 / (1.0 + jnp.exp(-s_ref[...]))
    col = lax.broadcasted_iota(jnp.int32, (2 * LANES, 2 * LANES), 0) & (LANES - 1)
    lane = lax.broadcasted_iota(jnp.int32, (2 * LANES, 2 * LANES), 1)
    spread = jnp.where(col == branch * 8 + g * 4 + lane // HEAD_DIM, 1.0, 0.0).astype(BF16)
    return _spread_lanes(sig, spread)


def _nsa_cmp_kernel(q_ref, kc_ref, vc_ref, tab_ref, ovl_ref, s_ref, o_ref, sel_ref, *, tq, ncp, nsp):
    g = pl.program_id(1)
    qi = pl.program_id(2)
    qs = _group_queries(q_ref.at[0], g)
    s = _dot_nt(qs, kc_ref[0]) + tab_ref[...].reshape(4 * tq, ncp)
    t = qi * tq + (lax.broadcasted_iota(jnp.int32, (4 * tq, ncp), 0) & (tq - 1))
    c = lax.broadcasted_iota(jnp.int32, (4 * tq, ncp), 1)
    ok = (t - (c * NSA_CMP_STRIDE + NSA_CMP_LEN - 1) >= 0) & (c < ncp - 1)
    s = jnp.where(ok, s, NEG)
    e = jnp.exp2(s - jnp.max(s, axis=-1, keepdims=True))
    p = jnp.where(ok, e / jnp.sum(e, axis=-1, keepdims=True), 0.0)
    o = _dot(p.astype(BF16), vc_ref[0])
    gate = _group_gates(s_ref.at[0], g, 0)
    o_ref[0] = (_group_outputs(o, g, tq) * gate).astype(BF16)

    psum = p[0:tq] + p[tq:2 * tq] + p[2 * tq:3 * tq] + p[3 * tq:4 * tq]
    imp = _dot_nt_hi(ovl_ref[...], psum)
    n = lax.broadcasted_iota(jnp.int32, (nsp, tq), 0)
    cur = (qi * tq + lax.broadcasted_iota(jnp.int32, (nsp, tq), 1)) // NSA_SEL_LEN
    forced = (n == 0) | (n == cur) | (n == cur - 1)
    valid = n <= cur
    imp = jnp.where(forced, BIG, jnp.where(valid, jnp.abs(imp), NEG))
    chosen = _top_rows_bisect(imp, NSA_TOP_N) & valid
    sel_ref[0] = jnp.where(chosen, 0.0, NEG).astype(F32).T.astype(BF16)


def _nsa_cmp(p3, kc, vc, tab, ovl_t, small3, tq=NSA_TQ_CMP):
    b, t, _ = p3.shape
    ncp = kc.shape[1]
    nsp = ovl_t.shape[0]
    return pl.pallas_call(
        functools.partial(_nsa_cmp_kernel, tq=tq, ncp=ncp, nsp=nsp),
        out_shape=(jax.ShapeDtypeStruct((b, t, 4 * LANES), BF16),
                   jax.ShapeDtypeStruct((b, t, 2 * nsp), BF16)),
        grid=(b, 2, t // tq),
        in_specs=[pl.BlockSpec((1, tq, 2 * LANES), lambda i, g, q: (i, q, g)),
                  pl.BlockSpec((1, ncp, LANES), lambda i, g, q: (i, 0, 0)),
                  pl.BlockSpec((1, ncp, LANES), lambda i, g, q: (i, 0, 0)),
                  pl.BlockSpec((4, 1, tq, ncp), lambda i, g, q: (g, q, 0, 0)),
                  pl.BlockSpec((nsp, ncp), lambda i, g, q: (0, 0)),
                  pl.BlockSpec((1, tq, LANES), lambda i, g, q: (i, q, 0))],
        out_specs=(pl.BlockSpec((1, tq, 2 * LANES), lambda i, g, q: (i, q, g)),
                   pl.BlockSpec((1, tq, nsp), lambda i, g, q: (i, q, g))),
        compiler_params=_cparams(("parallel", "parallel", "parallel")),
        name="nsa_compressed",
    )(p3, kc, vc, tab, ovl_t, small3)


def _nsa_dense_kernel(*refs, tq, tk, n_tab, selected, branch):
    if selected:
        (q_ref, k_ref, vt_ref, tab_ref, blk_ref, sel_ref, s_ref, o_ref,
         kaug_ref, sa_ref, sb_ref, m_ref, acc_ref) = refs
    else:
        q_ref, k_ref, vt_ref, tab_ref, s_ref, o_ref, sa_ref, sb_ref, m_ref, acc_ref = refs
    g = pl.program_id(1)
    qi = pl.program_id(2)
    qs = _group_queries(q_ref.at[0], g)
    kr = lax.broadcasted_iota(jnp.int32, (tk, 4 * tq), 0)
    qc = lax.broadcasted_iota(jnp.int32, (tk, 4 * tq), 1) & (tq - 1)
    j0 = (qi * tq) // tk
    g0 = pl.multiple_of(g * VT_ROWS, VT_ROWS)

    if selected:
        @pl.when(qi == 0)
        def _():
            kaug_ref[:, 0:LANES] = k_ref[0]
            kaug_ref[:, LANES:] = blk_ref[...]

        qs = jnp.concatenate([qs, jnp.concatenate([sel_ref[0]] * 4, axis=0)], axis=1)

        def qk(j):
            return _dot_nt(kaug_ref[pl.ds(pl.multiple_of(j * tk, tk), tk), :], qs)
    else:
        def qk(j):
            return _dot_nt(k_ref[0, pl.ds(pl.multiple_of(j * tk, tk), tk), :], qs)

    def consume(j, s, first):
        k0 = pl.multiple_of(j * tk, tk)
        s = s + tab_ref[0, jnp.minimum(qi - j * (tk // tq), n_tab - 1)]
        dist = (qi * tq - j * tk) + qc - kr
        if not selected:
            s = jnp.where((dist >= 0) & (dist <= NSA_WINDOW - 1), s, NEG)
        elif first:
            s = jnp.where(dist >= 0, s, NEG)
        p, alpha = _col_softmax(s, m_ref, first)
        vt = vt_ref[0, pl.ds(g0, VT_ROWS), pl.ds(k0, tk)]
        _accumulate(acc_ref, slice(None), _dot(vt, p), alpha)

    first_j = 0 if selected else jnp.maximum(j0 - (-(-(NSA_WINDOW - 1) // tk)), 0)
    _key_tiles(j0, first_j, j0 - first_j, qk, consume, sa_ref, sb_ref)
    o_t = acc_ref[0:HEAD_DIM, :] / acc_ref[HEAD_DIM:HEAD_DIM + 1, :]
    o = jnp.concatenate([o_t[:, h * tq:(h + 1) * tq] for h in range(4)], axis=0).T
    o_ref[0] = (o * _group_gates(s_ref.at[0], g, branch)).astype(BF16)


def _nsa_dense(p3, tab, small3, vt, negsel):
    b, t, _ = p3.shape
    tq = tk = min(NSA_TQ_DENSE, t)
    n_tab = tab.shape[1]
    selected = negsel is not None
    pb_k = PB_KSA if selected else PB_KWA
    in_specs = [pl.BlockSpec((1, tq, 2 * LANES), lambda i, g, q: (i, q, g)),
                pl.BlockSpec((1, t, LANES), lambda i, g, q: (i, 0, pb_k)),
                pl.BlockSpec((1, 2 * VT_ROWS, t), lambda i, g, q: (i, 0, 0)),
                pl.BlockSpec((1, n_tab, tk, 4 * tq), lambda i, g, q: (g, 0, 0, 0),
                             pipeline_mode=pl.Buffered(1))]
    args = [p3, p3, vt, tab]
    scratch = []
    if selected:
        nsp = negsel.shape[2] // 2
        block_of_key = (jnp.arange(t)[:, None] // NSA_SEL_LEN == jnp.arange(nsp)[None, :]).astype(BF16)
        in_specs += [pl.BlockSpec((t, nsp), lambda i, g, q: (0, 0)),
                     pl.BlockSpec((1, tq, nsp), lambda i, g, q: (i, q, g))]
        args += [block_of_key, negsel]
        scratch = [pltpu.VMEM((t, LANES + nsp), BF16)]
    in_specs.append(pl.BlockSpec((1, tq, LANES), lambda i, g, q: (i, q, 0)))
    args.append(small3)
    return pl.pallas_call(
        functools.partial(_nsa_dense_kernel, tq=tq, tk=tk, n_tab=n_tab, selected=selected,
                          branch=1 if selected else 2),
        out_shape=jax.ShapeDtypeStruct((b, t, 4 * LANES), BF16),
        grid=(b, 2, t // tq),
        in_specs=in_specs,
        out_specs=pl.BlockSpec((1, tq, 2 * LANES), lambda i, g, q: (i, q, g)),
        scratch_shapes=scratch + [pltpu.VMEM((tk, 4 * tq), F32), pltpu.VMEM((tk, 4 * tq), F32),
                                  pltpu.VMEM((1, 4 * tq), F32), pltpu.VMEM((VT_ROWS, 4 * tq), F32)],
        compiler_params=_cparams(("parallel", "parallel", "arbitrary")),
        name="nsa_selected" if selected else "nsa_window",
    )(*args)


def _dilated_kernel(q_ref, kp_ref, kc_ref, vtp_ref, vtc_ref, tab_ref, o_ref, lse_ref, *, tq):
    qi = pl.program_id(2)
    kr = lax.broadcasted_iota(jnp.int32, (2 * tq, 2 * tq), 0)
    qc = lax.broadcasted_iota(jnp.int32, (2 * tq, 2 * tq), 1) & (tq - 1)
    dist = tq + qc - kr
    valid = (dist >= 0) & (dist <= tq) & (kr >= jnp.where(qi > 0, 0, tq))
    outs, lses = [], []
    for pb in range(4):
        cols = slice(pb * LANES, (pb + 1) * LANES)
        qs = _pair_queries(q_ref.at[0, :, cols])
        q_both = jnp.concatenate([qs[0], qs[1]], axis=0)
        k2 = jnp.concatenate([kp_ref[0, :, cols], kc_ref[0, :, cols]], axis=0)
        s = jnp.where(valid, _dot_nt(k2, q_both) + tab_ref[pb, 0], NEG)
        m = jnp.max(s, axis=0, keepdims=True)
        p = jnp.exp2(s - m)
        l = jnp.sum(p, axis=0, keepdims=True)
        p = p.astype(BF16)
        lse = (m + jnp.log2(l)) * LN2
        for hh in range(2):
            rows = slice((2 * pb + hh) * HEAD_DIM, (2 * pb + hh + 1) * HEAD_DIM)
            qcols = slice(hh * tq, (hh + 1) * tq)
            vt = jnp.concatenate([vtp_ref[0, 0, rows, :], vtc_ref[0, 0, rows, :]], axis=1)
            outs.append(_dot(vt, p[:, qcols]) / l[:, qcols])
            lses.append(lse[:, qcols])
    o_ref[0] = jnp.concatenate(outs, axis=0).T.astype(BF16)
    lses.append(jnp.zeros((LANES - len(lses), tq), F32))
    lse_ref[0] = jnp.concatenate(lses, axis=0).T


def _dilated(pd3, tab, dil, tq=DIL_BAND):
    b, t, _ = pd3.shape
    l = t // dil
    qk = pd3[:, :, 0:2 * GW].reshape(b, l, dil * 2 * GW)
    vt = pd3[:, :, 2 * GW:3 * GW].reshape(b, l, dil, GW).transpose(0, 2, 3, 1)

    def prev(q):
        return jnp.maximum(q - 1, 0)

    out_spec = pl.BlockSpec((1, tq, GW), lambda i, r, q: (i, q, r))
    o, lse = pl.pallas_call(
        functools.partial(_dilated_kernel, tq=tq),
        out_shape=(jax.ShapeDtypeStruct((b, l, dil * GW), BF16),
                   jax.ShapeDtypeStruct((b, l, dil * LANES), F32)),
        grid=(b, dil, l // tq),
        in_specs=[pl.BlockSpec((1, tq, GW), lambda i, r, q: (i, q, 2 * r)),
                  pl.BlockSpec((1, tq, GW), lambda i, r, q: (i, prev(q), 2 * r + 1)),
                  pl.BlockSpec((1, tq, GW), lambda i, r, q: (i, q, 2 * r + 1)),
                  pl.BlockSpec((1, 1, GW, tq), lambda i, r, q: (i, r, 0, prev(q))),
                  pl.BlockSpec((1, 1, GW, tq), lambda i, r, q: (i, r, 0, q)),
                  pl.BlockSpec((4, 1, 2 * tq, 2 * tq), lambda i, r, q: (0, 0, 0, 0))],
        out_specs=(out_spec, pl.BlockSpec((1, tq, LANES), lambda i, r, q: (i, q, r))),
        compiler_params=_cparams(("parallel", "parallel", "parallel")),
        name="dilated_attention_%d" % dil,
    )(qk, qk, qk, vt, vt, tab)
    return o.reshape(b * t, GW), lse.reshape(b * t, LANES)


def _out_proj_kernel(x_ref, oc_ref, os_ref, ow_ref, ob_ref, of_ref,
                     d1_ref, d2_ref, d3_ref, l1_ref, l2_ref, l3_ref, w_ref, g_ref, o_ref, cat_ref):
    gw = 4 * LANES
    a = oc_ref[...].astype(F32) + os_ref[...].astype(F32) + ow_ref[...].astype(F32)
    cat_ref[:, 0:gw] = a.astype(BF16)
    cat_ref[:, gw:2 * gw] = ob_ref[...]
    cat_ref[:, 2 * gw:3 * gw] = of_ref[...]
    l1, l2, l3 = l1_ref[...], l2_ref[...], l3_ref[...]
    mx = jnp.maximum(jnp.maximum(l1, l2), l3)
    e1, e2, e3 = jnp.exp(l1 - mx), jnp.exp(l2 - mx), jnp.exp(l3 - mx)
    den = e1 + e2 + e3
    head = lax.broadcasted_iota(jnp.int32, (2 * LANES, gw), 0) & (LANES - 1)
    lane = lax.broadcasted_iota(jnp.int32, (2 * LANES, gw), 1)
    spread = jnp.where(head == lane // HEAD_DIM, 1.0, 0.0).astype(BF16)
    d = (_spread_lanes(e1 / den, spread) * d1_ref[...].astype(F32)
         + _spread_lanes(e2 / den, spread) * d2_ref[...].astype(F32)
         + _spread_lanes(e3 / den, spread) * d3_ref[...].astype(F32))
    cat_ref[:, 3 * gw:4 * gw] = d.astype(BF16)
    y = _dot(cat_ref[...], w_ref[...])
    y = y * lax.rsqrt(jnp.mean(y * y, axis=-1, keepdims=True) + EPS)
    o_ref[...] = x_ref[...] + y * g_ref[...]


def _out_proj(x2, heads, dils, lses, w_out, g, tm=256):
    m = x2.shape[0]
    gw = 4 * LANES
    row = lambda i: (i, 0)
    const = lambda i: (0, 0)
    in_specs = ([pl.BlockSpec((tm, D_MODEL), row)]
                + [pl.BlockSpec((tm, gw), row)] * 8 + [pl.BlockSpec((tm, LANES), row)] * 3
                + [pl.BlockSpec((D_MODEL, D_MODEL), const), pl.BlockSpec((1, D_MODEL), const)])
    return pl.pallas_call(
        _out_proj_kernel,
        out_shape=jax.ShapeDtypeStruct((m, D_MODEL), F32),
        grid=(m // tm,),
        in_specs=in_specs,
        out_specs=pl.BlockSpec((tm, D_MODEL), row),
        scratch_shapes=[pltpu.VMEM((tm, D_MODEL), BF16)],
        compiler_params=_cparams(("parallel",)),
        name="out_proj_residual",
    )(x2, *heads, *dils, *lses, w_out, g)


def _ffn_kernel(x_ref, gpre_ref, wu_ref, wd_ref, gpost_ref, o_ref, h_ref, acc_ref):
    f = pl.program_id(1)

    @pl.when(f == 0)
    def _():
        x = x_ref[...]
        y = x * lax.rsqrt(jnp.mean(x * x, axis=-1, keepdims=True) + EPS)
        h_ref[...] = (y * gpre_ref[...]).astype(BF16)
        acc_ref[...] = jnp.zeros_like(acc_ref)

    u = jnp.maximum(_dot(h_ref[...], wu_ref[...]), 0.0)
    acc_ref[...] += _dot((u * u).astype(BF16), wd_ref[...])

    @pl.when(f == pl.num_programs(1) - 1)
    def _():
        y = acc_ref[...]
        y = y * lax.rsqrt(jnp.mean(y * y, axis=-1, keepdims=True) + EPS)
        o_ref[...] = x_ref[...] + y * gpost_ref[...]


def _ffn(x2, g_pre, w_up, w_down, g_post, tm=512, tf=1024):
    m = x2.shape[0]
    return pl.pallas_call(
        _ffn_kernel,
        out_shape=jax.ShapeDtypeStruct((m, D_MODEL), F32),
        grid=(m // tm, D_FF // tf),
        in_specs=[pl.BlockSpec((tm, D_MODEL), lambda i, f: (i, 0)),
                  pl.BlockSpec((1, D_MODEL), lambda i, f: (0, 0)),
                  pl.BlockSpec((D_MODEL, tf), lambda i, f: (0, f)),
                  pl.BlockSpec((tf, D_MODEL), lambda i, f: (f, 0)),
                  pl.BlockSpec((1, D_MODEL), lambda i, f: (0, 0))],
        out_specs=pl.BlockSpec((tm, D_MODEL), lambda i, f: (i, 0)),
        scratch_shapes=[pltpu.VMEM((tm, D_MODEL), BF16), pltpu.VMEM((tm, D_MODEL), F32)],
        compiler_params=_cparams(("parallel", "arbitrary")),
        name="ffn_residual",
    )(x2, g_pre, w_up, w_down, g_post)


def _split_w_in(w):
    gw, kw = 4 * LANES, LANES
    o = 0
    cols = {}
    for name, width in (("qa", gw), ("kca", kw), ("vca", kw), ("ksa", kw), ("vsa", kw), ("kwa", kw),
                        ("vwa", kw), ("ga", 24), ("qb", gw), ("kb", gw), ("vb", gw), ("qc", gw),
                        ("kc", gw), ("vc", gw), ("fc", 8), ("qd", gw), ("kd", gw), ("vd", gw)):
        cols[name] = w[:, o:o + width] * (Q_FOLD if name[0] == "q" else 1.0)
        o += width
    main = jnp.concatenate([cols[n] for n in ("qa", "ksa", "vsa", "kwa", "vwa", "qb", "kb", "vb",
                                              "qc", "kc", "vc", "qd", "kd", "vd")], axis=1)
    pad = jnp.zeros((w.shape[0], LANES - 32), w.dtype)
    small = jnp.concatenate([cols["ga"], cols["fc"], pad, cols["kca"], cols["vca"]], axis=1)
    return main.astype(BF16), small.astype(BF16)


def _compress_weights(pe, w1, w2):
    half = NSA_CMP_LEN // 2
    hid = NSA_CMP_HIDDEN
    w1r = w1.reshape(2, half, HEAD_DIM, hid)
    z = jnp.zeros_like(w1r)
    per_group = jnp.stack([jnp.concatenate([w1r, z], axis=-1), jnp.concatenate([z, w1r], axis=-1)], axis=2)
    w_halves = per_group.reshape(2, half * 2 * HEAD_DIM, 2 * hid).astype(BF16)
    pe_r = jnp.broadcast_to(pe.reshape(2, half, 1, HEAD_DIM), (2, half, 2, HEAD_DIM)).reshape(2, half * LANES)
    z2 = jnp.zeros_like(w2)
    w2g = jnp.concatenate([jnp.concatenate([w2, z2], axis=1), jnp.concatenate([z2, w2], axis=1)], axis=0)
    return pe_r, w_halves[0], w_halves[1], w2g.astype(BF16)


def _bias_tables(rel_bias, t):
    hm = 8
    tq_a = NSA_TQ_CMP
    tq_d = tk_d = min(NSA_TQ_DENSE, t)
    n_a = min(_n_toeplitz_tiles(tq_d, tk_d), t // tq_d)
    tab_a = _bias_tiles(rel_bias, _toeplitz_idx_t(n_a, tk_d, tq_d, tq_d), 0, 2, 4)
    ncp = t // NSA_CMP_STRIDE
    qi = jnp.arange(t // tq_a, dtype=jnp.int32)[:, None, None]
    r = jnp.arange(tq_a, dtype=jnp.int32)[None, :, None]
    c = jnp.arange(ncp, dtype=jnp.int32)[None, None, :]
    idx_c = _t5_bucket(qi * tq_a + r - (c * NSA_CMP_STRIDE + NSA_CMP_LEN - 1))
    tab_c = _bias_tiles(rel_bias, idx_c, 0, hm)
    tq_b = min(MOBA_TQ, t)
    n_b = min(_n_toeplitz_tiles(tq_b, tq_b), t // tq_b)
    tab_b = _bias_tiles(rel_bias, _toeplitz_idx_t(n_b, tq_b, tq_b, tq_b), hm, 4, 2)
    kr = jnp.arange(2 * DIL_BAND, dtype=jnp.int32)[None, :, None]
    qc = jnp.arange(DIL_BAND, dtype=jnp.int32)[None, None, :]
    tabs_d = [_bias_tiles(rel_bias, _t5_bucket((DIL_BAND + qc - kr) * dil), 2 * hm, 4, 2)
              for _, dil in LONGNET_PATTERNS]
    return tab_a, tab_c, tab_b, tabs_d


def _overlap_t(t):
    ncp = t // NSA_CMP_STRIDE
    nsp = max(t // NSA_SEL_LEN, 8)
    c_start = jnp.arange(ncp)[None, :] * NSA_CMP_STRIDE
    s_start = jnp.arange(nsp)[:, None] * NSA_SEL_LEN
    ovl = (c_start < s_start + NSA_SEL_LEN) & (c_start + NSA_CMP_LEN - 1 >= s_start)
    ovl = ovl & (jnp.arange(ncp)[None, :] < ncp - 1) & (jnp.arange(nsp)[:, None] < t // NSA_SEL_LEN)
    return ovl.astype(F32)


def _transposed(p3, pb, n_blocks):
    b, t, _ = p3.shape
    vt = p3[:, :, pb * LANES:(pb + n_blocks) * LANES].transpose(0, 2, 1).reshape(b, 2 * n_blocks, HEAD_DIM, t)
    ones = jnp.ones((b, 2 * n_blocks, VT_ROWS - HEAD_DIM, t), BF16)
    return jnp.concatenate([vt, ones], axis=2).reshape(b, 2 * n_blocks * VT_ROWS, t)


def _mixer_layer(x2, b, t, w_main, w_small, g_pre, g_post, w_out, cmp_w_k, cmp_w_v, fox_bias, tabs, ovl_t):
    tab_a, tab_c, tab_b, tabs_d = tabs
    m = b * t
    p2, pd2, small, kca, vca = _proj(x2, g_pre, w_main, w_small)
    p3 = p2.reshape(b, t, P_WIDTH)
    pd3 = pd2.reshape(b, t, PD_WIDTH)
    small3 = small.reshape(b, t, LANES)
    row_w = NSA_CMP_STRIDE * LANES
    kc = _nsa_compress(kca.reshape(b, t // NSA_CMP_STRIDE, row_w), *cmp_w_k)
    vc = _nsa_compress(vca.reshape(b, t // NSA_CMP_STRIDE, row_w), *cmp_w_v)
    o_cmp, negsel = _nsa_cmp(p3, kc, vc, tab_c, ovl_t, small3)
    o_sel = _nsa_dense(p3, tab_a, small3, _transposed(p3, PB_VSA, 1), negsel)
    o_win = _nsa_dense(p3, tab_a, small3, _transposed(p3, PB_VWA, 1), None)
    o_b = _moba(p3, tab_b, _transposed(p3, PB_VB, 4))
    fox_bias_row = jnp.zeros((1, LANES), F32).at[0, FOX_HI:FOX_HI + 8].set(fox_bias)
    o_f = _fox(p3, _fox_cumsum(small3, fox_bias_row), _transposed(p3, PB_VC, 4))
    dil = [_dilated(pd3, tab, d) for tab, (_, d) in zip(tabs_d, LONGNET_PATTERNS)]
    heads = [a.reshape(m, 4 * LANES) for a in (o_cmp, o_sel, o_win, o_b, o_f)]
    return _out_proj(x2, heads, [d[0] for d in dil], [d[1] for d in dil], w_out, g_post)


def kernel(x, w_in, w_out, g_mix_pre, g_mix_post, g_mlp_pre, g_mlp_post, w_up, w_down, cmp_pe, phik_w1, phik_w2, phiv_w1, phiv_w2, fox_bias, rel_bias):
    b, t, d = x.shape
    depth = w_in.shape[0]
    tabs = _bias_tables(rel_bias, t)
    ovl_t = _overlap_t(t)
    x2 = x.reshape(b * t, d)
    for l in range(depth):
        w_main, w_small = _split_w_in(w_in[l])
        x2 = _mixer_layer(
            x2, b, t, w_main, w_small, g_mix_pre[l][None], g_mix_post[l][None], w_out[l].astype(BF16),
            _compress_weights(cmp_pe[l], phik_w1[l], phik_w2[l]),
            _compress_weights(cmp_pe[l], phiv_w1[l], phiv_w2[l]),
            fox_bias[l], tabs, ovl_t)
        x2 = _ffn(x2, g_mlp_pre[l][None], w_up[l].astype(BF16), w_down[l].astype(BF16), g_mlp_post[l][None])
    return x2.reshape(b, t, d)
```

```python
import functools
import math

import jax
import jax.numpy as jnp
from jax import lax
from jax.experimental import pallas as pl
from jax.experimental.pallas import tpu as pltpu

F32 = jnp.float32
BF16 = jnp.bfloat16

D_MODEL = 2048
D_FF = 4 * D_MODEL
HEAD_DIM = 64
LANES = 128
EPS = 1e-6
NEG = -1e30
BIG = 1e30
SCALE = HEAD_DIM ** -0.5
LOG2E = math.log2(math.e)
LN2 = math.log(2.0)
Q_FOLD = SCALE * LOG2E

N_BUCKETS = 32
MAX_EXACT = 16
MAX_DISTANCE = 4096
LAST_BUCKET_DIST = 2897

NSA_CMP_LEN = 32
NSA_CMP_STRIDE = 16
NSA_CMP_HIDDEN = 256
NSA_SEL_LEN = 64
NSA_TOP_N = 16
NSA_WINDOW = 512
MOBA_BLOCK = 256
MOBA_TOPK = 3
LONGNET_PATTERNS = ((128, 1), (512, 4), (2048, 16))
DIL_BAND = 128

FOX_TQ = 512
MOBA_TQ = 512
NSA_TQ_CMP = 256
NSA_TQ_DENSE = 256

P_WIDTH = 4096
PB_QA, PB_KSA, PB_VSA, PB_KWA, PB_VWA = 0, 4, 5, 6, 7
PB_QB, PB_KB, PB_VB = 8, 12, 16
PB_QC, PB_KC, PB_VC = 20, 24, 28
PD_WIDTH = 1536
GW = 4 * LANES
VT_ROWS = HEAD_DIM + 16

FOX_HI, FOX_MID, FOX_LO = 24, 32, 40

VMEM_LIMIT = 56 * 1024 * 1024


def _cparams(sem):
    return pltpu.CompilerParams(dimension_semantics=sem, vmem_limit_bytes=VMEM_LIMIT)


def _dot(a, b):
    return jnp.dot(a, b, preferred_element_type=F32)


def _dot_nt(a, b):
    return lax.dot_general(a, b, (((1,), (1,)), ((), ())), preferred_element_type=F32)


def _dot_hi(a, b):
    return jnp.dot(a, b, preferred_element_type=F32, precision=lax.Precision.HIGHEST)


def _spread_lanes(w, spread):
    hi = w.astype(BF16)
    lo = (w - hi.astype(F32)).astype(BF16)
    return _dot(jnp.concatenate([hi, lo], axis=1), spread)


def _t5_bucket(dist):
    dist = jnp.maximum(dist, 0)
    rel = jnp.log(jnp.maximum(dist, 1).astype(jnp.float32) / MAX_EXACT) / math.log(MAX_DISTANCE / MAX_EXACT)
    large = jnp.minimum(MAX_EXACT + (rel * (N_BUCKETS - MAX_EXACT)).astype(jnp.int32), N_BUCKETS - 1)
    return jnp.where(dist < MAX_EXACT, dist, large)


def _bias_tile_kernel(tab_ref, idx_ref, o_ref, *, head0, hpg, cols):
    grp = pl.program_id(0)
    idx = idx_ref[0]
    for hh in range(hpg):
        h = head0 + grp * hpg + hh
        acc = jnp.zeros(idx.shape, F32)
        for b in range(N_BUCKETS):
            acc = jnp.where(idx == b, tab_ref[b, h], acc)
        o_ref[0, 0, :, hh * cols:(hh + 1) * cols] = acc * LOG2E


def _bias_tiles(rel_bias, idx, head0, n_groups, hpg=1):
    n, r, c = idx.shape
    return pl.pallas_call(
        functools.partial(_bias_tile_kernel, head0=head0, hpg=hpg, cols=c),
        out_shape=jax.ShapeDtypeStruct((n_groups, n, r, hpg * c), F32),
        grid=(n_groups, n),
        in_specs=[pl.BlockSpec(memory_space=pltpu.SMEM),
                  pl.BlockSpec((1, r, c), lambda g, i: (i, 0, 0))],
        out_specs=pl.BlockSpec((1, 1, r, hpg * c), lambda g, i: (g, i, 0, 0)),
        compiler_params=_cparams(("parallel", "parallel")),
        name="bias_tiles",
    )(rel_bias, idx)


def _toeplitz_idx_t(n_tiles, krows, qcols, q_stride):
    nd = jnp.arange(n_tiles, dtype=jnp.int32)[:, None, None]
    r = jnp.arange(krows, dtype=jnp.int32)[None, :, None]
    c = jnp.arange(qcols, dtype=jnp.int32)[None, None, :]
    return _t5_bucket(nd * q_stride + c - r)


def _n_toeplitz_tiles(q_stride, keys):
    return -(-(LAST_BUCKET_DIST + keys - 1) // q_stride) + 1


def _proj_kernel(x_ref, g_ref, w_ref, ws_ref, p_ref, pd_ref, s_ref, kc_ref, vc_ref, h_ref, *, n_main):
    j = pl.program_id(1)

    @pl.when(j == 0)
    def _():
        x = x_ref[...]
        y = x * lax.rsqrt(jnp.mean(x * x, axis=-1, keepdims=True) + EPS)
        hb = (y * g_ref[...]).astype(BF16)
        h_ref[...] = hb
        small = _dot(hb, ws_ref[...])
        s_ref[...] = small[:, 0:LANES]
        kc_ref[...] = small[:, LANES:2 * LANES]
        vc_ref[...] = small[:, 2 * LANES:3 * LANES]

    y = _dot(h_ref[...], w_ref[...]).astype(BF16)

    @pl.when(j < n_main)
    def _():
        p_ref[...] = y

    @pl.when(j >= n_main)
    def _():
        pd_ref[...] = y


def _proj(x2, g, w_main, w_small, tm=1024, tn=512):
    m = x2.shape[0]
    tm = min(tm, m)
    n_main = P_WIDTH // tn
    return pl.pallas_call(
        functools.partial(_proj_kernel, n_main=n_main),
        out_shape=(jax.ShapeDtypeStruct((m, P_WIDTH), BF16),
                   jax.ShapeDtypeStruct((m, PD_WIDTH), BF16),
                   jax.ShapeDtypeStruct((m, LANES), F32),
                   jax.ShapeDtypeStruct((m, LANES), F32),
                   jax.ShapeDtypeStruct((m, LANES), F32)),
        grid=(m // tm, (P_WIDTH + PD_WIDTH) // tn),
        in_specs=[pl.BlockSpec((tm, D_MODEL), lambda i, j: (i, 0)),
                  pl.BlockSpec((1, D_MODEL), lambda i, j: (0, 0)),
                  pl.BlockSpec((D_MODEL, tn), lambda i, j: (0, j)),
                  pl.BlockSpec((D_MODEL, 3 * LANES), lambda i, j: (0, 0))],
        out_specs=(pl.BlockSpec((tm, tn), lambda i, j: (i, jnp.minimum(j, n_main - 1))),
                   pl.BlockSpec((tm, tn), lambda i, j: (i, jnp.maximum(j - n_main, 0))),
                   pl.BlockSpec((tm, LANES), lambda i, j: (i, 0)),
                   pl.BlockSpec((tm, LANES), lambda i, j: (i, 0)),
                   pl.BlockSpec((tm, LANES), lambda i, j: (i, 0))),
        scratch_shapes=[pltpu.VMEM((tm, D_MODEL), BF16)],
        compiler_params=_cparams(("parallel", "arbitrary")),
        name="rms_in_proj",
    )(x2, g, w_main, w_small)


def _lane_lo(shape):
    return lax.broadcasted_iota(jnp.int32, shape, 1) < HEAD_DIM


def _pair_queries(q_ref):
    q = q_ref[...]
    lo = _lane_lo(q.shape)
    zero = jnp.zeros_like(q)
    return jnp.where(lo, q, zero), jnp.where(lo, zero, q)


def _col_softmax(s, m_ref, first):
    if first:
        m = jnp.max(s, axis=0, keepdims=True)
        m_ref[...] = m
        return jnp.exp2(s - m).astype(BF16), None
    m_prev = m_ref[...]
    m_new = jnp.maximum(m_prev, jnp.max(s, axis=0, keepdims=True))
    m_ref[...] = m_new
    return jnp.exp2(s - m_new).astype(BF16), jnp.exp2(m_prev - m_new)


def _accumulate(acc_ref, cols, pv, alpha):
    if alpha is None:
        acc_ref[:, cols] = pv
    else:
        acc_ref[:, cols] = alpha[:, cols] * acc_ref[:, cols] + pv


def _key_tiles(j_diag, j_first, n_past, qk, consume, sa_ref, sb_ref):
    last = j_first + jnp.maximum(n_past - 1, 0)
    sa_ref[...] = qk(j_first)
    consume(j_diag, qk(j_diag), True)

    def run(base, count):
        for t in range(count):
            cur, nxt = (sa_ref, sb_ref) if t % 2 == 0 else (sb_ref, sa_ref)
            nxt[...] = qk(jnp.minimum(base + t + 1, last))
            consume(base + t, cur[...], False)

    quads = n_past // 4
    lax.fori_loop(0, quads, lambda i, c: (run(j_first + 4 * i, 4), c)[1], 0)
    rest = n_past - 4 * quads

    @pl.when(rest >= 2)
    def _():
        run(j_first + 4 * quads, 2)

    @pl.when(rest % 2 == 1)
    def _():
        consume(j_first + n_past - 1, sa_ref[...], False)


def _top_rows(v, k):
    n = v.shape[0]
    rows = lax.broadcasted_iota(jnp.int32, v.shape, 0)
    chosen = jnp.zeros(v.shape, F32)
    for _ in range(k):
        top = jnp.max(v, axis=0, keepdims=True)
        first = jnp.min(jnp.where(v == top, rows, n), axis=0, keepdims=True)
        pick = rows == first
        chosen = jnp.where(pick, 1.0, chosen)
        v = jnp.where(pick, -jnp.inf, v)
    return chosen


def _top_rows_bisect(v, k):
    n = v.shape[0]
    key = pltpu.bitcast(v, jnp.int32)
    thr = jnp.zeros((1, v.shape[1]), jnp.int32)
    def enough(cand):
        return jnp.sum(jnp.where(key >= cand, 1.0, 0.0), axis=0, keepdims=True) >= k

    for bit in range(30, 0, -2):
        c_hi, c_lo = thr | (1 << bit), thr | (1 << (bit - 1))
        c_both = c_hi | (1 << (bit - 1))
        thr = jnp.where(enough(c_both), c_both, jnp.where(enough(c_hi), c_hi,
                                                          jnp.where(enough(c_lo), c_lo, thr)))
    thr = jnp.where(enough(thr | 1), thr | 1, thr)
    above = key > thr
    tie = key == thr
    n_above = jnp.sum(jnp.where(above, 1.0, 0.0), axis=0, keepdims=True)
    ri = lax.broadcasted_iota(jnp.int32, (n, n), 0)
    ci = lax.broadcasted_iota(jnp.int32, (n, n), 1)
    before = jnp.where(ci < ri, 1.0, 0.0).astype(BF16)
    ties_before = _dot(before, jnp.where(tie, 1.0, 0.0).astype(BF16))
    return above | (tie & (ties_before < k - n_above))


def _pair_finish(acc_ref, o_ref, tq):
    o_t = acc_ref[0:HEAD_DIM, :] / acc_ref[HEAD_DIM:HEAD_DIM + 1, :]
    o_ref[0] = jnp.concatenate([o_t[:, 0:tq], o_t[:, tq:2 * tq]], axis=0).T.astype(BF16)


def _fox_cumsum_kernel(s_ref, fb_ref, c_ref, carry_ref, *, tc):
    @pl.when(pl.program_id(1) == 0)
    def _():
        carry_ref[...] = jnp.zeros_like(carry_ref)

    lane = lax.broadcasted_iota(jnp.int32, (tc, LANES), 1)
    z = s_ref[0] + fb_ref[...]
    logf = -(jnp.maximum(-z, 0.0) + jnp.log(1.0 + jnp.exp(-jnp.abs(z))))
    logf = jnp.where((lane >= FOX_HI) & (lane < FOX_MID), logf, 0.0)
    ri = lax.broadcasted_iota(jnp.int32, (tc, tc), 0)
    ci = lax.broadcasted_iota(jnp.int32, (tc, tc), 1)
    lower = jnp.where(ri >= ci, 1.0, 0.0).astype(F32)
    c = _dot_hi(lower, logf) + carry_ref[...]
    carry_ref[...] = c[tc - 1:tc, :]
    c2 = c * LOG2E
    hi = c2.astype(BF16).astype(F32)
    rest = c2 - hi
    mid = rest.astype(BF16).astype(F32)
    lo = rest - mid
    pieces = hi + pltpu.roll(mid, FOX_MID - FOX_HI, axis=1) + pltpu.roll(lo, FOX_LO - FOX_HI, axis=1)
    c_ref[0] = pieces.astype(BF16)


def _fox_cumsum(small3, fox_bias_row, tc=512):
    b, t, _ = small3.shape
    tc = min(tc, t)
    return pl.pallas_call(
        functools.partial(_fox_cumsum_kernel, tc=tc),
        out_shape=jax.ShapeDtypeStruct((b, t, LANES), BF16),
        grid=(b, t // tc),
        in_specs=[pl.BlockSpec((1, tc, LANES), lambda i, j: (i, j, 0)),
                  pl.BlockSpec((1, LANES), lambda i, j: (0, 0))],
        out_specs=pl.BlockSpec((1, tc, LANES), lambda i, j: (i, j, 0)),
        scratch_shapes=[pltpu.VMEM((1, LANES), F32)],
        compiler_params=_cparams(("parallel", "arbitrary")),
        name="fox_cumsum",
    )(small3, fox_bias_row)


def _fox_kernel(q_ref, k_ref, c_ref, vt_ref, o_ref, kaug_ref, sa_ref, sb_ref, m_ref, acc_ref, *, tq):
    hp = pl.program_id(1)
    qi = pl.program_id(2)

    @pl.when(qi == 0)
    def _():
        kaug_ref[:, 0:LANES] = k_ref[0]
        kaug_ref[:, LANES:2 * LANES] = c_ref[0]

    qs = _pair_queries(q_ref.at[0])
    lane = lax.broadcasted_iota(jnp.int32, (tq, LANES), 1)
    rows = []
    for hh in range(2):
        h = 2 * hp + hh
        pick = (lane == FOX_HI + h) | (lane == FOX_MID + h) | (lane == FOX_LO + h)
        rows.append(jnp.concatenate([qs[hh], jnp.where(pick, -1.0, 0.0).astype(BF16)], axis=1))
    q_aug = jnp.concatenate(rows, axis=0)
    kr = lax.broadcasted_iota(jnp.int32, (tq, 2 * tq), 0)
    qc = lax.broadcasted_iota(jnp.int32, (tq, 2 * tq), 1) & (tq - 1)

    def qk(j):
        return _dot_nt(kaug_ref[pl.ds(pl.multiple_of(j * tq, tq), tq), :], q_aug)

    def consume(j, s, first):
        k0 = pl.multiple_of(j * tq, tq)
        if first:
            s = jnp.where(kr <= qc, s, NEG)
        p, alpha = _col_softmax(s, m_ref, first)
        for hh in range(2):
            cols = slice(hh * tq, (hh + 1) * tq)
            vt = vt_ref[0, hh * VT_ROWS:(hh + 1) * VT_ROWS, pl.ds(k0, tq)]
            _accumulate(acc_ref, cols, _dot(vt, p[:, cols]), alpha)

    _key_tiles(qi, 0, qi, qk, consume, sa_ref, sb_ref)
    _pair_finish(acc_ref, o_ref, tq)


def _fox(p3, c_aug, vt, tq=FOX_TQ):
    b, t, _ = p3.shape
    tq = min(tq, t)
    return pl.pallas_call(
        functools.partial(_fox_kernel, tq=tq),
        out_shape=jax.ShapeDtypeStruct((b, t, 4 * LANES), BF16),
        grid=(b, 4, t // tq),
        in_specs=[pl.BlockSpec((1, tq, LANES), lambda i, h, q: (i, q, PB_QC + h)),
                  pl.BlockSpec((1, t, LANES), lambda i, h, q: (i, 0, PB_KC + h)),
                  pl.BlockSpec((1, t, LANES), lambda i, h, q: (i, 0, 0)),
                  pl.BlockSpec((1, 2 * VT_ROWS, t), lambda i, h, q: (i, h, 0))],
        out_specs=pl.BlockSpec((1, tq, LANES), lambda i, h, q: (i, q, h)),
        scratch_shapes=[pltpu.VMEM((t, 2 * LANES), BF16),
                        pltpu.VMEM((tq, 2 * tq), F32), pltpu.VMEM((tq, 2 * tq), F32),
                        pltpu.VMEM((1, 2 * tq), F32), pltpu.VMEM((VT_ROWS, 2 * tq), F32)],
        compiler_params=_cparams(("parallel", "parallel", "arbitrary")),
        name="fox_attention",
    )(p3, p3, c_aug, vt)


def _moba_kernel(q_ref, k_ref, vt_ref, tab_ref, o_ref, km_ref, kaug_ref, sa_ref, sb_ref, m_ref, acc_ref,
                 *, nkb, nkp, n_tab, tq):
    blk = MOBA_BLOCK
    qi = pl.program_id(2)

    @pl.when(qi == 0)
    def _():
        km_ref[...] = jnp.zeros_like(km_ref)
        lane = lax.broadcasted_iota(jnp.int32, (blk, LANES), 1)
        for n in range(nkb):
            kb = k_ref[0, n * blk:(n + 1) * blk, :]
            km_ref[n:n + 1, :] = jnp.mean(kb.astype(F32), axis=0, keepdims=True)
            kaug_ref[n * blk:(n + 1) * blk, 0:LANES] = kb
            kaug_ref[n * blk:(n + 1) * blk, LANES:2 * LANES] = jnp.where(lane == n, 1.0, 0.0).astype(BF16)

    qs = _pair_queries(q_ref.at[0])
    q_both = jnp.concatenate([qs[0], qs[1]], axis=0)
    km = km_ref[...]
    km_hi = km.astype(BF16)
    km_lo = (km - km_hi.astype(F32)).astype(BF16)
    gate = _dot_nt(jnp.concatenate([km_hi, km_lo], axis=0), q_both)
    gate = gate[0:nkp] + gate[nkp:2 * nkp]
    n = lax.broadcasted_iota(jnp.int32, gate.shape, 0)
    q_pos = lax.broadcasted_iota(jnp.int32, gate.shape, 1) & (tq - 1)
    own = qi * (tq // blk) + q_pos // blk
    past = n < own
    gate = jnp.where(past, gate, NEG)
    allowed = ((_top_rows(gate, MOBA_TOPK) > 0.5) & past) | (n == own)
    negsel = jnp.where(allowed, 0.0, NEG).astype(F32)
    if nkp < LANES:
        negsel = jnp.concatenate([negsel, jnp.zeros((LANES - nkp, 2 * tq), F32)], axis=0)
    q_aug = jnp.concatenate([q_both, negsel.T.astype(BF16)], axis=1)

    kr = lax.broadcasted_iota(jnp.int32, (tq, 2 * tq), 0)
    qc = lax.broadcasted_iota(jnp.int32, (tq, 2 * tq), 1) & (tq - 1)

    def qk(j):
        return _dot_nt(kaug_ref[pl.ds(pl.multiple_of(j * tq, tq), tq), :], q_aug)

    def consume(j, s, first):
        k0 = pl.multiple_of(j * tq, tq)
        s = s + tab_ref[0, jnp.minimum(qi - j, n_tab - 1)]
        if first:
            s = jnp.where(kr <= qc, s, NEG)
        p, alpha = _col_softmax(s, m_ref, first)
        for hh in range(2):
            cols = slice(hh * tq, (hh + 1) * tq)
            vt = vt_ref[0, hh * VT_ROWS:(hh + 1) * VT_ROWS, pl.ds(k0, tq)]
            _accumulate(acc_ref, cols, _dot(vt, p[:, cols]), alpha)

    _key_tiles(qi, 0, qi, qk, consume, sa_ref, sb_ref)
    _pair_finish(acc_ref, o_ref, tq)


def _moba(p3, tab, vt):
    b, t, _ = p3.shape
    tq = tab.shape[2]
    nkb = t // MOBA_BLOCK
    nkp = -(-nkb // 8) * 8
    n_tab = tab.shape[1]
    return pl.pallas_call(
        functools.partial(_moba_kernel, nkb=nkb, nkp=nkp, n_tab=n_tab, tq=tq),
        out_shape=jax.ShapeDtypeStruct((b, t, 4 * LANES), BF16),
        grid=(b, 4, t // tq),
        in_specs=[pl.BlockSpec((1, tq, LANES), lambda i, h, q: (i, q, PB_QB + h)),
                  pl.BlockSpec((1, t, LANES), lambda i, h, q: (i, 0, PB_KB + h)),
                  pl.BlockSpec((1, 2 * VT_ROWS, t), lambda i, h, q: (i, h, 0)),
                  pl.BlockSpec((1, n_tab, tq, 2 * tq), lambda i, h, q: (h, 0, 0, 0),
                               pipeline_mode=pl.Buffered(1))],
        out_specs=pl.BlockSpec((1, tq, LANES), lambda i, h, q: (i, q, h)),
        scratch_shapes=[pltpu.VMEM((nkp, LANES), F32), pltpu.VMEM((t, 2 * LANES), BF16),
                        pltpu.VMEM((tq, 2 * tq), F32), pltpu.VMEM((tq, 2 * tq), F32),
                        pltpu.VMEM((1, 2 * tq), F32), pltpu.VMEM((VT_ROWS, 2 * tq), F32)],
        compiler_params=_cparams(("parallel", "parallel", "arbitrary")),
        name="moba_attention",
    )(p3, p3, vt, tab)


def _gelu_tanh(x):
    return 0.5 * x * (1.0 + jnp.tanh(math.sqrt(2.0 / math.pi) * (x + 0.044715 * (x * x * x))))


def _nsa_compress_kernel(r_ref, pe_ref, wlo_ref, whi_ref, w2_ref, o_ref, *, nr):
    r = r_ref[0]
    pe = pe_ref[...]
    a = _dot((r + pe[0:1, :]).astype(BF16), wlo_ref[...])
    b = _dot((r + pe[1:2, :]).astype(BF16), whi_ref[...])
    hid = _gelu_tanh(a + pltpu.roll(b, nr - 1, axis=0))
    o_ref[0] = _dot(hid.astype(BF16), w2_ref[...]).astype(BF16)


def _nsa_compress(r3, pe2, wlo, whi, w2):
    b, nr, w = r3.shape
    return pl.pallas_call(
        functools.partial(_nsa_compress_kernel, nr=nr),
        out_shape=jax.ShapeDtypeStruct((b, nr, LANES), BF16),
        grid=(b,),
        in_specs=[pl.BlockSpec((1, nr, w), lambda i: (i, 0, 0)),
                  pl.BlockSpec((2, w), lambda i: (0, 0)),
                  pl.BlockSpec((w, 2 * NSA_CMP_HIDDEN), lambda i: (0, 0)),
                  pl.BlockSpec((w, 2 * NSA_CMP_HIDDEN), lambda i: (0, 0)),
                  pl.BlockSpec((2 * NSA_CMP_HIDDEN, LANES), lambda i: (0, 0))],
        out_specs=pl.BlockSpec((1, nr, LANES), lambda i: (i, 0, 0)),
        compiler_params=_cparams(("parallel",)),
        name="nsa_compress",
    )(r3, pe2, wlo, whi, w2)


def _group_queries(q_ref, g):
    half = lax.broadcasted_iota(jnp.int32, (q_ref.shape[0], LANES), 1) // HEAD_DIM
    lo = half == 0
    in_group = half == g
    out = []
    for pb in range(2):
        q = q_ref[:, pb * LANES:(pb + 1) * LANES].astype(F32)
        qr = pltpu.roll(q, HEAD_DIM, axis=1)
        for dup in (jnp.where(lo, q, qr), jnp.where(lo, qr, q)):
            out.append(jnp.where(in_group, dup, 0.0).astype(BF16))
    return jnp.concatenate(out, axis=0)


def _group_outputs(acc, g, tq):
    half = lax.broadcasted_iota(jnp.int32, (tq, LANES), 1) // HEAD_DIM
    lo = half == 0
    in_group = half == g
    dups = []
    for h in range(4):
        a = acc[h * tq:(h + 1) * tq, :]
        dups.append(jnp.where(in_group, a, pltpu.roll(a, HEAD_DIM, axis=1)))
    return jnp.concatenate([jnp.where(lo, dups[0], dups[1]), jnp.where(lo, dups[2], dups[3])], axis=1)


def _group_gates(s_ref, g, branch):
    sig = 1.0 / (1.0 + jnp.exp(-s_ref[...]))
    col = lax.broadcasted_iota(jnp.int32, (2 * LANES, 2 * LANES), 0) & (LANES - 1)
    lane = lax.broadcasted_iota(jnp.int32, (2 * LANES, 2 * LANES), 1)
    spread = jnp.where(col == branch * 8 + g * 4 + lane // HEAD_DIM, 1.0, 0.0).astype(BF16)
    return _spread_lanes(sig, spread)


def _nsa_cmp_kernel(q_ref, kc_ref, vc_ref, tab_ref, ovl_ref, s_ref, o_ref, sel_ref, *, tq, ncp, nsp):
    g = pl.program_id(1)
    qi = pl.program_id(2)
    qs = _group_queries(q_ref.at[0], g)
    s = _dot_nt(qs, kc_ref[0]) + tab_ref[...].reshape(4 * tq, ncp)
    t = qi * tq + (lax.broadcasted_iota(jnp.int32, (4 * tq, ncp), 0) & (tq - 1))
    c = lax.broadcasted_iota(jnp.int32, (4 * tq, ncp), 1)
    ok = (t - (c * NSA_CMP_STRIDE + NSA_CMP_LEN - 1) >= 0) & (c < ncp - 1)
    s = jnp.where(ok, s, NEG)
    e = jnp.exp2(s - jnp.max(s, axis=-1, keepdims=True))
    p = jnp.where(ok, e / jnp.sum(e, axis=-1, keepdims=True), 0.0)
    o = _dot(p.astype(BF16), vc_ref[0])
    gate = _group_gates(s_ref.at[0], g, 0)
    o_ref[0] = (_group_outputs(o, g, tq) * gate).astype(BF16)

    psum = p[0:tq] + p[tq:2 * tq] + p[2 * tq:3 * tq] + p[3 * tq:4 * tq]
    hi = psum.astype(BF16)
    lo = (psum - hi.astype(F32)).astype(BF16)
    imp = _dot_nt(ovl_ref[...], jnp.concatenate([hi, lo], axis=1))
    n = lax.broadcasted_iota(jnp.int32, (nsp, tq), 0)
    cur = (qi * tq + lax.broadcasted_iota(jnp.int32, (nsp, tq), 1)) // NSA_SEL_LEN
    forced = (n == 0) | (n == cur) | (n == cur - 1)
    valid = n <= cur
    imp = jnp.where(forced, BIG, jnp.where(valid, jnp.abs(imp), NEG))
    chosen = _top_rows_bisect(imp, NSA_TOP_N) & valid
    sel_ref[0] = jnp.where(chosen, 0.0, NEG).astype(F32).T.astype(BF16)


def _nsa_cmp(p3, kc, vc, tab, ovl_t, small3, tq=NSA_TQ_CMP):
    b, t, _ = p3.shape
    ncp = kc.shape[1]
    nsp = ovl_t.shape[0]
    return pl.pallas_call(
        functools.partial(_nsa_cmp_kernel, tq=tq, ncp=ncp, nsp=nsp),
        out_shape=(jax.ShapeDtypeStruct((b, t, 4 * LANES), BF16),
                   jax.ShapeDtypeStruct((b, t, 2 * nsp), BF16)),
        grid=(b, 2, t // tq),
        in_specs=[pl.BlockSpec((1, tq, 2 * LANES), lambda i, g, q: (i, q, g)),
                  pl.BlockSpec((1, ncp, LANES), lambda i, g, q: (i, 0, 0)),
                  pl.BlockSpec((1, ncp, LANES), lambda i, g, q: (i, 0, 0)),
                  pl.BlockSpec((4, 1, tq, ncp), lambda i, g, q: (g, q, 0, 0)),
                  pl.BlockSpec((nsp, 2 * ncp), lambda i, g, q: (0, 0)),
                  pl.BlockSpec((1, tq, LANES), lambda i, g, q: (i, q, 0))],
        out_specs=(pl.BlockSpec((1, tq, 2 * LANES), lambda i, g, q: (i, q, g)),
                   pl.BlockSpec((1, tq, nsp), lambda i, g, q: (i, q, g))),
        compiler_params=_cparams(("parallel", "parallel", "parallel")),
        name="nsa_compressed",
    )(p3, kc, vc, tab, ovl_t, small3)


def _nsa_dense_kernel(*refs, tq, tk, n_tab, selected, branch):
    if selected:
        (q_ref, k_ref, vt_ref, tab_ref, blk_ref, sel_ref, s_ref, o_ref,
         kaug_ref, sa_ref, sb_ref, m_ref, acc_ref) = refs
    else:
        q_ref, k_ref, vt_ref, tab_ref, s_ref, o_ref, sa_ref, sb_ref, m_ref, acc_ref = refs
    g = pl.program_id(1)
    qi = pl.program_id(2)
    qs = _group_queries(q_ref.at[0], g)
    kr = lax.broadcasted_iota(jnp.int32, (tk, 4 * tq), 0)
    qc = lax.broadcasted_iota(jnp.int32, (tk, 4 * tq), 1) & (tq - 1)
    j0 = (qi * tq) // tk
    g0 = pl.multiple_of(g * VT_ROWS, VT_ROWS)

    if selected:
        @pl.when(qi == 0)
        def _():
            kaug_ref[:, 0:LANES] = k_ref[0]
            kaug_ref[:, LANES:] = blk_ref[...]

        qs = jnp.concatenate([qs, jnp.concatenate([sel_ref[0]] * 4, axis=0)], axis=1)

        def qk(j):
            return _dot_nt(kaug_ref[pl.ds(pl.multiple_of(j * tk, tk), tk), :], qs)
    else:
        def qk(j):
            return _dot_nt(k_ref[0, pl.ds(pl.multiple_of(j * tk, tk), tk), :], qs)

    def consume(j, s, first):
        k0 = pl.multiple_of(j * tk, tk)
        s = s + tab_ref[0, jnp.minimum(qi - j * (tk // tq), n_tab - 1)]
        dist = (qi * tq - j * tk) + qc - kr
        if not selected:
            s = jnp.where((dist >= 0) & (dist <= NSA_WINDOW - 1), s, NEG)
        elif first:
            s = jnp.where(dist >= 0, s, NEG)
        p, alpha = _col_softmax(s, m_ref, first)
        vt = vt_ref[0, pl.ds(g0, VT_ROWS), pl.ds(k0, tk)]
        _accumulate(acc_ref, slice(None), _dot(vt, p), alpha)

    first_j = 0 if selected else jnp.maximum(j0 - (-(-(NSA_WINDOW - 1) // tk)), 0)
    _key_tiles(j0, first_j, j0 - first_j, qk, consume, sa_ref, sb_ref)
    o_t = acc_ref[0:HEAD_DIM, :] / acc_ref[HEAD_DIM:HEAD_DIM + 1, :]
    o = jnp.concatenate([o_t[:, h * tq:(h + 1) * tq] for h in range(4)], axis=0).T
    o_ref[0] = (o * _group_gates(s_ref.at[0], g, branch)).astype(BF16)


def _nsa_dense(p3, tab, small3, vt, negsel):
    b, t, _ = p3.shape
    tq = tk = min(NSA_TQ_DENSE, t)
    n_tab = tab.shape[1]
    selected = negsel is not None
    pb_k = PB_KSA if selected else PB_KWA
    in_specs = [pl.BlockSpec((1, tq, 2 * LANES), lambda i, g, q: (i, q, g)),
                pl.BlockSpec((1, t, LANES), lambda i, g, q: (i, 0, pb_k)),
                pl.BlockSpec((1, 2 * VT_ROWS, t), lambda i, g, q: (i, 0, 0)),
                pl.BlockSpec((1, n_tab, tk, 4 * tq), lambda i, g, q: (g, 0, 0, 0),
                             pipeline_mode=pl.Buffered(1))]
    args = [p3, p3, vt, tab]
    scratch = []
    if selected:
        nsp = negsel.shape[2] // 2
        block_of_key = (jnp.arange(t)[:, None] // NSA_SEL_LEN == jnp.arange(nsp)[None, :]).astype(BF16)
        in_specs += [pl.BlockSpec((t, nsp), lambda i, g, q: (0, 0)),
                     pl.BlockSpec((1, tq, nsp), lambda i, g, q: (i, q, g))]
        args += [block_of_key, negsel]
        scratch = [pltpu.VMEM((t, LANES + nsp), BF16)]
    in_specs.append(pl.BlockSpec((1, tq, LANES), lambda i, g, q: (i, q, 0)))
    args.append(small3)
    return pl.pallas_call(
        functools.partial(_nsa_dense_kernel, tq=tq, tk=tk, n_tab=n_tab, selected=selected,
                          branch=1 if selected else 2),
        out_shape=jax.ShapeDtypeStruct((b, t, 4 * LANES), BF16),
        grid=(b, 2, t // tq),
        in_specs=in_specs,
        out_specs=pl.BlockSpec((1, tq, 2 * LANES), lambda i, g, q: (i, q, g)),
        scratch_shapes=scratch + [pltpu.VMEM((tk, 4 * tq), F32), pltpu.VMEM((tk, 4 * tq), F32),
                                  pltpu.VMEM((1, 4 * tq), F32), pltpu.VMEM((VT_ROWS, 4 * tq), F32)],
        compiler_params=_cparams(("parallel", "parallel", "arbitrary")),
        name="nsa_selected" if selected else "nsa_window",
    )(*args)


def _dilated_kernel(q_ref, kp_ref, kc_ref, vtp_ref, vtc_ref, tab_ref, o_ref, lse_ref, *, tq):
    qi = pl.program_id(2)
    kr = lax.broadcasted_iota(jnp.int32, (2 * tq, 2 * tq), 0)
    qc = lax.broadcasted_iota(jnp.int32, (2 * tq, 2 * tq), 1) & (tq - 1)
    dist = tq + qc - kr
    valid = (dist >= 0) & (dist <= tq) & (kr >= jnp.where(qi > 0, 0, tq))
    outs, lses = [], []
    for pb in range(4):
        cols = slice(pb * LANES, (pb + 1) * LANES)
        qs = _pair_queries(q_ref.at[0, :, cols])
        q_both = jnp.concatenate([qs[0], qs[1]], axis=0)
        k2 = jnp.concatenate([kp_ref[0, :, cols], kc_ref[0, :, cols]], axis=0)
        s = jnp.where(valid, _dot_nt(k2, q_both) + tab_ref[pb, 0], NEG)
        m = jnp.max(s, axis=0, keepdims=True)
        p = jnp.exp2(s - m)
        l = jnp.sum(p, axis=0, keepdims=True)
        p = p.astype(BF16)
        lse = (m + jnp.log2(l)) * LN2
        for hh in range(2):
            rows = slice((2 * pb + hh) * HEAD_DIM, (2 * pb + hh + 1) * HEAD_DIM)
            qcols = slice(hh * tq, (hh + 1) * tq)
            vt = jnp.concatenate([vtp_ref[0, 0, rows, :], vtc_ref[0, 0, rows, :]], axis=1)
            outs.append(_dot(vt, p[:, qcols]) / l[:, qcols])
            lses.append(lse[:, qcols])
    o_ref[0] = jnp.concatenate(outs, axis=0).T.astype(BF16)
    lses.append(jnp.zeros((LANES - len(lses), tq), F32))
    lse_ref[0] = jnp.concatenate(lses, axis=0).T


def _dilated(pd3, tab, dil, tq=DIL_BAND):
    b, t, _ = pd3.shape
    l = t // dil
    qk = pd3[:, :, 0:2 * GW].reshape(b, l, dil * 2 * GW)
    vt = pd3[:, :, 2 * GW:3 * GW].reshape(b, l, dil, GW).transpose(0, 2, 3, 1)

    def prev(q):
        return jnp.maximum(q - 1, 0)

    out_spec = pl.BlockSpec((1, tq, GW), lambda i, r, q: (i, q, r))
    o, lse = pl.pallas_call(
        functools.partial(_dilated_kernel, tq=tq),
        out_shape=(jax.ShapeDtypeStruct((b, l, dil * GW), BF16),
                   jax.ShapeDtypeStruct((b, l, dil * LANES), F32)),
        grid=(b, dil, l // tq),
        in_specs=[pl.BlockSpec((1, tq, GW), lambda i, r, q: (i, q, 2 * r)),
                  pl.BlockSpec((1, tq, GW), lambda i, r, q: (i, prev(q), 2 * r + 1)),
                  pl.BlockSpec((1, tq, GW), lambda i, r, q: (i, q, 2 * r + 1)),
                  pl.BlockSpec((1, 1, GW, tq), lambda i, r, q: (i, r, 0, prev(q))),
                  pl.BlockSpec((1, 1, GW, tq), lambda i, r, q: (i, r, 0, q)),
                  pl.BlockSpec((4, 1, 2 * tq, 2 * tq), lambda i, r, q: (0, 0, 0, 0))],
        out_specs=(out_spec, pl.BlockSpec((1, tq, LANES), lambda i, r, q: (i, q, r))),
        compiler_params=_cparams(("parallel", "parallel", "parallel")),
        name="dilated_attention_%d" % dil,
    )(qk, qk, qk, vt, vt, tab)
    return o.reshape(b * t, GW), lse.reshape(b * t, LANES)


def _out_proj_kernel(x_ref, oc_ref, os_ref, ow_ref, ob_ref, of_ref,
                     d1_ref, d2_ref, d3_ref, l1_ref, l2_ref, l3_ref, w_ref, g_ref, o_ref, cat_ref):
    gw = 4 * LANES
    a = oc_ref[...].astype(F32) + os_ref[...].astype(F32) + ow_ref[...].astype(F32)
    cat_ref[:, 0:gw] = a.astype(BF16)
    cat_ref[:, gw:2 * gw] = ob_ref[...]
    cat_ref[:, 2 * gw:3 * gw] = of_ref[...]
    l1, l2, l3 = l1_ref[...], l2_ref[...], l3_ref[...]
    mx = jnp.maximum(jnp.maximum(l1, l2), l3)
    e1, e2, e3 = jnp.exp(l1 - mx), jnp.exp(l2 - mx), jnp.exp(l3 - mx)
    den = e1 + e2 + e3
    head = lax.broadcasted_iota(jnp.int32, (2 * LANES, gw), 0) & (LANES - 1)
    lane = lax.broadcasted_iota(jnp.int32, (2 * LANES, gw), 1)
    spread = jnp.where(head == lane // HEAD_DIM, 1.0, 0.0).astype(BF16)
    d = (_spread_lanes(e1 / den, spread) * d1_ref[...].astype(F32)
         + _spread_lanes(e2 / den, spread) * d2_ref[...].astype(F32)
         + _spread_lanes(e3 / den, spread) * d3_ref[...].astype(F32))
    cat_ref[:, 3 * gw:4 * gw] = d.astype(BF16)
    y = _dot(cat_ref[...], w_ref[...])
    y = y * lax.rsqrt(jnp.mean(y * y, axis=-1, keepdims=True) + EPS)
    o_ref[...] = x_ref[...] + y * g_ref[...]


def _out_proj(x2, heads, dils, lses, w_out, g, tm=256):
    m = x2.shape[0]
    gw = 4 * LANES
    row = lambda i: (i, 0)
    const = lambda i: (0, 0)
    in_specs = ([pl.BlockSpec((tm, D_MODEL), row)]
                + [pl.BlockSpec((tm, gw), row)] * 8 + [pl.BlockSpec((tm, LANES), row)] * 3
                + [pl.BlockSpec((D_MODEL, D_MODEL), const), pl.BlockSpec((1, D_MODEL), const)])
    return pl.pallas_call(
        _out_proj_kernel,
        out_shape=jax.ShapeDtypeStruct((m, D_MODEL), F32),
        grid=(m // tm,),
        in_specs=in_specs,
        out_specs=pl.BlockSpec((tm, D_MODEL), row),
        scratch_shapes=[pltpu.VMEM((tm, D_MODEL), BF16)],
        compiler_params=_cparams(("parallel",)),
        name="out_proj_residual",
    )(x2, *heads, *dils, *lses, w_out, g)


def _ffn_kernel(x_ref, gpre_ref, wu_ref, wd_ref, gpost_ref, o_ref, h_ref, acc_ref):
    f = pl.program_id(1)

    @pl.when(f == 0)
    def _():
        x = x_ref[...]
        y = x * lax.rsqrt(jnp.mean(x * x, axis=-1, keepdims=True) + EPS)
        h_ref[...] = (y * gpre_ref[...]).astype(BF16)
        acc_ref[...] = jnp.zeros_like(acc_ref)

    u = jnp.maximum(_dot(h_ref[...], wu_ref[...]), 0.0)
    acc_ref[...] += _dot((u * u).astype(BF16), wd_ref[...])

    @pl.when(f == pl.num_programs(1) - 1)
    def _():
        y = acc_ref[...]
        y = y * lax.rsqrt(jnp.mean(y * y, axis=-1, keepdims=True) + EPS)
        o_ref[...] = x_ref[...] + y * gpost_ref[...]


def _ffn(x2, g_pre, w_up, w_down, g_post, tm=512, tf=1024):
    m = x2.shape[0]
    return pl.pallas_call(
        _ffn_kernel,
        out_shape=jax.ShapeDtypeStruct((m, D_MODEL), F32),
        grid=(m // tm, D_FF // tf),
        in_specs=[pl.BlockSpec((tm, D_MODEL), lambda i, f: (i, 0)),
                  pl.BlockSpec((1, D_MODEL), lambda i, f: (0, 0)),
                  pl.BlockSpec((D_MODEL, tf), lambda i, f: (0, f)),
                  pl.BlockSpec((tf, D_MODEL), lambda i, f: (f, 0)),
                  pl.BlockSpec((1, D_MODEL), lambda i, f: (0, 0))],
        out_specs=pl.BlockSpec((tm, D_MODEL), lambda i, f: (i, 0)),
        scratch_shapes=[pltpu.VMEM((tm, D_MODEL), BF16), pltpu.VMEM((tm, D_MODEL), F32)],
        compiler_params=_cparams(("parallel", "arbitrary")),
        name="ffn_residual",
    )(x2, g_pre, w_up, w_down, g_post)


def _split_w_in(w):
    gw, kw = 4 * LANES, LANES
    o = 0
    cols = {}
    for name, width in (("qa", gw), ("kca", kw), ("vca", kw), ("ksa", kw), ("vsa", kw), ("kwa", kw),
                        ("vwa", kw), ("ga", 24), ("qb", gw), ("kb", gw), ("vb", gw), ("qc", gw),
                        ("kc", gw), ("vc", gw), ("fc", 8), ("qd", gw), ("kd", gw), ("vd", gw)):
        cols[name] = w[:, o:o + width] * (Q_FOLD if name[0] == "q" else 1.0)
        o += width
    main = jnp.concatenate([cols[n] for n in ("qa", "ksa", "vsa", "kwa", "vwa", "qb", "kb", "vb",
                                              "qc", "kc", "vc", "qd", "kd", "vd")], axis=1)
    pad = jnp.zeros((w.shape[0], LANES - 32), w.dtype)
    small = jnp.concatenate([cols["ga"], cols["fc"], pad, cols["kca"], cols["vca"]], axis=1)
    return main.astype(BF16), small.astype(BF16)


def _compress_weights(pe, w1, w2):
    half = NSA_CMP_LEN // 2
    hid = NSA_CMP_HIDDEN
    w1r = w1.reshape(2, half, HEAD_DIM, hid)
    z = jnp.zeros_like(w1r)
    per_group = jnp.stack([jnp.concatenate([w1r, z], axis=-1), jnp.concatenate([z, w1r], axis=-1)], axis=2)
    w_halves = per_group.reshape(2, half * 2 * HEAD_DIM, 2 * hid).astype(BF16)
    pe_r = jnp.broadcast_to(pe.reshape(2, half, 1, HEAD_DIM), (2, half, 2, HEAD_DIM)).reshape(2, half * LANES)
    z2 = jnp.zeros_like(w2)
    w2g = jnp.concatenate([jnp.concatenate([w2, z2], axis=1), jnp.concatenate([z2, w2], axis=1)], axis=0)
    return pe_r, w_halves[0], w_halves[1], w2g.astype(BF16)


def _bias_tables(rel_bias, t):
    hm = 8
    tq_a = NSA_TQ_CMP
    tq_d = tk_d = min(NSA_TQ_DENSE, t)
    n_a = min(_n_toeplitz_tiles(tq_d, tk_d), t // tq_d)
    tab_a = _bias_tiles(rel_bias, _toeplitz_idx_t(n_a, tk_d, tq_d, tq_d), 0, 2, 4)
    ncp = t // NSA_CMP_STRIDE
    qi = jnp.arange(t // tq_a, dtype=jnp.int32)[:, None, None]
    r = jnp.arange(tq_a, dtype=jnp.int32)[None, :, None]
    c = jnp.arange(ncp, dtype=jnp.int32)[None, None, :]
    idx_c = _t5_bucket(qi * tq_a + r - (c * NSA_CMP_STRIDE + NSA_CMP_LEN - 1))
    tab_c = _bias_tiles(rel_bias, idx_c, 0, hm)
    tq_b = min(MOBA_TQ, t)
    n_b = min(_n_toeplitz_tiles(tq_b, tq_b), t // tq_b)
    tab_b = _bias_tiles(rel_bias, _toeplitz_idx_t(n_b, tq_b, tq_b, tq_b), hm, 4, 2)
    kr = jnp.arange(2 * DIL_BAND, dtype=jnp.int32)[None, :, None]
    qc = jnp.arange(DIL_BAND, dtype=jnp.int32)[None, None, :]
    tabs_d = [_bias_tiles(rel_bias, _t5_bucket((DIL_BAND + qc - kr) * dil), 2 * hm, 4, 2)
              for _, dil in LONGNET_PATTERNS]
    return tab_a, tab_c, tab_b, tabs_d


def _overlap_t(t):
    ncp = t // NSA_CMP_STRIDE
    nsp = max(t // NSA_SEL_LEN, 8)
    c_start = jnp.arange(ncp)[None, :] * NSA_CMP_STRIDE
    s_start = jnp.arange(nsp)[:, None] * NSA_SEL_LEN
    ovl = (c_start < s_start + NSA_SEL_LEN) & (c_start + NSA_CMP_LEN - 1 >= s_start)
    ovl = ovl & (jnp.arange(ncp)[None, :] < ncp - 1) & (jnp.arange(nsp)[:, None] < t // NSA_SEL_LEN)
    return jnp.concatenate([ovl, ovl], axis=1).astype(BF16)


def _transposed(p3, pb, n_blocks):
    b, t, _ = p3.shape
    vt = p3[:, :, pb * LANES:(pb + n_blocks) * LANES].transpose(0, 2, 1).reshape(b, 2 * n_blocks, HEAD_DIM, t)
    ones = jnp.ones((b, 2 * n_blocks, VT_ROWS - HEAD_DIM, t), BF16)
    return jnp.concatenate([vt, ones], axis=2).reshape(b, 2 * n_blocks * VT_ROWS, t)


def _mixer_layer(x2, b, t, w_main, w_small, g_pre, g_post, w_out, cmp_w_k, cmp_w_v, fox_bias, tabs, ovl_t):
    tab_a, tab_c, tab_b, tabs_d = tabs
    m = b * t
    p2, pd2, small, kca, vca = _proj(x2, g_pre, w_main, w_small)
    p3 = p2.reshape(b, t, P_WIDTH)
    pd3 = pd2.reshape(b, t, PD_WIDTH)
    small3 = small.reshape(b, t, LANES)
    row_w = NSA_CMP_STRIDE * LANES
    kc = _nsa_compress(kca.reshape(b, t // NSA_CMP_STRIDE, row_w), *cmp_w_k)
    vc = _nsa_compress(vca.reshape(b, t // NSA_CMP_STRIDE, row_w), *cmp_w_v)
    o_cmp, negsel = _nsa_cmp(p3, kc, vc, tab_c, ovl_t, small3)
    o_sel = _nsa_dense(p3, tab_a, small3, _transposed(p3, PB_VSA, 1), negsel)
    o_win = _nsa_dense(p3, tab_a, small3, _transposed(p3, PB_VWA, 1), None)
    o_b = _moba(p3, tab_b, _transposed(p3, PB_VB, 4))
    fox_bias_row = jnp.zeros((1, LANES), F32).at[0, FOX_HI:FOX_HI + 8].set(fox_bias)
    o_f = _fox(p3, _fox_cumsum(small3, fox_bias_row), _transposed(p3, PB_VC, 4))
    dil = [_dilated(pd3, tab, d) for tab, (_, d) in zip(tabs_d, LONGNET_PATTERNS)]
    heads = [a.reshape(m, 4 * LANES) for a in (o_cmp, o_sel, o_win, o_b, o_f)]
    return _out_proj(x2, heads, [d[0] for d in dil], [d[1] for d in dil], w_out, g_post)


def kernel(x, w_in, w_out, g_mix_pre, g_mix_post, g_mlp_pre, g_mlp_post, w_up, w_down, cmp_pe, phik_w1, phik_w2, phiv_w1, phiv_w2, fox_bias, rel_bias):
    b, t, d = x.shape
    depth = w_in.shape[0]
    tabs = _bias_tables(rel_bias, t)
    ovl_t = _overlap_t(t)
    x2 = x.reshape(b * t, d)
    for l in range(depth):
        w_main, w_small = _split_w_in(w_in[l])
        x2 = _mixer_layer(
            x2, b, t, w_main, w_small, g_mix_pre[l][None], g_mix_post[l][None], w_out[l].astype(BF16),
            _compress_weights(cmp_pe[l], phik_w1[l], phik_w2[l]),
            _compress_weights(cmp_pe[l], phiv_w1[l], phiv_w2[l]),
            fox_bias[l], tabs, ovl_t)
        x2 = _ffn(x2, g_mlp_pre[l][None], w_up[l].astype(BF16), w_down[l].astype(BF16), g_mlp_post[l][None])
    return x2.reshape(b, t, d)
```

```python
import functools
import math

import jax
import jax.numpy as jnp
from jax import lax
from jax.experimental import pallas as pl
from jax.experimental.pallas import tpu as pltpu

F32 = jnp.float32
BF16 = jnp.bfloat16

D_MODEL = 2048
D_FF = 4 * D_MODEL
HEAD_DIM = 64
LANES = 128
EPS = 1e-6
NEG = -1e30
BIG = 1e30
SCALE = HEAD_DIM ** -0.5
LOG2E = math.log2(math.e)
LN2 = math.log(2.0)
Q_FOLD = SCALE * LOG2E

N_BUCKETS = 32
MAX_EXACT = 16
MAX_DISTANCE = 4096
LAST_BUCKET_DIST = 2897

NSA_CMP_LEN = 32
NSA_CMP_STRIDE = 16
NSA_CMP_HIDDEN = 256
NSA_SEL_LEN = 64
NSA_TOP_N = 16
NSA_WINDOW = 512
MOBA_BLOCK = 256
MOBA_TOPK = 3
LONGNET_PATTERNS = ((128, 1), (512, 4), (2048, 16))
DIL_BAND = 128

FOX_TQ = 512
MOBA_TQ = 512
NSA_TQ_CMP = 256
NSA_TQ_DENSE = 256

P_WIDTH = 4096
PB_QA, PB_KSA, PB_VSA, PB_KWA, PB_VWA = 0, 4, 5, 6, 7
PB_QB, PB_KB, PB_VB = 8, 12, 16
PB_QC, PB_KC, PB_VC = 20, 24, 28
PD_WIDTH = 1536
GW = 4 * LANES
VT_ROWS = HEAD_DIM + 16

FOX_HI, FOX_MID, FOX_LO = 24, 32, 40

VMEM_LIMIT = 56 * 1024 * 1024


def _cparams(sem):
    return pltpu.CompilerParams(dimension_semantics=sem, vmem_limit_bytes=VMEM_LIMIT)


def _dot(a, b):
    return jnp.dot(a, b, preferred_element_type=F32)


def _dot_nt(a, b):
    return lax.dot_general(a, b, (((1,), (1,)), ((), ())), preferred_element_type=F32)


def _dot_hi(a, b):
    return jnp.dot(a, b, preferred_element_type=F32, precision=lax.Precision.HIGHEST)


def _spread_lanes(w, spread):
    hi = w.astype(BF16)
    lo = (w - hi.astype(F32)).astype(BF16)
    return _dot(jnp.concatenate([hi, lo], axis=1), spread)


def _t5_bucket(dist):
    dist = jnp.maximum(dist, 0)
    rel = jnp.log(jnp.maximum(dist, 1).astype(jnp.float32) / MAX_EXACT) / math.log(MAX_DISTANCE / MAX_EXACT)
    large = jnp.minimum(MAX_EXACT + (rel * (N_BUCKETS - MAX_EXACT)).astype(jnp.int32), N_BUCKETS - 1)
    return jnp.where(dist < MAX_EXACT, dist, large)


def _bias_tile_kernel(tab_ref, idx_ref, o_ref, *, head0, hpg, cols):
    grp = pl.program_id(0)
    idx = idx_ref[0]
    n_bits = N_BUCKETS.bit_length() - 1
    bit_set = [((idx >> b) & 1) == 1 for b in range(n_bits)]
    for hh in range(hpg):
        h = head0 + grp * hpg + hh
        level = [tab_ref[b, h] for b in range(N_BUCKETS)]
        for b in range(n_bits):
            level = [jnp.where(bit_set[b], level[2 * i + 1], level[2 * i]) for i in range(len(level) // 2)]
        o_ref[0, 0, :, hh * cols:(hh + 1) * cols] = level[0] * LOG2E


def _bias_tiles(rel_bias, idx, head0, n_groups, hpg=1):
    n, r, c = idx.shape
    return pl.pallas_call(
        functools.partial(_bias_tile_kernel, head0=head0, hpg=hpg, cols=c),
        out_shape=jax.ShapeDtypeStruct((n_groups, n, r, hpg * c), F32),
        grid=(n_groups, n),
        in_specs=[pl.BlockSpec(memory_space=pltpu.SMEM),
                  pl.BlockSpec((1, r, c), lambda g, i: (i, 0, 0))],
        out_specs=pl.BlockSpec((1, 1, r, hpg * c), lambda g, i: (g, i, 0, 0)),
        compiler_params=_cparams(("parallel", "parallel")),
        name="bias_tiles",
    )(rel_bias, idx)


def _toeplitz_idx_t(n_tiles, krows, qcols, q_stride):
    nd = jnp.arange(n_tiles, dtype=jnp.int32)[:, None, None]
    r = jnp.arange(krows, dtype=jnp.int32)[None, :, None]
    c = jnp.arange(qcols, dtype=jnp.int32)[None, None, :]
    return _t5_bucket(nd * q_stride + c - r)


def _n_toeplitz_tiles(q_stride, keys):
    return -(-(LAST_BUCKET_DIST + keys - 1) // q_stride) + 1


def _proj_kernel(x_ref, g_ref, w_ref, ws_ref, p_ref, pd_ref, s_ref, kc_ref, vc_ref, h_ref, *, n_main):
    j = pl.program_id(1)

    @pl.when(j == 0)
    def _():
        x = x_ref[...]
        y = x * lax.rsqrt(jnp.mean(x * x, axis=-1, keepdims=True) + EPS)
        hb = (y * g_ref[...]).astype(BF16)
        h_ref[...] = hb
        small = _dot(hb, ws_ref[...])
        s_ref[...] = small[:, 0:LANES]
        kc_ref[...] = small[:, LANES:2 * LANES]
        vc_ref[...] = small[:, 2 * LANES:3 * LANES]

    y = _dot(h_ref[...], w_ref[...]).astype(BF16)

    @pl.when(j < n_main)
    def _():
        p_ref[...] = y

    @pl.when(j >= n_main)
    def _():
        pd_ref[...] = y


def _proj(x2, g, w_main, w_small, tm=1024, tn=512):
    m = x2.shape[0]
    tm = min(tm, m)
    n_main = P_WIDTH // tn
    return pl.pallas_call(
        functools.partial(_proj_kernel, n_main=n_main),
        out_shape=(jax.ShapeDtypeStruct((m, P_WIDTH), BF16),
                   jax.ShapeDtypeStruct((m, PD_WIDTH), BF16),
                   jax.ShapeDtypeStruct((m, LANES), F32),
                   jax.ShapeDtypeStruct((m, LANES), F32),
                   jax.ShapeDtypeStruct((m, LANES), F32)),
        grid=(m // tm, (P_WIDTH + PD_WIDTH) // tn),
        in_specs=[pl.BlockSpec((tm, D_MODEL), lambda i, j: (i, 0)),
                  pl.BlockSpec((1, D_MODEL), lambda i, j: (0, 0)),
                  pl.BlockSpec((D_MODEL, tn), lambda i, j: (0, j)),
                  pl.BlockSpec((D_MODEL, 3 * LANES), lambda i, j: (0, 0))],
        out_specs=(pl.BlockSpec((tm, tn), lambda i, j: (i, jnp.minimum(j, n_main - 1))),
                   pl.BlockSpec((tm, tn), lambda i, j: (i, jnp.maximum(j - n_main, 0))),
                   pl.BlockSpec((tm, LANES), lambda i, j: (i, 0)),
                   pl.BlockSpec((tm, LANES), lambda i, j: (i, 0)),
                   pl.BlockSpec((tm, LANES), lambda i, j: (i, 0))),
        scratch_shapes=[pltpu.VMEM((tm, D_MODEL), BF16)],
        compiler_params=_cparams(("parallel", "arbitrary")),
        name="rms_in_proj",
    )(x2, g, w_main, w_small)


def _lane_lo(shape):
    return lax.broadcasted_iota(jnp.int32, shape, 1) < HEAD_DIM


def _pair_queries(q_ref):
    q = q_ref[...]
    lo = _lane_lo(q.shape)
    zero = jnp.zeros_like(q)
    return jnp.where(lo, q, zero), jnp.where(lo, zero, q)


def _col_softmax(s, m_ref, first):
    if first:
        m = jnp.max(s, axis=0, keepdims=True)
        m_ref[...] = m
        return jnp.exp2(s - m).astype(BF16), None
    m_prev = m_ref[...]
    m_new = jnp.maximum(m_prev, jnp.max(s, axis=0, keepdims=True))
    m_ref[...] = m_new
    return jnp.exp2(s - m_new).astype(BF16), jnp.exp2(m_prev - m_new)


def _accumulate(acc_ref, cols, pv, alpha):
    if alpha is None:
        acc_ref[:, cols] = pv
    else:
        acc_ref[:, cols] = alpha[:, cols] * acc_ref[:, cols] + pv


def _key_tiles(j_diag, j_first, n_past, qk, consume, sa_ref, sb_ref):
    last = j_first + jnp.maximum(n_past - 1, 0)
    sa_ref[...] = qk(j_first)
    consume(j_diag, qk(j_diag), True)

    def run(base, count):
        for t in range(count):
            cur, nxt = (sa_ref, sb_ref) if t % 2 == 0 else (sb_ref, sa_ref)
            nxt[...] = qk(jnp.minimum(base + t + 1, last))
            consume(base + t, cur[...], False)

    quads = n_past // 4
    lax.fori_loop(0, quads, lambda i, c: (run(j_first + 4 * i, 4), c)[1], 0)
    rest = n_past - 4 * quads

    @pl.when(rest >= 2)
    def _():
        run(j_first + 4 * quads, 2)

    @pl.when(rest % 2 == 1)
    def _():
        consume(j_first + n_past - 1, sa_ref[...], False)


def _top_rows(v, k):
    n = v.shape[0]
    rows = lax.broadcasted_iota(jnp.int32, v.shape, 0)
    chosen = jnp.zeros(v.shape, F32)
    for _ in range(k):
        top = jnp.max(v, axis=0, keepdims=True)
        first = jnp.min(jnp.where(v == top, rows, n), axis=0, keepdims=True)
        pick = rows == first
        chosen = jnp.where(pick, 1.0, chosen)
        v = jnp.where(pick, -jnp.inf, v)
    return chosen


def _top_rows_bisect(v, k):
    n = v.shape[0]
    key = pltpu.bitcast(v, jnp.int32)
    thr = jnp.zeros((1, v.shape[1]), jnp.int32)
    def enough(cand):
        return jnp.sum(jnp.where(key >= cand, 1.0, 0.0), axis=0, keepdims=True) >= k

    for bit in range(30, 0, -2):
        c_hi, c_lo = thr | (1 << bit), thr | (1 << (bit - 1))
        c_both = c_hi | (1 << (bit - 1))
        thr = jnp.where(enough(c_both), c_both, jnp.where(enough(c_hi), c_hi,
                                                          jnp.where(enough(c_lo), c_lo, thr)))
    thr = jnp.where(enough(thr | 1), thr | 1, thr)
    above = key > thr
    tie = key == thr
    n_above = jnp.sum(jnp.where(above, 1.0, 0.0), axis=0, keepdims=True)
    ri = lax.broadcasted_iota(jnp.int32, (n, n), 0)
    ci = lax.broadcasted_iota(jnp.int32, (n, n), 1)
    before = jnp.where(ci < ri, 1.0, 0.0).astype(BF16)
    ties_before = _dot(before, jnp.where(tie, 1.0, 0.0).astype(BF16))
    return above | (tie & (ties_before < k - n_above))


def _pair_finish(acc_ref, o_ref, tq):
    o_t = acc_ref[0:HEAD_DIM, :] / acc_ref[HEAD_DIM:HEAD_DIM + 1, :]
    o_ref[0] = jnp.concatenate([o_t[:, 0:tq], o_t[:, tq:2 * tq]], axis=0).T.astype(BF16)


def _fox_cumsum_kernel(s_ref, fb_ref, c_ref, carry_ref, *, tc):
    @pl.when(pl.program_id(1) == 0)
    def _():
        carry_ref[...] = jnp.zeros_like(carry_ref)

    lane = lax.broadcasted_iota(jnp.int32, (tc, LANES), 1)
    z = s_ref[0] + fb_ref[...]
    logf = -(jnp.maximum(-z, 0.0) + jnp.log(1.0 + jnp.exp(-jnp.abs(z))))
    logf = jnp.where((lane >= FOX_HI) & (lane < FOX_MID), logf, 0.0)
    ri = lax.broadcasted_iota(jnp.int32, (tc, tc), 0)
    ci = lax.broadcasted_iota(jnp.int32, (tc, tc), 1)
    lower = jnp.where(ri >= ci, 1.0, 0.0).astype(F32)
    c = _dot_hi(lower, logf) + carry_ref[...]
    carry_ref[...] = c[tc - 1:tc, :]
    c2 = c * LOG2E
    hi = c2.astype(BF16).astype(F32)
    rest = c2 - hi
    mid = rest.astype(BF16).astype(F32)
    lo = rest - mid
    pieces = hi + pltpu.roll(mid, FOX_MID - FOX_HI, axis=1) + pltpu.roll(lo, FOX_LO - FOX_HI, axis=1)
    c_ref[0] = pieces.astype(BF16)


def _fox_cumsum(small3, fox_bias_row, tc=512):
    b, t, _ = small3.shape
    tc = min(tc, t)
    return pl.pallas_call(
        functools.partial(_fox_cumsum_kernel, tc=tc),
        out_shape=jax.ShapeDtypeStruct((b, t, LANES), BF16),
        grid=(b, t // tc),
        in_specs=[pl.BlockSpec((1, tc, LANES), lambda i, j: (i, j, 0)),
                  pl.BlockSpec((1, LANES), lambda i, j: (0, 0))],
        out_specs=pl.BlockSpec((1, tc, LANES), lambda i, j: (i, j, 0)),
        scratch_shapes=[pltpu.VMEM((1, LANES), F32)],
        compiler_params=_cparams(("parallel", "arbitrary")),
        name="fox_cumsum",
    )(small3, fox_bias_row)


def _fox_kernel(q_ref, k_ref, c_ref, vt_ref, o_ref, kaug_ref, sa_ref, sb_ref, m_ref, acc_ref, *, tq):
    hp = pl.program_id(1)
    qi = pl.program_id(2)

    @pl.when(qi == 0)
    def _():
        kaug_ref[:, 0:LANES] = k_ref[0]
        kaug_ref[:, LANES:2 * LANES] = c_ref[0]

    qs = _pair_queries(q_ref.at[0])
    lane = lax.broadcasted_iota(jnp.int32, (tq, LANES), 1)
    rows = []
    for hh in range(2):
        h = 2 * hp + hh
        pick = (lane == FOX_HI + h) | (lane == FOX_MID + h) | (lane == FOX_LO + h)
        rows.append(jnp.concatenate([qs[hh], jnp.where(pick, -1.0, 0.0).astype(BF16)], axis=1))
    q_aug = jnp.concatenate(rows, axis=0)
    kr = lax.broadcasted_iota(jnp.int32, (tq, 2 * tq), 0)
    qc = lax.broadcasted_iota(jnp.int32, (tq, 2 * tq), 1) & (tq - 1)

    def qk(j):
        return _dot_nt(kaug_ref[pl.ds(pl.multiple_of(j * tq, tq), tq), :], q_aug)

    def consume(j, s, first):
        k0 = pl.multiple_of(j * tq, tq)
        if first:
            s = jnp.where(kr <= qc, s, NEG)
        p, alpha = _col_softmax(s, m_ref, first)
        for hh in range(2):
            cols = slice(hh * tq, (hh + 1) * tq)
            vt = vt_ref[0, hh * VT_ROWS:(hh + 1) * VT_ROWS, pl.ds(k0, tq)]
            _accumulate(acc_ref, cols, _dot(vt, p[:, cols]), alpha)

    _key_tiles(qi, 0, qi, qk, consume, sa_ref, sb_ref)
    _pair_finish(acc_ref, o_ref, tq)


def _fox(p3, c_aug, vt, tq=FOX_TQ):
    b, t, _ = p3.shape
    tq = min(tq, t)
    return pl.pallas_call(
        functools.partial(_fox_kernel, tq=tq),
        out_shape=jax.ShapeDtypeStruct((b, t, 4 * LANES), BF16),
        grid=(b, 4, t // tq),
        in_specs=[pl.BlockSpec((1, tq, LANES), lambda i, h, q: (i, q, PB_QC + h)),
                  pl.BlockSpec((1, t, LANES), lambda i, h, q: (i, 0, PB_KC + h)),
                  pl.BlockSpec((1, t, LANES), lambda i, h, q: (i, 0, 0)),
                  pl.BlockSpec((1, 2 * VT_ROWS, t), lambda i, h, q: (i, h, 0))],
        out_specs=pl.BlockSpec((1, tq, LANES), lambda i, h, q: (i, q, h)),
        scratch_shapes=[pltpu.VMEM((t, 2 * LANES), BF16),
                        pltpu.VMEM((tq, 2 * tq), F32), pltpu.VMEM((tq, 2 * tq), F32),
                        pltpu.VMEM((1, 2 * tq), F32), pltpu.VMEM((VT_ROWS, 2 * tq), F32)],
        compiler_params=_cparams(("parallel", "parallel", "arbitrary")),
        name="fox_attention",
    )(p3, p3, c_aug, vt)


def _moba_kernel(q_ref, k_ref, vt_ref, tab_ref, o_ref, km_ref, kaug_ref, sa_ref, sb_ref, m_ref, acc_ref,
                 *, nkb, nkp, n_tab, tq):
    blk = MOBA_BLOCK
    qi = pl.program_id(2)

    @pl.when(qi == 0)
    def _():
        km_ref[...] = jnp.zeros_like(km_ref)
        lane = lax.broadcasted_iota(jnp.int32, (blk, LANES), 1)
        for n in range(nkb):
            kb = k_ref[0, n * blk:(n + 1) * blk, :]
            km_ref[n:n + 1, :] = jnp.mean(kb.astype(F32), axis=0, keepdims=True)
            kaug_ref[n * blk:(n + 1) * blk, 0:LANES] = kb
            kaug_ref[n * blk:(n + 1) * blk, LANES:2 * LANES] = jnp.where(lane == n, 1.0, 0.0).astype(BF16)

    qs = _pair_queries(q_ref.at[0])
    q_both = jnp.concatenate([qs[0], qs[1]], axis=0)
    km = km_ref[...]
    km_hi = km.astype(BF16)
    km_lo = (km - km_hi.astype(F32)).astype(BF16)
    gate = _dot_nt(jnp.concatenate([km_hi, km_lo], axis=0), q_both)
    gate = gate[0:nkp] + gate[nkp:2 * nkp]
    n = lax.broadcasted_iota(jnp.int32, gate.shape, 0)
    q_pos = lax.broadcasted_iota(jnp.int32, gate.shape, 1) & (tq - 1)
    own = qi * (tq // blk) + q_pos // blk
    past = n < own
    gate = jnp.where(past, gate, NEG)
    allowed = ((_top_rows(gate, MOBA_TOPK) > 0.5) & past) | (n == own)
    negsel = jnp.where(allowed, 0.0, NEG).astype(F32)
    if nkp < LANES:
        negsel = jnp.concatenate([negsel, jnp.zeros((LANES - nkp, 2 * tq), F32)], axis=0)
    q_aug = jnp.concatenate([q_both, negsel.T.astype(BF16)], axis=1)

    kr = lax.broadcasted_iota(jnp.int32, (tq, 2 * tq), 0)
    qc = lax.broadcasted_iota(jnp.int32, (tq, 2 * tq), 1) & (tq - 1)

    def qk(j):
        return _dot_nt(kaug_ref[pl.ds(pl.multiple_of(j * tq, tq), tq), :], q_aug)

    def consume(j, s, first):
        k0 = pl.multiple_of(j * tq, tq)
        s = s + tab_ref[0, jnp.minimum(qi - j, n_tab - 1)]
        if first:
            s = jnp.where(kr <= qc, s, NEG)
        p, alpha = _col_softmax(s, m_ref, first)
        for hh in range(2):
            cols = slice(hh * tq, (hh + 1) * tq)
            vt = vt_ref[0, hh * VT_ROWS:(hh + 1) * VT_ROWS, pl.ds(k0, tq)]
            _accumulate(acc_ref, cols, _dot(vt, p[:, cols]), alpha)

    _key_tiles(qi, 0, qi, qk, consume, sa_ref, sb_ref)
    _pair_finish(acc_ref, o_ref, tq)


def _moba(p3, tab, vt):
    b, t, _ = p3.shape
    tq = tab.shape[2]
    nkb = t // MOBA_BLOCK
    nkp = -(-nkb // 8) * 8
    n_tab = tab.shape[1]
    return pl.pallas_call(
        functools.partial(_moba_kernel, nkb=nkb, nkp=nkp, n_tab=n_tab, tq=tq),
        out_shape=jax.ShapeDtypeStruct((b, t, 4 * LANES), BF16),
        grid=(b, 4, t // tq),
        in_specs=[pl.BlockSpec((1, tq, LANES), lambda i, h, q: (i, q, PB_QB + h)),
                  pl.BlockSpec((1, t, LANES), lambda i, h, q: (i, 0, PB_KB + h)),
                  pl.BlockSpec((1, 2 * VT_ROWS, t), lambda i, h, q: (i, h, 0)),
                  pl.BlockSpec((1, n_tab, tq, 2 * tq), lambda i, h, q: (h, 0, 0, 0),
                               pipeline_mode=pl.Buffered(1))],
        out_specs=pl.BlockSpec((1, tq, LANES), lambda i, h, q: (i, q, h)),
        scratch_shapes=[pltpu.VMEM((nkp, LANES), F32), pltpu.VMEM((t, 2 * LANES), BF16),
                        pltpu.VMEM((tq, 2 * tq), F32), pltpu.VMEM((tq, 2 * tq), F32),
                        pltpu.VMEM((1, 2 * tq), F32), pltpu.VMEM((VT_ROWS, 2 * tq), F32)],
        compiler_params=_cparams(("parallel", "parallel", "arbitrary")),
        name="moba_attention",
    )(p3, p3, vt, tab)


def _gelu_tanh(x):
    return 0.5 * x * (1.0 + jnp.tanh(math.sqrt(2.0 / math.pi) * (x + 0.044715 * (x * x * x))))


def _nsa_compress_kernel(r_ref, pe_ref, wlo_ref, whi_ref, w2_ref, o_ref, *, nr):
    r = r_ref[0]
    pe = pe_ref[...]
    a = _dot((r + pe[0:1, :]).astype(BF16), wlo_ref[...])
    b = _dot((r + pe[1:2, :]).astype(BF16), whi_ref[...])
    hid = _gelu_tanh(a + pltpu.roll(b, nr - 1, axis=0))
    o_ref[0] = _dot(hid.astype(BF16), w2_ref[...]).astype(BF16)


def _nsa_compress(r3, pe2, wlo, whi, w2):
    b, nr, w = r3.shape
    return pl.pallas_call(
        functools.partial(_nsa_compress_kernel, nr=nr),
        out_shape=jax.ShapeDtypeStruct((b, nr, LANES), BF16),
        grid=(b,),
        in_specs=[pl.BlockSpec((1, nr, w), lambda i: (i, 0, 0)),
                  pl.BlockSpec((2, w), lambda i: (0, 0)),
                  pl.BlockSpec((w, 2 * NSA_CMP_HIDDEN), lambda i: (0, 0)),
                  pl.BlockSpec((w, 2 * NSA_CMP_HIDDEN), lambda i: (0, 0)),
                  pl.BlockSpec((2 * NSA_CMP_HIDDEN, LANES), lambda i: (0, 0))],
        out_specs=pl.BlockSpec((1, nr, LANES), lambda i: (i, 0, 0)),
        compiler_params=_cparams(("parallel",)),
        name="nsa_compress",
    )(r3, pe2, wlo, whi, w2)


def _group_queries(q_ref, g):
    half = lax.broadcasted_iota(jnp.int32, (q_ref.shape[0], LANES), 1) // HEAD_DIM
    lo = half == 0
    in_group = half == g
    out = []
    for pb in range(2):
        q = q_ref[:, pb * LANES:(pb + 1) * LANES].astype(F32)
        qr = pltpu.roll(q, HEAD_DIM, axis=1)
        for dup in (jnp.where(lo, q, qr), jnp.where(lo, qr, q)):
            out.append(jnp.where(in_group, dup, 0.0).astype(BF16))
    return jnp.concatenate(out, axis=0)


def _group_gates(s_ref, g, branch):
    sig = 1.0 / (1.0 + jnp.exp(-s_ref[...]))
    col = lax.broadcasted_iota(jnp.int32, (2 * LANES, 2 * LANES), 0) & (LANES - 1)
    lane = lax.broadcasted_iota(jnp.int32, (2 * LANES, 2 * LANES), 1)
    spread = jnp.where(col == branch * 8 + g * 4 + lane // HEAD_DIM, 1.0, 0.0).astype(BF16)
    return _spread_lanes(sig, spread)


def _nsa_cmp_kernel(q_ref, kc_ref, vct_ref, tab_ref, ovl_ref, s_ref, o_ref, sel_ref, *, tq, ncp, nsp):
    g = pl.program_id(1)
    qi = pl.program_id(2)
    qs = _group_queries(q_ref.at[0], g)
    s = _dot_nt(kc_ref[0], qs) + tab_ref[0, 0]
    c = lax.broadcasted_iota(jnp.int32, (ncp, 4 * tq), 0)
    t = qi * tq + (lax.broadcasted_iota(jnp.int32, (ncp, 4 * tq), 1) & (tq - 1))
    ok = (t - (c * NSA_CMP_STRIDE + NSA_CMP_LEN - 1) >= 0) & (c < ncp - 1)
    s = jnp.where(ok, s, NEG)
    e = jnp.exp2(s - jnp.max(s, axis=0, keepdims=True))
    p = jnp.where(ok, e / jnp.sum(e, axis=0, keepdims=True), 0.0)
    vt = vct_ref[0, pl.ds(pl.multiple_of(g * VT_ROWS, VT_ROWS), VT_ROWS), :]
    o_t = _dot(vt, p.astype(BF16))[0:HEAD_DIM, :]
    o = jnp.concatenate([o_t[:, h * tq:(h + 1) * tq] for h in range(4)], axis=0).T
    o_ref[0] = (o * _group_gates(s_ref.at[0], g, 0)).astype(BF16)

    psum = p[:, 0:tq] + p[:, tq:2 * tq] + p[:, 2 * tq:3 * tq] + p[:, 3 * tq:4 * tq]
    hi = psum.astype(BF16)
    lo = (psum - hi.astype(F32)).astype(BF16)
    imp = _dot(ovl_ref[...], jnp.concatenate([hi, lo], axis=0))
    n = lax.broadcasted_iota(jnp.int32, (nsp, tq), 0)
    cur = (qi * tq + lax.broadcasted_iota(jnp.int32, (nsp, tq), 1)) // NSA_SEL_LEN
    forced = (n == 0) | (n == cur) | (n == cur - 1)
    valid = n <= cur
    imp = jnp.where(forced, BIG, jnp.where(valid, jnp.abs(imp), NEG))
    chosen = _top_rows_bisect(imp, NSA_TOP_N) & valid
    sel_ref[0] = jnp.where(chosen, 0.0, NEG).astype(F32).T.astype(BF16)


def _nsa_cmp(p3, kc, vct, tab, ovl_t, small3, tq=NSA_TQ_CMP):
    b, t, _ = p3.shape
    ncp = kc.shape[1]
    nsp = ovl_t.shape[0]
    tq = min(tq, t)
    return pl.pallas_call(
        functools.partial(_nsa_cmp_kernel, tq=tq, ncp=ncp, nsp=nsp),
        out_shape=(jax.ShapeDtypeStruct((b, t, 4 * LANES), BF16),
                   jax.ShapeDtypeStruct((b, t, 2 * nsp), BF16)),
        grid=(b, 2, t // tq),
        in_specs=[pl.BlockSpec((1, tq, 2 * LANES), lambda i, g, q: (i, q, g)),
                  pl.BlockSpec((1, ncp, LANES), lambda i, g, q: (i, 0, 0)),
                  pl.BlockSpec((1, 2 * VT_ROWS, ncp), lambda i, g, q: (i, 0, 0)),
                  pl.BlockSpec((1, 1, ncp, 4 * tq), lambda i, g, q: (g, q, 0, 0)),
                  pl.BlockSpec((nsp, 2 * ncp), lambda i, g, q: (0, 0)),
                  pl.BlockSpec((1, tq, LANES), lambda i, g, q: (i, q, 0))],
        out_specs=(pl.BlockSpec((1, tq, 2 * LANES), lambda i, g, q: (i, q, g)),
                   pl.BlockSpec((1, tq, nsp), lambda i, g, q: (i, q, g))),
        compiler_params=_cparams(("parallel", "parallel", "parallel")),
        name="nsa_compressed",
    )(p3, kc, vct, tab, ovl_t, small3)


def _nsa_dense_kernel(*refs, tq, tk, n_tab, selected, branch):
    if selected:
        (q_ref, k_ref, vt_ref, tab_ref, blk_ref, sel_ref, s_ref, o_ref,
         kaug_ref, sa_ref, sb_ref, m_ref, acc_ref) = refs
    else:
        q_ref, k_ref, vt_ref, tab_ref, s_ref, o_ref, sa_ref, sb_ref, m_ref, acc_ref = refs
    g = pl.program_id(1)
    qi = pl.program_id(2)
    qs = _group_queries(q_ref.at[0], g)
    kr = lax.broadcasted_iota(jnp.int32, (tk, 4 * tq), 0)
    qc = lax.broadcasted_iota(jnp.int32, (tk, 4 * tq), 1) & (tq - 1)
    j0 = (qi * tq) // tk
    g0 = pl.multiple_of(g * VT_ROWS, VT_ROWS)

    if selected:
        @pl.when(qi == 0)
        def _():
            kaug_ref[:, 0:LANES] = k_ref[0]
            kaug_ref[:, LANES:] = blk_ref[...]

        qs = jnp.concatenate([qs, jnp.concatenate([sel_ref[0]] * 4, axis=0)], axis=1)

        def qk(j):
            return _dot_nt(kaug_ref[pl.ds(pl.multiple_of(j * tk, tk), tk), :], qs)
    else:
        def qk(j):
            return _dot_nt(k_ref[0, pl.ds(pl.multiple_of(j * tk, tk), tk), :], qs)

    def consume(j, s, first):
        k0 = pl.multiple_of(j * tk, tk)
        s = s + tab_ref[0, jnp.minimum(qi - j * (tk // tq), n_tab - 1)]
        dist = (qi * tq - j * tk) + qc - kr
        if not selected:
            s = jnp.where((dist >= 0) & (dist <= NSA_WINDOW - 1), s, NEG)
        elif first:
            s = jnp.where(dist >= 0, s, NEG)
        p, alpha = _col_softmax(s, m_ref, first)
        vt = vt_ref[0, pl.ds(g0, VT_ROWS), pl.ds(k0, tk)]
        _accumulate(acc_ref, slice(None), _dot(vt, p), alpha)

    first_j = 0 if selected else jnp.maximum(j0 - (-(-(NSA_WINDOW - 1) // tk)), 0)
    _key_tiles(j0, first_j, j0 - first_j, qk, consume, sa_ref, sb_ref)
    o_t = acc_ref[0:HEAD_DIM, :] / acc_ref[HEAD_DIM:HEAD_DIM + 1, :]
    o = jnp.concatenate([o_t[:, h * tq:(h + 1) * tq] for h in range(4)], axis=0).T
    o_ref[0] = (o * _group_gates(s_ref.at[0], g, branch)).astype(BF16)


def _nsa_dense(p3, tab, small3, vt, negsel):
    b, t, _ = p3.shape
    tq = tk = min(NSA_TQ_DENSE, t)
    n_tab = tab.shape[1]
    selected = negsel is not None
    pb_k = PB_KSA if selected else PB_KWA
    in_specs = [pl.BlockSpec((1, tq, 2 * LANES), lambda i, g, q: (i, q, g)),
                pl.BlockSpec((1, t, LANES), lambda i, g, q: (i, 0, pb_k)),
                pl.BlockSpec((1, 2 * VT_ROWS, t), lambda i, g, q: (i, 0, 0)),
                pl.BlockSpec((1, n_tab, tk, 4 * tq), lambda i, g, q: (g, 0, 0, 0),
                             pipeline_mode=pl.Buffered(1))]
    args = [p3, p3, vt, tab]
    scratch = []
    if selected:
        nsp = negsel.shape[2] // 2
        block_of_key = (jnp.arange(t)[:, None] // NSA_SEL_LEN == jnp.arange(nsp)[None, :]).astype(BF16)
        in_specs += [pl.BlockSpec((t, nsp), lambda i, g, q: (0, 0)),
                     pl.BlockSpec((1, tq, nsp), lambda i, g, q: (i, q, g))]
        args += [block_of_key, negsel]
        scratch = [pltpu.VMEM((t, LANES + nsp), BF16)]
    in_specs.append(pl.BlockSpec((1, tq, LANES), lambda i, g, q: (i, q, 0)))
    args.append(small3)
    return pl.pallas_call(
        functools.partial(_nsa_dense_kernel, tq=tq, tk=tk, n_tab=n_tab, selected=selected,
                          branch=1 if selected else 2),
        out_shape=jax.ShapeDtypeStruct((b, t, 4 * LANES), BF16),
        grid=(b, 2, t // tq),
        in_specs=in_specs,
        out_specs=pl.BlockSpec((1, tq, 2 * LANES), lambda i, g, q: (i, q, g)),
        scratch_shapes=scratch + [pltpu.VMEM((tk, 4 * tq), F32), pltpu.VMEM((tk, 4 * tq), F32),
                                  pltpu.VMEM((1, 4 * tq), F32), pltpu.VMEM((VT_ROWS, 4 * tq), F32)],
        compiler_params=_cparams(("parallel", "parallel", "arbitrary")),
        name="nsa_selected" if selected else "nsa_window",
    )(*args)


def _dilated_kernel(q_ref, kp_ref, kc_ref, vtp_ref, vtc_ref, tab_ref, o_ref, lse_ref, *, tq):
    qi = pl.program_id(2)
    kr = lax.broadcasted_iota(jnp.int32, (2 * tq, 2 * tq), 0)
    qc = lax.broadcasted_iota(jnp.int32, (2 * tq, 2 * tq), 1) & (tq - 1)
    dist = tq + qc - kr
    valid = (dist >= 0) & (dist <= tq) & (kr >= jnp.where(qi > 0, 0, tq))
    outs, lses = [], []
    for pb in range(4):
        cols = slice(pb * LANES, (pb + 1) * LANES)
        qs = _pair_queries(q_ref.at[0, :, cols])
        q_both = jnp.concatenate([qs[0], qs[1]], axis=0)
        k2 = jnp.concatenate([kp_ref[0, :, cols], kc_ref[0, :, cols]], axis=0)
        s = jnp.where(valid, _dot_nt(k2, q_both) + tab_ref[pb, 0], NEG)
        m = jnp.max(s, axis=0, keepdims=True)
        p = jnp.exp2(s - m)
        l = jnp.sum(p, axis=0, keepdims=True)
        p = p.astype(BF16)
        lse = (m + jnp.log2(l)) * LN2
        for hh in range(2):
            rows = slice((2 * pb + hh) * HEAD_DIM, (2 * pb + hh + 1) * HEAD_DIM)
            qcols = slice(hh * tq, (hh + 1) * tq)
            vt = jnp.concatenate([vtp_ref[0, 0, rows, :], vtc_ref[0, 0, rows, :]], axis=1)
            outs.append(_dot(vt, p[:, qcols]) / l[:, qcols])
            lses.append(lse[:, qcols])
    o_ref[0] = jnp.concatenate(outs, axis=0).T.astype(BF16)
    lses.append(jnp.zeros((LANES - len(lses), tq), F32))
    lse_ref[0] = jnp.concatenate(lses, axis=0).T


def _dilated(pd3, tab, dil, tq=DIL_BAND):
    b, t, _ = pd3.shape
    l = t // dil
    qk = pd3[:, :, 0:2 * GW].reshape(b, l, dil * 2 * GW)
    vt = pd3[:, :, 2 * GW:3 * GW].reshape(b, l, dil, GW).transpose(0, 2, 3, 1)

    def prev(q):
        return jnp.maximum(q - 1, 0)

    out_spec = pl.BlockSpec((1, tq, GW), lambda i, r, q: (i, q, r))
    o, lse = pl.pallas_call(
        functools.partial(_dilated_kernel, tq=tq),
        out_shape=(jax.ShapeDtypeStruct((b, l, dil * GW), BF16),
                   jax.ShapeDtypeStruct((b, l, dil * LANES), F32)),
        grid=(b, dil, l // tq),
        in_specs=[pl.BlockSpec((1, tq, GW), lambda i, r, q: (i, q, 2 * r)),
                  pl.BlockSpec((1, tq, GW), lambda i, r, q: (i, prev(q), 2 * r + 1)),
                  pl.BlockSpec((1, tq, GW), lambda i, r, q: (i, q, 2 * r + 1)),
                  pl.BlockSpec((1, 1, GW, tq), lambda i, r, q: (i, r, 0, prev(q))),
                  pl.BlockSpec((1, 1, GW, tq), lambda i, r, q: (i, r, 0, q)),
                  pl.BlockSpec((4, 1, 2 * tq, 2 * tq), lambda i, r, q: (0, 0, 0, 0))],
        out_specs=(out_spec, pl.BlockSpec((1, tq, LANES), lambda i, r, q: (i, q, r))),
        compiler_params=_cparams(("parallel", "parallel", "parallel")),
        name="dilated_attention_%d" % dil,
    )(qk, qk, qk, vt, vt, tab)
    return o.reshape(b * t, GW), lse.reshape(b * t, LANES)


def _out_proj_kernel(x_ref, oc_ref, os_ref, ow_ref, ob_ref, of_ref,
                     d1_ref, d2_ref, d3_ref, l1_ref, l2_ref, l3_ref, w_ref, g_ref, o_ref, cat_ref):
    gw = 4 * LANES
    a = oc_ref[...].astype(F32) + os_ref[...].astype(F32) + ow_ref[...].astype(F32)
    cat_ref[:, 0:gw] = a.astype(BF16)
    cat_ref[:, gw:2 * gw] = ob_ref[...]
    cat_ref[:, 2 * gw:3 * gw] = of_ref[...]
    l1, l2, l3 = l1_ref[...], l2_ref[...], l3_ref[...]
    mx = jnp.maximum(jnp.maximum(l1, l2), l3)
    e1, e2, e3 = jnp.exp(l1 - mx), jnp.exp(l2 - mx), jnp.exp(l3 - mx)
    den = e1 + e2 + e3
    head = lax.broadcasted_iota(jnp.int32, (2 * LANES, gw), 0) & (LANES - 1)
    lane = lax.broadcasted_iota(jnp.int32, (2 * LANES, gw), 1)
    spread = jnp.where(head == lane // HEAD_DIM, 1.0, 0.0).astype(BF16)
    d = (_spread_lanes(e1 / den, spread) * d1_ref[...].astype(F32)
         + _spread_lanes(e2 / den, spread) * d2_ref[...].astype(F32)
         + _spread_lanes(e3 / den, spread) * d3_ref[...].astype(F32))
    cat_ref[:, 3 * gw:4 * gw] = d.astype(BF16)
    y = _dot(cat_ref[...], w_ref[...])
    y = y * lax.rsqrt(jnp.mean(y * y, axis=-1, keepdims=True) + EPS)
    o_ref[...] = x_ref[...] + y * g_ref[...]


def _out_proj(x2, heads, dils, lses, w_out, g, tm=256):
    m = x2.shape[0]
    gw = 4 * LANES
    row = lambda i: (i, 0)
    const = lambda i: (0, 0)
    in_specs = ([pl.BlockSpec((tm, D_MODEL), row)]
                + [pl.BlockSpec((tm, gw), row)] * 8 + [pl.BlockSpec((tm, LANES), row)] * 3
                + [pl.BlockSpec((D_MODEL, D_MODEL), const), pl.BlockSpec((1, D_MODEL), const)])
    return pl.pallas_call(
        _out_proj_kernel,
        out_shape=jax.ShapeDtypeStruct((m, D_MODEL), F32),
        grid=(m // tm,),
        in_specs=in_specs,
        out_specs=pl.BlockSpec((tm, D_MODEL), row),
        scratch_shapes=[pltpu.VMEM((tm, D_MODEL), BF16)],
        compiler_params=_cparams(("parallel",)),
        name="out_proj_residual",
    )(x2, *heads, *dils, *lses, w_out, g)


def _ffn_kernel(x_ref, gpre_ref, wu_ref, wd_ref, gpost_ref, o_ref, h_ref, acc_ref):
    f = pl.program_id(1)

    @pl.when(f == 0)
    def _():
        x = x_ref[...]
        y = x * lax.rsqrt(jnp.mean(x * x, axis=-1, keepdims=True) + EPS)
        h_ref[...] = (y * gpre_ref[...]).astype(BF16)
        acc_ref[...] = jnp.zeros_like(acc_ref)

    u = jnp.maximum(_dot(h_ref[...], wu_ref[...]), 0.0)
    acc_ref[...] += _dot((u * u).astype(BF16), wd_ref[...])

    @pl.when(f == pl.num_programs(1) - 1)
    def _():
        y = acc_ref[...]
        y = y * lax.rsqrt(jnp.mean(y * y, axis=-1, keepdims=True) + EPS)
        o_ref[...] = x_ref[...] + y * gpost_ref[...]


def _ffn(x2, g_pre, w_up, w_down, g_post, tm=512, tf=1024):
    m = x2.shape[0]
    return pl.pallas_call(
        _ffn_kernel,
        out_shape=jax.ShapeDtypeStruct((m, D_MODEL), F32),
        grid=(m // tm, D_FF // tf),
        in_specs=[pl.BlockSpec((tm, D_MODEL), lambda i, f: (i, 0)),
                  pl.BlockSpec((1, D_MODEL), lambda i, f: (0, 0)),
                  pl.BlockSpec((D_MODEL, tf), lambda i, f: (0, f)),
                  pl.BlockSpec((tf, D_MODEL), lambda i, f: (f, 0)),
                  pl.BlockSpec((1, D_MODEL), lambda i, f: (0, 0))],
        out_specs=pl.BlockSpec((tm, D_MODEL), lambda i, f: (i, 0)),
        scratch_shapes=[pltpu.VMEM((tm, D_MODEL), BF16), pltpu.VMEM((tm, D_MODEL), F32)],
        compiler_params=_cparams(("parallel", "arbitrary")),
        name="ffn_residual",
    )(x2, g_pre, w_up, w_down, g_post)


def _split_w_in(w):
    gw, kw = 4 * LANES, LANES
    o = 0
    cols = {}
    for name, width in (("qa", gw), ("kca", kw), ("vca", kw), ("ksa", kw), ("vsa", kw), ("kwa", kw),
                        ("vwa", kw), ("ga", 24), ("qb", gw), ("kb", gw), ("vb", gw), ("qc", gw),
                        ("kc", gw), ("vc", gw), ("fc", 8), ("qd", gw), ("kd", gw), ("vd", gw)):
        cols[name] = w[:, o:o + width] * (Q_FOLD if name[0] == "q" else 1.0)
        o += width
    main = jnp.concatenate([cols[n] for n in ("qa", "ksa", "vsa", "kwa", "vwa", "qb", "kb", "vb",
                                              "qc", "kc", "vc", "qd", "kd", "vd")], axis=1)
    pad = jnp.zeros((w.shape[0], LANES - 32), w.dtype)
    small = jnp.concatenate([cols["ga"], cols["fc"], pad, cols["kca"], cols["vca"]], axis=1)
    return main.astype(BF16), small.astype(BF16)


def _compress_weights(pe, w1, w2):
    half = NSA_CMP_LEN // 2
    hid = NSA_CMP_HIDDEN
    w1r = w1.reshape(2, half, HEAD_DIM, hid)
    z = jnp.zeros_like(w1r)
    per_group = jnp.stack([jnp.concatenate([w1r, z], axis=-1), jnp.concatenate([z, w1r], axis=-1)], axis=2)
    w_halves = per_group.reshape(2, half * 2 * HEAD_DIM, 2 * hid).astype(BF16)
    pe_r = jnp.broadcast_to(pe.reshape(2, half, 1, HEAD_DIM), (2, half, 2, HEAD_DIM)).reshape(2, half * LANES)
    z2 = jnp.zeros_like(w2)
    w2g = jnp.concatenate([jnp.concatenate([w2, z2], axis=1), jnp.concatenate([z2, w2], axis=1)], axis=0)
    return pe_r, w_halves[0], w_halves[1], w2g.astype(BF16)


def _bias_tables(rel_bias, t):
    hm = 8
    tq_a = min(NSA_TQ_CMP, t)
    tq_d = tk_d = min(NSA_TQ_DENSE, t)
    n_a = min(_n_toeplitz_tiles(tq_d, tk_d), t // tq_d)
    tab_a = _bias_tiles(rel_bias, _toeplitz_idx_t(n_a, tk_d, tq_d, tq_d), 0, 2, 4)
    ncp = t // NSA_CMP_STRIDE
    qi = jnp.arange(t // tq_a, dtype=jnp.int32)[:, None, None]
    c = jnp.arange(ncp, dtype=jnp.int32)[None, :, None]
    q = jnp.arange(tq_a, dtype=jnp.int32)[None, None, :]
    idx_c = _t5_bucket(qi * tq_a + q - (c * NSA_CMP_STRIDE + NSA_CMP_LEN - 1))
    tab_c = _bias_tiles(rel_bias, idx_c, 0, 2, 4)
    tq_b = min(MOBA_TQ, t)
    n_b = min(_n_toeplitz_tiles(tq_b, tq_b), t // tq_b)
    tab_b = _bias_tiles(rel_bias, _toeplitz_idx_t(n_b, tq_b, tq_b, tq_b), hm, 4, 2)
    kr = jnp.arange(2 * DIL_BAND, dtype=jnp.int32)[None, :, None]
    qc = jnp.arange(DIL_BAND, dtype=jnp.int32)[None, None, :]
    tabs_d = [_bias_tiles(rel_bias, _t5_bucket((DIL_BAND + qc - kr) * dil), 2 * hm, 4, 2)
              for _, dil in LONGNET_PATTERNS]
    return tab_a, tab_c, tab_b, tabs_d


def _overlap_t(t):
    ncp = t // NSA_CMP_STRIDE
    nsp = max(t // NSA_SEL_LEN, 8)
    c_start = jnp.arange(ncp)[None, :] * NSA_CMP_STRIDE
    s_start = jnp.arange(nsp)[:, None] * NSA_SEL_LEN
    ovl = (c_start < s_start + NSA_SEL_LEN) & (c_start + NSA_CMP_LEN - 1 >= s_start)
    ovl = ovl & (jnp.arange(ncp)[None, :] < ncp - 1) & (jnp.arange(nsp)[:, None] < t // NSA_SEL_LEN)
    return jnp.concatenate([ovl, ovl], axis=1).astype(BF16)


def _transposed(p3, pb, n_blocks):
    b, t, _ = p3.shape
    vt = p3[:, :, pb * LANES:(pb + n_blocks) * LANES].transpose(0, 2, 1).reshape(b, 2 * n_blocks, HEAD_DIM, t)
    ones = jnp.ones((b, 2 * n_blocks, VT_ROWS - HEAD_DIM, t), BF16)
    return jnp.concatenate([vt, ones], axis=2).reshape(b, 2 * n_blocks * VT_ROWS, t)


def _mixer_layer(x2, b, t, w_main, w_small, g_pre, g_post, w_out, cmp_w_k, cmp_w_v, fox_bias, tabs, ovl_t):
    tab_a, tab_c, tab_b, tabs_d = tabs
    m = b * t
    p2, pd2, small, kca, vca = _proj(x2, g_pre, w_main, w_small)
    p3 = p2.reshape(b, t, P_WIDTH)
    pd3 = pd2.reshape(b, t, PD_WIDTH)
    small3 = small.reshape(b, t, LANES)
    row_w = NSA_CMP_STRIDE * LANES
    kc = _nsa_compress(kca.reshape(b, t // NSA_CMP_STRIDE, row_w), *cmp_w_k)
    vc = _nsa_compress(vca.reshape(b, t // NSA_CMP_STRIDE, row_w), *cmp_w_v)
    o_cmp, negsel = _nsa_cmp(p3, kc, _transposed(vc, 0, 1), tab_c, ovl_t, small3)
    o_sel = _nsa_dense(p3, tab_a, small3, _transposed(p3, PB_VSA, 1), negsel)
    o_win = _nsa_dense(p3, tab_a, small3, _transposed(p3, PB_VWA, 1), None)
    o_b = _moba(p3, tab_b, _transposed(p3, PB_VB, 4))
    fox_bias_row = jnp.zeros((1, LANES), F32).at[0, FOX_HI:FOX_HI + 8].set(fox_bias)
    o_f = _fox(p3, _fox_cumsum(small3, fox_bias_row), _transposed(p3, PB_VC, 4))
    dil = [_dilated(pd3, tab, d) for tab, (_, d) in zip(tabs_d, LONGNET_PATTERNS)]
    heads = [a.reshape(m, 4 * LANES) for a in (o_cmp, o_sel, o_win, o_b, o_f)]
    return _out_proj(x2, heads, [d[0] for d in dil], [d[1] for d in dil], w_out, g_post)


def kernel(x, w_in, w_out, g_mix_pre, g_mix_post, g_mlp_pre, g_mlp_post, w_up, w_down, cmp_pe, phik_w1, phik_w2, phiv_w1, phiv_w2, fox_bias, rel_bias):
    b, t, d = x.shape
    depth = w_in.shape[0]
    tabs = _bias_tables(rel_bias, t)
    ovl_t = _overlap_t(t)
    x2 = x.reshape(b * t, d)
    for l in range(depth):
        w_main, w_small = _split_w_in(w_in[l])
        x2 = _mixer_layer(
            x2, b, t, w_main, w_small, g_mix_pre[l][None], g_mix_post[l][None], w_out[l].astype(BF16),
            _compress_weights(cmp_pe[l], phik_w1[l], phik_w2[l]),
            _compress_weights(cmp_pe[l], phiv_w1[l], phiv_w2[l]),
            fox_bias[l], tabs, ovl_t)
        x2 = _ffn(x2, g_mlp_pre[l][None], w_up[l].astype(BF16), w_down[l].astype(BF16), g_mlp_post[l][None])
    return x2.reshape(b, t, d)
```

```python
import functools
import math

import jax
import jax.numpy as jnp
from jax import lax
from jax.experimental import pallas as pl
from jax.experimental.pallas import tpu as pltpu

F32 = jnp.float32
BF16 = jnp.bfloat16

D_MODEL = 2048
D_FF = 4 * D_MODEL
HEAD_DIM = 64
LANES = 128
EPS = 1e-6
NEG = -1e30
BIG = 1e30
SCALE = HEAD_DIM ** -0.5
LOG2E = math.log2(math.e)
LN2 = math.log(2.0)
Q_FOLD = SCALE * LOG2E

N_BUCKETS = 32
MAX_EXACT = 16
MAX_DISTANCE = 4096
LAST_BUCKET_DIST = 2897

NSA_CMP_LEN = 32
NSA_CMP_STRIDE = 16
NSA_CMP_HIDDEN = 256
NSA_SEL_LEN = 64
NSA_TOP_N = 16
NSA_WINDOW = 512
MOBA_BLOCK = 256
MOBA_TOPK = 3
LONGNET_PATTERNS = ((128, 1), (512, 4), (2048, 16))
DIL_BAND = 128

FOX_TQ = 512
MOBA_TQ = 512
NSA_TQ_CMP = 256
NSA_TQ_DENSE = 256
PROJ_TM, PROJ_TN = 1024, 512
OUT_TM = 256
FFN_TM, FFN_TF = 512, 1024
CUMSUM_TC = 512

P_WIDTH = 4096
PB_QA, PB_KSA, PB_VSA, PB_KWA, PB_VWA = 0, 4, 5, 6, 7
PB_QB, PB_KB, PB_VB = 8, 12, 16
PB_QC, PB_KC, PB_VC = 20, 24, 28
PD_WIDTH = 1536
GW = 4 * LANES
BF16_SUBLANES = 16
VT_ROWS = HEAD_DIM + BF16_SUBLANES

FOX_HI, FOX_MID, FOX_LO = 24, 32, 40

V7X_VMEM_BYTES = 64 * 1024 * 1024
VMEM_LIMIT = V7X_VMEM_BYTES - 8 * 1024 * 1024


def _cparams(sem):
    return pltpu.CompilerParams(dimension_semantics=sem, vmem_limit_bytes=VMEM_LIMIT)


def _dot(a, b):
    return jnp.dot(a, b, preferred_element_type=F32)


def _dot_nt(a, b):
    return lax.dot_general(a, b, (((1,), (1,)), ((), ())), preferred_element_type=F32)


def _dot_hi(a, b):
    return jnp.dot(a, b, preferred_element_type=F32, precision=lax.Precision.HIGHEST)


def _spread_lanes(w, spread):
    hi = w.astype(BF16)
    lo = (w - hi.astype(F32)).astype(BF16)
    return _dot(jnp.concatenate([hi, lo], axis=1), spread)


def _t5_bucket(dist):
    dist = jnp.maximum(dist, 0)
    rel = jnp.log(jnp.maximum(dist, 1).astype(jnp.float32) / MAX_EXACT) / math.log(MAX_DISTANCE / MAX_EXACT)
    large = jnp.minimum(MAX_EXACT + (rel * (N_BUCKETS - MAX_EXACT)).astype(jnp.int32), N_BUCKETS - 1)
    return jnp.where(dist < MAX_EXACT, dist, large)


def _bias_tile_kernel(tab_ref, idx_ref, o_ref, *, head0, hpg, cols):
    grp = pl.program_id(0)
    idx = idx_ref[0]
    n_bits = N_BUCKETS.bit_length() - 1
    bit_set = [((idx >> b) & 1) == 1 for b in range(n_bits)]
    for hh in range(hpg):
        h = head0 + grp * hpg + hh
        level = [tab_ref[b, h] for b in range(N_BUCKETS)]
        for b in range(n_bits):
            level = [jnp.where(bit_set[b], level[2 * i + 1], level[2 * i]) for i in range(len(level) // 2)]
        o_ref[0, 0, :, hh * cols:(hh + 1) * cols] = level[0] * LOG2E


def _bias_tiles(rel_bias, idx, head0, n_groups, hpg=1):
    n, r, c = idx.shape
    return pl.pallas_call(
        functools.partial(_bias_tile_kernel, head0=head0, hpg=hpg, cols=c),
        out_shape=jax.ShapeDtypeStruct((n_groups, n, r, hpg * c), F32),
        grid=(n_groups, n),
        in_specs=[pl.BlockSpec(memory_space=pltpu.SMEM),
                  pl.BlockSpec((1, r, c), lambda g, i: (i, 0, 0))],
        out_specs=pl.BlockSpec((1, 1, r, hpg * c), lambda g, i: (g, i, 0, 0)),
        compiler_params=_cparams(("parallel", "parallel")),
        name="bias_tiles",
    )(rel_bias, idx)


def _toeplitz_idx_t(n_tiles, krows, qcols, q_stride):
    nd = jnp.arange(n_tiles, dtype=jnp.int32)[:, None, None]
    r = jnp.arange(krows, dtype=jnp.int32)[None, :, None]
    c = jnp.arange(qcols, dtype=jnp.int32)[None, None, :]
    return _t5_bucket(nd * q_stride + c - r)


def _n_toeplitz_tiles(q_stride, keys):
    return -(-(LAST_BUCKET_DIST + keys - 1) // q_stride) + 1


def _proj_kernel(x_ref, g_ref, w_ref, ws_ref, p_ref, pd_ref, s_ref, kc_ref, vc_ref, h_ref, *, n_main):
    j = pl.program_id(1)

    @pl.when(j == 0)
    def _():
        x = x_ref[...]
        y = x * lax.rsqrt(jnp.mean(x * x, axis=-1, keepdims=True) + EPS)
        hb = (y * g_ref[...]).astype(BF16)
        h_ref[...] = hb
        small = _dot(hb, ws_ref[...])
        s_ref[...] = small[:, 0:LANES]
        kc_ref[...] = small[:, LANES:2 * LANES]
        vc_ref[...] = small[:, 2 * LANES:3 * LANES]

    y = _dot(h_ref[...], w_ref[...]).astype(BF16)

    @pl.when(j < n_main)
    def _():
        p_ref[...] = y

    @pl.when(j >= n_main)
    def _():
        pd_ref[...] = y


def _proj(x2, g, w_main, w_small, tm=PROJ_TM, tn=PROJ_TN):
    m = x2.shape[0]
    tm = min(tm, m)
    n_main = P_WIDTH // tn
    return pl.pallas_call(
        functools.partial(_proj_kernel, n_main=n_main),
        out_shape=(jax.ShapeDtypeStruct((m, P_WIDTH), BF16),
                   jax.ShapeDtypeStruct((m, PD_WIDTH), BF16),
                   jax.ShapeDtypeStruct((m, LANES), F32),
                   jax.ShapeDtypeStruct((m, LANES), F32),
                   jax.ShapeDtypeStruct((m, LANES), F32)),
        grid=(m // tm, (P_WIDTH + PD_WIDTH) // tn),
        in_specs=[pl.BlockSpec((tm, D_MODEL), lambda i, j: (i, 0)),
                  pl.BlockSpec((1, D_MODEL), lambda i, j: (0, 0)),
                  pl.BlockSpec((D_MODEL, tn), lambda i, j: (0, j)),
                  pl.BlockSpec((D_MODEL, 3 * LANES), lambda i, j: (0, 0))],
        out_specs=(pl.BlockSpec((tm, tn), lambda i, j: (i, jnp.minimum(j, n_main - 1))),
                   pl.BlockSpec((tm, tn), lambda i, j: (i, jnp.maximum(j - n_main, 0))),
                   pl.BlockSpec((tm, LANES), lambda i, j: (i, 0)),
                   pl.BlockSpec((tm, LANES), lambda i, j: (i, 0)),
                   pl.BlockSpec((tm, LANES), lambda i, j: (i, 0))),
        scratch_shapes=[pltpu.VMEM((tm, D_MODEL), BF16)],
        compiler_params=_cparams(("parallel", "arbitrary")),
        name="rms_in_proj",
    )(x2, g, w_main, w_small)


def _lane_lo(shape):
    return lax.broadcasted_iota(jnp.int32, shape, 1) < HEAD_DIM


def _pair_queries(q_ref):
    q = q_ref[...]
    lo = _lane_lo(q.shape)
    zero = jnp.zeros_like(q)
    return jnp.where(lo, q, zero), jnp.where(lo, zero, q)


def _col_softmax(s, m_ref, first):
    if first:
        m = jnp.max(s, axis=0, keepdims=True)
        m_ref[...] = m
        return jnp.exp2(s - m).astype(BF16), None
    m_prev = m_ref[...]
    m_new = jnp.maximum(m_prev, jnp.max(s, axis=0, keepdims=True))
    m_ref[...] = m_new
    return jnp.exp2(s - m_new).astype(BF16), jnp.exp2(m_prev - m_new)


def _accumulate(acc_ref, cols, pv, alpha):
    if alpha is None:
        acc_ref[:, cols] = pv
    else:
        acc_ref[:, cols] = alpha[:, cols] * acc_ref[:, cols] + pv


def _key_tiles(j_diag, j_first, n_past, qk, consume, sa_ref, sb_ref):
    last = j_first + jnp.maximum(n_past - 1, 0)
    sa_ref[...] = qk(j_first)
    consume(j_diag, qk(j_diag), True)

    def run(base, count):
        for t in range(count):
            cur, nxt = (sa_ref, sb_ref) if t % 2 == 0 else (sb_ref, sa_ref)
            nxt[...] = qk(jnp.minimum(base + t + 1, last))
            consume(base + t, cur[...], False)

    quads = n_past // 4
    lax.fori_loop(0, quads, lambda i, c: (run(j_first + 4 * i, 4), c)[1], 0)
    rest = n_past - 4 * quads

    @pl.when(rest >= 2)
    def _():
        run(j_first + 4 * quads, 2)

    @pl.when(rest % 2 == 1)
    def _():
        consume(j_first + n_past - 1, sa_ref[...], False)


def _top_rows(v, k):
    n = v.shape[0]
    rows = lax.broadcasted_iota(jnp.int32, v.shape, 0)
    chosen = jnp.zeros(v.shape, F32)
    for _ in range(k):
        top = jnp.max(v, axis=0, keepdims=True)
        first = jnp.min(jnp.where(v == top, rows, n), axis=0, keepdims=True)
        pick = rows == first
        chosen = jnp.where(pick, 1.0, chosen)
        v = jnp.where(pick, -jnp.inf, v)
    return chosen


def _top_rows_bisect(v, k):
    n = v.shape[0]
    key = pltpu.bitcast(v, jnp.int32)
    thr = jnp.zeros((1, v.shape[1]), jnp.int32)
    def enough(cand):
        return jnp.sum(jnp.where(key >= cand, 1.0, 0.0), axis=0, keepdims=True) >= k

    for bit in range(30, 0, -2):
        c_hi, c_lo = thr | (1 << bit), thr | (1 << (bit - 1))
        c_both = c_hi | (1 << (bit - 1))
        thr = jnp.where(enough(c_both), c_both, jnp.where(enough(c_hi), c_hi,
                                                          jnp.where(enough(c_lo), c_lo, thr)))
    thr = jnp.where(enough(thr | 1), thr | 1, thr)
    above = key > thr
    tie = key == thr
    n_above = jnp.sum(jnp.where(above, 1.0, 0.0), axis=0, keepdims=True)
    ri = lax.broadcasted_iota(jnp.int32, (n, n), 0)
    ci = lax.broadcasted_iota(jnp.int32, (n, n), 1)
    before = jnp.where(ci < ri, 1.0, 0.0).astype(BF16)
    ties_before = _dot(before, jnp.where(tie, 1.0, 0.0).astype(BF16))
    return above | (tie & (ties_before < k - n_above))


def _pair_finish(acc_ref, o_ref, tq):
    o_t = acc_ref[0:HEAD_DIM, :] / acc_ref[HEAD_DIM:HEAD_DIM + 1, :]
    o_ref[0] = jnp.concatenate([o_t[:, 0:tq], o_t[:, tq:2 * tq]], axis=0).T.astype(BF16)


def _fox_cumsum_kernel(s_ref, fb_ref, c_ref, carry_ref, *, tc):
    @pl.when(pl.program_id(1) == 0)
    def _():
        carry_ref[...] = jnp.zeros_like(carry_ref)

    lane = lax.broadcasted_iota(jnp.int32, (tc, LANES), 1)
    z = s_ref[0] + fb_ref[...]
    logf = -(jnp.maximum(-z, 0.0) + jnp.log(1.0 + jnp.exp(-jnp.abs(z))))
    logf = jnp.where((lane >= FOX_HI) & (lane < FOX_MID), logf, 0.0)
    ri = lax.broadcasted_iota(jnp.int32, (tc, tc), 0)
    ci = lax.broadcasted_iota(jnp.int32, (tc, tc), 1)
    lower = jnp.where(ri >= ci, 1.0, 0.0).astype(F32)
    c = _dot_hi(lower, logf) + carry_ref[...]
    carry_ref[...] = c[tc - 1:tc, :]
    c2 = c * LOG2E
    hi = c2.astype(BF16).astype(F32)
    rest = c2 - hi
    mid = rest.astype(BF16).astype(F32)
    lo = rest - mid
    pieces = hi + pltpu.roll(mid, FOX_MID - FOX_HI, axis=1) + pltpu.roll(lo, FOX_LO - FOX_HI, axis=1)
    c_ref[0] = pieces.astype(BF16)


def _fox_cumsum(small3, fox_bias_row, tc=CUMSUM_TC):
    b, t, _ = small3.shape
    tc = min(tc, t)
    return pl.pallas_call(
        functools.partial(_fox_cumsum_kernel, tc=tc),
        out_shape=jax.ShapeDtypeStruct((b, t, LANES), BF16),
        grid=(b, t // tc),
        in_specs=[pl.BlockSpec((1, tc, LANES), lambda i, j: (i, j, 0)),
                  pl.BlockSpec((1, LANES), lambda i, j: (0, 0))],
        out_specs=pl.BlockSpec((1, tc, LANES), lambda i, j: (i, j, 0)),
        scratch_shapes=[pltpu.VMEM((1, LANES), F32)],
        compiler_params=_cparams(("parallel", "arbitrary")),
        name="fox_cumsum",
    )(small3, fox_bias_row)


def _fox_kernel(q_ref, k_ref, c_ref, vt_ref, o_ref, kaug_ref, sa_ref, sb_ref, m_ref, acc_ref, *, tq):
    hp = pl.program_id(1)
    qi = pl.program_id(2)

    @pl.when(qi == 0)
    def _():
        kaug_ref[:, 0:LANES] = k_ref[0]
        kaug_ref[:, LANES:2 * LANES] = c_ref[0]

    qs = _pair_queries(q_ref.at[0])
    lane = lax.broadcasted_iota(jnp.int32, (tq, LANES), 1)
    rows = []
    for hh in range(2):
        h = 2 * hp + hh
        pick = (lane == FOX_HI + h) | (lane == FOX_MID + h) | (lane == FOX_LO + h)
        rows.append(jnp.concatenate([qs[hh], jnp.where(pick, -1.0, 0.0).astype(BF16)], axis=1))
    q_aug = jnp.concatenate(rows, axis=0)
    kr = lax.broadcasted_iota(jnp.int32, (tq, 2 * tq), 0)
    qc = lax.broadcasted_iota(jnp.int32, (tq, 2 * tq), 1) & (tq - 1)

    def qk(j):
        return _dot_nt(kaug_ref[pl.ds(pl.multiple_of(j * tq, tq), tq), :], q_aug)

    def consume(j, s, first):
        k0 = pl.multiple_of(j * tq, tq)
        if first:
            s = jnp.where(kr <= qc, s, NEG)
        p, alpha = _col_softmax(s, m_ref, first)
        for hh in range(2):
            cols = slice(hh * tq, (hh + 1) * tq)
            vt = vt_ref[0, hh * VT_ROWS:(hh + 1) * VT_ROWS, pl.ds(k0, tq)]
            _accumulate(acc_ref, cols, _dot(vt, p[:, cols]), alpha)

    _key_tiles(qi, 0, qi, qk, consume, sa_ref, sb_ref)
    _pair_finish(acc_ref, o_ref, tq)


def _fox(p3, c_aug, vt, tq=FOX_TQ):
    b, t, _ = p3.shape
    tq = min(tq, t)
    return pl.pallas_call(
        functools.partial(_fox_kernel, tq=tq),
        out_shape=jax.ShapeDtypeStruct((b, t, GW), BF16),
        grid=(b, 4, t // tq),
        in_specs=[pl.BlockSpec((1, tq, LANES), lambda i, h, q: (i, q, PB_QC + h)),
                  pl.BlockSpec((1, t, LANES), lambda i, h, q: (i, 0, PB_KC + h)),
                  pl.BlockSpec((1, t, LANES), lambda i, h, q: (i, 0, 0)),
                  pl.BlockSpec((1, 2 * VT_ROWS, t), lambda i, h, q: (i, h, 0))],
        out_specs=pl.BlockSpec((1, tq, LANES), lambda i, h, q: (i, q, h)),
        scratch_shapes=[pltpu.VMEM((t, 2 * LANES), BF16),
                        pltpu.VMEM((tq, 2 * tq), F32), pltpu.VMEM((tq, 2 * tq), F32),
                        pltpu.VMEM((1, 2 * tq), F32), pltpu.VMEM((VT_ROWS, 2 * tq), F32)],
        compiler_params=_cparams(("parallel", "parallel", "arbitrary")),
        name="fox_attention",
    )(p3, p3, c_aug, vt)


def _moba_kernel(q_ref, k_ref, vt_ref, tab_ref, o_ref, km_ref, kaug_ref, sa_ref, sb_ref, m_ref, acc_ref,
                 *, nkb, nkp, n_tab, tq):
    blk = MOBA_BLOCK
    qi = pl.program_id(2)

    @pl.when(qi == 0)
    def _():
        km_ref[...] = jnp.zeros_like(km_ref)
        lane = lax.broadcasted_iota(jnp.int32, (blk, LANES), 1)
        for n in range(nkb):
            kb = k_ref[0, n * blk:(n + 1) * blk, :]
            km_ref[n:n + 1, :] = jnp.mean(kb.astype(F32), axis=0, keepdims=True)
            kaug_ref[n * blk:(n + 1) * blk, 0:LANES] = kb
            kaug_ref[n * blk:(n + 1) * blk, LANES:2 * LANES] = jnp.where(lane == n, 1.0, 0.0).astype(BF16)

    qs = _pair_queries(q_ref.at[0])
    q_both = jnp.concatenate([qs[0], qs[1]], axis=0)
    km = km_ref[...]
    km_hi = km.astype(BF16)
    km_lo = (km - km_hi.astype(F32)).astype(BF16)
    gate = _dot_nt(jnp.concatenate([km_hi, km_lo], axis=0), q_both)
    gate = gate[0:nkp] + gate[nkp:2 * nkp]
    n = lax.broadcasted_iota(jnp.int32, gate.shape, 0)
    q_pos = lax.broadcasted_iota(jnp.int32, gate.shape, 1) & (tq - 1)
    own = qi * (tq // blk) + q_pos // blk
    past = n < own
    gate = jnp.where(past, gate, NEG)
    allowed = ((_top_rows(gate, MOBA_TOPK) > 0.5) & past) | (n == own)
    negsel = jnp.where(allowed, 0.0, NEG).astype(F32)
    if nkp < LANES:
        negsel = jnp.concatenate([negsel, jnp.zeros((LANES - nkp, 2 * tq), F32)], axis=0)
    q_aug = jnp.concatenate([q_both, negsel.T.astype(BF16)], axis=1)

    kr = lax.broadcasted_iota(jnp.int32, (tq, 2 * tq), 0)
    qc = lax.broadcasted_iota(jnp.int32, (tq, 2 * tq), 1) & (tq - 1)

    def qk(j):
        return _dot_nt(kaug_ref[pl.ds(pl.multiple_of(j * tq, tq), tq), :], q_aug)

    def consume(j, s, first):
        k0 = pl.multiple_of(j * tq, tq)
        s = s + tab_ref[0, jnp.minimum(qi - j, n_tab - 1)]
        if first:
            s = jnp.where(kr <= qc, s, NEG)
        p, alpha = _col_softmax(s, m_ref, first)
        for hh in range(2):
            cols = slice(hh * tq, (hh + 1) * tq)
            vt = vt_ref[0, hh * VT_ROWS:(hh + 1) * VT_ROWS, pl.ds(k0, tq)]
            _accumulate(acc_ref, cols, _dot(vt, p[:, cols]), alpha)

    _key_tiles(qi, 0, qi, qk, consume, sa_ref, sb_ref)
    _pair_finish(acc_ref, o_ref, tq)


def _moba(p3, tab, vt):
    b, t, _ = p3.shape
    tq = tab.shape[2]
    nkb = t // MOBA_BLOCK
    nkp = -(-nkb // 8) * 8
    n_tab = tab.shape[1]
    return pl.pallas_call(
        functools.partial(_moba_kernel, nkb=nkb, nkp=nkp, n_tab=n_tab, tq=tq),
        out_shape=jax.ShapeDtypeStruct((b, t, GW), BF16),
        grid=(b, 4, t // tq),
        in_specs=[pl.BlockSpec((1, tq, LANES), lambda i, h, q: (i, q, PB_QB + h)),
                  pl.BlockSpec((1, t, LANES), lambda i, h, q: (i, 0, PB_KB + h)),
                  pl.BlockSpec((1, 2 * VT_ROWS, t), lambda i, h, q: (i, h, 0)),
                  pl.BlockSpec((1, n_tab, tq, 2 * tq), lambda i, h, q: (h, 0, 0, 0),
                               pipeline_mode=pl.Buffered(1))],
        out_specs=pl.BlockSpec((1, tq, LANES), lambda i, h, q: (i, q, h)),
        scratch_shapes=[pltpu.VMEM((nkp, LANES), F32), pltpu.VMEM((t, 2 * LANES), BF16),
                        pltpu.VMEM((tq, 2 * tq), F32), pltpu.VMEM((tq, 2 * tq), F32),
                        pltpu.VMEM((1, 2 * tq), F32), pltpu.VMEM((VT_ROWS, 2 * tq), F32)],
        compiler_params=_cparams(("parallel", "parallel", "arbitrary")),
        name="moba_attention",
    )(p3, p3, vt, tab)


def _gelu_tanh(x):
    return 0.5 * x * (1.0 + jnp.tanh(math.sqrt(2.0 / math.pi) * (x + 0.044715 * (x * x * x))))


def _nsa_compress_kernel(r_ref, pe_ref, wlo_ref, whi_ref, w2_ref, o_ref, *, nr):
    r = r_ref[0]
    pe = pe_ref[...]
    a = _dot((r + pe[0:1, :]).astype(BF16), wlo_ref[...])
    b = _dot((r + pe[1:2, :]).astype(BF16), whi_ref[...])
    hid = _gelu_tanh(a + pltpu.roll(b, nr - 1, axis=0))
    o_ref[0] = _dot(hid.astype(BF16), w2_ref[...]).astype(BF16)


def _nsa_compress(r3, pe2, wlo, whi, w2):
    b, nr, w = r3.shape
    return pl.pallas_call(
        functools.partial(_nsa_compress_kernel, nr=nr),
        out_shape=jax.ShapeDtypeStruct((b, nr, LANES), BF16),
        grid=(b,),
        in_specs=[pl.BlockSpec((1, nr, w), lambda i: (i, 0, 0)),
                  pl.BlockSpec((2, w), lambda i: (0, 0)),
                  pl.BlockSpec((w, 2 * NSA_CMP_HIDDEN), lambda i: (0, 0)),
                  pl.BlockSpec((w, 2 * NSA_CMP_HIDDEN), lambda i: (0, 0)),
                  pl.BlockSpec((2 * NSA_CMP_HIDDEN, LANES), lambda i: (0, 0))],
        out_specs=pl.BlockSpec((1, nr, LANES), lambda i: (i, 0, 0)),
        compiler_params=_cparams(("parallel",)),
        name="nsa_compress",
    )(r3, pe2, wlo, whi, w2)


def _group_queries(q_ref, g):
    half = lax.broadcasted_iota(jnp.int32, (q_ref.shape[0], LANES), 1) // HEAD_DIM
    lo = half == 0
    in_group = half == g
    out = []
    for pb in range(2):
        q = q_ref[:, pb * LANES:(pb + 1) * LANES].astype(F32)
        qr = pltpu.roll(q, HEAD_DIM, axis=1)
        for dup in (jnp.where(lo, q, qr), jnp.where(lo, qr, q)):
            out.append(jnp.where(in_group, dup, 0.0).astype(BF16))
    return jnp.concatenate(out, axis=0)


def _group_gates(s_ref, g, branch):
    sig = 1.0 / (1.0 + jnp.exp(-s_ref[...]))
    col = lax.broadcasted_iota(jnp.int32, (2 * LANES, 2 * LANES), 0) & (LANES - 1)
    lane = lax.broadcasted_iota(jnp.int32, (2 * LANES, 2 * LANES), 1)
    spread = jnp.where(col == branch * 8 + g * 4 + lane // HEAD_DIM, 1.0, 0.0).astype(BF16)
    return _spread_lanes(sig, spread)


def _nsa_cmp_kernel(q_ref, kc_ref, vct_ref, tab_ref, ovl_ref, s_ref, o_ref, sel_ref, *, tq, ncp, nsp):
    g = pl.program_id(1)
    qi = pl.program_id(2)
    qs = _group_queries(q_ref.at[0], g)
    s = _dot_nt(kc_ref[0], qs) + tab_ref[0, 0]
    c = lax.broadcasted_iota(jnp.int32, (ncp, 4 * tq), 0)
    t = qi * tq + (lax.broadcasted_iota(jnp.int32, (ncp, 4 * tq), 1) & (tq - 1))
    ok = (t - (c * NSA_CMP_STRIDE + NSA_CMP_LEN - 1) >= 0) & (c < ncp - 1)
    s = jnp.where(ok, s, NEG)
    e = jnp.exp2(s - jnp.max(s, axis=0, keepdims=True))
    p = jnp.where(ok, e / jnp.sum(e, axis=0, keepdims=True), 0.0)
    vt = vct_ref[0, pl.ds(pl.multiple_of(g * VT_ROWS, VT_ROWS), VT_ROWS), :]
    o_t = _dot(vt, p.astype(BF16))[0:HEAD_DIM, :]
    o = jnp.concatenate([o_t[:, h * tq:(h + 1) * tq] for h in range(4)], axis=0).T
    o_ref[0] = (o * _group_gates(s_ref.at[0], g, 0)).astype(BF16)

    psum = p[:, 0:tq] + p[:, tq:2 * tq] + p[:, 2 * tq:3 * tq] + p[:, 3 * tq:4 * tq]
    hi = psum.astype(BF16)
    lo = (psum - hi.astype(F32)).astype(BF16)
    imp = _dot(ovl_ref[...], jnp.concatenate([hi, lo], axis=0))
    n = lax.broadcasted_iota(jnp.int32, (nsp, tq), 0)
    cur = (qi * tq + lax.broadcasted_iota(jnp.int32, (nsp, tq), 1)) // NSA_SEL_LEN
    forced = (n == 0) | (n == cur) | (n == cur - 1)
    valid = n <= cur
    imp = jnp.where(forced, BIG, jnp.where(valid, jnp.abs(imp), NEG))
    chosen = _top_rows_bisect(imp, NSA_TOP_N) & valid
    sel_ref[0] = jnp.where(chosen, 0.0, NEG).astype(F32).T.astype(BF16)


def _nsa_cmp(p3, kc, vct, tab, ovl_t, small3, tq=NSA_TQ_CMP):
    b, t, _ = p3.shape
    ncp = kc.shape[1]
    nsp = ovl_t.shape[0]
    tq = min(tq, t)
    return pl.pallas_call(
        functools.partial(_nsa_cmp_kernel, tq=tq, ncp=ncp, nsp=nsp),
        out_shape=(jax.ShapeDtypeStruct((b, t, GW), BF16),
                   jax.ShapeDtypeStruct((b, t, 2 * nsp), BF16)),
        grid=(b, 2, t // tq),
        in_specs=[pl.BlockSpec((1, tq, 2 * LANES), lambda i, g, q: (i, q, g)),
                  pl.BlockSpec((1, ncp, LANES), lambda i, g, q: (i, 0, 0)),
                  pl.BlockSpec((1, 2 * VT_ROWS, ncp), lambda i, g, q: (i, 0, 0)),
                  pl.BlockSpec((1, 1, ncp, 4 * tq), lambda i, g, q: (g, q, 0, 0)),
                  pl.BlockSpec((nsp, 2 * ncp), lambda i, g, q: (0, 0)),
                  pl.BlockSpec((1, tq, LANES), lambda i, g, q: (i, q, 0))],
        out_specs=(pl.BlockSpec((1, tq, 2 * LANES), lambda i, g, q: (i, q, g)),
                   pl.BlockSpec((1, tq, nsp), lambda i, g, q: (i, q, g))),
        compiler_params=_cparams(("parallel", "parallel", "parallel")),
        name="nsa_compressed",
    )(p3, kc, vct, tab, ovl_t, small3)


def _nsa_dense_kernel(q_ref, ks_ref, vts_ref, kw_ref, vtw_ref, tab_ref, blk_ref, sel_ref, s_ref, o_ref,
                      kaug_ref, sa_ref, sb_ref, m_ref, acc_ref, *, tq, tk, n_tab):
    g = pl.program_id(1)
    qi = pl.program_id(2)
    qs = _group_queries(q_ref.at[0], g)
    kr = lax.broadcasted_iota(jnp.int32, (tk, 4 * tq), 0)
    qc = lax.broadcasted_iota(jnp.int32, (tk, 4 * tq), 1) & (tq - 1)
    j0 = (qi * tq) // tk
    g0 = pl.multiple_of(g * VT_ROWS, VT_ROWS)

    @pl.when(qi == 0)
    def _():
        kaug_ref[:, 0:LANES] = ks_ref[0]
        kaug_ref[:, LANES:] = blk_ref[...]

    qs_sel = jnp.concatenate([qs, jnp.concatenate([sel_ref[0]] * 4, axis=0)], axis=1)

    def branch(selected):
        k_tile = ((lambda k0: kaug_ref[pl.ds(k0, tk), :]) if selected
                  else (lambda k0: kw_ref[0, pl.ds(k0, tk), :]))
        q_all = qs_sel if selected else qs
        vt_ref = vts_ref if selected else vtw_ref

        def qk(j):
            return _dot_nt(k_tile(pl.multiple_of(j * tk, tk)), q_all)

        def consume(j, s, first):
            k0 = pl.multiple_of(j * tk, tk)
            s = s + tab_ref[0, jnp.minimum(qi - j * (tk // tq), n_tab - 1)]
            dist = (qi * tq - j * tk) + qc - kr
            if not selected:
                s = jnp.where((dist >= 0) & (dist <= NSA_WINDOW - 1), s, NEG)
            elif first:
                s = jnp.where(dist >= 0, s, NEG)
            p, alpha = _col_softmax(s, m_ref, first)
            vt = vt_ref[0, pl.ds(g0, VT_ROWS), pl.ds(k0, tk)]
            _accumulate(acc_ref, slice(None), _dot(vt, p), alpha)

        first_j = 0 if selected else jnp.maximum(j0 - (-(-(NSA_WINDOW - 1) // tk)), 0)
        _key_tiles(j0, first_j, j0 - first_j, qk, consume, sa_ref, sb_ref)
        o_t = acc_ref[0:HEAD_DIM, :] / acc_ref[HEAD_DIM:HEAD_DIM + 1, :]
        o = jnp.concatenate([o_t[:, h * tq:(h + 1) * tq] for h in range(4)], axis=0).T
        return o * _group_gates(s_ref.at[0], g, 1 if selected else 2)

    o_ref[0] = (branch(True) + branch(False)).astype(BF16)


def _nsa_dense(p3, tab, small3, vt_sel, vt_win, negsel):
    b, t, _ = p3.shape
    tq = tk = min(NSA_TQ_DENSE, t)
    n_tab = tab.shape[1]
    nsp = negsel.shape[2] // 2
    block_of_key = (jnp.arange(t)[:, None] // NSA_SEL_LEN == jnp.arange(nsp)[None, :]).astype(BF16)
    keys = lambda pb: pl.BlockSpec((1, t, LANES), lambda i, g, q: (i, 0, pb))
    values = pl.BlockSpec((1, 2 * VT_ROWS, t), lambda i, g, q: (i, 0, 0))
    return pl.pallas_call(
        functools.partial(_nsa_dense_kernel, tq=tq, tk=tk, n_tab=n_tab),
        out_shape=jax.ShapeDtypeStruct((b, t, GW), BF16),
        grid=(b, 2, t // tq),
        in_specs=[pl.BlockSpec((1, tq, 2 * LANES), lambda i, g, q: (i, q, g)),
                  keys(PB_KSA), values, keys(PB_KWA), values,
                  pl.BlockSpec((1, n_tab, tk, 4 * tq), lambda i, g, q: (g, 0, 0, 0),
                               pipeline_mode=pl.Buffered(1)),
                  pl.BlockSpec((t, nsp), lambda i, g, q: (0, 0)),
                  pl.BlockSpec((1, tq, nsp), lambda i, g, q: (i, q, g)),
                  pl.BlockSpec((1, tq, LANES), lambda i, g, q: (i, q, 0))],
        out_specs=pl.BlockSpec((1, tq, 2 * LANES), lambda i, g, q: (i, q, g)),
        scratch_shapes=[pltpu.VMEM((t, LANES + nsp), BF16),
                        pltpu.VMEM((tk, 4 * tq), F32), pltpu.VMEM((tk, 4 * tq), F32),
                        pltpu.VMEM((1, 4 * tq), F32), pltpu.VMEM((VT_ROWS, 4 * tq), F32)],
        compiler_params=_cparams(("parallel", "parallel", "arbitrary")),
        name="nsa_selected_window",
    )(p3, p3, vt_sel, p3, vt_win, tab, block_of_key, negsel, small3)


def _dilated_kernel(q_ref, kp_ref, kc_ref, vtp_ref, vtc_ref, tab_ref, o_ref, lse_ref, *, tq):
    qi = pl.program_id(2)
    kr = lax.broadcasted_iota(jnp.int32, (2 * tq, 2 * tq), 0)
    qc = lax.broadcasted_iota(jnp.int32, (2 * tq, 2 * tq), 1) & (tq - 1)
    dist = tq + qc - kr
    valid = (dist >= 0) & (dist <= tq) & (kr >= jnp.where(qi > 0, 0, tq))
    outs, lses = [], []
    for pb in range(4):
        cols = slice(pb * LANES, (pb + 1) * LANES)
        qs = _pair_queries(q_ref.at[0, :, cols])
        q_both = jnp.concatenate([qs[0], qs[1]], axis=0)
        k2 = jnp.concatenate([kp_ref[0, :, cols], kc_ref[0, :, cols]], axis=0)
        s = jnp.where(valid, _dot_nt(k2, q_both) + tab_ref[pb, 0], NEG)
        m = jnp.max(s, axis=0, keepdims=True)
        p = jnp.exp2(s - m)
        l = jnp.sum(p, axis=0, keepdims=True)
        p = p.astype(BF16)
        lse = (m + jnp.log2(l)) * LN2
        for hh in range(2):
            rows = slice((2 * pb + hh) * HEAD_DIM, (2 * pb + hh + 1) * HEAD_DIM)
            qcols = slice(hh * tq, (hh + 1) * tq)
            vt = jnp.concatenate([vtp_ref[0, 0, rows, :], vtc_ref[0, 0, rows, :]], axis=1)
            outs.append(_dot(vt, p[:, qcols]) / l[:, qcols])
            lses.append(lse[:, qcols])
    o_ref[0] = jnp.concatenate(outs, axis=0).T.astype(BF16)
    lses.append(jnp.zeros((LANES - len(lses), tq), F32))
    lse_ref[0] = jnp.concatenate(lses, axis=0).T


def _dilated(pd3, tab, dil, tq=DIL_BAND):
    b, t, _ = pd3.shape
    l = t // dil
    qk = pd3[:, :, 0:2 * GW].reshape(b, l, dil * 2 * GW)
    vt = pd3[:, :, 2 * GW:3 * GW].reshape(b, l, dil, GW).transpose(0, 2, 3, 1)

    def prev(q):
        return jnp.maximum(q - 1, 0)

    out_spec = pl.BlockSpec((1, tq, GW), lambda i, r, q: (i, q, r))
    o, lse = pl.pallas_call(
        functools.partial(_dilated_kernel, tq=tq),
        out_shape=(jax.ShapeDtypeStruct((b, l, dil * GW), BF16),
                   jax.ShapeDtypeStruct((b, l, dil * LANES), F32)),
        grid=(b, dil, l // tq),
        in_specs=[pl.BlockSpec((1, tq, GW), lambda i, r, q: (i, q, 2 * r)),
                  pl.BlockSpec((1, tq, GW), lambda i, r, q: (i, prev(q), 2 * r + 1)),
                  pl.BlockSpec((1, tq, GW), lambda i, r, q: (i, q, 2 * r + 1)),
                  pl.BlockSpec((1, 1, GW, tq), lambda i, r, q: (i, r, 0, prev(q))),
                  pl.BlockSpec((1, 1, GW, tq), lambda i, r, q: (i, r, 0, q)),
                  pl.BlockSpec((4, 1, 2 * tq, 2 * tq), lambda i, r, q: (0, 0, 0, 0))],
        out_specs=(out_spec, pl.BlockSpec((1, tq, LANES), lambda i, r, q: (i, q, r))),
        compiler_params=_cparams(("parallel", "parallel", "parallel")),
        name="dilated_attention_%d" % dil,
    )(qk, qk, qk, vt, vt, tab)
    return o.reshape(b * t, GW), lse.reshape(b * t, LANES)


def _out_proj_kernel(x_ref, oc_ref, od_ref, ob_ref, of_ref,
                     d1_ref, d2_ref, d3_ref, l1_ref, l2_ref, l3_ref, w_ref, g_ref, o_ref, cat_ref):
    gw = GW
    a = oc_ref[...].astype(F32) + od_ref[...].astype(F32)
    cat_ref[:, 0:gw] = a.astype(BF16)
    cat_ref[:, gw:2 * gw] = ob_ref[...]
    cat_ref[:, 2 * gw:3 * gw] = of_ref[...]
    l1, l2, l3 = l1_ref[...], l2_ref[...], l3_ref[...]
    mx = jnp.maximum(jnp.maximum(l1, l2), l3)
    e1, e2, e3 = jnp.exp(l1 - mx), jnp.exp(l2 - mx), jnp.exp(l3 - mx)
    den = e1 + e2 + e3
    head = lax.broadcasted_iota(jnp.int32, (2 * LANES, gw), 0) & (LANES - 1)
    lane = lax.broadcasted_iota(jnp.int32, (2 * LANES, gw), 1)
    spread = jnp.where(head == lane // HEAD_DIM, 1.0, 0.0).astype(BF16)
    d = (_spread_lanes(e1 / den, spread) * d1_ref[...].astype(F32)
         + _spread_lanes(e2 / den, spread) * d2_ref[...].astype(F32)
         + _spread_lanes(e3 / den, spread) * d3_ref[...].astype(F32))
    cat_ref[:, 3 * gw:4 * gw] = d.astype(BF16)
    y = _dot(cat_ref[...], w_ref[...])
    y = y * lax.rsqrt(jnp.mean(y * y, axis=-1, keepdims=True) + EPS)
    o_ref[...] = x_ref[...] + y * g_ref[...]


def _out_proj(x2, heads, dils, lses, w_out, g, tm=OUT_TM):
    m = x2.shape[0]
    gw = GW
    row = lambda i: (i, 0)
    const = lambda i: (0, 0)
    in_specs = ([pl.BlockSpec((tm, D_MODEL), row)]
                + [pl.BlockSpec((tm, gw), row)] * 7 + [pl.BlockSpec((tm, LANES), row)] * 3
                + [pl.BlockSpec((D_MODEL, D_MODEL), const), pl.BlockSpec((1, D_MODEL), const)])
    return pl.pallas_call(
        _out_proj_kernel,
        out_shape=jax.ShapeDtypeStruct((m, D_MODEL), F32),
        grid=(m // tm,),
        in_specs=in_specs,
        out_specs=pl.BlockSpec((tm, D_MODEL), row),
        scratch_shapes=[pltpu.VMEM((tm, D_MODEL), BF16)],
        compiler_params=_cparams(("parallel",)),
        name="out_proj_residual",
    )(x2, *heads, *dils, *lses, w_out, g)


def _ffn_kernel(x_ref, gpre_ref, wu_ref, wd_ref, gpost_ref, o_ref, h_ref, acc_ref):
    f = pl.program_id(1)

    @pl.when(f == 0)
    def _():
        x = x_ref[...]
        y = x * lax.rsqrt(jnp.mean(x * x, axis=-1, keepdims=True) + EPS)
        h_ref[...] = (y * gpre_ref[...]).astype(BF16)
        acc_ref[...] = jnp.zeros_like(acc_ref)

    u = jnp.maximum(_dot(h_ref[...], wu_ref[...]), 0.0)
    acc_ref[...] += _dot((u * u).astype(BF16), wd_ref[...])

    @pl.when(f == pl.num_programs(1) - 1)
    def _():
        y = acc_ref[...]
        y = y * lax.rsqrt(jnp.mean(y * y, axis=-1, keepdims=True) + EPS)
        o_ref[...] = x_ref[...] + y * gpost_ref[...]


def _ffn(x2, g_pre, w_up, w_down, g_post, tm=FFN_TM, tf=FFN_TF):
    m = x2.shape[0]
    return pl.pallas_call(
        _ffn_kernel,
        out_shape=jax.ShapeDtypeStruct((m, D_MODEL), F32),
        grid=(m // tm, D_FF // tf),
        in_specs=[pl.BlockSpec((tm, D_MODEL), lambda i, f: (i, 0)),
                  pl.BlockSpec((1, D_MODEL), lambda i, f: (0, 0)),
                  pl.BlockSpec((D_MODEL, tf), lambda i, f: (0, f)),
                  pl.BlockSpec((tf, D_MODEL), lambda i, f: (f, 0)),
                  pl.BlockSpec((1, D_MODEL), lambda i, f: (0, 0))],
        out_specs=pl.BlockSpec((tm, D_MODEL), lambda i, f: (i, 0)),
        scratch_shapes=[pltpu.VMEM((tm, D_MODEL), BF16), pltpu.VMEM((tm, D_MODEL), F32)],
        compiler_params=_cparams(("parallel", "arbitrary")),
        name="ffn_residual",
    )(x2, g_pre, w_up, w_down, g_post)


def _split_w_in(w):
    gw, kw = GW, LANES
    o = 0
    cols = {}
    for name, width in (("qa", gw), ("kca", kw), ("vca", kw), ("ksa", kw), ("vsa", kw), ("kwa", kw),
                        ("vwa", kw), ("ga", 24), ("qb", gw), ("kb", gw), ("vb", gw), ("qc", gw),
                        ("kc", gw), ("vc", gw), ("fc", 8), ("qd", gw), ("kd", gw), ("vd", gw)):
        cols[name] = w[:, o:o + width] * (Q_FOLD if name[0] == "q" else 1.0)
        o += width
    main = jnp.concatenate([cols[n] for n in ("qa", "ksa", "vsa", "kwa", "vwa", "qb", "kb", "vb",
                                              "qc", "kc", "vc", "qd", "kd", "vd")], axis=1)
    pad = jnp.zeros((w.shape[0], LANES - 32), w.dtype)
    small = jnp.concatenate([cols["ga"], cols["fc"], pad, cols["kca"], cols["vca"]], axis=1)
    return main.astype(BF16), small.astype(BF16)


def _compress_weights(pe, w1, w2):
    half = NSA_CMP_LEN // 2
    hid = NSA_CMP_HIDDEN
    w1r = w1.reshape(2, half, HEAD_DIM, hid)
    z = jnp.zeros_like(w1r)
    per_group = jnp.stack([jnp.concatenate([w1r, z], axis=-1), jnp.concatenate([z, w1r], axis=-1)], axis=2)
    w_halves = per_group.reshape(2, half * 2 * HEAD_DIM, 2 * hid).astype(BF16)
    pe_r = jnp.broadcast_to(pe.reshape(2, half, 1, HEAD_DIM), (2, half, 2, HEAD_DIM)).reshape(2, half * LANES)
    z2 = jnp.zeros_like(w2)
    w2g = jnp.concatenate([jnp.concatenate([w2, z2], axis=1), jnp.concatenate([z2, w2], axis=1)], axis=0)
    return pe_r, w_halves[0], w_halves[1], w2g.astype(BF16)


def _bias_tables(rel_bias, t):
    hm = 8
    tq_a = min(NSA_TQ_CMP, t)
    tq_d = tk_d = min(NSA_TQ_DENSE, t)
    n_a = min(_n_toeplitz_tiles(tq_d, tk_d), t // tq_d)
    tab_a = _bias_tiles(rel_bias, _toeplitz_idx_t(n_a, tk_d, tq_d, tq_d), 0, 2, 4)
    ncp = t // NSA_CMP_STRIDE
    qi = jnp.arange(t // tq_a, dtype=jnp.int32)[:, None, None]
    c = jnp.arange(ncp, dtype=jnp.int32)[None, :, None]
    q = jnp.arange(tq_a, dtype=jnp.int32)[None, None, :]
    idx_c = _t5_bucket(qi * tq_a + q - (c * NSA_CMP_STRIDE + NSA_CMP_LEN - 1))
    tab_c = _bias_tiles(rel_bias, idx_c, 0, 2, 4)
    tq_b = min(MOBA_TQ, t)
    n_b = min(_n_toeplitz_tiles(tq_b, tq_b), t // tq_b)
    tab_b = _bias_tiles(rel_bias, _toeplitz_idx_t(n_b, tq_b, tq_b, tq_b), hm, 4, 2)
    kr = jnp.arange(2 * DIL_BAND, dtype=jnp.int32)[None, :, None]
    qc = jnp.arange(DIL_BAND, dtype=jnp.int32)[None, None, :]
    tabs_d = [_bias_tiles(rel_bias, _t5_bucket((DIL_BAND + qc - kr) * dil), 2 * hm, 4, 2)
              for _, dil in LONGNET_PATTERNS]
    return tab_a, tab_c, tab_b, tabs_d


def _overlap_t(t):
    ncp = t // NSA_CMP_STRIDE
    nsp = max(t // NSA_SEL_LEN, 8)
    c_start = jnp.arange(ncp)[None, :] * NSA_CMP_STRIDE
    s_start = jnp.arange(nsp)[:, None] * NSA_SEL_LEN
    ovl = (c_start < s_start + NSA_SEL_LEN) & (c_start + NSA_CMP_LEN - 1 >= s_start)
    ovl = ovl & (jnp.arange(ncp)[None, :] < ncp - 1) & (jnp.arange(nsp)[:, None] < t // NSA_SEL_LEN)
    return jnp.concatenate([ovl, ovl], axis=1).astype(BF16)


def _transposed(p3, pb, n_blocks):
    b, t, _ = p3.shape
    vt = p3[:, :, pb * LANES:(pb + n_blocks) * LANES].transpose(0, 2, 1).reshape(b, 2 * n_blocks, HEAD_DIM, t)
    ones = jnp.ones((b, 2 * n_blocks, VT_ROWS - HEAD_DIM, t), BF16)
    return jnp.concatenate([vt, ones], axis=2).reshape(b, 2 * n_blocks * VT_ROWS, t)


def _mixer_layer(x2, b, t, w_main, w_small, g_pre, g_post, w_out, cmp_w_k, cmp_w_v, fox_bias, tabs, ovl_t):
    tab_a, tab_c, tab_b, tabs_d = tabs
    m = b * t
    p2, pd2, small, kca, vca = _proj(x2, g_pre, w_main, w_small)
    p3 = p2.reshape(b, t, P_WIDTH)
    pd3 = pd2.reshape(b, t, PD_WIDTH)
    small3 = small.reshape(b, t, LANES)
    row_w = NSA_CMP_STRIDE * LANES
    kc = _nsa_compress(kca.reshape(b, t // NSA_CMP_STRIDE, row_w), *cmp_w_k)
    vc = _nsa_compress(vca.reshape(b, t // NSA_CMP_STRIDE, row_w), *cmp_w_v)
    o_cmp, negsel = _nsa_cmp(p3, kc, _transposed(vc, 0, 1), tab_c, ovl_t, small3)
    o_dense = _nsa_dense(p3, tab_a, small3, _transposed(p3, PB_VSA, 1), _transposed(p3, PB_VWA, 1), negsel)
    o_b = _moba(p3, tab_b, _transposed(p3, PB_VB, 4))
    fox_bias_row = jnp.zeros((1, LANES), F32).at[0, FOX_HI:FOX_HI + 8].set(fox_bias)
    o_f = _fox(p3, _fox_cumsum(small3, fox_bias_row), _transposed(p3, PB_VC, 4))
    dil = [_dilated(pd3, tab, d) for tab, (_, d) in zip(tabs_d, LONGNET_PATTERNS)]
    heads = [a.reshape(m, GW) for a in (o_cmp, o_dense, o_b, o_f)]
    return _out_proj(x2, heads, [d[0] for d in dil], [d[1] for d in dil], w_out, g_post)


def kernel(x, w_in, w_out, g_mix_pre, g_mix_post, g_mlp_pre, g_mlp_post, w_up, w_down, cmp_pe, phik_w1, phik_w2, phiv_w1, phiv_w2, fox_bias, rel_bias):
    b, t, d = x.shape
    depth = w_in.shape[0]
    tabs = _bias_tables(rel_bias, t)
    ovl_t = _overlap_t(t)
    x2 = x.reshape(b * t, d)
    for l in range(depth):
        w_main, w_small = _split_w_in(w_in[l])
        x2 = _mixer_layer(
            x2, b, t, w_main, w_small, g_mix_pre[l][None], g_mix_post[l][None], w_out[l].astype(BF16),
            _compress_weights(cmp_pe[l], phik_w1[l], phik_w2[l]),
            _compress_weights(cmp_pe[l], phiv_w1[l], phiv_w2[l]),
            fox_bias[l], tabs, ovl_t)
        x2 = _ffn(x2, g_mlp_pre[l][None], w_up[l].astype(BF16), w_down[l].astype(BF16), g_mlp_post[l][None])
    return x2.reshape(b, t, d)
```

```python
import functools
import math

import jax
import jax.numpy as jnp
from jax import lax
from jax.experimental import pallas as pl
from jax.experimental.pallas import tpu as pltpu

F32 = jnp.float32
BF16 = jnp.bfloat16

D_MODEL = 2048
D_FF = 4 * D_MODEL
HEAD_DIM = 64
LANES = 128
EPS = 1e-6
NEG = -1e30
BIG = 1e30
SCALE = HEAD_DIM ** -0.5
LOG2E = math.log2(math.e)
LN2 = math.log(2.0)
Q_FOLD = SCALE * LOG2E

N_BUCKETS = 32
MAX_EXACT = 16
MAX_DISTANCE = 4096
LAST_BUCKET_DIST = 2897

NSA_CMP_LEN = 32
NSA_CMP_STRIDE = 16
NSA_CMP_HIDDEN = 256
NSA_SEL_LEN = 64
NSA_TOP_N = 16
NSA_WINDOW = 512
MOBA_BLOCK = 256
MOBA_TOPK = 3
LONGNET_PATTERNS = ((128, 1), (512, 4), (2048, 16))
DIL_BAND = 128

FOX_TQ = 512
MOBA_TQ = 512
NSA_TQ_CMP = 256
NSA_TQ_DENSE = 256
PROJ_TM, PROJ_TN = 1024, 512
OUT_TM = 256
FFN_TM, FFN_TF = 1024, 512
CUMSUM_TC = 512

P_WIDTH = 4096
PB_QA, PB_KSA, PB_VSA, PB_KWA, PB_VWA = 0, 4, 5, 6, 7
PB_QB, PB_KB, PB_VB = 8, 12, 16
PB_QC, PB_KC, PB_VC = 20, 24, 28
PD_WIDTH = 1536
GW = 4 * LANES
BF16_SUBLANES = 16
VT_ROWS = HEAD_DIM + BF16_SUBLANES

FOX_HI, FOX_MID, FOX_LO = 24, 32, 40

V7X_VMEM_BYTES = 64 * 1024 * 1024
VMEM_LIMIT = V7X_VMEM_BYTES - 8 * 1024 * 1024


def _cparams(sem):
    return pltpu.CompilerParams(dimension_semantics=sem, vmem_limit_bytes=VMEM_LIMIT)


def _dot(a, b):
    return jnp.dot(a, b, preferred_element_type=F32)


def _dot_nt(a, b):
    return lax.dot_general(a, b, (((1,), (1,)), ((), ())), preferred_element_type=F32)


def _dot_hi(a, b):
    return jnp.dot(a, b, preferred_element_type=F32, precision=lax.Precision.HIGHEST)


def _spread_lanes(w, spread):
    hi = w.astype(BF16)
    lo = (w - hi.astype(F32)).astype(BF16)
    return _dot(jnp.concatenate([hi, lo], axis=1), spread)


def _t5_bucket(dist):
    dist = jnp.maximum(dist, 0)
    rel = jnp.log(jnp.maximum(dist, 1).astype(jnp.float32) / MAX_EXACT) / math.log(MAX_DISTANCE / MAX_EXACT)
    large = jnp.minimum(MAX_EXACT + (rel * (N_BUCKETS - MAX_EXACT)).astype(jnp.int32), N_BUCKETS - 1)
    return jnp.where(dist < MAX_EXACT, dist, large)


def _bias_tile_kernel(tab_ref, idx_ref, o_ref, *, head0, hpg, cols):
    grp = pl.program_id(0)
    idx = idx_ref[0]
    n_bits = N_BUCKETS.bit_length() - 1
    bit_set = [((idx >> b) & 1) == 1 for b in range(n_bits)]
    for hh in range(hpg):
        h = head0 + grp * hpg + hh
        level = [tab_ref[b, h] for b in range(N_BUCKETS)]
        for b in range(n_bits):
            level = [jnp.where(bit_set[b], level[2 * i + 1], level[2 * i]) for i in range(len(level) // 2)]
        o_ref[0, 0, :, hh * cols:(hh + 1) * cols] = level[0] * LOG2E


def _bias_tiles(rel_bias, idx, head0, n_groups, hpg=1):
    n, r, c = idx.shape
    return pl.pallas_call(
        functools.partial(_bias_tile_kernel, head0=head0, hpg=hpg, cols=c),
        out_shape=jax.ShapeDtypeStruct((n_groups, n, r, hpg * c), F32),
        grid=(n_groups, n),
        in_specs=[pl.BlockSpec(memory_space=pltpu.SMEM),
                  pl.BlockSpec((1, r, c), lambda g, i: (i, 0, 0))],
        out_specs=pl.BlockSpec((1, 1, r, hpg * c), lambda g, i: (g, i, 0, 0)),
        compiler_params=_cparams(("parallel", "parallel")),
        name="bias_tiles",
    )(rel_bias, idx)


def _toeplitz_idx_t(n_tiles, krows, qcols, q_stride):
    nd = jnp.arange(n_tiles, dtype=jnp.int32)[:, None, None]
    r = jnp.arange(krows, dtype=jnp.int32)[None, :, None]
    c = jnp.arange(qcols, dtype=jnp.int32)[None, None, :]
    return _t5_bucket(nd * q_stride + c - r)


def _n_toeplitz_tiles(q_stride, keys):
    return -(-(LAST_BUCKET_DIST + keys - 1) // q_stride) + 1


def _proj_kernel(x_ref, g_ref, w_ref, ws_ref, p_ref, pd_ref, s_ref, kc_ref, vc_ref, h_ref, *, n_main):
    j = pl.program_id(1)

    @pl.when(j == 0)
    def _():
        x = x_ref[...]
        y = x * lax.rsqrt(jnp.mean(x * x, axis=-1, keepdims=True) + EPS)
        hb = (y * g_ref[...]).astype(BF16)
        h_ref[...] = hb
        small = _dot(hb, ws_ref[...])
        s_ref[...] = small[:, 0:LANES]
        kc_ref[...] = small[:, LANES:2 * LANES]
        vc_ref[...] = small[:, 2 * LANES:3 * LANES]

    y = _dot(h_ref[...], w_ref[...]).astype(BF16)

    @pl.when(j < n_main)
    def _():
        p_ref[...] = y

    @pl.when(j >= n_main)
    def _():
        pd_ref[...] = y


def _proj(x2, g, w_main, w_small, tm=PROJ_TM, tn=PROJ_TN):
    m = x2.shape[0]
    tm = min(tm, m)
    n_main = P_WIDTH // tn
    return pl.pallas_call(
        functools.partial(_proj_kernel, n_main=n_main),
        out_shape=(jax.ShapeDtypeStruct((m, P_WIDTH), BF16),
                   jax.ShapeDtypeStruct((m, PD_WIDTH), BF16),
                   jax.ShapeDtypeStruct((m, LANES), F32),
                   jax.ShapeDtypeStruct((m, LANES), F32),
                   jax.ShapeDtypeStruct((m, LANES), F32)),
        grid=(m // tm, (P_WIDTH + PD_WIDTH) // tn),
        in_specs=[pl.BlockSpec((tm, D_MODEL), lambda i, j: (i, 0)),
                  pl.BlockSpec((1, D_MODEL), lambda i, j: (0, 0)),
                  pl.BlockSpec((D_MODEL, tn), lambda i, j: (0, j)),
                  pl.BlockSpec((D_MODEL, 3 * LANES), lambda i, j: (0, 0))],
        out_specs=(pl.BlockSpec((tm, tn), lambda i, j: (i, jnp.minimum(j, n_main - 1))),
                   pl.BlockSpec((tm, tn), lambda i, j: (i, jnp.maximum(j - n_main, 0))),
                   pl.BlockSpec((tm, LANES), lambda i, j: (i, 0)),
                   pl.BlockSpec((tm, LANES), lambda i, j: (i, 0)),
                   pl.BlockSpec((tm, LANES), lambda i, j: (i, 0))),
        scratch_shapes=[pltpu.VMEM((tm, D_MODEL), BF16)],
        compiler_params=_cparams(("parallel", "arbitrary")),
        name="rms_in_proj",
    )(x2, g, w_main, w_small)


def _lane_lo(shape):
    return lax.broadcasted_iota(jnp.int32, shape, 1) < HEAD_DIM


def _pair_queries(q_ref):
    q = q_ref[...]
    lo = _lane_lo(q.shape)
    zero = jnp.zeros_like(q)
    return jnp.where(lo, q, zero), jnp.where(lo, zero, q)


def _col_softmax(s, m_ref, first):
    if first:
        m = jnp.max(s, axis=0, keepdims=True)
        m_ref[...] = m
        return jnp.exp2(s - m).astype(BF16), None
    m_prev = m_ref[...]
    m_new = jnp.maximum(m_prev, jnp.max(s, axis=0, keepdims=True))
    m_ref[...] = m_new
    return jnp.exp2(s - m_new).astype(BF16), jnp.exp2(m_prev - m_new)


def _accumulate(acc_ref, cols, pv, alpha):
    if alpha is None:
        acc_ref[:, cols] = pv
    else:
        acc_ref[:, cols] = alpha[:, cols] * acc_ref[:, cols] + pv


def _key_tiles(j_diag, j_first, n_past, qk, consume, sa_ref, sb_ref):
    last = j_first + jnp.maximum(n_past - 1, 0)
    sa_ref[...] = qk(j_first)
    consume(j_diag, qk(j_diag), True)

    def run(base, count):
        for t in range(count):
            cur, nxt = (sa_ref, sb_ref) if t % 2 == 0 else (sb_ref, sa_ref)
            nxt[...] = qk(jnp.minimum(base + t + 1, last))
            consume(base + t, cur[...], False)

    quads = n_past // 4
    lax.fori_loop(0, quads, lambda i, c: (run(j_first + 4 * i, 4), c)[1], 0)
    rest = n_past - 4 * quads

    @pl.when(rest >= 2)
    def _():
        run(j_first + 4 * quads, 2)

    @pl.when(rest % 2 == 1)
    def _():
        consume(j_first + n_past - 1, sa_ref[...], False)


def _top_rows(v, k):
    n = v.shape[0]
    rows = lax.broadcasted_iota(jnp.int32, v.shape, 0)
    chosen = jnp.zeros(v.shape, F32)
    for _ in range(k):
        top = jnp.max(v, axis=0, keepdims=True)
        first = jnp.min(jnp.where(v == top, rows, n), axis=0, keepdims=True)
        pick = rows == first
        chosen = jnp.where(pick, 1.0, chosen)
        v = jnp.where(pick, -jnp.inf, v)
    return chosen


def _top_rows_bisect(v, k):
    n = v.shape[0]
    key = pltpu.bitcast(v, jnp.int32)
    thr = jnp.zeros((1, v.shape[1]), jnp.int32)
    def enough(cand):
        return jnp.sum(jnp.where(key >= cand, 1.0, 0.0), axis=0, keepdims=True) >= k

    for bit in range(30, 0, -2):
        c_hi, c_lo = thr | (1 << bit), thr | (1 << (bit - 1))
        c_both = c_hi | (1 << (bit - 1))
        thr = jnp.where(enough(c_both), c_both, jnp.where(enough(c_hi), c_hi,
                                                          jnp.where(enough(c_lo), c_lo, thr)))
    thr = jnp.where(enough(thr | 1), thr | 1, thr)
    above = key > thr
    tie = key == thr
    n_above = jnp.sum(jnp.where(above, 1.0, 0.0), axis=0, keepdims=True)
    ri = lax.broadcasted_iota(jnp.int32, (n, n), 0)
    ci = lax.broadcasted_iota(jnp.int32, (n, n), 1)
    before = jnp.where(ci < ri, 1.0, 0.0).astype(BF16)
    ties_before = _dot(before, jnp.where(tie, 1.0, 0.0).astype(BF16))
    return above | (tie & (ties_before < k - n_above))


def _pair_finish(acc_ref, o_ref, tq):
    o_t = acc_ref[0:HEAD_DIM, :] / acc_ref[HEAD_DIM:HEAD_DIM + 1, :]
    o_ref[0] = jnp.concatenate([o_t[:, 0:tq], o_t[:, tq:2 * tq]], axis=0).T.astype(BF16)


def _fox_cumsum_kernel(s_ref, fb_ref, c_ref, carry_ref, *, tc):
    @pl.when(pl.program_id(1) == 0)
    def _():
        carry_ref[...] = jnp.zeros_like(carry_ref)

    lane = lax.broadcasted_iota(jnp.int32, (tc, LANES), 1)
    z = s_ref[0] + fb_ref[...]
    logf = -(jnp.maximum(-z, 0.0) + jnp.log(1.0 + jnp.exp(-jnp.abs(z))))
    logf = jnp.where((lane >= FOX_HI) & (lane < FOX_MID), logf, 0.0)
    ri = lax.broadcasted_iota(jnp.int32, (tc, tc), 0)
    ci = lax.broadcasted_iota(jnp.int32, (tc, tc), 1)
    lower = jnp.where(ri >= ci, 1.0, 0.0).astype(F32)
    c = _dot_hi(lower, logf) + carry_ref[...]
    carry_ref[...] = c[tc - 1:tc, :]
    c2 = c * LOG2E
    hi = c2.astype(BF16).astype(F32)
    rest = c2 - hi
    mid = rest.astype(BF16).astype(F32)
    lo = rest - mid
    pieces = hi + pltpu.roll(mid, FOX_MID - FOX_HI, axis=1) + pltpu.roll(lo, FOX_LO - FOX_HI, axis=1)
    c_ref[0] = pieces.astype(BF16)


def _fox_cumsum(small3, fox_bias_row, tc=CUMSUM_TC):
    b, t, _ = small3.shape
    tc = min(tc, t)
    return pl.pallas_call(
        functools.partial(_fox_cumsum_kernel, tc=tc),
        out_shape=jax.ShapeDtypeStruct((b, t, LANES), BF16),
        grid=(b, t // tc),
        in_specs=[pl.BlockSpec((1, tc, LANES), lambda i, j: (i, j, 0)),
                  pl.BlockSpec((1, LANES), lambda i, j: (0, 0))],
        out_specs=pl.BlockSpec((1, tc, LANES), lambda i, j: (i, j, 0)),
        scratch_shapes=[pltpu.VMEM((1, LANES), F32)],
        compiler_params=_cparams(("parallel", "arbitrary")),
        name="fox_cumsum",
    )(small3, fox_bias_row)


def _fox_kernel(q_ref, k_ref, c_ref, vt_ref, o_ref, kaug_ref, sa_ref, sb_ref, m_ref, acc_ref, *, tq):
    hp = pl.program_id(1)
    qi = pl.program_id(2)

    @pl.when(qi == 0)
    def _():
        kaug_ref[:, 0:LANES] = k_ref[0]
        kaug_ref[:, LANES:2 * LANES] = c_ref[0]

    qs = _pair_queries(q_ref.at[0])
    lane = lax.broadcasted_iota(jnp.int32, (tq, LANES), 1)
    rows = []
    for hh in range(2):
        h = 2 * hp + hh
        pick = (lane == FOX_HI + h) | (lane == FOX_MID + h) | (lane == FOX_LO + h)
        rows.append(jnp.concatenate([qs[hh], jnp.where(pick, -1.0, 0.0).astype(BF16)], axis=1))
    q_aug = jnp.concatenate(rows, axis=0)
    kr = lax.broadcasted_iota(jnp.int32, (tq, 2 * tq), 0)
    qc = lax.broadcasted_iota(jnp.int32, (tq, 2 * tq), 1) & (tq - 1)

    def qk(j):
        return _dot_nt(kaug_ref[pl.ds(pl.multiple_of(j * tq, tq), tq), :], q_aug)

    def consume(j, s, first):
        k0 = pl.multiple_of(j * tq, tq)
        if first:
            s = jnp.where(kr <= qc, s, NEG)
        p, alpha = _col_softmax(s, m_ref, first)
        for hh in range(2):
            cols = slice(hh * tq, (hh + 1) * tq)
            vt = vt_ref[0, hh * VT_ROWS:(hh + 1) * VT_ROWS, pl.ds(k0, tq)]
            _accumulate(acc_ref, cols, _dot(vt, p[:, cols]), alpha)

    _key_tiles(qi, 0, qi, qk, consume, sa_ref, sb_ref)
    _pair_finish(acc_ref, o_ref, tq)


def _fox(p3, c_aug, vt, tq=FOX_TQ):
    b, t, _ = p3.shape
    tq = min(tq, t)
    return pl.pallas_call(
        functools.partial(_fox_kernel, tq=tq),
        out_shape=jax.ShapeDtypeStruct((b, t, GW), BF16),
        grid=(b, 4, t // tq),
        in_specs=[pl.BlockSpec((1, tq, LANES), lambda i, h, q: (i, q, PB_QC + h)),
                  pl.BlockSpec((1, t, LANES), lambda i, h, q: (i, 0, PB_KC + h)),
                  pl.BlockSpec((1, t, LANES), lambda i, h, q: (i, 0, 0)),
                  pl.BlockSpec((1, 2 * VT_ROWS, t), lambda i, h, q: (i, h, 0))],
        out_specs=pl.BlockSpec((1, tq, LANES), lambda i, h, q: (i, q, h)),
        scratch_shapes=[pltpu.VMEM((t, 2 * LANES), BF16),
                        pltpu.VMEM((tq, 2 * tq), F32), pltpu.VMEM((tq, 2 * tq), F32),
                        pltpu.VMEM((1, 2 * tq), F32), pltpu.VMEM((VT_ROWS, 2 * tq), F32)],
        compiler_params=_cparams(("parallel", "parallel", "arbitrary")),
        name="fox_attention",
    )(p3, p3, c_aug, vt)


def _moba_kernel(q_ref, k_ref, vt_ref, tab_ref, o_ref, km_ref, kaug_ref, sa_ref, sb_ref, m_ref, acc_ref,
                 *, nkb, nkp, n_tab, tq):
    blk = MOBA_BLOCK
    qi = pl.program_id(2)

    @pl.when(qi == 0)
    def _():
        km_ref[...] = jnp.zeros_like(km_ref)
        lane = lax.broadcasted_iota(jnp.int32, (blk, LANES), 1)
        for n in range(nkb):
            kb = k_ref[0, n * blk:(n + 1) * blk, :]
            km_ref[n:n + 1, :] = jnp.mean(kb.astype(F32), axis=0, keepdims=True)
            kaug_ref[n * blk:(n + 1) * blk, 0:LANES] = kb
            kaug_ref[n * blk:(n + 1) * blk, LANES:2 * LANES] = jnp.where(lane == n, 1.0, 0.0).astype(BF16)

    qs = _pair_queries(q_ref.at[0])
    q_both = jnp.concatenate([qs[0], qs[1]], axis=0)
    km = km_ref[...]
    km_hi = km.astype(BF16)
    km_lo = (km - km_hi.astype(F32)).astype(BF16)
    gate = _dot_nt(jnp.concatenate([km_hi, km_lo], axis=0), q_both)
    gate = gate[0:nkp] + gate[nkp:2 * nkp]
    n = lax.broadcasted_iota(jnp.int32, gate.shape, 0)
    q_pos = lax.broadcasted_iota(jnp.int32, gate.shape, 1) & (tq - 1)
    own = qi * (tq // blk) + q_pos // blk
    past = n < own
    gate = jnp.where(past, gate, NEG)
    allowed = ((_top_rows(gate, MOBA_TOPK) > 0.5) & past) | (n == own)
    negsel = jnp.where(allowed, 0.0, NEG).astype(F32)
    if nkp < LANES:
        negsel = jnp.concatenate([negsel, jnp.zeros((LANES - nkp, 2 * tq), F32)], axis=0)
    q_aug = jnp.concatenate([q_both, negsel.T.astype(BF16)], axis=1)

    kr = lax.broadcasted_iota(jnp.int32, (tq, 2 * tq), 0)
    qc = lax.broadcasted_iota(jnp.int32, (tq, 2 * tq), 1) & (tq - 1)

    def qk(j):
        return _dot_nt(kaug_ref[pl.ds(pl.multiple_of(j * tq, tq), tq), :], q_aug)

    def consume(j, s, first):
        k0 = pl.multiple_of(j * tq, tq)
        s = s + tab_ref[0, jnp.minimum(qi - j, n_tab - 1)]
        if first:
            s = jnp.where(kr <= qc, s, NEG)
        p, alpha = _col_softmax(s, m_ref, first)
        for hh in range(2):
            cols = slice(hh * tq, (hh + 1) * tq)
            vt = vt_ref[0, hh * VT_ROWS:(hh + 1) * VT_ROWS, pl.ds(k0, tq)]
            _accumulate(acc_ref, cols, _dot(vt, p[:, cols]), alpha)

    _key_tiles(qi, 0, qi, qk, consume, sa_ref, sb_ref)
    _pair_finish(acc_ref, o_ref, tq)


def _moba(p3, tab, vt):
    b, t, _ = p3.shape
    tq = tab.shape[2]
    nkb = t // MOBA_BLOCK
    nkp = -(-nkb // 8) * 8
    n_tab = tab.shape[1]
    return pl.pallas_call(
        functools.partial(_moba_kernel, nkb=nkb, nkp=nkp, n_tab=n_tab, tq=tq),
        out_shape=jax.ShapeDtypeStruct((b, t, GW), BF16),
        grid=(b, 4, t // tq),
        in_specs=[pl.BlockSpec((1, tq, LANES), lambda i, h, q: (i, q, PB_QB + h)),
                  pl.BlockSpec((1, t, LANES), lambda i, h, q: (i, 0, PB_KB + h)),
                  pl.BlockSpec((1, 2 * VT_ROWS, t), lambda i, h, q: (i, h, 0)),
                  pl.BlockSpec((1, n_tab, tq, 2 * tq), lambda i, h, q: (h, 0, 0, 0),
                               pipeline_mode=pl.Buffered(1))],
        out_specs=pl.BlockSpec((1, tq, LANES), lambda i, h, q: (i, q, h)),
        scratch_shapes=[pltpu.VMEM((nkp, LANES), F32), pltpu.VMEM((t, 2 * LANES), BF16),
                        pltpu.VMEM((tq, 2 * tq), F32), pltpu.VMEM((tq, 2 * tq), F32),
                        pltpu.VMEM((1, 2 * tq), F32), pltpu.VMEM((VT_ROWS, 2 * tq), F32)],
        compiler_params=_cparams(("parallel", "parallel", "arbitrary")),
        name="moba_attention",
    )(p3, p3, vt, tab)


def _gelu_tanh(x):
    return 0.5 * x * (1.0 + jnp.tanh(math.sqrt(2.0 / math.pi) * (x + 0.044715 * (x * x * x))))


def _nsa_compress_kernel(r_ref, pe_ref, wlo_ref, whi_ref, w2_ref, o_ref, *, nr):
    r = r_ref[0]
    pe = pe_ref[...]
    a = _dot((r + pe[0:1, :]).astype(BF16), wlo_ref[...])
    b = _dot((r + pe[1:2, :]).astype(BF16), whi_ref[...])
    hid = _gelu_tanh(a + pltpu.roll(b, nr - 1, axis=0))
    o_ref[0] = _dot(hid.astype(BF16), w2_ref[...]).astype(BF16)


def _nsa_compress(r3, pe2, wlo, whi, w2):
    b, nr, w = r3.shape
    return pl.pallas_call(
        functools.partial(_nsa_compress_kernel, nr=nr),
        out_shape=jax.ShapeDtypeStruct((b, nr, LANES), BF16),
        grid=(b,),
        in_specs=[pl.BlockSpec((1, nr, w), lambda i: (i, 0, 0)),
                  pl.BlockSpec((2, w), lambda i: (0, 0)),
                  pl.BlockSpec((w, 2 * NSA_CMP_HIDDEN), lambda i: (0, 0)),
                  pl.BlockSpec((w, 2 * NSA_CMP_HIDDEN), lambda i: (0, 0)),
                  pl.BlockSpec((2 * NSA_CMP_HIDDEN, LANES), lambda i: (0, 0))],
        out_specs=pl.BlockSpec((1, nr, LANES), lambda i: (i, 0, 0)),
        compiler_params=_cparams(("parallel",)),
        name="nsa_compress",
    )(r3, pe2, wlo, whi, w2)


def _group_queries(q_ref, g):
    half = lax.broadcasted_iota(jnp.int32, (q_ref.shape[0], LANES), 1) // HEAD_DIM
    lo = half == 0
    in_group = half == g
    out = []
    for pb in range(2):
        q = q_ref[:, pb * LANES:(pb + 1) * LANES].astype(F32)
        qr = pltpu.roll(q, HEAD_DIM, axis=1)
        for dup in (jnp.where(lo, q, qr), jnp.where(lo, qr, q)):
            out.append(jnp.where(in_group, dup, 0.0).astype(BF16))
    return jnp.concatenate(out, axis=0)


def _group_gates(s_ref, g, branch):
    sig = 1.0 / (1.0 + jnp.exp(-s_ref[...]))
    col = lax.broadcasted_iota(jnp.int32, (2 * LANES, 2 * LANES), 0) & (LANES - 1)
    lane = lax.broadcasted_iota(jnp.int32, (2 * LANES, 2 * LANES), 1)
    spread = jnp.where(col == branch * 8 + g * 4 + lane // HEAD_DIM, 1.0, 0.0).astype(BF16)
    return _spread_lanes(sig, spread)


def _nsa_cmp_kernel(q_ref, kc_ref, vct_ref, tab_ref, ovl_ref, s_ref, o_ref, sel_ref, *, tq, ncp, nsp):
    g = pl.program_id(1)
    qi = pl.program_id(2)
    qs = _group_queries(q_ref.at[0], g)
    s = _dot_nt(kc_ref[0], qs) + tab_ref[0, 0]
    c = lax.broadcasted_iota(jnp.int32, (ncp, 4 * tq), 0)
    t = qi * tq + (lax.broadcasted_iota(jnp.int32, (ncp, 4 * tq), 1) & (tq - 1))
    ok = (t - (c * NSA_CMP_STRIDE + NSA_CMP_LEN - 1) >= 0) & (c < ncp - 1)
    s = jnp.where(ok, s, NEG)
    e = jnp.exp2(s - jnp.max(s, axis=0, keepdims=True))
    p = jnp.where(ok, e / jnp.sum(e, axis=0, keepdims=True), 0.0)
    vt = vct_ref[0, pl.ds(pl.multiple_of(g * VT_ROWS, VT_ROWS), VT_ROWS), :]
    o_t = _dot(vt, p.astype(BF16))[0:HEAD_DIM, :]
    o = jnp.concatenate([o_t[:, h * tq:(h + 1) * tq] for h in range(4)], axis=0).T
    o_ref[0] = (o * _group_gates(s_ref.at[0], g, 0)).astype(BF16)

    psum = p[:, 0:tq] + p[:, tq:2 * tq] + p[:, 2 * tq:3 * tq] + p[:, 3 * tq:4 * tq]
    hi = psum.astype(BF16)
    lo = (psum - hi.astype(F32)).astype(BF16)
    imp = _dot(ovl_ref[...], jnp.concatenate([hi, lo], axis=0))
    n = lax.broadcasted_iota(jnp.int32, (nsp, tq), 0)
    cur = (qi * tq + lax.broadcasted_iota(jnp.int32, (nsp, tq), 1)) // NSA_SEL_LEN
    forced = (n == 0) | (n == cur) | (n == cur - 1)
    valid = n <= cur
    imp = jnp.where(forced, BIG, jnp.where(valid, jnp.abs(imp), NEG))
    chosen = _top_rows_bisect(imp, NSA_TOP_N) & valid
    sel_ref[0] = jnp.where(chosen, 0.0, NEG).astype(F32).T.astype(BF16)


def _nsa_cmp(p3, kc, vct, tab, ovl_t, small3, tq=NSA_TQ_CMP):
    b, t, _ = p3.shape
    ncp = kc.shape[1]
    nsp = ovl_t.shape[0]
    tq = min(tq, t)
    return pl.pallas_call(
        functools.partial(_nsa_cmp_kernel, tq=tq, ncp=ncp, nsp=nsp),
        out_shape=(jax.ShapeDtypeStruct((b, t, GW), BF16),
                   jax.ShapeDtypeStruct((b, t, 2 * nsp), BF16)),
        grid=(b, 2, t // tq),
        in_specs=[pl.BlockSpec((1, tq, 2 * LANES), lambda i, g, q: (i, q, g)),
                  pl.BlockSpec((1, ncp, LANES), lambda i, g, q: (i, 0, 0)),
                  pl.BlockSpec((1, 2 * VT_ROWS, ncp), lambda i, g, q: (i, 0, 0)),
                  pl.BlockSpec((1, 1, ncp, 4 * tq), lambda i, g, q: (g, q, 0, 0)),
                  pl.BlockSpec((nsp, 2 * ncp), lambda i, g, q: (0, 0)),
                  pl.BlockSpec((1, tq, LANES), lambda i, g, q: (i, q, 0))],
        out_specs=(pl.BlockSpec((1, tq, 2 * LANES), lambda i, g, q: (i, q, g)),
                   pl.BlockSpec((1, tq, nsp), lambda i, g, q: (i, q, g))),
        compiler_params=_cparams(("parallel", "parallel", "parallel")),
        name="nsa_compressed",
    )(p3, kc, vct, tab, ovl_t, small3)


def _nsa_dense_kernel(q_ref, ks_ref, vts_ref, kw_ref, vtw_ref, tab_ref, blk_ref, sel_ref, s_ref, o_ref,
                      kaug_ref, sa_ref, sb_ref, m_ref, acc_ref, *, tq, tk, n_tab):
    g = pl.program_id(1)
    qi = pl.program_id(2)
    qs = _group_queries(q_ref.at[0], g)
    kr = lax.broadcasted_iota(jnp.int32, (tk, 4 * tq), 0)
    qc = lax.broadcasted_iota(jnp.int32, (tk, 4 * tq), 1) & (tq - 1)
    j0 = (qi * tq) // tk
    g0 = pl.multiple_of(g * VT_ROWS, VT_ROWS)

    @pl.when(qi == 0)
    def _():
        kaug_ref[:, 0:LANES] = ks_ref[0]
        kaug_ref[:, LANES:] = blk_ref[...]

    qs_sel = jnp.concatenate([qs, jnp.concatenate([sel_ref[0]] * 4, axis=0)], axis=1)

    def branch(selected):
        k_tile = ((lambda k0: kaug_ref[pl.ds(k0, tk), :]) if selected
                  else (lambda k0: kw_ref[0, pl.ds(k0, tk), :]))
        q_all = qs_sel if selected else qs
        vt_ref = vts_ref if selected else vtw_ref

        def qk(j):
            return _dot_nt(k_tile(pl.multiple_of(j * tk, tk)), q_all)

        def consume(j, s, first):
            k0 = pl.multiple_of(j * tk, tk)
            s = s + tab_ref[0, jnp.minimum(qi - j * (tk // tq), n_tab - 1)]
            dist = (qi * tq - j * tk) + qc - kr
            if not selected:
                s = jnp.where((dist >= 0) & (dist <= NSA_WINDOW - 1), s, NEG)
            elif first:
                s = jnp.where(dist >= 0, s, NEG)
            p, alpha = _col_softmax(s, m_ref, first)
            vt = vt_ref[0, pl.ds(g0, VT_ROWS), pl.ds(k0, tk)]
            _accumulate(acc_ref, slice(None), _dot(vt, p), alpha)

        first_j = 0 if selected else jnp.maximum(j0 - (-(-(NSA_WINDOW - 1) // tk)), 0)
        _key_tiles(j0, first_j, j0 - first_j, qk, consume, sa_ref, sb_ref)
        o_t = acc_ref[0:HEAD_DIM, :] / acc_ref[HEAD_DIM:HEAD_DIM + 1, :]
        o = jnp.concatenate([o_t[:, h * tq:(h + 1) * tq] for h in range(4)], axis=0).T
        return o * _group_gates(s_ref.at[0], g, 1 if selected else 2)

    o_ref[0] = (branch(True) + branch(False)).astype(BF16)


def _nsa_dense(p3, tab, small3, vt_sel, vt_win, negsel):
    b, t, _ = p3.shape
    tq = tk = min(NSA_TQ_DENSE, t)
    n_tab = tab.shape[1]
    nsp = negsel.shape[2] // 2
    block_of_key = (jnp.arange(t)[:, None] // NSA_SEL_LEN == jnp.arange(nsp)[None, :]).astype(BF16)
    keys = lambda pb: pl.BlockSpec((1, t, LANES), lambda i, g, q: (i, 0, pb))
    values = pl.BlockSpec((1, 2 * VT_ROWS, t), lambda i, g, q: (i, 0, 0))
    return pl.pallas_call(
        functools.partial(_nsa_dense_kernel, tq=tq, tk=tk, n_tab=n_tab),
        out_shape=jax.ShapeDtypeStruct((b, t, GW), BF16),
        grid=(b, 2, t // tq),
        in_specs=[pl.BlockSpec((1, tq, 2 * LANES), lambda i, g, q: (i, q, g)),
                  keys(PB_KSA), values, keys(PB_KWA), values,
                  pl.BlockSpec((1, n_tab, tk, 4 * tq), lambda i, g, q: (g, 0, 0, 0),
                               pipeline_mode=pl.Buffered(1)),
                  pl.BlockSpec((t, nsp), lambda i, g, q: (0, 0)),
                  pl.BlockSpec((1, tq, nsp), lambda i, g, q: (i, q, g)),
                  pl.BlockSpec((1, tq, LANES), lambda i, g, q: (i, q, 0))],
        out_specs=pl.BlockSpec((1, tq, 2 * LANES), lambda i, g, q: (i, q, g)),
        scratch_shapes=[pltpu.VMEM((t, LANES + nsp), BF16),
                        pltpu.VMEM((tk, 4 * tq), F32), pltpu.VMEM((tk, 4 * tq), F32),
                        pltpu.VMEM((1, 4 * tq), F32), pltpu.VMEM((VT_ROWS, 4 * tq), F32)],
        compiler_params=_cparams(("parallel", "parallel", "arbitrary")),
        name="nsa_selected_window",
    )(p3, p3, vt_sel, p3, vt_win, tab, block_of_key, negsel, small3)


def _dilated_kernel(q_ref, kp_ref, kc_ref, vtp_ref, vtc_ref, tab_ref, o_ref, lse_ref, *, tq):
    qi = pl.program_id(2)
    kr = lax.broadcasted_iota(jnp.int32, (2 * tq, 2 * tq), 0)
    qc = lax.broadcasted_iota(jnp.int32, (2 * tq, 2 * tq), 1) & (tq - 1)
    dist = tq + qc - kr
    valid = (dist >= 0) & (dist <= tq) & (kr >= jnp.where(qi > 0, 0, tq))
    outs, lses = [], []
    for pb in range(4):
        cols = slice(pb * LANES, (pb + 1) * LANES)
        qs = _pair_queries(q_ref.at[0, :, cols])
        q_both = jnp.concatenate([qs[0], qs[1]], axis=0)
        k2 = jnp.concatenate([kp_ref[0, :, cols], kc_ref[0, :, cols]], axis=0)
        s = jnp.where(valid, _dot_nt(k2, q_both) + tab_ref[pb, 0], NEG)
        m = jnp.max(s, axis=0, keepdims=True)
        p = jnp.exp2(s - m)
        l = jnp.sum(p, axis=0, keepdims=True)
        p = p.astype(BF16)
        lse = (m + jnp.log2(l)) * LN2
        for hh in range(2):
            rows = slice((2 * pb + hh) * HEAD_DIM, (2 * pb + hh + 1) * HEAD_DIM)
            qcols = slice(hh * tq, (hh + 1) * tq)
            vt = jnp.concatenate([vtp_ref[0, 0, rows, :], vtc_ref[0, 0, rows, :]], axis=1)
            outs.append(_dot(vt, p[:, qcols]) / l[:, qcols])
            lses.append(lse[:, qcols])
    o_ref[0] = jnp.concatenate(outs, axis=0).T.astype(BF16)
    lses.append(jnp.zeros((LANES - len(lses), tq), F32))
    lse_ref[0] = jnp.concatenate(lses, axis=0).T


def _dilated(pd3, tab, dil, tq=DIL_BAND):
    b, t, _ = pd3.shape
    l = t // dil
    qk = pd3[:, :, 0:2 * GW].reshape(b, l, dil * 2 * GW)
    vt = pd3[:, :, 2 * GW:3 * GW].reshape(b, l, dil, GW).transpose(0, 2, 3, 1)

    def prev(q):
        return jnp.maximum(q - 1, 0)

    out_spec = pl.BlockSpec((1, tq, GW), lambda i, r, q: (i, q, r))
    o, lse = pl.pallas_call(
        functools.partial(_dilated_kernel, tq=tq),
        out_shape=(jax.ShapeDtypeStruct((b, l, dil * GW), BF16),
                   jax.ShapeDtypeStruct((b, l, dil * LANES), F32)),
        grid=(b, dil, l // tq),
        in_specs=[pl.BlockSpec((1, tq, GW), lambda i, r, q: (i, q, 2 * r)),
                  pl.BlockSpec((1, tq, GW), lambda i, r, q: (i, prev(q), 2 * r + 1)),
                  pl.BlockSpec((1, tq, GW), lambda i, r, q: (i, q, 2 * r + 1)),
                  pl.BlockSpec((1, 1, GW, tq), lambda i, r, q: (i, r, 0, prev(q))),
                  pl.BlockSpec((1, 1, GW, tq), lambda i, r, q: (i, r, 0, q)),
                  pl.BlockSpec((4, 1, 2 * tq, 2 * tq), lambda i, r, q: (0, 0, 0, 0))],
        out_specs=(out_spec, pl.BlockSpec((1, tq, LANES), lambda i, r, q: (i, q, r))),
        compiler_params=_cparams(("parallel", "parallel", "parallel")),
        name="dilated_attention_%d" % dil,
    )(qk, qk, qk, vt, vt, tab)
    return o.reshape(b * t, GW), lse.reshape(b * t, LANES)


def _out_proj_kernel(x_ref, oc_ref, od_ref, ob_ref, of_ref,
                     d1_ref, d2_ref, d3_ref, l1_ref, l2_ref, l3_ref, w_ref, g_ref, o_ref, cat_ref):
    gw = GW
    a = oc_ref[...].astype(F32) + od_ref[...].astype(F32)
    cat_ref[:, 0:gw] = a.astype(BF16)
    cat_ref[:, gw:2 * gw] = ob_ref[...]
    cat_ref[:, 2 * gw:3 * gw] = of_ref[...]
    l1, l2, l3 = l1_ref[...], l2_ref[...], l3_ref[...]
    mx = jnp.maximum(jnp.maximum(l1, l2), l3)
    e1, e2, e3 = jnp.exp(l1 - mx), jnp.exp(l2 - mx), jnp.exp(l3 - mx)
    den = e1 + e2 + e3
    head = lax.broadcasted_iota(jnp.int32, (2 * LANES, gw), 0) & (LANES - 1)
    lane = lax.broadcasted_iota(jnp.int32, (2 * LANES, gw), 1)
    spread = jnp.where(head == lane // HEAD_DIM, 1.0, 0.0).astype(BF16)
    d = (_spread_lanes(e1 / den, spread) * d1_ref[...].astype(F32)
         + _spread_lanes(e2 / den, spread) * d2_ref[...].astype(F32)
         + _spread_lanes(e3 / den, spread) * d3_ref[...].astype(F32))
    cat_ref[:, 3 * gw:4 * gw] = d.astype(BF16)
    y = _dot(cat_ref[...], w_ref[...])
    y = y * lax.rsqrt(jnp.mean(y * y, axis=-1, keepdims=True) + EPS)
    o_ref[...] = x_ref[...] + y * g_ref[...]


def _out_proj(x2, heads, dils, lses, w_out, g, tm=OUT_TM):
    m = x2.shape[0]
    gw = GW
    row = lambda i: (i, 0)
    const = lambda i: (0, 0)
    in_specs = ([pl.BlockSpec((tm, D_MODEL), row)]
                + [pl.BlockSpec((tm, gw), row)] * 7 + [pl.BlockSpec((tm, LANES), row)] * 3
                + [pl.BlockSpec((D_MODEL, D_MODEL), const), pl.BlockSpec((1, D_MODEL), const)])
    return pl.pallas_call(
        _out_proj_kernel,
        out_shape=jax.ShapeDtypeStruct((m, D_MODEL), F32),
        grid=(m // tm,),
        in_specs=in_specs,
        out_specs=pl.BlockSpec((tm, D_MODEL), row),
        scratch_shapes=[pltpu.VMEM((tm, D_MODEL), BF16)],
        compiler_params=_cparams(("parallel",)),
        name="out_proj_residual",
    )(x2, *heads, *dils, *lses, w_out, g)


def _ffn_kernel(x_ref, gpre_ref, wu_ref, wd_ref, gpost_ref, o_ref, h_ref):
    f = pl.program_id(1)

    @pl.when(f == 0)
    def _():
        x = x_ref[...]
        y = x * lax.rsqrt(jnp.mean(x * x, axis=-1, keepdims=True) + EPS)
        h_ref[...] = (y * gpre_ref[...]).astype(BF16)
        o_ref[...] = jnp.zeros_like(o_ref)

    u = jnp.maximum(_dot(h_ref[...], wu_ref[...]), 0.0)
    o_ref[...] += _dot((u * u).astype(BF16), wd_ref[...])

    @pl.when(f == pl.num_programs(1) - 1)
    def _():
        y = o_ref[...]
        y = y * lax.rsqrt(jnp.mean(y * y, axis=-1, keepdims=True) + EPS)
        o_ref[...] = x_ref[...] + y * gpost_ref[...]


def _ffn(x2, g_pre, w_up, w_down, g_post, tm=FFN_TM, tf=FFN_TF):
    m = x2.shape[0]
    return pl.pallas_call(
        _ffn_kernel,
        out_shape=jax.ShapeDtypeStruct((m, D_MODEL), F32),
        grid=(m // tm, D_FF // tf),
        in_specs=[pl.BlockSpec((tm, D_MODEL), lambda i, f: (i, 0), pipeline_mode=pl.Buffered(1)),
                  pl.BlockSpec((1, D_MODEL), lambda i, f: (0, 0)),
                  pl.BlockSpec((D_MODEL, tf), lambda i, f: (0, f)),
                  pl.BlockSpec((tf, D_MODEL), lambda i, f: (f, 0)),
                  pl.BlockSpec((1, D_MODEL), lambda i, f: (0, 0))],
        out_specs=pl.BlockSpec((tm, D_MODEL), lambda i, f: (i, 0)),
        scratch_shapes=[pltpu.VMEM((tm, D_MODEL), BF16)],
        compiler_params=_cparams(("parallel", "arbitrary")),
        name="ffn_residual",
    )(x2, g_pre, w_up, w_down, g_post)


def _split_w_in(w):
    gw, kw = GW, LANES
    o = 0
    cols = {}
    for name, width in (("qa", gw), ("kca", kw), ("vca", kw), ("ksa", kw), ("vsa", kw), ("kwa", kw),
                        ("vwa", kw), ("ga", 24), ("qb", gw), ("kb", gw), ("vb", gw), ("qc", gw),
                        ("kc", gw), ("vc", gw), ("fc", 8), ("qd", gw), ("kd", gw), ("vd", gw)):
        cols[name] = w[:, o:o + width] * (Q_FOLD if name[0] == "q" else 1.0)
        o += width
    main = jnp.concatenate([cols[n] for n in ("qa", "ksa", "vsa", "kwa", "vwa", "qb", "kb", "vb",
                                              "qc", "kc", "vc", "qd", "kd", "vd")], axis=1)
    pad = jnp.zeros((w.shape[0], LANES - 32), w.dtype)
    small = jnp.concatenate([cols["ga"], cols["fc"], pad, cols["kca"], cols["vca"]], axis=1)
    return main.astype(BF16), small.astype(BF16)


def _compress_weights(pe, w1, w2):
    half = NSA_CMP_LEN // 2
    hid = NSA_CMP_HIDDEN
    w1r = w1.reshape(2, half, HEAD_DIM, hid)
    z = jnp.zeros_like(w1r)
    per_group = jnp.stack([jnp.concatenate([w1r, z], axis=-1), jnp.concatenate([z, w1r], axis=-1)], axis=2)
    w_halves = per_group.reshape(2, half * 2 * HEAD_DIM, 2 * hid).astype(BF16)
    pe_r = jnp.broadcast_to(pe.reshape(2, half, 1, HEAD_DIM), (2, half, 2, HEAD_DIM)).reshape(2, half * LANES)
    z2 = jnp.zeros_like(w2)
    w2g = jnp.concatenate([jnp.concatenate([w2, z2], axis=1), jnp.concatenate([z2, w2], axis=1)], axis=0)
    return pe_r, w_halves[0], w_halves[1], w2g.astype(BF16)


def _bias_tables(rel_bias, t):
    hm = 8
    tq_a = min(NSA_TQ_CMP, t)
    tq_d = tk_d = min(NSA_TQ_DENSE, t)
    n_a = min(_n_toeplitz_tiles(tq_d, tk_d), t // tq_d)
    tab_a = _bias_tiles(rel_bias, _toeplitz_idx_t(n_a, tk_d, tq_d, tq_d), 0, 2, 4)
    ncp = t // NSA_CMP_STRIDE
    qi = jnp.arange(t // tq_a, dtype=jnp.int32)[:, None, None]
    c = jnp.arange(ncp, dtype=jnp.int32)[None, :, None]
    q = jnp.arange(tq_a, dtype=jnp.int32)[None, None, :]
    idx_c = _t5_bucket(qi * tq_a + q - (c * NSA_CMP_STRIDE + NSA_CMP_LEN - 1))
    tab_c = _bias_tiles(rel_bias, idx_c, 0, 2, 4)
    tq_b = min(MOBA_TQ, t)
    n_b = min(_n_toeplitz_tiles(tq_b, tq_b), t // tq_b)
    tab_b = _bias_tiles(rel_bias, _toeplitz_idx_t(n_b, tq_b, tq_b, tq_b), hm, 4, 2)
    kr = jnp.arange(2 * DIL_BAND, dtype=jnp.int32)[None, :, None]
    qc = jnp.arange(DIL_BAND, dtype=jnp.int32)[None, None, :]
    tabs_d = [_bias_tiles(rel_bias, _t5_bucket((DIL_BAND + qc - kr) * dil), 2 * hm, 4, 2)
              for _, dil in LONGNET_PATTERNS]
    return tab_a, tab_c, tab_b, tabs_d


def _overlap_t(t):
    ncp = t // NSA_CMP_STRIDE
    nsp = max(t // NSA_SEL_LEN, 8)
    c_start = jnp.arange(ncp)[None, :] * NSA_CMP_STRIDE
    s_start = jnp.arange(nsp)[:, None] * NSA_SEL_LEN
    ovl = (c_start < s_start + NSA_SEL_LEN) & (c_start + NSA_CMP_LEN - 1 >= s_start)
    ovl = ovl & (jnp.arange(ncp)[None, :] < ncp - 1) & (jnp.arange(nsp)[:, None] < t // NSA_SEL_LEN)
    return jnp.concatenate([ovl, ovl], axis=1).astype(BF16)


def _transposed(p3, pb, n_blocks):
    b, t, _ = p3.shape
    vt = p3[:, :, pb * LANES:(pb + n_blocks) * LANES].transpose(0, 2, 1).reshape(b, 2 * n_blocks, HEAD_DIM, t)
    ones = jnp.ones((b, 2 * n_blocks, VT_ROWS - HEAD_DIM, t), BF16)
    return jnp.concatenate([vt, ones], axis=2).reshape(b, 2 * n_blocks * VT_ROWS, t)


def _mixer_layer(x2, b, t, w_main, w_small, g_pre, g_post, w_out, cmp_w_k, cmp_w_v, fox_bias, tabs, ovl_t):
    tab_a, tab_c, tab_b, tabs_d = tabs
    m = b * t
    p2, pd2, small, kca, vca = _proj(x2, g_pre, w_main, w_small)
    p3 = p2.reshape(b, t, P_WIDTH)
    pd3 = pd2.reshape(b, t, PD_WIDTH)
    small3 = small.reshape(b, t, LANES)
    row_w = NSA_CMP_STRIDE * LANES
    kc = _nsa_compress(kca.reshape(b, t // NSA_CMP_STRIDE, row_w), *cmp_w_k)
    vc = _nsa_compress(vca.reshape(b, t // NSA_CMP_STRIDE, row_w), *cmp_w_v)
    o_cmp, negsel = _nsa_cmp(p3, kc, _transposed(vc, 0, 1), tab_c, ovl_t, small3)
    o_dense = _nsa_dense(p3, tab_a, small3, _transposed(p3, PB_VSA, 1), _transposed(p3, PB_VWA, 1), negsel)
    o_b = _moba(p3, tab_b, _transposed(p3, PB_VB, 4))
    fox_bias_row = jnp.zeros((1, LANES), F32).at[0, FOX_HI:FOX_HI + 8].set(fox_bias)
    o_f = _fox(p3, _fox_cumsum(small3, fox_bias_row), _transposed(p3, PB_VC, 4))
    dil = [_dilated(pd3, tab, d) for tab, (_, d) in zip(tabs_d, LONGNET_PATTERNS)]
    heads = [a.reshape(m, GW) for a in (o_cmp, o_dense, o_b, o_f)]
    return _out_proj(x2, heads, [d[0] for d in dil], [d[1] for d in dil], w_out, g_post)


def kernel(x, w_in, w_out, g_mix_pre, g_mix_post, g_mlp_pre, g_mlp_post, w_up, w_down, cmp_pe, phik_w1, phik_w2, phiv_w1, phiv_w2, fox_bias, rel_bias):
    b, t, d = x.shape
    depth = w_in.shape[0]
    tabs = _bias_tables(rel_bias, t)
    ovl_t = _overlap_t(t)
    x2 = x.reshape(b * t, d)
    for l in range(depth):
        w_main, w_small = _split_w_in(w_in[l])
        x2 = _mixer_layer(
            x2, b, t, w_main, w_small, g_mix_pre[l][None], g_mix_post[l][None], w_out[l].astype(BF16),
            _compress_weights(cmp_pe[l], phik_w1[l], phik_w2[l]),
            _compress_weights(cmp_pe[l], phiv_w1[l], phiv_w2[l]),
            fox_bias[l], tabs, ovl_t)
        x2 = _ffn(x2, g_mlp_pre[l][None], w_up[l].astype(BF16), w_down[l].astype(BF16), g_mlp_post[l][None])
    return x2.reshape(b, t, d)
```

```python
import functools
import math

import jax
import jax.numpy as jnp
from jax import lax
from jax.experimental import pallas as pl
from jax.experimental.pallas import tpu as pltpu

F32 = jnp.float32
BF16 = jnp.bfloat16

D_MODEL = 2048
D_FF = 4 * D_MODEL
HEAD_DIM = 64
LANES = 128
EPS = 1e-6
NEG = -1e30
BIG = 1e30
SCALE = HEAD_DIM ** -0.5
LOG2E = math.log2(math.e)
LN2 = math.log(2.0)
Q_FOLD = SCALE * LOG2E

N_BUCKETS = 32
MAX_EXACT = 16
MAX_DISTANCE = 4096
LAST_BUCKET_DIST = 2897

NSA_CMP_LEN = 32
NSA_CMP_STRIDE = 16
NSA_CMP_HIDDEN = 256
NSA_SEL_LEN = 64
NSA_TOP_N = 16
NSA_WINDOW = 512
MOBA_BLOCK = 256
MOBA_TOPK = 3
LONGNET_PATTERNS = ((128, 1), (512, 4), (2048, 16))
DIL_BAND = 128

FOX_TQ = 512
MOBA_TQ = 512
NSA_TQ_CMP = 256
NSA_TQ_DENSE = 256
PROJ_TM, PROJ_TN = 1024, 512
OUT_TM = 256
FFN_TM, FFN_TF = 512, 1024
CUMSUM_TC = 512

P_WIDTH = 4096
PB_QA, PB_KSA, PB_VSA, PB_KWA, PB_VWA = 0, 4, 5, 6, 7
PB_QB, PB_KB, PB_VB = 8, 12, 16
PB_QC, PB_KC, PB_VC = 20, 24, 28
PD_WIDTH = 1536
GW = 4 * LANES
BF16_SUBLANES = 16
VT_ROWS = HEAD_DIM + BF16_SUBLANES

FOX_HI, FOX_MID, FOX_LO = 24, 32, 40

V7X_VMEM_BYTES = 64 * 1024 * 1024
VMEM_LIMIT = V7X_VMEM_BYTES * 7 // 8


def _cparams(sem):
    return pltpu.CompilerParams(dimension_semantics=sem, vmem_limit_bytes=VMEM_LIMIT)


def _dot(a, b):
    return jnp.dot(a, b, preferred_element_type=F32)


def _dot_nt(a, b):
    return lax.dot_general(a, b, (((1,), (1,)), ((), ())), preferred_element_type=F32)


def _dot_hi(a, b):
    return jnp.dot(a, b, preferred_element_type=F32, precision=lax.Precision.HIGHEST)


def _spread_lanes(w, spread):
    hi = w.astype(BF16)
    lo = (w - hi.astype(F32)).astype(BF16)
    return _dot(jnp.concatenate([hi, lo], axis=1), spread)


def _t5_bucket(dist):
    dist = jnp.maximum(dist, 0)
    rel = jnp.log(jnp.maximum(dist, 1).astype(jnp.float32) / MAX_EXACT) / math.log(MAX_DISTANCE / MAX_EXACT)
    large = jnp.minimum(MAX_EXACT + (rel * (N_BUCKETS - MAX_EXACT)).astype(jnp.int32), N_BUCKETS - 1)
    return jnp.where(dist < MAX_EXACT, dist, large)


def _bias_tile_kernel(tab_ref, idx_ref, o_ref, *, head0, hpg, cols):
    grp = pl.program_id(0)
    idx = idx_ref[0]
    n_bits = N_BUCKETS.bit_length() - 1
    bit_set = [((idx >> b) & 1) == 1 for b in range(n_bits)]
    for hh in range(hpg):
        h = head0 + grp * hpg + hh
        level = [tab_ref[b, h] for b in range(N_BUCKETS)]
        for b in range(n_bits):
            level = [jnp.where(bit_set[b], level[2 * i + 1], level[2 * i]) for i in range(len(level) // 2)]
        o_ref[0, 0, :, hh * cols:(hh + 1) * cols] = level[0] * LOG2E


def _bias_tiles(rel_bias, idx, head0, n_groups, hpg=1):
    n, r, c = idx.shape
    return pl.pallas_call(
        functools.partial(_bias_tile_kernel, head0=head0, hpg=hpg, cols=c),
        out_shape=jax.ShapeDtypeStruct((n_groups, n, r, hpg * c), F32),
        grid=(n_groups, n),
        in_specs=[pl.BlockSpec(memory_space=pltpu.SMEM),
                  pl.BlockSpec((1, r, c), lambda g, i: (i, 0, 0))],
        out_specs=pl.BlockSpec((1, 1, r, hpg * c), lambda g, i: (g, i, 0, 0)),
        compiler_params=_cparams(("parallel", "parallel")),
        name="bias_tiles",
    )(rel_bias, idx)


def _toeplitz_idx_t(n_tiles, krows, qcols, q_stride):
    nd = jnp.arange(n_tiles, dtype=jnp.int32)[:, None, None]
    r = jnp.arange(krows, dtype=jnp.int32)[None, :, None]
    c = jnp.arange(qcols, dtype=jnp.int32)[None, None, :]
    return _t5_bucket(nd * q_stride + c - r)


def _n_toeplitz_tiles(q_stride, keys):
    return -(-(LAST_BUCKET_DIST + keys - 1) // q_stride) + 1


def _proj_kernel(x_ref, g_ref, w_ref, ws_ref, p_ref, pd_ref, s_ref, kc_ref, vc_ref, h_ref, *, n_main):
    j = pl.program_id(1)

    @pl.when(j == 0)
    def _():
        x = x_ref[...]
        y = x * lax.rsqrt(jnp.mean(x * x, axis=-1, keepdims=True) + EPS)
        hb = (y * g_ref[...]).astype(BF16)
        h_ref[...] = hb
        small = _dot(hb, ws_ref[...])
        s_ref[...] = small[:, 0:LANES]
        kc_ref[...] = small[:, LANES:2 * LANES]
        vc_ref[...] = small[:, 2 * LANES:3 * LANES]

    y = _dot(h_ref[...], w_ref[...]).astype(BF16)

    @pl.when(j < n_main)
    def _():
        p_ref[...] = y

    @pl.when(j >= n_main)
    def _():
        pd_ref[...] = y


def _proj(x2, g, w_main, w_small, tm=PROJ_TM, tn=PROJ_TN):
    m = x2.shape[0]
    tm = min(tm, m)
    n_main = P_WIDTH // tn
    return pl.pallas_call(
        functools.partial(_proj_kernel, n_main=n_main),
        out_shape=(jax.ShapeDtypeStruct((m, P_WIDTH), BF16),
                   jax.ShapeDtypeStruct((m, PD_WIDTH), BF16),
                   jax.ShapeDtypeStruct((m, LANES), F32),
                   jax.ShapeDtypeStruct((m, LANES), F32),
                   jax.ShapeDtypeStruct((m, LANES), F32)),
        grid=(m // tm, (P_WIDTH + PD_WIDTH) // tn),
        in_specs=[pl.BlockSpec((tm, D_MODEL), lambda i, j: (i, 0)),
                  pl.BlockSpec((1, D_MODEL), lambda i, j: (0, 0)),
                  pl.BlockSpec((D_MODEL, tn), lambda i, j: (0, j)),
                  pl.BlockSpec((D_MODEL, 3 * LANES), lambda i, j: (0, 0))],
        out_specs=(pl.BlockSpec((tm, tn), lambda i, j: (i, jnp.minimum(j, n_main - 1))),
                   pl.BlockSpec((tm, tn), lambda i, j: (i, jnp.maximum(j - n_main, 0))),
                   pl.BlockSpec((tm, LANES), lambda i, j: (i, 0)),
                   pl.BlockSpec((tm, LANES), lambda i, j: (i, 0)),
                   pl.BlockSpec((tm, LANES), lambda i, j: (i, 0))),
        scratch_shapes=[pltpu.VMEM((tm, D_MODEL), BF16)],
        compiler_params=_cparams(("parallel", "arbitrary")),
        name="rms_in_proj",
    )(x2, g, w_main, w_small)


def _lane_lo(shape):
    return lax.broadcasted_iota(jnp.int32, shape, 1) < HEAD_DIM


def _pair_queries(q_ref):
    q = q_ref[...]
    lo = _lane_lo(q.shape)
    zero = jnp.zeros_like(q)
    return jnp.where(lo, q, zero), jnp.where(lo, zero, q)


def _col_softmax(s, m_ref, first):
    if first:
        m = jnp.max(s, axis=0, keepdims=True)
        m_ref[...] = m
        return jnp.exp2(s - m).astype(BF16), None
    m_prev = m_ref[...]
    m_new = jnp.maximum(m_prev, jnp.max(s, axis=0, keepdims=True))
    m_ref[...] = m_new
    return jnp.exp2(s - m_new).astype(BF16), jnp.exp2(m_prev - m_new)


def _accumulate(acc_ref, cols, pv, alpha):
    if alpha is None:
        acc_ref[:, cols] = pv
    else:
        acc_ref[:, cols] = alpha[:, cols] * acc_ref[:, cols] + pv


def _key_tiles(j_diag, j_first, n_past, qk, consume, sa_ref, sb_ref):
    last = j_first + jnp.maximum(n_past - 1, 0)
    sa_ref[...] = qk(j_first)
    consume(j_diag, qk(j_diag), True)

    def run(base, count):
        for t in range(count):
            cur, nxt = (sa_ref, sb_ref) if t % 2 == 0 else (sb_ref, sa_ref)
            nxt[...] = qk(jnp.minimum(base + t + 1, last))
            consume(base + t, cur[...], False)

    quads = n_past // 4
    lax.fori_loop(0, quads, lambda i, c: (run(j_first + 4 * i, 4), c)[1], 0)
    rest = n_past - 4 * quads

    @pl.when(rest >= 2)
    def _():
        run(j_first + 4 * quads, 2)

    @pl.when(rest % 2 == 1)
    def _():
        consume(j_first + n_past - 1, sa_ref[...], False)


def _top_rows(v, k):
    n = v.shape[0]
    rows = lax.broadcasted_iota(jnp.int32, v.shape, 0)
    chosen = jnp.zeros(v.shape, F32)
    for _ in range(k):
        top = jnp.max(v, axis=0, keepdims=True)
        first = jnp.min(jnp.where(v == top, rows, n), axis=0, keepdims=True)
        pick = rows == first
        chosen = jnp.where(pick, 1.0, chosen)
        v = jnp.where(pick, -jnp.inf, v)
    return chosen


def _top_rows_bisect(v, k):
    n = v.shape[0]
    key = pltpu.bitcast(v, jnp.int32)
    thr = jnp.zeros((1, v.shape[1]), jnp.int32)
    def enough(cand):
        return jnp.sum(jnp.where(key >= cand, 1.0, 0.0), axis=0, keepdims=True) >= k

    for bit in range(30, 0, -2):
        c_hi, c_lo = thr | (1 << bit), thr | (1 << (bit - 1))
        c_both = c_hi | (1 << (bit - 1))
        thr = jnp.where(enough(c_both), c_both, jnp.where(enough(c_hi), c_hi,
                                                          jnp.where(enough(c_lo), c_lo, thr)))
    thr = jnp.where(enough(thr | 1), thr | 1, thr)
    above = key > thr
    tie = key == thr
    n_above = jnp.sum(jnp.where(above, 1.0, 0.0), axis=0, keepdims=True)
    ri = lax.broadcasted_iota(jnp.int32, (n, n), 0)
    ci = lax.broadcasted_iota(jnp.int32, (n, n), 1)
    before = jnp.where(ci < ri, 1.0, 0.0).astype(BF16)
    ties_before = _dot(before, jnp.where(tie, 1.0, 0.0).astype(BF16))
    return above | (tie & (ties_before < k - n_above))


def _pair_finish(acc_ref, o_ref, tq):
    o_t = acc_ref[0:HEAD_DIM, :] / acc_ref[HEAD_DIM:HEAD_DIM + 1, :]
    o_ref[0] = jnp.concatenate([o_t[:, 0:tq], o_t[:, tq:2 * tq]], axis=0).T.astype(BF16)


def _fox_cumsum_kernel(s_ref, fb_ref, c_ref, carry_ref, *, tc):
    @pl.when(pl.program_id(1) == 0)
    def _():
        carry_ref[...] = jnp.zeros_like(carry_ref)

    lane = lax.broadcasted_iota(jnp.int32, (tc, LANES), 1)
    z = s_ref[0] + fb_ref[...]
    logf = -(jnp.maximum(-z, 0.0) + jnp.log(1.0 + jnp.exp(-jnp.abs(z))))
    logf = jnp.where((lane >= FOX_HI) & (lane < FOX_MID), logf, 0.0)
    ri = lax.broadcasted_iota(jnp.int32, (tc, tc), 0)
    ci = lax.broadcasted_iota(jnp.int32, (tc, tc), 1)
    lower = jnp.where(ri >= ci, 1.0, 0.0).astype(F32)
    c = _dot_hi(lower, logf) + carry_ref[...]
    carry_ref[...] = c[tc - 1:tc, :]
    c2 = c * LOG2E
    hi = c2.astype(BF16).astype(F32)
    rest = c2 - hi
    mid = rest.astype(BF16).astype(F32)
    lo = rest - mid
    pieces = hi + pltpu.roll(mid, FOX_MID - FOX_HI, axis=1) + pltpu.roll(lo, FOX_LO - FOX_HI, axis=1)
    c_ref[0] = pieces.astype(BF16)


def _fox_cumsum(small3, fox_bias_row, tc=CUMSUM_TC):
    b, t, _ = small3.shape
    tc = min(tc, t)
    return pl.pallas_call(
        functools.partial(_fox_cumsum_kernel, tc=tc),
        out_shape=jax.ShapeDtypeStruct((b, t, LANES), BF16),
        grid=(b, t // tc),
        in_specs=[pl.BlockSpec((1, tc, LANES), lambda i, j: (i, j, 0)),
                  pl.BlockSpec((1, LANES), lambda i, j: (0, 0))],
        out_specs=pl.BlockSpec((1, tc, LANES), lambda i, j: (i, j, 0)),
        scratch_shapes=[pltpu.VMEM((1, LANES), F32)],
        compiler_params=_cparams(("parallel", "arbitrary")),
        name="fox_cumsum",
    )(small3, fox_bias_row)


def _fox_kernel(q_ref, k_ref, c_ref, vt_ref, o_ref, kaug_ref, sa_ref, sb_ref, m_ref, acc_ref, *, tq):
    hp = pl.program_id(1)
    qi = pl.program_id(2)

    @pl.when(qi == 0)
    def _():
        kaug_ref[:, 0:LANES] = k_ref[0]
        kaug_ref[:, LANES:2 * LANES] = c_ref[0]

    qs = _pair_queries(q_ref.at[0])
    lane = lax.broadcasted_iota(jnp.int32, (tq, LANES), 1)
    rows = []
    for hh in range(2):
        h = 2 * hp + hh
        pick = (lane == FOX_HI + h) | (lane == FOX_MID + h) | (lane == FOX_LO + h)
        rows.append(jnp.concatenate([qs[hh], jnp.where(pick, -1.0, 0.0).astype(BF16)], axis=1))
    q_aug = jnp.concatenate(rows, axis=0)
    kr = lax.broadcasted_iota(jnp.int32, (tq, 2 * tq), 0)
    qc = lax.broadcasted_iota(jnp.int32, (tq, 2 * tq), 1) & (tq - 1)

    def qk(j):
        return _dot_nt(kaug_ref[pl.ds(pl.multiple_of(j * tq, tq), tq), :], q_aug)

    def consume(j, s, first):
        k0 = pl.multiple_of(j * tq, tq)
        if first:
            s = jnp.where(kr <= qc, s, NEG)
        p, alpha = _col_softmax(s, m_ref, first)
        for hh in range(2):
            cols = slice(hh * tq, (hh + 1) * tq)
            vt = vt_ref[0, hh * VT_ROWS:(hh + 1) * VT_ROWS, pl.ds(k0, tq)]
            _accumulate(acc_ref, cols, _dot(vt, p[:, cols]), alpha)

    _key_tiles(qi, 0, qi, qk, consume, sa_ref, sb_ref)
    _pair_finish(acc_ref, o_ref, tq)


def _fox(p3, c_aug, vt, tq=FOX_TQ):
    b, t, _ = p3.shape
    tq = min(tq, t)
    return pl.pallas_call(
        functools.partial(_fox_kernel, tq=tq),
        out_shape=jax.ShapeDtypeStruct((b, t, GW), BF16),
        grid=(b, 4, t // tq),
        in_specs=[pl.BlockSpec((1, tq, LANES), lambda i, h, q: (i, q, PB_QC + h)),
                  pl.BlockSpec((1, t, LANES), lambda i, h, q: (i, 0, PB_KC + h)),
                  pl.BlockSpec((1, t, LANES), lambda i, h, q: (i, 0, 0)),
                  pl.BlockSpec((1, 2 * VT_ROWS, t), lambda i, h, q: (i, h, 0))],
        out_specs=pl.BlockSpec((1, tq, LANES), lambda i, h, q: (i, q, h)),
        scratch_shapes=[pltpu.VMEM((t, 2 * LANES), BF16),
                        pltpu.VMEM((tq, 2 * tq), F32), pltpu.VMEM((tq, 2 * tq), F32),
                        pltpu.VMEM((1, 2 * tq), F32), pltpu.VMEM((VT_ROWS, 2 * tq), F32)],
        compiler_params=_cparams(("parallel", "parallel", "arbitrary")),
        name="fox_attention",
    )(p3, p3, c_aug, vt)


def _moba_kernel(q_ref, k_ref, vt_ref, tab_ref, o_ref, km_ref, kaug_ref, sa_ref, sb_ref, m_ref, acc_ref,
                 *, nkb, nkp, n_tab, tq):
    blk = MOBA_BLOCK
    qi = pl.program_id(2)

    @pl.when(qi == 0)
    def _():
        km_ref[...] = jnp.zeros_like(km_ref)
        lane = lax.broadcasted_iota(jnp.int32, (blk, LANES), 1)
        for n in range(nkb):
            kb = k_ref[0, n * blk:(n + 1) * blk, :]
            km_ref[n:n + 1, :] = jnp.mean(kb.astype(F32), axis=0, keepdims=True)
            kaug_ref[n * blk:(n + 1) * blk, 0:LANES] = kb
            kaug_ref[n * blk:(n + 1) * blk, LANES:2 * LANES] = jnp.where(lane == n, 1.0, 0.0).astype(BF16)

    qs = _pair_queries(q_ref.at[0])
    q_both = jnp.concatenate([qs[0], qs[1]], axis=0)
    km = km_ref[...]
    km_hi = km.astype(BF16)
    km_lo = (km - km_hi.astype(F32)).astype(BF16)
    gate = _dot_nt(jnp.concatenate([km_hi, km_lo], axis=0), q_both)
    gate = gate[0:nkp] + gate[nkp:2 * nkp]
    n = lax.broadcasted_iota(jnp.int32, gate.shape, 0)
    q_pos = lax.broadcasted_iota(jnp.int32, gate.shape, 1) & (tq - 1)
    own = qi * (tq // blk) + q_pos // blk
    past = n < own
    gate = jnp.where(past, gate, NEG)
    allowed = ((_top_rows(gate, MOBA_TOPK) > 0.5) & past) | (n == own)
    negsel = jnp.where(allowed, 0.0, NEG).astype(F32)
    if nkp < LANES:
        negsel = jnp.concatenate([negsel, jnp.zeros((LANES - nkp, 2 * tq), F32)], axis=0)
    q_aug = jnp.concatenate([q_both, negsel.T.astype(BF16)], axis=1)

    kr = lax.broadcasted_iota(jnp.int32, (tq, 2 * tq), 0)
    qc = lax.broadcasted_iota(jnp.int32, (tq, 2 * tq), 1) & (tq - 1)

    def qk(j):
        return _dot_nt(kaug_ref[pl.ds(pl.multiple_of(j * tq, tq), tq), :], q_aug)

    def consume(j, s, first):
        k0 = pl.multiple_of(j * tq, tq)
        s = s + tab_ref[0, jnp.minimum(qi - j, n_tab - 1)]
        if first:
            s = jnp.where(kr <= qc, s, NEG)
        p, alpha = _col_softmax(s, m_ref, first)
        for hh in range(2):
            cols = slice(hh * tq, (hh + 1) * tq)
            vt = vt_ref[0, hh * VT_ROWS:(hh + 1) * VT_ROWS, pl.ds(k0, tq)]
            _accumulate(acc_ref, cols, _dot(vt, p[:, cols]), alpha)

    _key_tiles(qi, 0, qi, qk, consume, sa_ref, sb_ref)
    _pair_finish(acc_ref, o_ref, tq)


def _moba(p3, tab, vt):
    b, t, _ = p3.shape
    tq = tab.shape[2]
    nkb = t // MOBA_BLOCK
    nkp = -(-nkb // 8) * 8
    n_tab = tab.shape[1]
    return pl.pallas_call(
        functools.partial(_moba_kernel, nkb=nkb, nkp=nkp, n_tab=n_tab, tq=tq),
        out_shape=jax.ShapeDtypeStruct((b, t, GW), BF16),
        grid=(b, 4, t // tq),
        in_specs=[pl.BlockSpec((1, tq, LANES), lambda i, h, q: (i, q, PB_QB + h)),
                  pl.BlockSpec((1, t, LANES), lambda i, h, q: (i, 0, PB_KB + h)),
                  pl.BlockSpec((1, 2 * VT_ROWS, t), lambda i, h, q: (i, h, 0)),
                  pl.BlockSpec((1, n_tab, tq, 2 * tq), lambda i, h, q: (h, 0, 0, 0),
                               pipeline_mode=pl.Buffered(1))],
        out_specs=pl.BlockSpec((1, tq, LANES), lambda i, h, q: (i, q, h)),
        scratch_shapes=[pltpu.VMEM((nkp, LANES), F32), pltpu.VMEM((t, 2 * LANES), BF16),
                        pltpu.VMEM((tq, 2 * tq), F32), pltpu.VMEM((tq, 2 * tq), F32),
                        pltpu.VMEM((1, 2 * tq), F32), pltpu.VMEM((VT_ROWS, 2 * tq), F32)],
        compiler_params=_cparams(("parallel", "parallel", "arbitrary")),
        name="moba_attention",
    )(p3, p3, vt, tab)


def _gelu_tanh(x):
    return 0.5 * x * (1.0 + jnp.tanh(math.sqrt(2.0 / math.pi) * (x + 0.044715 * (x * x * x))))


def _nsa_compress_kernel(r_ref, pe_ref, wlo_ref, whi_ref, w2_ref, o_ref, *, nr):
    r = r_ref[0]
    pe = pe_ref[...]
    a = _dot((r + pe[0:1, :]).astype(BF16), wlo_ref[...])
    b = _dot((r + pe[1:2, :]).astype(BF16), whi_ref[...])
    hid = _gelu_tanh(a + pltpu.roll(b, nr - 1, axis=0))
    o_ref[0] = _dot(hid.astype(BF16), w2_ref[...]).astype(BF16)


def _nsa_compress(r3, pe2, wlo, whi, w2):
    b, nr, w = r3.shape
    return pl.pallas_call(
        functools.partial(_nsa_compress_kernel, nr=nr),
        out_shape=jax.ShapeDtypeStruct((b, nr, LANES), BF16),
        grid=(b,),
        in_specs=[pl.BlockSpec((1, nr, w), lambda i: (i, 0, 0)),
                  pl.BlockSpec((2, w), lambda i: (0, 0)),
                  pl.BlockSpec((w, 2 * NSA_CMP_HIDDEN), lambda i: (0, 0)),
                  pl.BlockSpec((w, 2 * NSA_CMP_HIDDEN), lambda i: (0, 0)),
                  pl.BlockSpec((2 * NSA_CMP_HIDDEN, LANES), lambda i: (0, 0))],
        out_specs=pl.BlockSpec((1, nr, LANES), lambda i: (i, 0, 0)),
        compiler_params=_cparams(("parallel",)),
        name="nsa_compress",
    )(r3, pe2, wlo, whi, w2)


def _group_queries(q_ref, g):
    half = lax.broadcasted_iota(jnp.int32, (q_ref.shape[0], LANES), 1) // HEAD_DIM
    lo = half == 0
    in_group = half == g
    out = []
    for pb in range(2):
        q = q_ref[:, pb * LANES:(pb + 1) * LANES].astype(F32)
        qr = pltpu.roll(q, HEAD_DIM, axis=1)
        for dup in (jnp.where(lo, q, qr), jnp.where(lo, qr, q)):
            out.append(jnp.where(in_group, dup, 0.0).astype(BF16))
    return jnp.concatenate(out, axis=0)


def _group_gates(s_ref, g, branch):
    sig = 1.0 / (1.0 + jnp.exp(-s_ref[...]))
    col = lax.broadcasted_iota(jnp.int32, (2 * LANES, 2 * LANES), 0) & (LANES - 1)
    lane = lax.broadcasted_iota(jnp.int32, (2 * LANES, 2 * LANES), 1)
    spread = jnp.where(col == branch * 8 + g * 4 + lane // HEAD_DIM, 1.0, 0.0).astype(BF16)
    return _spread_lanes(sig, spread)


def _nsa_cmp_kernel(q_ref, kc_ref, vct_ref, tab_ref, ovl_ref, s_ref, o_ref, sel_ref, *, tq, ncp, nsp):
    g = pl.program_id(1)
    qi = pl.program_id(2)
    qs = _group_queries(q_ref.at[0], g)
    s = _dot_nt(kc_ref[0], qs) + tab_ref[0, 0]
    c = lax.broadcasted_iota(jnp.int32, (ncp, 4 * tq), 0)
    t = qi * tq + (lax.broadcasted_iota(jnp.int32, (ncp, 4 * tq), 1) & (tq - 1))
    ok = (t - (c * NSA_CMP_STRIDE + NSA_CMP_LEN - 1) >= 0) & (c < ncp - 1)
    s = jnp.where(ok, s, NEG)
    e = jnp.exp2(s - jnp.max(s, axis=0, keepdims=True))
    p = jnp.where(ok, e / jnp.sum(e, axis=0, keepdims=True), 0.0)
    vt = vct_ref[0, pl.ds(pl.multiple_of(g * VT_ROWS, VT_ROWS), VT_ROWS), :]
    o_t = _dot(vt, p.astype(BF16))[0:HEAD_DIM, :]
    o = jnp.concatenate([o_t[:, h * tq:(h + 1) * tq] for h in range(4)], axis=0).T
    o_ref[0] = (o * _group_gates(s_ref.at[0], g, 0)).astype(BF16)

    psum = p[:, 0:tq] + p[:, tq:2 * tq] + p[:, 2 * tq:3 * tq] + p[:, 3 * tq:4 * tq]
    hi = psum.astype(BF16)
    lo = (psum - hi.astype(F32)).astype(BF16)
    imp = _dot(ovl_ref[...], jnp.concatenate([hi, lo], axis=0))
    n = lax.broadcasted_iota(jnp.int32, (nsp, tq), 0)
    cur = (qi * tq + lax.broadcasted_iota(jnp.int32, (nsp, tq), 1)) // NSA_SEL_LEN
    forced = (n == 0) | (n == cur) | (n == cur - 1)
    valid = n <= cur
    imp = jnp.where(forced, BIG, jnp.where(valid, jnp.abs(imp), NEG))
    chosen = _top_rows_bisect(imp, NSA_TOP_N) & valid
    sel_ref[0] = jnp.where(chosen, 0.0, NEG).astype(F32).T.astype(BF16)


def _nsa_cmp(p3, kc, vct, tab, ovl_t, small3, tq=NSA_TQ_CMP):
    b, t, _ = p3.shape
    ncp = kc.shape[1]
    nsp = ovl_t.shape[0]
    tq = min(tq, t)
    return pl.pallas_call(
        functools.partial(_nsa_cmp_kernel, tq=tq, ncp=ncp, nsp=nsp),
        out_shape=(jax.ShapeDtypeStruct((b, t, GW), BF16),
                   jax.ShapeDtypeStruct((b, t, 2 * nsp), BF16)),
        grid=(b, 2, t // tq),
        in_specs=[pl.BlockSpec((1, tq, 2 * LANES), lambda i, g, q: (i, q, g)),
                  pl.BlockSpec((1, ncp, LANES), lambda i, g, q: (i, 0, 0)),
                  pl.BlockSpec((1, 2 * VT_ROWS, ncp), lambda i, g, q: (i, 0, 0)),
                  pl.BlockSpec((1, 1, ncp, 4 * tq), lambda i, g, q: (g, q, 0, 0)),
                  pl.BlockSpec((nsp, 2 * ncp), lambda i, g, q: (0, 0)),
                  pl.BlockSpec((1, tq, LANES), lambda i, g, q: (i, q, 0))],
        out_specs=(pl.BlockSpec((1, tq, 2 * LANES), lambda i, g, q: (i, q, g)),
                   pl.BlockSpec((1, tq, nsp), lambda i, g, q: (i, q, g))),
        compiler_params=_cparams(("parallel", "parallel", "parallel")),
        name="nsa_compressed",
    )(p3, kc, vct, tab, ovl_t, small3)


def _nsa_dense_kernel(q_ref, ks_ref, vts_ref, kw_ref, vtw_ref, tab_ref, blk_ref, sel_ref, s_ref, o_ref,
                      kaug_ref, sa_ref, sb_ref, m_ref, acc_ref, *, tq, tk, n_tab):
    g = pl.program_id(1)
    qi = pl.program_id(2)
    qs = _group_queries(q_ref.at[0], g)
    kr = lax.broadcasted_iota(jnp.int32, (tk, 4 * tq), 0)
    qc = lax.broadcasted_iota(jnp.int32, (tk, 4 * tq), 1) & (tq - 1)
    j0 = (qi * tq) // tk
    g0 = pl.multiple_of(g * VT_ROWS, VT_ROWS)

    @pl.when(qi == 0)
    def _():
        kaug_ref[:, 0:LANES] = ks_ref[0]
        kaug_ref[:, LANES:] = blk_ref[...]

    qs_sel = jnp.concatenate([qs, jnp.concatenate([sel_ref[0]] * 4, axis=0)], axis=1)

    def branch(selected):
        k_tile = ((lambda k0: kaug_ref[pl.ds(k0, tk), :]) if selected
                  else (lambda k0: kw_ref[0, pl.ds(k0, tk), :]))
        q_all = qs_sel if selected else qs
        vt_ref = vts_ref if selected else vtw_ref

        def qk(j):
            return _dot_nt(k_tile(pl.multiple_of(j * tk, tk)), q_all)

        def consume(j, s, first):
            k0 = pl.multiple_of(j * tk, tk)
            s = s + tab_ref[0, jnp.minimum(qi - j * (tk // tq), n_tab - 1)]
            dist = (qi * tq - j * tk) + qc - kr
            if not selected:
                s = jnp.where((dist >= 0) & (dist <= NSA_WINDOW - 1), s, NEG)
            elif first:
                s = jnp.where(dist >= 0, s, NEG)
            p, alpha = _col_softmax(s, m_ref, first)
            vt = vt_ref[0, pl.ds(g0, VT_ROWS), pl.ds(k0, tk)]
            _accumulate(acc_ref, slice(None), _dot(vt, p), alpha)

        first_j = 0 if selected else jnp.maximum(j0 - (-(-(NSA_WINDOW - 1) // tk)), 0)
        _key_tiles(j0, first_j, j0 - first_j, qk, consume, sa_ref, sb_ref)
        o_t = acc_ref[0:HEAD_DIM, :] / acc_ref[HEAD_DIM:HEAD_DIM + 1, :]
        o = jnp.concatenate([o_t[:, h * tq:(h + 1) * tq] for h in range(4)], axis=0).T
        return o * _group_gates(s_ref.at[0], g, 1 if selected else 2)

    o_ref[0] = (branch(True) + branch(False)).astype(BF16)


def _nsa_dense(p3, tab, small3, vt_sel, vt_win, negsel):
    b, t, _ = p3.shape
    tq = tk = min(NSA_TQ_DENSE, t)
    n_tab = tab.shape[1]
    nsp = negsel.shape[2] // 2
    block_of_key = (jnp.arange(t)[:, None] // NSA_SEL_LEN == jnp.arange(nsp)[None, :]).astype(BF16)
    keys = lambda pb: pl.BlockSpec((1, t, LANES), lambda i, g, q: (i, 0, pb))
    values = pl.BlockSpec((1, 2 * VT_ROWS, t), lambda i, g, q: (i, 0, 0))
    return pl.pallas_call(
        functools.partial(_nsa_dense_kernel, tq=tq, tk=tk, n_tab=n_tab),
        out_shape=jax.ShapeDtypeStruct((b, t, GW), BF16),
        grid=(b, 2, t // tq),
        in_specs=[pl.BlockSpec((1, tq, 2 * LANES), lambda i, g, q: (i, q, g)),
                  keys(PB_KSA), values, keys(PB_KWA), values,
                  pl.BlockSpec((1, n_tab, tk, 4 * tq), lambda i, g, q: (g, 0, 0, 0),
                               pipeline_mode=pl.Buffered(1)),
                  pl.BlockSpec((t, nsp), lambda i, g, q: (0, 0)),
                  pl.BlockSpec((1, tq, nsp), lambda i, g, q: (i, q, g)),
                  pl.BlockSpec((1, tq, LANES), lambda i, g, q: (i, q, 0))],
        out_specs=pl.BlockSpec((1, tq, 2 * LANES), lambda i, g, q: (i, q, g)),
        scratch_shapes=[pltpu.VMEM((t, LANES + nsp), BF16),
                        pltpu.VMEM((tk, 4 * tq), F32), pltpu.VMEM((tk, 4 * tq), F32),
                        pltpu.VMEM((1, 4 * tq), F32), pltpu.VMEM((VT_ROWS, 4 * tq), F32)],
        compiler_params=_cparams(("parallel", "parallel", "arbitrary")),
        name="nsa_selected_window",
    )(p3, p3, vt_sel, p3, vt_win, tab, block_of_key, negsel, small3)


def _dilated_kernel(q_ref, kp_ref, kc_ref, vtp_ref, vtc_ref, tab_ref, o_ref, lse_ref, *, tq):
    qi = pl.program_id(2)
    kr = lax.broadcasted_iota(jnp.int32, (2 * tq, 2 * tq), 0)
    qc = lax.broadcasted_iota(jnp.int32, (2 * tq, 2 * tq), 1) & (tq - 1)
    dist = tq + qc - kr
    valid = (dist >= 0) & (dist <= tq) & (kr >= jnp.where(qi > 0, 0, tq))
    outs, lses = [], []
    for pb in range(4):
        cols = slice(pb * LANES, (pb + 1) * LANES)
        qs = _pair_queries(q_ref.at[0, :, cols])
        q_both = jnp.concatenate([qs[0], qs[1]], axis=0)
        k2 = jnp.concatenate([kp_ref[0, :, cols], kc_ref[0, :, cols]], axis=0)
        s = jnp.where(valid, _dot_nt(k2, q_both) + tab_ref[pb, 0], NEG)
        m = jnp.max(s, axis=0, keepdims=True)
        p = jnp.exp2(s - m)
        l = jnp.sum(p, axis=0, keepdims=True)
        p = p.astype(BF16)
        lse = (m + jnp.log2(l)) * LN2
        for hh in range(2):
            rows = slice((2 * pb + hh) * HEAD_DIM, (2 * pb + hh + 1) * HEAD_DIM)
            qcols = slice(hh * tq, (hh + 1) * tq)
            vt = jnp.concatenate([vtp_ref[0, 0, rows, :], vtc_ref[0, 0, rows, :]], axis=1)
            outs.append(_dot(vt, p[:, qcols]) / l[:, qcols])
            lses.append(lse[:, qcols])
    o_ref[0] = jnp.concatenate(outs, axis=0).T.astype(BF16)
    lses.append(jnp.zeros((LANES - len(lses), tq), F32))
    lse_ref[0] = jnp.concatenate(lses, axis=0).T


def _dilated(pd3, tab, dil, tq=DIL_BAND):
    b, t, _ = pd3.shape
    l = t // dil
    qk = pd3[:, :, 0:2 * GW].reshape(b, l, dil * 2 * GW)
    vt = pd3[:, :, 2 * GW:3 * GW].reshape(b, l, dil, GW).transpose(0, 2, 3, 1)

    def prev(q):
        return jnp.maximum(q - 1, 0)

    out_spec = pl.BlockSpec((1, tq, GW), lambda i, r, q: (i, q, r))
    o, lse = pl.pallas_call(
        functools.partial(_dilated_kernel, tq=tq),
        out_shape=(jax.ShapeDtypeStruct((b, l, dil * GW), BF16),
                   jax.ShapeDtypeStruct((b, l, dil * LANES), F32)),
        grid=(b, dil, l // tq),
        in_specs=[pl.BlockSpec((1, tq, GW), lambda i, r, q: (i, q, 2 * r)),
                  pl.BlockSpec((1, tq, GW), lambda i, r, q: (i, prev(q), 2 * r + 1)),
                  pl.BlockSpec((1, tq, GW), lambda i, r, q: (i, q, 2 * r + 1)),
                  pl.BlockSpec((1, 1, GW, tq), lambda i, r, q: (i, r, 0, prev(q))),
                  pl.BlockSpec((1, 1, GW, tq), lambda i, r, q: (i, r, 0, q)),
                  pl.BlockSpec((4, 1, 2 * tq, 2 * tq), lambda i, r, q: (0, 0, 0, 0))],
        out_specs=(out_spec, pl.BlockSpec((1, tq, LANES), lambda i, r, q: (i, q, r))),
        compiler_params=_cparams(("parallel", "parallel", "parallel")),
        name="dilated_attention_%d" % dil,
    )(qk, qk, qk, vt, vt, tab)
    return o.reshape(b * t, GW), lse.reshape(b * t, LANES)


def _out_proj_kernel(x_ref, oc_ref, od_ref, ob_ref, of_ref,
                     d1_ref, d2_ref, d3_ref, l1_ref, l2_ref, l3_ref, w_ref, g_ref, o_ref, cat_ref):
    gw = GW
    a = oc_ref[...].astype(F32) + od_ref[...].astype(F32)
    cat_ref[:, 0:gw] = a.astype(BF16)
    cat_ref[:, gw:2 * gw] = ob_ref[...]
    cat_ref[:, 2 * gw:3 * gw] = of_ref[...]
    l1, l2, l3 = l1_ref[...], l2_ref[...], l3_ref[...]
    mx = jnp.maximum(jnp.maximum(l1, l2), l3)
    e1, e2, e3 = jnp.exp(l1 - mx), jnp.exp(l2 - mx), jnp.exp(l3 - mx)
    den = e1 + e2 + e3
    head = lax.broadcasted_iota(jnp.int32, (2 * LANES, gw), 0) & (LANES - 1)
    lane = lax.broadcasted_iota(jnp.int32, (2 * LANES, gw), 1)
    spread = jnp.where(head == lane // HEAD_DIM, 1.0, 0.0).astype(BF16)
    d = (_spread_lanes(e1 / den, spread) * d1_ref[...].astype(F32)
         + _spread_lanes(e2 / den, spread) * d2_ref[...].astype(F32)
         + _spread_lanes(e3 / den, spread) * d3_ref[...].astype(F32))
    cat_ref[:, 3 * gw:4 * gw] = d.astype(BF16)
    y = _dot(cat_ref[...], w_ref[...])
    y = y * lax.rsqrt(jnp.mean(y * y, axis=-1, keepdims=True) + EPS)
    o_ref[...] = x_ref[...] + y * g_ref[...]


def _out_proj(x2, heads, dils, lses, w_out, g, tm=OUT_TM):
    m = x2.shape[0]
    gw = GW
    row = lambda i: (i, 0)
    const = lambda i: (0, 0)
    in_specs = ([pl.BlockSpec((tm, D_MODEL), row)]
                + [pl.BlockSpec((tm, gw), row)] * 7 + [pl.BlockSpec((tm, LANES), row)] * 3
                + [pl.BlockSpec((D_MODEL, D_MODEL), const), pl.BlockSpec((1, D_MODEL), const)])
    return pl.pallas_call(
        _out_proj_kernel,
        out_shape=jax.ShapeDtypeStruct((m, D_MODEL), F32),
        grid=(m // tm,),
        in_specs=in_specs,
        out_specs=pl.BlockSpec((tm, D_MODEL), row),
        scratch_shapes=[pltpu.VMEM((tm, D_MODEL), BF16)],
        compiler_params=_cparams(("parallel",)),
        name="out_proj_residual",
    )(x2, *heads, *dils, *lses, w_out, g)


def _ffn_kernel(x_ref, gpre_ref, wu_ref, wd_ref, gpost_ref, o_ref, h_ref, acc_ref):
    f = pl.program_id(1)

    @pl.when(f == 0)
    def _():
        x = x_ref[...]
        y = x * lax.rsqrt(jnp.mean(x * x, axis=-1, keepdims=True) + EPS)
        h_ref[...] = (y * gpre_ref[...]).astype(BF16)
        acc_ref[...] = jnp.zeros_like(acc_ref)

    u = jnp.maximum(_dot(h_ref[...], wu_ref[...]), 0.0)
    acc_ref[...] += _dot((u * u).astype(BF16), wd_ref[...])

    @pl.when(f == pl.num_programs(1) - 1)
    def _():
        y = acc_ref[...]
        y = y * lax.rsqrt(jnp.mean(y * y, axis=-1, keepdims=True) + EPS)
        o_ref[...] = x_ref[...] + y * gpost_ref[...]


def _ffn(x2, g_pre, w_up, w_down, g_post, tm=FFN_TM, tf=FFN_TF):
    m = x2.shape[0]
    return pl.pallas_call(
        _ffn_kernel,
        out_shape=jax.ShapeDtypeStruct((m, D_MODEL), F32),
        grid=(m // tm, D_FF // tf),
        in_specs=[pl.BlockSpec((tm, D_MODEL), lambda i, f: (i, 0)),
                  pl.BlockSpec((1, D_MODEL), lambda i, f: (0, 0)),
                  pl.BlockSpec((D_MODEL, tf), lambda i, f: (0, f)),
                  pl.BlockSpec((tf, D_MODEL), lambda i, f: (f, 0)),
                  pl.BlockSpec((1, D_MODEL), lambda i, f: (0, 0))],
        out_specs=pl.BlockSpec((tm, D_MODEL), lambda i, f: (i, 0)),
        scratch_shapes=[pltpu.VMEM((tm, D_MODEL), BF16), pltpu.VMEM((tm, D_MODEL), F32)],
        compiler_params=_cparams(("parallel", "arbitrary")),
        name="ffn_residual",
    )(x2, g_pre, w_up, w_down, g_post)


def _split_w_in(w):
    gw, kw = GW, LANES
    o = 0
    cols = {}
    for name, width in (("qa", gw), ("kca", kw), ("vca", kw), ("ksa", kw), ("vsa", kw), ("kwa", kw),
                        ("vwa", kw), ("ga", 24), ("qb", gw), ("kb", gw), ("vb", gw), ("qc", gw),
                        ("kc", gw), ("vc", gw), ("fc", 8), ("qd", gw), ("kd", gw), ("vd", gw)):
        cols[name] = w[:, o:o + width] * (Q_FOLD if name[0] == "q" else 1.0)
        o += width
    main = jnp.concatenate([cols[n] for n in ("qa", "ksa", "vsa", "kwa", "vwa", "qb", "kb", "vb",
                                              "qc", "kc", "vc", "qd", "kd", "vd")], axis=1)
    pad = jnp.zeros((w.shape[0], LANES - 32), w.dtype)
    small = jnp.concatenate([cols["ga"], cols["fc"], pad, cols["kca"], cols["vca"]], axis=1)
    return main.astype(BF16), small.astype(BF16)


def _compress_weights(pe, w1, w2):
    half = NSA_CMP_LEN // 2
    hid = NSA_CMP_HIDDEN
    w1r = w1.reshape(2, half, HEAD_DIM, hid)
    z = jnp.zeros_like(w1r)
    per_group = jnp.stack([jnp.concatenate([w1r, z], axis=-1), jnp.concatenate([z, w1r], axis=-1)], axis=2)
    w_halves = per_group.reshape(2, half * 2 * HEAD_DIM, 2 * hid).astype(BF16)
    pe_r = jnp.broadcast_to(pe.reshape(2, half, 1, HEAD_DIM), (2, half, 2, HEAD_DIM)).reshape(2, half * LANES)
    z2 = jnp.zeros_like(w2)
    w2g = jnp.concatenate([jnp.concatenate([w2, z2], axis=1), jnp.concatenate([z2, w2], axis=1)], axis=0)
    return pe_r, w_halves[0], w_halves[1], w2g.astype(BF16)


def _bias_tables(rel_bias, t):
    hm = 8
    tq_a = min(NSA_TQ_CMP, t)
    tq_d = tk_d = min(NSA_TQ_DENSE, t)
    n_a = min(_n_toeplitz_tiles(tq_d, tk_d), t // tq_d)
    tab_a = _bias_tiles(rel_bias, _toeplitz_idx_t(n_a, tk_d, tq_d, tq_d), 0, 2, 4)
    ncp = t // NSA_CMP_STRIDE
    qi = jnp.arange(t // tq_a, dtype=jnp.int32)[:, None, None]
    c = jnp.arange(ncp, dtype=jnp.int32)[None, :, None]
    q = jnp.arange(tq_a, dtype=jnp.int32)[None, None, :]
    idx_c = _t5_bucket(qi * tq_a + q - (c * NSA_CMP_STRIDE + NSA_CMP_LEN - 1))
    tab_c = _bias_tiles(rel_bias, idx_c, 0, 2, 4)
    tq_b = min(MOBA_TQ, t)
    n_b = min(_n_toeplitz_tiles(tq_b, tq_b), t // tq_b)
    tab_b = _bias_tiles(rel_bias, _toeplitz_idx_t(n_b, tq_b, tq_b, tq_b), hm, 4, 2)
    kr = jnp.arange(2 * DIL_BAND, dtype=jnp.int32)[None, :, None]
    qc = jnp.arange(DIL_BAND, dtype=jnp.int32)[None, None, :]
    tabs_d = [_bias_tiles(rel_bias, _t5_bucket((DIL_BAND + qc - kr) * dil), 2 * hm, 4, 2)
              for _, dil in LONGNET_PATTERNS]
    return tab_a, tab_c, tab_b, tabs_d


def _overlap_t(t):
    ncp = t // NSA_CMP_STRIDE
    nsp = max(t // NSA_SEL_LEN, 8)
    c_start = jnp.arange(ncp)[None, :] * NSA_CMP_STRIDE
    s_start = jnp.arange(nsp)[:, None] * NSA_SEL_LEN
    ovl = (c_start < s_start + NSA_SEL_LEN) & (c_start + NSA_CMP_LEN - 1 >= s_start)
    ovl = ovl & (jnp.arange(ncp)[None, :] < ncp - 1) & (jnp.arange(nsp)[:, None] < t // NSA_SEL_LEN)
    return jnp.concatenate([ovl, ovl], axis=1).astype(BF16)


def _transposed(p3, pb, n_blocks):
    b, t, _ = p3.shape
    vt = p3[:, :, pb * LANES:(pb + n_blocks) * LANES].transpose(0, 2, 1).reshape(b, 2 * n_blocks, HEAD_DIM, t)
    ones = jnp.ones((b, 2 * n_blocks, VT_ROWS - HEAD_DIM, t), BF16)
    return jnp.concatenate([vt, ones], axis=2).reshape(b, 2 * n_blocks * VT_ROWS, t)


def _mixer_layer(x2, b, t, w_main, w_small, g_pre, g_post, w_out, cmp_w_k, cmp_w_v, fox_bias, tabs, ovl_t):
    tab_a, tab_c, tab_b, tabs_d = tabs
    m = b * t
    p2, pd2, small, kca, vca = _proj(x2, g_pre, w_main, w_small)
    p3 = p2.reshape(b, t, P_WIDTH)
    pd3 = pd2.reshape(b, t, PD_WIDTH)
    small3 = small.reshape(b, t, LANES)
    row_w = NSA_CMP_STRIDE * LANES
    kc = _nsa_compress(kca.reshape(b, t // NSA_CMP_STRIDE, row_w), *cmp_w_k)
    vc = _nsa_compress(vca.reshape(b, t // NSA_CMP_STRIDE, row_w), *cmp_w_v)
    o_cmp, negsel = _nsa_cmp(p3, kc, _transposed(vc, 0, 1), tab_c, ovl_t, small3)
    o_dense = _nsa_dense(p3, tab_a, small3, _transposed(p3, PB_VSA, 1), _transposed(p3, PB_VWA, 1), negsel)
    o_b = _moba(p3, tab_b, _transposed(p3, PB_VB, 4))
    fox_bias_row = jnp.zeros((1, LANES), F32).at[0, FOX_HI:FOX_HI + 8].set(fox_bias)
    o_f = _fox(p3, _fox_cumsum(small3, fox_bias_row), _transposed(p3, PB_VC, 4))
    dil = [_dilated(pd3, tab, d) for tab, (_, d) in zip(tabs_d, LONGNET_PATTERNS)]
    heads = [a.reshape(m, GW) for a in (o_cmp, o_dense, o_b, o_f)]
    return _out_proj(x2, heads, [d[0] for d in dil], [d[1] for d in dil], w_out, g_post)


def kernel(x, w_in, w_out, g_mix_pre, g_mix_post, g_mlp_pre, g_mlp_post, w_up, w_down, cmp_pe, phik_w1, phik_w2, phiv_w1, phiv_w2, fox_bias, rel_bias):
    b, t, d = x.shape
    depth = w_in.shape[0]
    tabs = _bias_tables(rel_bias, t)
    ovl_t = _overlap_t(t)
    x2 = x.reshape(b * t, d)
    for l in range(depth):
        w_main, w_small = _split_w_in(w_in[l])
        x2 = _mixer_layer(
            x2, b, t, w_main, w_small, g_mix_pre[l][None], g_mix_post[l][None], w_out[l].astype(BF16),
            _compress_weights(cmp_pe[l], phik_w1[l], phik_w2[l]),
            _compress_weights(cmp_pe[l], phiv_w1[l], phiv_w2[l]),
            fox_bias[l], tabs, ovl_t)
        x2 = _ffn(x2, g_mlp_pre[l][None], w_up[l].astype(BF16), w_down[l].astype(BF16), g_mlp_post[l][None])
    return x2.reshape(b, t, d)
```

```python
import functools
import math

import jax
import jax.numpy as jnp
from jax import lax
from jax.experimental import pallas as pl
from jax.experimental.pallas import tpu as pltpu

F32 = jnp.float32
BF16 = jnp.bfloat16

D_MODEL = 2048
D_FF = 4 * D_MODEL
HEAD_DIM = 64
LANES = 128
EPS = 1e-6
NEG = -1e30
BIG = 1e30
SCALE = HEAD_DIM ** -0.5
LOG2E = math.log2(math.e)
LN2 = math.log(2.0)
Q_FOLD = SCALE * LOG2E

N_BUCKETS = 32
MAX_EXACT = 16
MAX_DISTANCE = 4096
LAST_BUCKET_DIST = 2897

NSA_CMP_LEN = 32
NSA_CMP_STRIDE = 16
NSA_CMP_HIDDEN = 256
NSA_SEL_LEN = 64
NSA_TOP_N = 16
NSA_WINDOW = 512
MOBA_BLOCK = 256
MOBA_TOPK = 3
LONGNET_PATTERNS = ((128, 1), (512, 4), (2048, 16))
DIL_BAND = 128

FOX_TQ = 512
MOBA_TQ = 512
NSA_TQ_CMP = 512
NSA_TQ_DENSE = 256
PROJ_TM, PROJ_TN = 1024, 512
OUT_TM = 256
FFN_TM, FFN_TF = 512, 1024
CUMSUM_TC = 512

P_WIDTH = 4096
PB_QA, PB_KSA, PB_VSA, PB_KWA, PB_VWA = 0, 4, 5, 6, 7
PB_QB, PB_KB, PB_VB = 8, 12, 16
PB_QC, PB_KC, PB_VC = 20, 24, 28
PD_WIDTH = 1536
GW = 4 * LANES
BF16_SUBLANES = 16
VT_ROWS = HEAD_DIM + BF16_SUBLANES

FOX_HI, FOX_MID, FOX_LO = 24, 32, 40

V7X_VMEM_BYTES = 64 * 1024 * 1024
VMEM_LIMIT = V7X_VMEM_BYTES * 7 // 8


def _cparams(sem):
    return pltpu.CompilerParams(dimension_semantics=sem, vmem_limit_bytes=VMEM_LIMIT)


def _dot(a, b):
    return jnp.dot(a, b, preferred_element_type=F32)


def _dot_nt(a, b):
    return lax.dot_general(a, b, (((1,), (1,)), ((), ())), preferred_element_type=F32)


def _dot_hi(a, b):
    return jnp.dot(a, b, preferred_element_type=F32, precision=lax.Precision.HIGHEST)


def _spread_lanes(w, spread):
    hi = w.astype(BF16)
    lo = (w - hi.astype(F32)).astype(BF16)
    return _dot(jnp.concatenate([hi, lo], axis=1), spread)


def _t5_bucket(dist):
    dist = jnp.maximum(dist, 0)
    rel = jnp.log(jnp.maximum(dist, 1).astype(jnp.float32) / MAX_EXACT) / math.log(MAX_DISTANCE / MAX_EXACT)
    large = jnp.minimum(MAX_EXACT + (rel * (N_BUCKETS - MAX_EXACT)).astype(jnp.int32), N_BUCKETS - 1)
    return jnp.where(dist < MAX_EXACT, dist, large)


def _bias_tile_kernel(tab_ref, idx_ref, o_ref, *, head0, hpg, cols):
    grp = pl.program_id(0)
    idx = idx_ref[0]
    n_bits = N_BUCKETS.bit_length() - 1
    bit_set = [((idx >> b) & 1) == 1 for b in range(n_bits)]
    for hh in range(hpg):
        h = head0 + grp * hpg + hh
        level = [tab_ref[b, h] for b in range(N_BUCKETS)]
        for b in range(n_bits):
            level = [jnp.where(bit_set[b], level[2 * i + 1], level[2 * i]) for i in range(len(level) // 2)]
        o_ref[0, 0, :, hh * cols:(hh + 1) * cols] = level[0] * LOG2E


def _bias_tiles(rel_bias, idx, head0, n_groups, hpg=1):
    n, r, c = idx.shape
    return pl.pallas_call(
        functools.partial(_bias_tile_kernel, head0=head0, hpg=hpg, cols=c),
        out_shape=jax.ShapeDtypeStruct((n_groups, n, r, hpg * c), F32),
        grid=(n_groups, n),
        in_specs=[pl.BlockSpec(memory_space=pltpu.SMEM),
                  pl.BlockSpec((1, r, c), lambda g, i: (i, 0, 0))],
        out_specs=pl.BlockSpec((1, 1, r, hpg * c), lambda g, i: (g, i, 0, 0)),
        compiler_params=_cparams(("parallel", "parallel")),
        name="bias_tiles",
    )(rel_bias, idx)


def _toeplitz_idx_t(n_tiles, krows, qcols, q_stride):
    nd = jnp.arange(n_tiles, dtype=jnp.int32)[:, None, None]
    r = jnp.arange(krows, dtype=jnp.int32)[None, :, None]
    c = jnp.arange(qcols, dtype=jnp.int32)[None, None, :]
    return _t5_bucket(nd * q_stride + c - r)


def _n_toeplitz_tiles(q_stride, keys):
    return -(-(LAST_BUCKET_DIST + keys - 1) // q_stride) + 1


def _proj_kernel(x_ref, g_ref, w_ref, ws_ref, p_ref, pd_ref, s_ref, kc_ref, vc_ref, h_ref, *, n_main):
    j = pl.program_id(1)

    @pl.when(j == 0)
    def _():
        x = x_ref[...]
        y = x * lax.rsqrt(jnp.mean(x * x, axis=-1, keepdims=True) + EPS)
        hb = (y * g_ref[...]).astype(BF16)
        h_ref[...] = hb
        small = _dot(hb, ws_ref[...])
        s_ref[...] = small[:, 0:LANES]
        kc_ref[...] = small[:, LANES:2 * LANES]
        vc_ref[...] = small[:, 2 * LANES:3 * LANES]

    y = _dot(h_ref[...], w_ref[...]).astype(BF16)

    @pl.when(j < n_main)
    def _():
        p_ref[...] = y

    @pl.when(j >= n_main)
    def _():
        pd_ref[...] = y


def _proj(x2, g, w_main, w_small, tm=PROJ_TM, tn=PROJ_TN):
    m = x2.shape[0]
    tm = min(tm, m)
    n_main = P_WIDTH // tn
    return pl.pallas_call(
        functools.partial(_proj_kernel, n_main=n_main),
        out_shape=(jax.ShapeDtypeStruct((m, P_WIDTH), BF16),
                   jax.ShapeDtypeStruct((m, PD_WIDTH), BF16),
                   jax.ShapeDtypeStruct((m, LANES), F32),
                   jax.ShapeDtypeStruct((m, LANES), F32),
                   jax.ShapeDtypeStruct((m, LANES), F32)),
        grid=(m // tm, (P_WIDTH + PD_WIDTH) // tn),
        in_specs=[pl.BlockSpec((tm, D_MODEL), lambda i, j: (i, 0)),
                  pl.BlockSpec((1, D_MODEL), lambda i, j: (0, 0)),
                  pl.BlockSpec((D_MODEL, tn), lambda i, j: (0, j)),
                  pl.BlockSpec((D_MODEL, 3 * LANES), lambda i, j: (0, 0))],
        out_specs=(pl.BlockSpec((tm, tn), lambda i, j: (i, jnp.minimum(j, n_main - 1))),
                   pl.BlockSpec((tm, tn), lambda i, j: (i, jnp.maximum(j - n_main, 0))),
                   pl.BlockSpec((tm, LANES), lambda i, j: (i, 0)),
                   pl.BlockSpec((tm, LANES), lambda i, j: (i, 0)),
                   pl.BlockSpec((tm, LANES), lambda i, j: (i, 0))),
        scratch_shapes=[pltpu.VMEM((tm, D_MODEL), BF16)],
        compiler_params=_cparams(("parallel", "arbitrary")),
        name="rms_in_proj",
    )(x2, g, w_main, w_small)


def _lane_lo(shape):
    return lax.broadcasted_iota(jnp.int32, shape, 1) < HEAD_DIM


def _pair_queries(q_ref):
    q = q_ref[...]
    lo = _lane_lo(q.shape)
    zero = jnp.zeros_like(q)
    return jnp.where(lo, q, zero), jnp.where(lo, zero, q)


def _col_softmax(s, m_ref, first):
    if first:
        m = jnp.max(s, axis=0, keepdims=True)
        m_ref[...] = m
        return jnp.exp2(s - m).astype(BF16), None
    m_prev = m_ref[...]
    m_new = jnp.maximum(m_prev, jnp.max(s, axis=0, keepdims=True))
    m_ref[...] = m_new
    return jnp.exp2(s - m_new).astype(BF16), jnp.exp2(m_prev - m_new)


def _accumulate(acc_ref, cols, pv, alpha):
    if alpha is None:
        acc_ref[:, cols] = pv
    else:
        acc_ref[:, cols] = alpha[:, cols] * acc_ref[:, cols] + pv


def _key_tiles(j_diag, j_first, n_past, qk, consume, sa_ref, sb_ref):
    last = j_first + jnp.maximum(n_past - 1, 0)
    sa_ref[...] = qk(j_first)
    consume(j_diag, qk(j_diag), True)

    def run(base, count):
        for t in range(count):
            cur, nxt = (sa_ref, sb_ref) if t % 2 == 0 else (sb_ref, sa_ref)
            nxt[...] = qk(jnp.minimum(base + t + 1, last))
            consume(base + t, cur[...], False)

    quads = n_past // 4
    lax.fori_loop(0, quads, lambda i, c: (run(j_first + 4 * i, 4), c)[1], 0)
    rest = n_past - 4 * quads

    @pl.when(rest >= 2)
    def _():
        run(j_first + 4 * quads, 2)

    @pl.when(rest % 2 == 1)
    def _():
        consume(j_first + n_past - 1, sa_ref[...], False)


def _top_rows(v, k):
    n = v.shape[0]
    rows = lax.broadcasted_iota(jnp.int32, v.shape, 0)
    chosen = jnp.zeros(v.shape, F32)
    for _ in range(k):
        top = jnp.max(v, axis=0, keepdims=True)
        first = jnp.min(jnp.where(v == top, rows, n), axis=0, keepdims=True)
        pick = rows == first
        chosen = jnp.where(pick, 1.0, chosen)
        v = jnp.where(pick, -jnp.inf, v)
    return chosen


def _top_rows_bisect(v, k):
    n = v.shape[0]
    key = pltpu.bitcast(v, jnp.int32)
    thr = jnp.zeros((1, v.shape[1]), jnp.int32)
    def enough(cand):
        return jnp.sum(jnp.where(key >= cand, 1.0, 0.0), axis=0, keepdims=True) >= k

    for bit in range(30, 0, -2):
        c_hi, c_lo = thr | (1 << bit), thr | (1 << (bit - 1))
        c_both = c_hi | (1 << (bit - 1))
        thr = jnp.where(enough(c_both), c_both, jnp.where(enough(c_hi), c_hi,
                                                          jnp.where(enough(c_lo), c_lo, thr)))
    thr = jnp.where(enough(thr | 1), thr | 1, thr)
    above = key > thr
    tie = key == thr
    n_above = jnp.sum(jnp.where(above, 1.0, 0.0), axis=0, keepdims=True)
    ri = lax.broadcasted_iota(jnp.int32, (n, n), 0)
    ci = lax.broadcasted_iota(jnp.int32, (n, n), 1)
    before = jnp.where(ci < ri, 1.0, 0.0).astype(BF16)
    ties_before = _dot(before, jnp.where(tie, 1.0, 0.0).astype(BF16))
    return above | (tie & (ties_before < k - n_above))


def _pair_finish(acc_ref, o_ref, tq):
    o_t = acc_ref[0:HEAD_DIM, :] / acc_ref[HEAD_DIM:HEAD_DIM + 1, :]
    o_ref[0] = jnp.concatenate([o_t[:, 0:tq], o_t[:, tq:2 * tq]], axis=0).T.astype(BF16)


def _fox_cumsum_kernel(s_ref, fb_ref, c_ref, carry_ref, *, tc):
    @pl.when(pl.program_id(1) == 0)
    def _():
        carry_ref[...] = jnp.zeros_like(carry_ref)

    lane = lax.broadcasted_iota(jnp.int32, (tc, LANES), 1)
    z = s_ref[0] + fb_ref[...]
    logf = -(jnp.maximum(-z, 0.0) + jnp.log(1.0 + jnp.exp(-jnp.abs(z))))
    logf = jnp.where((lane >= FOX_HI) & (lane < FOX_MID), logf, 0.0)
    ri = lax.broadcasted_iota(jnp.int32, (tc, tc), 0)
    ci = lax.broadcasted_iota(jnp.int32, (tc, tc), 1)
    lower = jnp.where(ri >= ci, 1.0, 0.0).astype(F32)
    c = _dot_hi(lower, logf) + carry_ref[...]
    carry_ref[...] = c[tc - 1:tc, :]
    c2 = c * LOG2E
    hi = c2.astype(BF16).astype(F32)
    rest = c2 - hi
    mid = rest.astype(BF16).astype(F32)
    lo = rest - mid
    pieces = hi + pltpu.roll(mid, FOX_MID - FOX_HI, axis=1) + pltpu.roll(lo, FOX_LO - FOX_HI, axis=1)
    c_ref[0] = pieces.astype(BF16)


def _fox_cumsum(small3, fox_bias_row, tc=CUMSUM_TC):
    b, t, _ = small3.shape
    tc = min(tc, t)
    return pl.pallas_call(
        functools.partial(_fox_cumsum_kernel, tc=tc),
        out_shape=jax.ShapeDtypeStruct((b, t, LANES), BF16),
        grid=(b, t // tc),
        in_specs=[pl.BlockSpec((1, tc, LANES), lambda i, j: (i, j, 0)),
                  pl.BlockSpec((1, LANES), lambda i, j: (0, 0))],
        out_specs=pl.BlockSpec((1, tc, LANES), lambda i, j: (i, j, 0)),
        scratch_shapes=[pltpu.VMEM((1, LANES), F32)],
        compiler_params=_cparams(("parallel", "arbitrary")),
        name="fox_cumsum",
    )(small3, fox_bias_row)


def _fox_kernel(q_ref, k_ref, c_ref, vt_ref, o_ref, kaug_ref, sa_ref, sb_ref, m_ref, acc_ref, *, tq):
    hp = pl.program_id(1)
    qi = pl.program_id(2)

    @pl.when(qi == 0)
    def _():
        kaug_ref[:, 0:LANES] = k_ref[0]
        kaug_ref[:, LANES:2 * LANES] = c_ref[0]

    qs = _pair_queries(q_ref.at[0])
    lane = lax.broadcasted_iota(jnp.int32, (tq, LANES), 1)
    rows = []
    for hh in range(2):
        h = 2 * hp + hh
        pick = (lane == FOX_HI + h) | (lane == FOX_MID + h) | (lane == FOX_LO + h)
        rows.append(jnp.concatenate([qs[hh], jnp.where(pick, -1.0, 0.0).astype(BF16)], axis=1))
    q_aug = jnp.concatenate(rows, axis=0)
    kr = lax.broadcasted_iota(jnp.int32, (tq, 2 * tq), 0)
    qc = lax.broadcasted_iota(jnp.int32, (tq, 2 * tq), 1) & (tq - 1)

    def qk(j):
        return _dot_nt(kaug_ref[pl.ds(pl.multiple_of(j * tq, tq), tq), :], q_aug)

    def consume(j, s, first):
        k0 = pl.multiple_of(j * tq, tq)
        if first:
            s = jnp.where(kr <= qc, s, NEG)
        p, alpha = _col_softmax(s, m_ref, first)
        for hh in range(2):
            cols = slice(hh * tq, (hh + 1) * tq)
            vt = vt_ref[0, hh * VT_ROWS:(hh + 1) * VT_ROWS, pl.ds(k0, tq)]
            _accumulate(acc_ref, cols, _dot(vt, p[:, cols]), alpha)

    _key_tiles(qi, 0, qi, qk, consume, sa_ref, sb_ref)
    _pair_finish(acc_ref, o_ref, tq)


def _fox(p3, c_aug, vt, tq=FOX_TQ):
    b, t, _ = p3.shape
    tq = min(tq, t)
    return pl.pallas_call(
        functools.partial(_fox_kernel, tq=tq),
        out_shape=jax.ShapeDtypeStruct((b, t, GW), BF16),
        grid=(b, 4, t // tq),
        in_specs=[pl.BlockSpec((1, tq, LANES), lambda i, h, q: (i, q, PB_QC + h)),
                  pl.BlockSpec((1, t, LANES), lambda i, h, q: (i, 0, PB_KC + h)),
                  pl.BlockSpec((1, t, LANES), lambda i, h, q: (i, 0, 0)),
                  pl.BlockSpec((1, 2 * VT_ROWS, t), lambda i, h, q: (i, h, 0))],
        out_specs=pl.BlockSpec((1, tq, LANES), lambda i, h, q: (i, q, h)),
        scratch_shapes=[pltpu.VMEM((t, 2 * LANES), BF16),
                        pltpu.VMEM((tq, 2 * tq), F32), pltpu.VMEM((tq, 2 * tq), F32),
                        pltpu.VMEM((1, 2 * tq), F32), pltpu.VMEM((VT_ROWS, 2 * tq), F32)],
        compiler_params=_cparams(("parallel", "parallel", "arbitrary")),
        name="fox_attention",
    )(p3, p3, c_aug, vt)


def _moba_kernel(q_ref, k_ref, vt_ref, tab_ref, o_ref, km_ref, kaug_ref, sa_ref, sb_ref, m_ref, acc_ref,
                 *, nkb, nkp, n_tab, tq):
    blk = MOBA_BLOCK
    qi = pl.program_id(2)

    @pl.when(qi == 0)
    def _():
        km_ref[...] = jnp.zeros_like(km_ref)
        lane = lax.broadcasted_iota(jnp.int32, (blk, LANES), 1)
        for n in range(nkb):
            kb = k_ref[0, n * blk:(n + 1) * blk, :]
            km_ref[n:n + 1, :] = jnp.mean(kb.astype(F32), axis=0, keepdims=True)
            kaug_ref[n * blk:(n + 1) * blk, 0:LANES] = kb
            kaug_ref[n * blk:(n + 1) * blk, LANES:2 * LANES] = jnp.where(lane == n, 1.0, 0.0).astype(BF16)

    qs = _pair_queries(q_ref.at[0])
    q_both = jnp.concatenate([qs[0], qs[1]], axis=0)
    km = km_ref[...]
    km_hi = km.astype(BF16)
    km_lo = (km - km_hi.astype(F32)).astype(BF16)
    gate = _dot_nt(jnp.concatenate([km_hi, km_lo], axis=0), q_both)
    gate = gate[0:nkp] + gate[nkp:2 * nkp]
    n = lax.broadcasted_iota(jnp.int32, gate.shape, 0)
    q_pos = lax.broadcasted_iota(jnp.int32, gate.shape, 1) & (tq - 1)
    own = qi * (tq // blk) + q_pos // blk
    past = n < own
    gate = jnp.where(past, gate, NEG)
    allowed = ((_top_rows(gate, MOBA_TOPK) > 0.5) & past) | (n == own)
    negsel = jnp.where(allowed, 0.0, NEG).astype(F32)
    if nkp < LANES:
        negsel = jnp.concatenate([negsel, jnp.zeros((LANES - nkp, 2 * tq), F32)], axis=0)
    q_aug = jnp.concatenate([q_both, negsel.T.astype(BF16)], axis=1)

    kr = lax.broadcasted_iota(jnp.int32, (tq, 2 * tq), 0)
    qc = lax.broadcasted_iota(jnp.int32, (tq, 2 * tq), 1) & (tq - 1)

    def qk(j):
        return _dot_nt(kaug_ref[pl.ds(pl.multiple_of(j * tq, tq), tq), :], q_aug)

    def consume(j, s, first):
        k0 = pl.multiple_of(j * tq, tq)
        s = s + tab_ref[0, jnp.minimum(qi - j, n_tab - 1)]
        if first:
            s = jnp.where(kr <= qc, s, NEG)
        p, alpha = _col_softmax(s, m_ref, first)
        for hh in range(2):
            cols = slice(hh * tq, (hh + 1) * tq)
            vt = vt_ref[0, hh * VT_ROWS:(hh + 1) * VT_ROWS, pl.ds(k0, tq)]
            _accumulate(acc_ref, cols, _dot(vt, p[:, cols]), alpha)

    _key_tiles(qi, 0, qi, qk, consume, sa_ref, sb_ref)
    _pair_finish(acc_ref, o_ref, tq)


def _moba(p3, tab, vt):
    b, t, _ = p3.shape
    tq = tab.shape[2]
    nkb = t // MOBA_BLOCK
    nkp = -(-nkb // 8) * 8
    n_tab = tab.shape[1]
    return pl.pallas_call(
        functools.partial(_moba_kernel, nkb=nkb, nkp=nkp, n_tab=n_tab, tq=tq),
        out_shape=jax.ShapeDtypeStruct((b, t, GW), BF16),
        grid=(b, 4, t // tq),
        in_specs=[pl.BlockSpec((1, tq, LANES), lambda i, h, q: (i, q, PB_QB + h)),
                  pl.BlockSpec((1, t, LANES), lambda i, h, q: (i, 0, PB_KB + h)),
                  pl.BlockSpec((1, 2 * VT_ROWS, t), lambda i, h, q: (i, h, 0)),
                  pl.BlockSpec((1, n_tab, tq, 2 * tq), lambda i, h, q: (h, 0, 0, 0),
                               pipeline_mode=pl.Buffered(1))],
        out_specs=pl.BlockSpec((1, tq, LANES), lambda i, h, q: (i, q, h)),
        scratch_shapes=[pltpu.VMEM((nkp, LANES), F32), pltpu.VMEM((t, 2 * LANES), BF16),
                        pltpu.VMEM((tq, 2 * tq), F32), pltpu.VMEM((tq, 2 * tq), F32),
                        pltpu.VMEM((1, 2 * tq), F32), pltpu.VMEM((VT_ROWS, 2 * tq), F32)],
        compiler_params=_cparams(("parallel", "parallel", "arbitrary")),
        name="moba_attention",
    )(p3, p3, vt, tab)


def _gelu_tanh(x):
    return 0.5 * x * (1.0 + jnp.tanh(math.sqrt(2.0 / math.pi) * (x + 0.044715 * (x * x * x))))


def _nsa_compress_kernel(r_ref, pe_ref, wlo_ref, whi_ref, w2_ref, o_ref, *, nr):
    r = r_ref[0]
    pe = pe_ref[...]
    a = _dot((r + pe[0:1, :]).astype(BF16), wlo_ref[...])
    b = _dot((r + pe[1:2, :]).astype(BF16), whi_ref[...])
    hid = _gelu_tanh(a + pltpu.roll(b, nr - 1, axis=0))
    o_ref[0] = _dot(hid.astype(BF16), w2_ref[...]).astype(BF16)


def _nsa_compress(r3, pe2, wlo, whi, w2):
    b, nr, w = r3.shape
    return pl.pallas_call(
        functools.partial(_nsa_compress_kernel, nr=nr),
        out_shape=jax.ShapeDtypeStruct((b, nr, LANES), BF16),
        grid=(b,),
        in_specs=[pl.BlockSpec((1, nr, w), lambda i: (i, 0, 0)),
                  pl.BlockSpec((2, w), lambda i: (0, 0)),
                  pl.BlockSpec((w, 2 * NSA_CMP_HIDDEN), lambda i: (0, 0)),
                  pl.BlockSpec((w, 2 * NSA_CMP_HIDDEN), lambda i: (0, 0)),
                  pl.BlockSpec((2 * NSA_CMP_HIDDEN, LANES), lambda i: (0, 0))],
        out_specs=pl.BlockSpec((1, nr, LANES), lambda i: (i, 0, 0)),
        compiler_params=_cparams(("parallel",)),
        name="nsa_compress",
    )(r3, pe2, wlo, whi, w2)


def _group_queries(q_ref, g):
    half = lax.broadcasted_iota(jnp.int32, (q_ref.shape[0], LANES), 1) // HEAD_DIM
    lo = half == 0
    in_group = half == g
    out = []
    for pb in range(2):
        q = q_ref[:, pb * LANES:(pb + 1) * LANES].astype(F32)
        qr = pltpu.roll(q, HEAD_DIM, axis=1)
        for dup in (jnp.where(lo, q, qr), jnp.where(lo, qr, q)):
            out.append(jnp.where(in_group, dup, 0.0).astype(BF16))
    return jnp.concatenate(out, axis=0)


def _group_gates(s_ref, g, branch):
    sig = 1.0 / (1.0 + jnp.exp(-s_ref[...]))
    col = lax.broadcasted_iota(jnp.int32, (2 * LANES, 2 * LANES), 0) & (LANES - 1)
    lane = lax.broadcasted_iota(jnp.int32, (2 * LANES, 2 * LANES), 1)
    spread = jnp.where(col == branch * 8 + g * 4 + lane // HEAD_DIM, 1.0, 0.0).astype(BF16)
    return _spread_lanes(sig, spread)


def _nsa_cmp_kernel(q_ref, kc_ref, vct_ref, tab_ref, ovl_ref, s_ref, o_ref, sel_ref, *, tq, ncp, nsp):
    g = pl.program_id(1)
    qi = pl.program_id(2)
    qs = _group_queries(q_ref.at[0], g)
    s = _dot_nt(kc_ref[0], qs) + tab_ref[0, 0]
    c = lax.broadcasted_iota(jnp.int32, (ncp, 4 * tq), 0)
    t = qi * tq + (lax.broadcasted_iota(jnp.int32, (ncp, 4 * tq), 1) & (tq - 1))
    ok = (t - (c * NSA_CMP_STRIDE + NSA_CMP_LEN - 1) >= 0) & (c < ncp - 1)
    s = jnp.where(ok, s, NEG)
    e = jnp.exp2(s - jnp.max(s, axis=0, keepdims=True))
    p = jnp.where(ok, e / jnp.sum(e, axis=0, keepdims=True), 0.0)
    vt = vct_ref[0, pl.ds(pl.multiple_of(g * VT_ROWS, VT_ROWS), VT_ROWS), :]
    o_t = _dot(vt, p.astype(BF16))[0:HEAD_DIM, :]
    o = jnp.concatenate([o_t[:, h * tq:(h + 1) * tq] for h in range(4)], axis=0).T
    o_ref[0] = (o * _group_gates(s_ref.at[0], g, 0)).astype(BF16)

    psum = p[:, 0:tq] + p[:, tq:2 * tq] + p[:, 2 * tq:3 * tq] + p[:, 3 * tq:4 * tq]
    hi = psum.astype(BF16)
    lo = (psum - hi.astype(F32)).astype(BF16)
    imp = _dot(ovl_ref[...], jnp.concatenate([hi, lo], axis=0))
    n = lax.broadcasted_iota(jnp.int32, (nsp, tq), 0)
    cur = (qi * tq + lax.broadcasted_iota(jnp.int32, (nsp, tq), 1)) // NSA_SEL_LEN
    forced = (n == 0) | (n == cur) | (n == cur - 1)
    valid = n <= cur
    imp = jnp.where(forced, BIG, jnp.where(valid, jnp.abs(imp), NEG))
    chosen = _top_rows_bisect(imp, NSA_TOP_N) & valid
    sel_ref[0] = jnp.where(chosen, 0.0, NEG).astype(F32).T.astype(BF16)


def _nsa_cmp(p3, kc, vct, tab, ovl_t, small3, tq=NSA_TQ_CMP):
    b, t, _ = p3.shape
    ncp = kc.shape[1]
    nsp = ovl_t.shape[0]
    tq = min(tq, t)
    return pl.pallas_call(
        functools.partial(_nsa_cmp_kernel, tq=tq, ncp=ncp, nsp=nsp),
        out_shape=(jax.ShapeDtypeStruct((b, t, GW), BF16),
                   jax.ShapeDtypeStruct((b, t, 2 * nsp), BF16)),
        grid=(b, 2, t // tq),
        in_specs=[pl.BlockSpec((1, tq, 2 * LANES), lambda i, g, q: (i, q, g)),
                  pl.BlockSpec((1, ncp, LANES), lambda i, g, q: (i, 0, 0)),
                  pl.BlockSpec((1, 2 * VT_ROWS, ncp), lambda i, g, q: (i, 0, 0)),
                  pl.BlockSpec((1, 1, ncp, 4 * tq), lambda i, g, q: (g, q, 0, 0)),
                  pl.BlockSpec((nsp, 2 * ncp), lambda i, g, q: (0, 0)),
                  pl.BlockSpec((1, tq, LANES), lambda i, g, q: (i, q, 0))],
        out_specs=(pl.BlockSpec((1, tq, 2 * LANES), lambda i, g, q: (i, q, g)),
                   pl.BlockSpec((1, tq, nsp), lambda i, g, q: (i, q, g))),
        compiler_params=_cparams(("parallel", "parallel", "parallel")),
        name="nsa_compressed",
    )(p3, kc, vct, tab, ovl_t, small3)


def _nsa_dense_kernel(q_ref, ks_ref, vts_ref, kw_ref, vtw_ref, tab_ref, blk_ref, sel_ref, s_ref, o_ref,
                      kaug_ref, sa_ref, sb_ref, m_ref, acc_ref, *, tq, tk, n_tab):
    g = pl.program_id(1)
    qi = pl.program_id(2)
    qs = _group_queries(q_ref.at[0], g)
    kr = lax.broadcasted_iota(jnp.int32, (tk, 4 * tq), 0)
    qc = lax.broadcasted_iota(jnp.int32, (tk, 4 * tq), 1) & (tq - 1)
    j0 = (qi * tq) // tk
    g0 = pl.multiple_of(g * VT_ROWS, VT_ROWS)

    @pl.when(qi == 0)
    def _():
        kaug_ref[:, 0:LANES] = ks_ref[0]
        kaug_ref[:, LANES:] = blk_ref[...]

    qs_sel = jnp.concatenate([qs, jnp.concatenate([sel_ref[0]] * 4, axis=0)], axis=1)

    def branch(selected):
        k_tile = ((lambda k0: kaug_ref[pl.ds(k0, tk), :]) if selected
                  else (lambda k0: kw_ref[0, pl.ds(k0, tk), :]))
        q_all = qs_sel if selected else qs
        vt_ref = vts_ref if selected else vtw_ref

        def qk(j):
            return _dot_nt(k_tile(pl.multiple_of(j * tk, tk)), q_all)

        def consume(j, s, first):
            k0 = pl.multiple_of(j * tk, tk)
            s = s + tab_ref[0, jnp.minimum(qi - j * (tk // tq), n_tab - 1)]
            dist = (qi * tq - j * tk) + qc - kr
            if not selected:
                s = jnp.where((dist >= 0) & (dist <= NSA_WINDOW - 1), s, NEG)
            elif first:
                s = jnp.where(dist >= 0, s, NEG)
            p, alpha = _col_softmax(s, m_ref, first)
            vt = vt_ref[0, pl.ds(g0, VT_ROWS), pl.ds(k0, tk)]
            _accumulate(acc_ref, slice(None), _dot(vt, p), alpha)

        first_j = 0 if selected else jnp.maximum(j0 - (-(-(NSA_WINDOW - 1) // tk)), 0)
        _key_tiles(j0, first_j, j0 - first_j, qk, consume, sa_ref, sb_ref)
        o_t = acc_ref[0:HEAD_DIM, :] / acc_ref[HEAD_DIM:HEAD_DIM + 1, :]
        o = jnp.concatenate([o_t[:, h * tq:(h + 1) * tq] for h in range(4)], axis=0).T
        return o * _group_gates(s_ref.at[0], g, 1 if selected else 2)

    o_ref[0] = (branch(True) + branch(False)).astype(BF16)


def _nsa_dense(p3, tab, small3, vt_sel, vt_win, negsel):
    b, t, _ = p3.shape
    tq = tk = min(NSA_TQ_DENSE, t)
    n_tab = tab.shape[1]
    nsp = negsel.shape[2] // 2
    block_of_key = (jnp.arange(t)[:, None] // NSA_SEL_LEN == jnp.arange(nsp)[None, :]).astype(BF16)
    keys = lambda pb: pl.BlockSpec((1, t, LANES), lambda i, g, q: (i, 0, pb))
    values = pl.BlockSpec((1, 2 * VT_ROWS, t), lambda i, g, q: (i, 0, 0))
    return pl.pallas_call(
        functools.partial(_nsa_dense_kernel, tq=tq, tk=tk, n_tab=n_tab),
        out_shape=jax.ShapeDtypeStruct((b, t, GW), BF16),
        grid=(b, 2, t // tq),
        in_specs=[pl.BlockSpec((1, tq, 2 * LANES), lambda i, g, q: (i, q, g)),
                  keys(PB_KSA), values, keys(PB_KWA), values,
                  pl.BlockSpec((1, n_tab, tk, 4 * tq), lambda i, g, q: (g, 0, 0, 0),
                               pipeline_mode=pl.Buffered(1)),
                  pl.BlockSpec((t, nsp), lambda i, g, q: (0, 0)),
                  pl.BlockSpec((1, tq, nsp), lambda i, g, q: (i, q, g)),
                  pl.BlockSpec((1, tq, LANES), lambda i, g, q: (i, q, 0))],
        out_specs=pl.BlockSpec((1, tq, 2 * LANES), lambda i, g, q: (i, q, g)),
        scratch_shapes=[pltpu.VMEM((t, LANES + nsp), BF16),
                        pltpu.VMEM((tk, 4 * tq), F32), pltpu.VMEM((tk, 4 * tq), F32),
                        pltpu.VMEM((1, 4 * tq), F32), pltpu.VMEM((VT_ROWS, 4 * tq), F32)],
        compiler_params=_cparams(("parallel", "parallel", "arbitrary")),
        name="nsa_selected_window",
    )(p3, p3, vt_sel, p3, vt_win, tab, block_of_key, negsel, small3)


def _dilated_kernel(q_ref, kp_ref, kc_ref, vtp_ref, vtc_ref, tab_ref, o_ref, lse_ref, *, tq):
    qi = pl.program_id(2)
    kr = lax.broadcasted_iota(jnp.int32, (2 * tq, 2 * tq), 0)
    qc = lax.broadcasted_iota(jnp.int32, (2 * tq, 2 * tq), 1) & (tq - 1)
    dist = tq + qc - kr
    valid = (dist >= 0) & (dist <= tq) & (kr >= jnp.where(qi > 0, 0, tq))
    outs, lses = [], []
    for pb in range(4):
        cols = slice(pb * LANES, (pb + 1) * LANES)
        qs = _pair_queries(q_ref.at[0, :, cols])
        q_both = jnp.concatenate([qs[0], qs[1]], axis=0)
        k2 = jnp.concatenate([kp_ref[0, :, cols], kc_ref[0, :, cols]], axis=0)
        s = jnp.where(valid, _dot_nt(k2, q_both) + tab_ref[pb, 0], NEG)
        m = jnp.max(s, axis=0, keepdims=True)
        p = jnp.exp2(s - m)
        l = jnp.sum(p, axis=0, keepdims=True)
        p = p.astype(BF16)
        lse = (m + jnp.log2(l)) * LN2
        for hh in range(2):
            rows = slice((2 * pb + hh) * HEAD_DIM, (2 * pb + hh + 1) * HEAD_DIM)
            qcols = slice(hh * tq, (hh + 1) * tq)
            vt = jnp.concatenate([vtp_ref[0, 0, rows, :], vtc_ref[0, 0, rows, :]], axis=1)
            outs.append(_dot(vt, p[:, qcols]) / l[:, qcols])
            lses.append(lse[:, qcols])
    o_ref[0] = jnp.concatenate(outs, axis=0).T.astype(BF16)
    lses.append(jnp.zeros((LANES - len(lses), tq), F32))
    lse_ref[0] = jnp.concatenate(lses, axis=0).T


def _dilated(pd3, tab, dil, tq=DIL_BAND):
    b, t, _ = pd3.shape
    l = t // dil
    qk = pd3[:, :, 0:2 * GW].reshape(b, l, dil * 2 * GW)
    vt = pd3[:, :, 2 * GW:3 * GW].reshape(b, l, dil, GW).transpose(0, 2, 3, 1)

    def prev(q):
        return jnp.maximum(q - 1, 0)

    out_spec = pl.BlockSpec((1, tq, GW), lambda i, r, q: (i, q, r))
    o, lse = pl.pallas_call(
        functools.partial(_dilated_kernel, tq=tq),
        out_shape=(jax.ShapeDtypeStruct((b, l, dil * GW), BF16),
                   jax.ShapeDtypeStruct((b, l, dil * LANES), F32)),
        grid=(b, dil, l // tq),
        in_specs=[pl.BlockSpec((1, tq, GW), lambda i, r, q: (i, q, 2 * r)),
                  pl.BlockSpec((1, tq, GW), lambda i, r, q: (i, prev(q), 2 * r + 1)),
                  pl.BlockSpec((1, tq, GW), lambda i, r, q: (i, q, 2 * r + 1)),
                  pl.BlockSpec((1, 1, GW, tq), lambda i, r, q: (i, r, 0, prev(q))),
                  pl.BlockSpec((1, 1, GW, tq), lambda i, r, q: (i, r, 0, q)),
                  pl.BlockSpec((4, 1, 2 * tq, 2 * tq), lambda i, r, q: (0, 0, 0, 0))],
        out_specs=(out_spec, pl.BlockSpec((1, tq, LANES), lambda i, r, q: (i, q, r))),
        compiler_params=_cparams(("parallel", "parallel", "parallel")),
        name="dilated_attention_%d" % dil,
    )(qk, qk, qk, vt, vt, tab)
    return o.reshape(b * t, GW), lse.reshape(b * t, LANES)


def _out_proj_kernel(x_ref, oc_ref, od_ref, ob_ref, of_ref,
                     d1_ref, d2_ref, d3_ref, l1_ref, l2_ref, l3_ref, w_ref, g_ref, o_ref, cat_ref):
    gw = GW
    a = oc_ref[...].astype(F32) + od_ref[...].astype(F32)
    cat_ref[:, 0:gw] = a.astype(BF16)
    cat_ref[:, gw:2 * gw] = ob_ref[...]
    cat_ref[:, 2 * gw:3 * gw] = of_ref[...]
    l1, l2, l3 = l1_ref[...], l2_ref[...], l3_ref[...]
    mx = jnp.maximum(jnp.maximum(l1, l2), l3)
    e1, e2, e3 = jnp.exp(l1 - mx), jnp.exp(l2 - mx), jnp.exp(l3 - mx)
    den = e1 + e2 + e3
    head = lax.broadcasted_iota(jnp.int32, (2 * LANES, gw), 0) & (LANES - 1)
    lane = lax.broadcasted_iota(jnp.int32, (2 * LANES, gw), 1)
    spread = jnp.where(head == lane // HEAD_DIM, 1.0, 0.0).astype(BF16)
    d = (_spread_lanes(e1 / den, spread) * d1_ref[...].astype(F32)
         + _spread_lanes(e2 / den, spread) * d2_ref[...].astype(F32)
         + _spread_lanes(e3 / den, spread) * d3_ref[...].astype(F32))
    cat_ref[:, 3 * gw:4 * gw] = d.astype(BF16)
    y = _dot(cat_ref[...], w_ref[...])
    y = y * lax.rsqrt(jnp.mean(y * y, axis=-1, keepdims=True) + EPS)
    o_ref[...] = x_ref[...] + y * g_ref[...]


def _out_proj(x2, heads, dils, lses, w_out, g, tm=OUT_TM):
    m = x2.shape[0]
    gw = GW
    row = lambda i: (i, 0)
    const = lambda i: (0, 0)
    in_specs = ([pl.BlockSpec((tm, D_MODEL), row)]
                + [pl.BlockSpec((tm, gw), row)] * 7 + [pl.BlockSpec((tm, LANES), row)] * 3
                + [pl.BlockSpec((D_MODEL, D_MODEL), const), pl.BlockSpec((1, D_MODEL), const)])
    return pl.pallas_call(
        _out_proj_kernel,
        out_shape=jax.ShapeDtypeStruct((m, D_MODEL), F32),
        grid=(m // tm,),
        in_specs=in_specs,
        out_specs=pl.BlockSpec((tm, D_MODEL), row),
        scratch_shapes=[pltpu.VMEM((tm, D_MODEL), BF16)],
        compiler_params=_cparams(("parallel",)),
        name="out_proj_residual",
    )(x2, *heads, *dils, *lses, w_out, g)


def _ffn_kernel(x_ref, gpre_ref, wu_ref, wd_ref, gpost_ref, o_ref, h_ref, acc_ref):
    f = pl.program_id(1)

    @pl.when(f == 0)
    def _():
        x = x_ref[...]
        y = x * lax.rsqrt(jnp.mean(x * x, axis=-1, keepdims=True) + EPS)
        h_ref[...] = (y * gpre_ref[...]).astype(BF16)
        acc_ref[...] = jnp.zeros_like(acc_ref)

    u = jnp.maximum(_dot(h_ref[...], wu_ref[...]), 0.0)
    acc_ref[...] += _dot((u * u).astype(BF16), wd_ref[...])

    @pl.when(f == pl.num_programs(1) - 1)
    def _():
        y = acc_ref[...]
        y = y * lax.rsqrt(jnp.mean(y * y, axis=-1, keepdims=True) + EPS)
        o_ref[...] = x_ref[...] + y * gpost_ref[...]


def _ffn(x2, g_pre, w_up, w_down, g_post, tm=FFN_TM, tf=FFN_TF):
    m = x2.shape[0]
    return pl.pallas_call(
        _ffn_kernel,
        out_shape=jax.ShapeDtypeStruct((m, D_MODEL), F32),
        grid=(m // tm, D_FF // tf),
        in_specs=[pl.BlockSpec((tm, D_MODEL), lambda i, f: (i, 0)),
                  pl.BlockSpec((1, D_MODEL), lambda i, f: (0, 0)),
                  pl.BlockSpec((D_MODEL, tf), lambda i, f: (0, f)),
                  pl.BlockSpec((tf, D_MODEL), lambda i, f: (f, 0)),
                  pl.BlockSpec((1, D_MODEL), lambda i, f: (0, 0))],
        out_specs=pl.BlockSpec((tm, D_MODEL), lambda i, f: (i, 0)),
        scratch_shapes=[pltpu.VMEM((tm, D_MODEL), BF16), pltpu.VMEM((tm, D_MODEL), F32)],
        compiler_params=_cparams(("parallel", "arbitrary")),
        name="ffn_residual",
    )(x2, g_pre, w_up, w_down, g_post)


def _split_w_in(w):
    gw, kw = GW, LANES
    o = 0
    cols = {}
    for name, width in (("qa", gw), ("kca", kw), ("vca", kw), ("ksa", kw), ("vsa", kw), ("kwa", kw),
                        ("vwa", kw), ("ga", 24), ("qb", gw), ("kb", gw), ("vb", gw), ("qc", gw),
                        ("kc", gw), ("vc", gw), ("fc", 8), ("qd", gw), ("kd", gw), ("vd", gw)):
        cols[name] = w[:, o:o + width] * (Q_FOLD if name[0] == "q" else 1.0)
        o += width
    main = jnp.concatenate([cols[n] for n in ("qa", "ksa", "vsa", "kwa", "vwa", "qb", "kb", "vb",
                                              "qc", "kc", "vc", "qd", "kd", "vd")], axis=1)
    pad = jnp.zeros((w.shape[0], LANES - 32), w.dtype)
    small = jnp.concatenate([cols["ga"], cols["fc"], pad, cols["kca"], cols["vca"]], axis=1)
    return main.astype(BF16), small.astype(BF16)


def _compress_weights(pe, w1, w2):
    half = NSA_CMP_LEN // 2
    hid = NSA_CMP_HIDDEN
    w1r = w1.reshape(2, half, HEAD_DIM, hid)
    z = jnp.zeros_like(w1r)
    per_group = jnp.stack([jnp.concatenate([w1r, z], axis=-1), jnp.concatenate([z, w1r], axis=-1)], axis=2)
    w_halves = per_group.reshape(2, half * 2 * HEAD_DIM, 2 * hid).astype(BF16)
    pe_r = jnp.broadcast_to(pe.reshape(2, half, 1, HEAD_DIM), (2, half, 2, HEAD_DIM)).reshape(2, half * LANES)
    z2 = jnp.zeros_like(w2)
    w2g = jnp.concatenate([jnp.concatenate([w2, z2], axis=1), jnp.concatenate([z2, w2], axis=1)], axis=0)
    return pe_r, w_halves[0], w_halves[1], w2g.astype(BF16)


def _bias_tables(rel_bias, t):
    hm = 8
    tq_a = min(NSA_TQ_CMP, t)
    tq_d = tk_d = min(NSA_TQ_DENSE, t)
    n_a = min(_n_toeplitz_tiles(tq_d, tk_d), t // tq_d)
    tab_a = _bias_tiles(rel_bias, _toeplitz_idx_t(n_a, tk_d, tq_d, tq_d), 0, 2, 4)
    ncp = t // NSA_CMP_STRIDE
    qi = jnp.arange(t // tq_a, dtype=jnp.int32)[:, None, None]
    c = jnp.arange(ncp, dtype=jnp.int32)[None, :, None]
    q = jnp.arange(tq_a, dtype=jnp.int32)[None, None, :]
    idx_c = _t5_bucket(qi * tq_a + q - (c * NSA_CMP_STRIDE + NSA_CMP_LEN - 1))
    tab_c = _bias_tiles(rel_bias, idx_c, 0, 2, 4)
    tq_b = min(MOBA_TQ, t)
    n_b = min(_n_toeplitz_tiles(tq_b, tq_b), t // tq_b)
    tab_b = _bias_tiles(rel_bias, _toeplitz_idx_t(n_b, tq_b, tq_b, tq_b), hm, 4, 2)
    kr = jnp.arange(2 * DIL_BAND, dtype=jnp.int32)[None, :, None]
    qc = jnp.arange(DIL_BAND, dtype=jnp.int32)[None, None, :]
    tabs_d = [_bias_tiles(rel_bias, _t5_bucket((DIL_BAND + qc - kr) * dil), 2 * hm, 4, 2)
              for _, dil in LONGNET_PATTERNS]
    return tab_a, tab_c, tab_b, tabs_d


def _overlap_t(t):
    ncp = t // NSA_CMP_STRIDE
    nsp = max(t // NSA_SEL_LEN, 8)
    c_start = jnp.arange(ncp)[None, :] * NSA_CMP_STRIDE
    s_start = jnp.arange(nsp)[:, None] * NSA_SEL_LEN
    ovl = (c_start < s_start + NSA_SEL_LEN) & (c_start + NSA_CMP_LEN - 1 >= s_start)
    ovl = ovl & (jnp.arange(ncp)[None, :] < ncp - 1) & (jnp.arange(nsp)[:, None] < t // NSA_SEL_LEN)
    return jnp.concatenate([ovl, ovl], axis=1).astype(BF16)


def _transposed(p3, pb, n_blocks):
    b, t, _ = p3.shape
    vt = p3[:, :, pb * LANES:(pb + n_blocks) * LANES].transpose(0, 2, 1).reshape(b, 2 * n_blocks, HEAD_DIM, t)
    ones = jnp.ones((b, 2 * n_blocks, VT_ROWS - HEAD_DIM, t), BF16)
    return jnp.concatenate([vt, ones], axis=2).reshape(b, 2 * n_blocks * VT_ROWS, t)


def _mixer_layer(x2, b, t, w_main, w_small, g_pre, g_post, w_out, cmp_w_k, cmp_w_v, fox_bias, tabs, ovl_t):
    tab_a, tab_c, tab_b, tabs_d = tabs
    m = b * t
    p2, pd2, small, kca, vca = _proj(x2, g_pre, w_main, w_small)
    p3 = p2.reshape(b, t, P_WIDTH)
    pd3 = pd2.reshape(b, t, PD_WIDTH)
    small3 = small.reshape(b, t, LANES)
    row_w = NSA_CMP_STRIDE * LANES
    kc = _nsa_compress(kca.reshape(b, t // NSA_CMP_STRIDE, row_w), *cmp_w_k)
    vc = _nsa_compress(vca.reshape(b, t // NSA_CMP_STRIDE, row_w), *cmp_w_v)
    o_cmp, negsel = _nsa_cmp(p3, kc, _transposed(vc, 0, 1), tab_c, ovl_t, small3)
    o_dense = _nsa_dense(p3, tab_a, small3, _transposed(p3, PB_VSA, 1), _transposed(p3, PB_VWA, 1), negsel)
    o_b = _moba(p3, tab_b, _transposed(p3, PB_VB, 4))
    fox_bias_row = jnp.zeros((1, LANES), F32).at[0, FOX_HI:FOX_HI + 8].set(fox_bias)
    o_f = _fox(p3, _fox_cumsum(small3, fox_bias_row), _transposed(p3, PB_VC, 4))
    dil = [_dilated(pd3, tab, d) for tab, (_, d) in zip(tabs_d, LONGNET_PATTERNS)]
    heads = [a.reshape(m, GW) for a in (o_cmp, o_dense, o_b, o_f)]
    return _out_proj(x2, heads, [d[0] for d in dil], [d[1] for d in dil], w_out, g_post)


def kernel(x, w_in, w_out, g_mix_pre, g_mix_post, g_mlp_pre, g_mlp_post, w_up, w_down, cmp_pe, phik_w1, phik_w2, phiv_w1, phiv_w2, fox_bias, rel_bias):
    b, t, d = x.shape
    depth = w_in.shape[0]
    tabs = _bias_tables(rel_bias, t)
    ovl_t = _overlap_t(t)
    x2 = x.reshape(b * t, d)
    for l in range(depth):
        w_main, w_small = _split_w_in(w_in[l])
        x2 = _mixer_layer(
            x2, b, t, w_main, w_small, g_mix_pre[l][None], g_mix_post[l][None], w_out[l].astype(BF16),
            _compress_weights(cmp_pe[l], phik_w1[l], phik_w2[l]),
            _compress_weights(cmp_pe[l], phiv_w1[l], phiv_w2[l]),
            fox_bias[l], tabs, ovl_t)
        x2 = _ffn(x2, g_mlp_pre[l][None], w_up[l].astype(BF16), w_down[l].astype(BF16), g_mlp_post[l][None])
    return x2.reshape(b, t, d)
```
